```python
import jax, jax.numpy as jnp
from jax import lax
import numpy as np

D_MODEL = 2048
BATCH = 4
SEQ = 2048
DEPTH = 2
DEC_BATCH = 8
DEC_SEQ = 4
PAST_LEN = 16384
PAGE_SIZE = 128

HEAD_DIM = 128
ROT_FRACTION = 4
ROPE_THETA = 500000.0
NSA_HEADS = 8
NSA_KV = 2
CMP_LEN = 32
CMP_STRIDE = 16
SEL_BLOCK = 64
SEL_TOP = 16
WINDOW = 512
NSA_QBLOCK = 32
WIN_QBLOCK = 128
FORCE_SCORE = 1e4
DSA_HEADS = 4
DSA_KV = 2
IDX_HEADS = 4
IDX_DIM = 64
DSA_TOPK = 256
DSA_QBLOCK = 128
CONV_DIM = 512
CONV_WIDTH = 3
N_EXPERTS = 16
N_GROUPS = 4
EXPERTS_PER_GROUP = N_EXPERTS // N_GROUPS
MOE_TOPK = 2
D_FF = 512
LN_EPS = 1e-5
ALPHA = (2 * DEPTH) ** 0.25
BETA = (8 * DEPTH) ** -0.25
IN_WIDTHS = (NSA_HEADS * HEAD_DIM, 6 * NSA_KV * HEAD_DIM, 3 * NSA_HEADS,
             DSA_HEADS * HEAD_DIM, 2 * DSA_KV * HEAD_DIM, IDX_HEADS * IDX_DIM, IDX_DIM, IDX_HEADS,
             3 * CONV_DIM, 3 * D_MODEL)
D_IN = sum(IN_WIDTHS)

kernel_name = 'hybrid_nsa_dsa_shortconv_grouped_moe_step'


def _layernorm(x, g, b):
    xf = x.astype(jnp.float32)
    mu = xf.mean(-1, keepdims=True)
    var = jnp.square(xf - mu).mean(-1, keepdims=True)
    return ((xf - mu) * lax.rsqrt(var + LN_EPS) * g + b).astype(x.dtype)


def _rope(x, pos):
    d = x.shape[-1]
    rot = d // ROT_FRACTION
    half = rot // 2
    inv = ROPE_THETA ** (-jnp.arange(half, dtype=jnp.float32) / half)
    ang = pos.astype(jnp.float32)[:, None] * inv[None, :]
    cos = jnp.cos(ang)[None, :, None, :]
    sin = jnp.sin(ang)[None, :, None, :]
    xf = x.astype(jnp.float32)
    x1, x2 = xf[..., :half], xf[..., half:rot]
    out = jnp.concatenate([x1 * cos - x2 * sin, x1 * sin + x2 * cos, xf[..., rot:]], axis=-1)
    return out.astype(x.dtype)


def _masked_softmax(s, mask):
    s = jnp.where(mask, s.astype(jnp.float32), -jnp.inf)
    m = jnp.max(s, axis=-1, keepdims=True)
    m = jnp.where(jnp.isfinite(m), m, 0.0)
    p = jnp.exp(s - m)
    return p / jnp.maximum(p.sum(-1, keepdims=True), 1e-30)


def _sweep(fn, qblock, qs, qpos):
    T = qpos.shape[0]
    if T <= qblock or T % qblock:
        return fn(*qs, qpos)
    nb = T // qblock
    xs = tuple(a.reshape((a.shape[0], nb, qblock) + a.shape[2:]).swapaxes(0, 1) for a in qs)
    out = lax.map(lambda args: fn(*args), xs + (qpos.reshape(nb, qblock),))
    return jax.tree_util.tree_map(lambda o: o.swapaxes(0, 1).reshape((o.shape[1], T) + o.shape[3:]), out)


def _project(x, pos, w_in):
    B, T, _ = x.shape
    h = x @ w_in
    split_at = [int(v) for v in np.cumsum(IN_WIDTHS)[:-1]]
    nq, nkv, ng, dq, dkv, iq, ik, iw, cv, mg = jnp.split(h, split_at, axis=-1)
    q_raw = nq.reshape(B, T, NSA_HEADS, HEAD_DIM)
    q_rot = _rope(q_raw, pos)
    nkv = nkv.reshape(B, T, 6, NSA_KV, HEAD_DIM)
    nsa4 = jnp.stack([nkv[:, :, 0], nkv[:, :, 1], _rope(nkv[:, :, 2], pos), nkv[:, :, 3]], axis=2)
    win2 = jnp.stack([_rope(nkv[:, :, 4], pos), nkv[:, :, 5]], axis=2)
    nsa_g = jax.nn.sigmoid(ng.reshape(B, T, 3, NSA_HEADS))
    dkv = dkv.reshape(B, T, 2, DSA_KV, HEAD_DIM)
    dsa2 = jnp.stack([_rope(dkv[:, :, 0], pos), dkv[:, :, 1]], axis=2)
    dq = _rope(dq.reshape(B, T, DSA_HEADS, HEAD_DIM), pos)
    iq = _rope(iq.reshape(B, T, IDX_HEADS, IDX_DIM), pos)
    ik = _rope(ik[:, :, None, :], pos)[:, :, 0]
    iw = iw * IDX_HEADS ** -0.5
    mg = jax.nn.sigmoid(mg.reshape(B, T, 3, D_MODEL))
    return q_raw, q_rot, nsa4, win2, nsa_g, dq, dsa2, iq, ik, iw, cv, mg


def _nsa_compress(kv, pos_emb, w1, w2):
    B, L, G, Dh = kv.shape
    n_chunks = -(-L // CMP_STRIDE)
    r = CMP_LEN // CMP_STRIDE
    kv = jnp.pad(kv, ((0, 0), (0, n_chunks * CMP_STRIDE - L), (0, 0), (0, 0)))
    ch = kv.reshape(B, n_chunks, CMP_STRIDE, G, Dh)
    n_cmp = n_chunks - r + 1
    blk = jnp.concatenate([ch[:, i:i + n_cmp] for i in range(r)], axis=2)
    blk = blk + pos_emb[None, None, :, None, :]
    flat = blk.transpose(0, 1, 3, 2, 4).reshape(B, n_cmp, G, CMP_LEN * Dh)
    return jax.nn.gelu(flat @ w1) @ w2


def _overlap(n_cmp, n_sel):
    cs = jnp.arange(n_cmp)[:, None] * CMP_STRIDE
    ss = jnp.arange(n_sel)[None, :] * SEL_BLOCK
    return ((cs < ss + SEL_BLOCK) & (cs + CMP_LEN > ss)).astype(jnp.float32)


def _nsa_cmp_sel(q_raw, q_rot, qpos, k_cmp, v_cmp, k_sel, v_sel):
    B, Q, H, Dh = q_rot.shape
    G = k_cmp.shape[2]
    R = H // G
    n_cmp = k_cmp.shape[1]
    n_sel = k_sel.shape[2]
    scale = Dh ** -0.5
    s_c = jnp.einsum('bqgrd,bjgd->bqgrj', q_raw.reshape(B, Q, G, R, Dh), k_cmp) * scale
    cmp_end = jnp.arange(n_cmp) * CMP_STRIDE + (CMP_LEN - 1)
    p_c = _masked_softmax(s_c, (cmp_end[None, :] <= qpos[:, None])[None, :, None, None, :])
    o_cmp = jnp.einsum('bqgrj,bjgd->bqgrd', p_c.astype(v_cmp.dtype), v_cmp)
    imp = jnp.einsum('bqgj,js->bqgs', p_c.sum(3), _overlap(n_cmp, n_sel))
    blk = jnp.arange(n_sel)[None, :]
    cur = (qpos // SEL_BLOCK)[:, None]
    forced = (blk == 0) | (blk == cur) | (blk == cur - 1)
    imp = jnp.where(forced[None, :, None, :], imp + FORCE_SCORE, imp)
    imp = jnp.where((blk <= cur)[None, :, None, :], imp, -jnp.inf)
    _, idx = lax.top_k(imp, min(SEL_TOP, n_sel))
    n_top = idx.shape[-1]
    bi = jnp.arange(B)[:, None, None, None]
    gi = jnp.arange(G)[None, None, :, None]
    kg = k_sel[bi, gi, idx].reshape(B, Q, G, n_top * SEL_BLOCK, Dh)
    vg = v_sel[bi, gi, idx].reshape(B, Q, G, n_top * SEL_BLOCK, Dh)
    kpos = (idx[..., None] * SEL_BLOCK + jnp.arange(SEL_BLOCK)).reshape(B, Q, G, 1, n_top * SEL_BLOCK)
    s_s = jnp.einsum('bqgrd,bqgmd->bqgrm', q_rot.reshape(B, Q, G, R, Dh), kg) * scale
    p_s = _masked_softmax(s_s, kpos <= qpos[None, :, None, None, None])
    o_sel = jnp.einsum('bqgrm,bqgmd->bqgrd', p_s.astype(vg.dtype), vg)
    return o_cmp.reshape(B, Q, H, Dh), o_sel.reshape(B, Q, H, Dh)


def _nsa_attend(q_raw, q_rot, qpos, kv4, phi_pos, phi_w1, phi_w2):
    B, L, _, G, Dh = kv4.shape
    k_cmp = _nsa_compress(kv4[:, :, 0], phi_pos[0], phi_w1[0], phi_w2[0])
    v_cmp = _nsa_compress(kv4[:, :, 1], phi_pos[1], phi_w1[1], phi_w2[1])
    n_sel = -(-L // SEL_BLOCK)

    def blocks(a):
        a = jnp.pad(a, ((0, 0), (0, n_sel * SEL_BLOCK - L), (0, 0), (0, 0)))
        return a.reshape(B, n_sel, SEL_BLOCK, G, Dh).transpose(0, 3, 1, 2, 4)

    k_sel = blocks(kv4[:, :, 2])
    v_sel = blocks(kv4[:, :, 3])
    fn = lambda qr, qo, qp: _nsa_cmp_sel(qr, qo, qp, k_cmp, v_cmp, k_sel, v_sel)
    return _sweep(fn, NSA_QBLOCK, (q_raw, q_rot), qpos)


def _attend_window(q, qpos, k, v, kpos):
    B, Tq, H, Dh = q.shape
    G = k.shape[2]
    R = H // G
    s = jnp.einsum('bqgrd,bkgd->bqgrk', q.reshape(B, Tq, G, R, Dh), k) * Dh ** -0.5
    rel = qpos[:, None] - kpos[None, :]
    mask = (rel >= 0) & (rel < WINDOW) & (kpos[None, :] >= 0)
    p = _masked_softmax(s, mask[None, :, None, None, :])
    o = jnp.einsum('bqgrk,bkgd->bqgrd', p.astype(v.dtype), v)
    return o.reshape(B, Tq, H, Dh)


def _window_prompt(q, k, v):
    pad = ((0, 0), (WINDOW, 0), (0, 0), (0, 0))
    kp = jnp.pad(k, pad)
    vp = jnp.pad(v, pad)

    def fn(qb, qpos_b):
        start = qpos_b[0]
        n = qb.shape[1] + WINDOW
        kb = lax.dynamic_slice_in_dim(kp, start, n, axis=1)
        vb = lax.dynamic_slice_in_dim(vp, start, n, axis=1)
        kpos = start - WINDOW + jnp.arange(n)
        return _attend_window(qb, qpos_b, kb, vb, kpos)

    return _sweep(fn, WIN_QBLOCK, (q,), jnp.arange(q.shape[1]))


def _dsa_block(q, qi, wi, qpos, k, v, kidx):
    B, Q, H, Dh = q.shape
    L, G = k.shape[1], k.shape[2]
    R = H // G
    n_keep = min(DSA_TOPK, L // 4)
    rel = jax.nn.relu(jnp.einsum('bqhd,bsd->bqhs', qi, kidx))
    score = jnp.einsum('bqh,bqhs->bqs', wi, rel).astype(jnp.float32)
    score = jnp.where((jnp.arange(L)[None, :] <= qpos[:, None])[None], score, -jnp.inf)
    _, sel = lax.top_k(score, n_keep)
    bi = jnp.arange(B)[:, None, None]
    kg = k[bi, sel]
    vg = v[bi, sel]
    s = jnp.einsum('bqgrd,bqngd->bqgrn', q.reshape(B, Q, G, R, Dh), kg) * Dh ** -0.5
    p = _masked_softmax(s, (sel <= qpos[None, :, None])[:, :, None, None, :])
    o = jnp.einsum('bqgrn,bqngd->bqgrd', p.astype(vg.dtype), vg)
    return o.reshape(B, Q, H, Dh)


def _dsa_attend(dq, iq, iw, qpos, kv2, kidx):
    k = kv2[:, :, 0]
    v = kv2[:, :, 1]
    fn = lambda a, b, c, qp: _dsa_block(a, b, c, qp, k, v, kidx)
    return _sweep(fn, DSA_QBLOCK, (dq, iq, iw), qpos)


def _short_conv(cv, w, past):
    b, c, u = jnp.split(cv, 3, axis=-1)
    u = c * u
    ext = jnp.concatenate([past, u], axis=1)
    T = u.shape[1]
    y = w[0] * ext[:, 0:T]
    for i in range(1, CONV_WIDTH):
        y = y + w[i] * ext[:, i:i + T]
    return b * y, ext[:, T:]


def _mixer_tail(q_raw, q_rot, nsa_g, dq, iq, iw, mg, pos, nsa_full, dsa_full, kidx_full, o_win, conv_out,
                phi_pos, phi_w1, phi_w2, w_a, w_b, w_c, w_o):
    o_cmp, o_sel = _nsa_attend(q_raw, q_rot, pos, nsa_full, phi_pos, phi_w1, phi_w2)
    o_dsa = _dsa_attend(dq, iq, iw, pos, dsa_full, kidx_full)
    B, T = q_raw.shape[:2]
    g = nsa_g[..., None]
    o_nsa = g[:, :, 0] * o_cmp + g[:, :, 1] * o_sel + g[:, :, 2] * o_win
    p_a = o_nsa.reshape(B, T, -1) @ w_a
    p_b = o_dsa.reshape(B, T, -1) @ w_b
    p_c = conv_out @ w_c
    return (mg[:, :, 0] * p_a + mg[:, :, 1] * p_b + mg[:, :, 2] * p_c) @ w_o


def _mixer_prompt(x, lw):
    w_in, phi_pos, phi_w1, phi_w2, conv_w, w_a, w_b, w_c, w_o = lw
    B, T, _ = x.shape
    pos = jnp.arange(T, dtype=jnp.int32)
    q_raw, q_rot, nsa4, win2, nsa_g, dq, dsa2, iq, ik, iw, cv, mg = _project(x, pos, w_in)
    o_win = _window_prompt(q_rot, win2[:, :, 0], win2[:, :, 1])
    conv_out, conv_state = _short_conv(cv, conv_w, jnp.zeros((B, CONV_WIDTH - 1, CONV_DIM), x.dtype))
    y = _mixer_tail(q_raw, q_rot, nsa_g, dq, iq, iw, mg, pos, nsa4, dsa2, ik, o_win, conv_out,
                    phi_pos, phi_w1, phi_w2, w_a, w_b, w_c, w_o)
    return y, (nsa4, dsa2, ik, win2[:, -min(WINDOW, T):], conv_state)


def _mixer_sample(x, c_nsa, c_dsa, c_kidx, s_win, s_conv, page_table, lw):
    w_in, phi_pos, phi_w1, phi_w2, conv_w, w_a, w_b, w_c, w_o = lw
    B, T, _ = x.shape
    pos = PAST_LEN + jnp.arange(T, dtype=jnp.int32)
    q_raw, q_rot, nsa4, win2, nsa_g, dq, dsa2, iq, ik, iw, cv, mg = _project(x, pos, w_in)

    def past(pool):
        return pool[page_table].reshape((B, -1) + pool.shape[2:])

    nsa_full = jnp.concatenate([past(c_nsa), nsa4], axis=1)
    dsa_full = jnp.concatenate([past(c_dsa), dsa2], axis=1)
    kidx_full = jnp.concatenate([past(c_kidx), ik], axis=1)
    wb = s_win.shape[1]
    win_cat = jnp.concatenate([s_win, win2], axis=1)
    kpos = PAST_LEN - wb + jnp.arange(wb + T)
    o_win = _attend_window(q_rot, pos, win_cat[:, :, 0], win_cat[:, :, 1], kpos)
    conv_out, conv_state = _short_conv(cv, conv_w, s_conv)
    y = _mixer_tail(q_raw, q_rot, nsa_g, dq, iq, iw, mg, pos, nsa_full, dsa_full, kidx_full, o_win, conv_out,
                    phi_pos, phi_w1, phi_w2, w_a, w_b, w_c, w_o)
    return y, (nsa4, dsa2, ik, win_cat[:, -wb:], conv_state)


def _moe(x, w_router, router_bias, w_gate, w_up, w_down):
    B, T, D = x.shape
    xt = x.reshape(-1, D)
    n = xt.shape[0]
    aff = jax.nn.sigmoid((xt @ w_router).astype(jnp.float32))
    biased = (aff + router_bias.astype(jnp.float32)).reshape(n, N_GROUPS, EXPERTS_PER_GROUP)
    g_score = lax.top_k(biased, MOE_TOPK)[0].sum(-1)
    g_best = jnp.argmax(g_score, axis=-1)
    in_grp = jnp.take_along_axis(biased, g_best[:, None, None], axis=1)[:, 0]
    _, loc = lax.top_k(in_grp, MOE_TOPK)
    e_idx = g_best[:, None] * EXPERTS_PER_GROUP + loc
    w_sel = jnp.take_along_axis(aff, e_idx, axis=-1)
    w_sel = w_sel / w_sel.sum(-1, keepdims=True)
    gate = (jax.nn.one_hot(e_idx, N_EXPERTS, dtype=jnp.float32) * w_sel[..., None]).sum(1)
    h = jax.nn.silu(jnp.einsum('nd,edf->nef', xt, w_gate)) * jnp.einsum('nd,edf->nef', xt, w_up)
    h = h * gate.astype(h.dtype)[..., None]
    return jnp.einsum('nef,efd->nd', h, w_down).reshape(B, T, D)


def setup_inputs(seed: int = 0) -> dict:
    key = jax.random.key(seed)
    ks = jax.random.split(key, 32)
    f32 = jnp.float32
    n_pages = PAST_LEN // PAGE_SIZE
    n_used = DEC_BATCH * n_pages
    n_pool = n_used + max(1, n_used // 4)
    wbuf = min(WINDOW, PAST_LEN)

    def rn(i, shape, scale=1.0):
        return jax.random.normal(ks[i], shape, f32) * scale

    page_table = jax.random.permutation(ks[0], n_pool)[:n_used].reshape(DEC_BATCH, n_pages).astype(jnp.int32)
    return {
        'x_prompt': rn(1, (BATCH, SEQ, D_MODEL)),
        'x_sample': rn(2, (DEC_BATCH, DEC_SEQ, D_MODEL)),
        'cache_nsa_kv': rn(3, (DEPTH, n_pool, PAGE_SIZE, 4, NSA_KV, HEAD_DIM)),
        'cache_dsa_kv': rn(4, (DEPTH, n_pool, PAGE_SIZE, 2, DSA_KV, HEAD_DIM)),
        'cache_dsa_kidx': rn(5, (DEPTH, n_pool, PAGE_SIZE, IDX_DIM)),
        'state_nsa_win': rn(6, (DEPTH, DEC_BATCH, wbuf, 2, NSA_KV, HEAD_DIM)),
        'state_conv': rn(7, (DEPTH, DEC_BATCH, CONV_WIDTH - 1, CONV_DIM)),
        'page_table': page_table,
        'w_in': rn(8, (DEPTH, D_MODEL, D_IN), D_MODEL ** -0.5),
        'nsa_phi_pos': rn(9, (DEPTH, 2, CMP_LEN, HEAD_DIM), 0.5),
        'nsa_phi_w1': rn(10, (DEPTH, 2, CMP_LEN * HEAD_DIM, HEAD_DIM), (CMP_LEN * HEAD_DIM) ** -0.5),
        'nsa_phi_w2': rn(11, (DEPTH, 2, HEAD_DIM, HEAD_DIM), 2.0 * HEAD_DIM ** -0.5),
        'conv_w': rn(12, (DEPTH, CONV_WIDTH, CONV_DIM), CONV_WIDTH ** -0.5),
        'w_br_a': rn(13, (DEPTH, NSA_HEADS * HEAD_DIM, D_MODEL), BETA * (NSA_HEADS * HEAD_DIM) ** -0.5),
        'w_br_b': rn(14, (DEPTH, DSA_HEADS * HEAD_DIM, D_MODEL), BETA * (DSA_HEADS * HEAD_DIM) ** -0.5),
        'w_br_c': rn(15, (DEPTH, CONV_DIM, D_MODEL), BETA * CONV_DIM ** -0.5),
        'w_out': rn(16, (DEPTH, D_MODEL, D_MODEL), BETA * D_MODEL ** -0.5),
        'ln_mix_g': 1.0 + rn(17, (DEPTH, D_MODEL), 0.01),
        'ln_mix_b': rn(18, (DEPTH, D_MODEL), 0.01),
        'ln_ffn_g': 1.0 + rn(19, (DEPTH, D_MODEL), 0.01),
        'ln_ffn_b': rn(20, (DEPTH, D_MODEL), 0.01),
        'w_router': rn(21, (D_MODEL, N_EXPERTS), D_MODEL ** -0.5),
        'router_bias': rn(22, (N_EXPERTS,), 0.01),
        'w_e_gate': rn(23, (DEPTH, N_EXPERTS, D_MODEL, D_FF), D_MODEL ** -0.5),
        'w_e_up': rn(24, (DEPTH, N_EXPERTS, D_MODEL, D_FF), D_MODEL ** -0.5),
        'w_e_down': rn(25, (DEPTH, N_EXPERTS, D_FF, D_MODEL), BETA * D_FF ** -0.5),
    }


def reference(x_prompt, x_sample, cache_nsa_kv, cache_dsa_kv, cache_dsa_kidx, state_nsa_win, state_conv,
              page_table, w_in, nsa_phi_pos, nsa_phi_w1, nsa_phi_w2, conv_w, w_br_a, w_br_b, w_br_c, w_out,
              ln_mix_g, ln_mix_b, ln_ffn_g, ln_ffn_b, w_router, router_bias, w_e_gate, w_e_up, w_e_down):
    xp = x_prompt
    xs = x_sample
    sp_list = [[] for _ in range(5)]
    ss_list = [[] for _ in range(5)]
    for l in range(DEPTH):
        lw = (w_in[l], nsa_phi_pos[l], nsa_phi_w1[l], nsa_phi_w2[l], conv_w[l],
              w_br_a[l], w_br_b[l], w_br_c[l], w_out[l])
        mix_p, st_p = _mixer_prompt(xp, lw)
        mix_s, st_s = _mixer_sample(xs, cache_nsa_kv[l], cache_dsa_kv[l], cache_dsa_kidx[l],
                                    state_nsa_win[l], state_conv[l], page_table, lw)
        for i in range(5):
            sp_list[i].append(st_p[i])
            ss_list[i].append(st_s[i])
        xp = _layernorm(ALPHA * xp + mix_p, ln_mix_g[l], ln_mix_b[l])
        xs = _layernorm(ALPHA * xs + mix_s, ln_mix_g[l], ln_mix_b[l])
        xp = _layernorm(ALPHA * xp + _moe(xp, w_router, router_bias, w_e_gate[l], w_e_up[l], w_e_down[l]),
                        ln_ffn_g[l], ln_ffn_b[l])
        xs = _layernorm(ALPHA * xs + _moe(xs, w_router, router_bias, w_e_gate[l], w_e_up[l], w_e_down[l]),
                        ln_ffn_g[l], ln_ffn_b[l])
    nsa_kv_p, dsa_kv_p, kidx_p, win_p, conv_p = [jnp.stack(a, axis=0) for a in sp_list]
    nsa_kv_s, dsa_kv_s, kidx_s, win_s, conv_s = [jnp.stack(a, axis=0) for a in ss_list]
    return (xp, xs, nsa_kv_p, nsa_kv_s, dsa_kv_p, dsa_kv_s, kidx_p, kidx_s, win_p, win_s, conv_p, conv_s)
```

```python
import functools

import numpy as np
import jax
import jax.numpy as jnp
from jax import lax
from jax.experimental import pallas as pl
from jax.experimental.pallas import tpu as pltpu

F32 = jnp.float32
BF16 = jnp.bfloat16
HIGHEST = lax.Precision.HIGHEST
NEG_INF = float("-inf")

D_MODEL = 2048
DEPTH = 2
PAST_LEN = 16384
PAGE = 128
HEAD_DIM = 128
ROPE_THETA = 500000.0
NSA_HEADS = 8
NSA_KV = 2
CMP_LEN = 32
CMP_STRIDE = 16
SEL_BLOCK = 64
SEL_TOP = 16
WINDOW = 512
FORCE_SCORE = 1e4
DSA_HEADS = 4
DSA_KV = 2
IDX_HEADS = 4
IDX_DIM = 64
DSA_TOPK = 256
CONV_DIM = 512
N_EXPERTS = 16
N_GROUPS = 4
EXPERTS_PER_GROUP = 4
D_FF = 512
LN_EPS = 1e-5
ALPHA = (2 * DEPTH) ** 0.25
IN_WIDTHS = (1024, 1536, 24, 512, 512, 256, 64, 4, 1536, 6144)
ATT_SCALE = HEAD_DIM ** -0.5

LANES = 128
SUBLANES = 8
VMEM_LIMIT = 56 * 1024 * 1024

C_QRAW = 0
C_QROT = 1024
C_CMP = 2048
C_SLC = 2560
C_WIN = 3072
C_DQ = 3584
C_DSA = 4096
C_CV = 4608
C_MG = 6144
C_IQ = 12288
C_MISC = 12544
NCOL = 12800
MISC_IW = 64
MISC_NG = 68
PROJ_TN = 512
PAGES_PER_STEP = 16
KSTEP = PAGES_PER_STEP * PAGE

K_PLAIN, K_ROPE128, K_ROPE64, K_SIGMOID, K_MISC = 0, 1, 2, 3, 4


def _col_kinds():
    kinds = np.zeros(NCOL // LANES, np.int32)

    def mark(c0, n, k):
        kinds[c0 // LANES:(c0 + n) // LANES] = k

    mark(C_QROT, 1024, K_ROPE128)
    mark(C_SLC, 256, K_ROPE128)
    mark(C_WIN, 256, K_ROPE128)
    mark(C_DQ, 512, K_ROPE128)
    mark(C_DSA, 256, K_ROPE128)
    mark(C_IQ, 256, K_ROPE64)
    mark(C_MISC, 128, K_MISC)
    mark(C_MG, 6144, K_SIGMOID)
    return kinds


def _cparams(*sem):
    return pltpu.CompilerParams(dimension_semantics=sem, vmem_limit_bytes=VMEM_LIMIT)


def _sigmoid(x):
    return 1.0 / (1.0 + jnp.exp(-x))


def _layernorm(x, g, b):
    mu = jnp.mean(x, axis=-1, keepdims=True)
    xc = x - mu
    var = jnp.mean(xc * xc, axis=-1, keepdims=True)
    return xc * lax.rsqrt(var + LN_EPS) * g + b


def _dot_nt(a, b, precision=None):
    return lax.dot_general(a, b, (((1,), (1,)), ((), ())), preferred_element_type=F32, precision=precision)


def _softmax_parts(s):
    m = jnp.max(s, axis=-1, keepdims=True)
    m = jnp.where(m == NEG_INF, 0.0, m)
    p = jnp.exp(s - m)
    return p, jnp.maximum(jnp.sum(p, axis=-1, keepdims=True), 1e-30)


def _proj_body(kinds_ref, x_ref, w_ref, tab_ref, o_ref):
    nsub = PROJ_TN // LANES
    j = pl.program_id(1)
    h = jnp.dot(x_ref[...], w_ref[...], preferred_element_type=F32)
    lane = lax.broadcasted_iota(jnp.int32, (1, LANES), 1)

    def rope(hs, t0, sh):
        return (hs * tab_ref[t0] + pltpu.roll(hs, sh, 1) * tab_ref[t0 + 1]
                + pltpu.roll(hs, LANES - sh, 1) * tab_ref[t0 + 2])

    for s in range(nsub):
        kind = kinds_ref[j * nsub + s]
        hs = h[:, s * LANES:(s + 1) * LANES]
        sl = slice(s * LANES, (s + 1) * LANES)

        @pl.when(kind == K_PLAIN)
        def _():
            o_ref[:, sl] = hs

        @pl.when(kind == K_ROPE128)
        def _():
            o_ref[:, sl] = rope(hs, 0, 16)

        @pl.when(kind == K_ROPE64)
        def _():
            o_ref[:, sl] = rope(hs, 3, 8)

        @pl.when(kind == K_SIGMOID)
        def _():
            o_ref[:, sl] = _sigmoid(hs)

        @pl.when(kind == K_MISC)
        def _():
            r = rope(hs, 3, 8)
            o_ref[:, sl] = jnp.where(lane < MISC_IW, r,
                                     jnp.where(lane < MISC_NG, hs * (IDX_HEADS ** -0.5),
                                               jnp.where(lane < MISC_NG + 24, _sigmoid(hs), hs)))


def _proj(x_bf, w_bf, tabs, kinds, tm):
    n = x_bf.shape[0]
    n_tab = tabs.shape[1] // tm
    grid = (n // tm, NCOL // PROJ_TN)
    return pl.pallas_call(
        _proj_body,
        grid_spec=pltpu.PrefetchScalarGridSpec(
            num_scalar_prefetch=1, grid=grid,
            in_specs=[pl.BlockSpec((tm, D_MODEL), lambda i, j, k: (i, 0)),
                      pl.BlockSpec((D_MODEL, PROJ_TN), lambda i, j, k: (0, j)),
                      pl.BlockSpec((6, tm, LANES), lambda i, j, k: (0, i % n_tab, 0))],
            out_specs=pl.BlockSpec((tm, PROJ_TN), lambda i, j, k: (i, j))),
        out_shape=jax.ShapeDtypeStruct((n, NCOL), F32),
        compiler_params=_cparams("arbitrary", "arbitrary"),
    )(kinds, x_bf, w_bf, tabs)


def _rope_tables(pos):
    out = []
    lane = jnp.arange(LANES)
    for d in (HEAD_DIM, IDX_DIM):
        rot = d // 4
        half = rot // 2
        inv = ROPE_THETA ** (-jnp.arange(half, dtype=F32) / half)
        ang = pos.astype(F32)[:, None] * inv[None, :]
        cos = jnp.cos(ang)
        sin = jnp.sin(ang)
        li = lane % d
        ci = jnp.take(cos, li % half, axis=1)
        si = jnp.take(sin, li % half, axis=1)
        out.append(jnp.where(li[None] < rot, ci, 1.0))
        out.append(jnp.where((li[None] >= half) & (li[None] < rot), si, 0.0))
        out.append(jnp.where(li[None] < half, -si, 0.0))
    return jnp.stack(out, axis=0)


def _permute_w_in(w):
    offs = np.cumsum((0,) + IN_WIDTHS)
    nq, nkv, ng, dq, dkv, iq, ik, iw, cv, mg = [w[:, offs[i]:offs[i + 1]] for i in range(10)]
    z = lambda n: jnp.zeros((w.shape[0], n), w.dtype)
    cols = [nq, nq, nkv, dq, dkv, cv, mg, iq, ik, iw, ng, z(LANES - 92), z(NCOL - C_MISC - LANES)]
    return jnp.concatenate(cols, axis=1).astype(BF16)


def _conv_body(cv_ref, prev_ref, past_ref, w_ref, y_ref, cu_ref, s_ref, *, tiles_per_seq, tm):
    i = pl.program_id(0)
    b = cv_ref[:, 0:CONV_DIM]
    cu = cv_ref[:, CONV_DIM:2 * CONV_DIM] * cv_ref[:, 2 * CONV_DIM:3 * CONV_DIM]
    first = (i % tiles_per_seq) == 0
    prev = prev_ref[:, CONV_DIM:2 * CONV_DIM] * prev_ref[:, 2 * CONV_DIM:3 * CONV_DIM]
    s_ref[0:SUBLANES, :] = jnp.where(first, past_ref[0], prev)
    s_ref[SUBLANES:SUBLANES + tm, :] = cu
    y = (w_ref[0:1, :] * s_ref[pl.ds(SUBLANES - 2, tm), :] + w_ref[1:2, :] * s_ref[pl.ds(SUBLANES - 1, tm), :]
         + w_ref[2:3, :] * cu)
    y_ref[...] = b * y
    cu_ref[...] = cu


def _conv(hp, past8, conv_w8, tm, tiles_per_seq):
    n = hp.shape[0]
    cvb = C_CV // (3 * CONV_DIM)
    rb = tm // SUBLANES
    return pl.pallas_call(
        functools.partial(_conv_body, tiles_per_seq=tiles_per_seq, tm=tm),
        grid=(n // tm,),
        in_specs=[pl.BlockSpec((tm, 3 * CONV_DIM), lambda i: (i, cvb)),
                  pl.BlockSpec((SUBLANES, 3 * CONV_DIM), lambda i: (jnp.maximum(i * rb - 1, 0), cvb)),
                  pl.BlockSpec((1, SUBLANES, CONV_DIM), lambda i: (i // tiles_per_seq, 0, 0)),
                  pl.BlockSpec((SUBLANES, CONV_DIM), lambda i: (0, 0))],
        out_specs=[pl.BlockSpec((tm, CONV_DIM), lambda i: (i, 0)),
                   pl.BlockSpec((tm, CONV_DIM), lambda i: (i, 0))],
        out_shape=[jax.ShapeDtypeStruct((n, CONV_DIM), F32), jax.ShapeDtypeStruct((n, CONV_DIM), F32)],
        scratch_shapes=[pltpu.VMEM((tm + SUBLANES, CONV_DIM), F32)],
        compiler_params=_cparams("arbitrary"),
    )(hp, hp, past8, conv_w8)


def _page_specs(width, col_block, n_lead, kg_n=None):
    def spec(k):
        def imap(*a):
            ids, pt = a[:n_lead], a[-1]
            cb = col_block(*ids) if callable(col_block) else col_block
            kg = ids[-1] % kg_n if kg_n else ids[-1]
            return (pt[ids[0], kg * PAGES_PER_STEP + k], 0, cb)
        return pl.BlockSpec((1, PAGE, width), imap)
    return [spec(k) for k in range(PAGES_PER_STEP)]


def _cmp1_body(pt_ref, *refs):
    pages = refs[:PAGES_PER_STEP]
    w_ref, pe_ref = refs[PAGES_PER_STEP:PAGES_PER_STEP + 2]
    o_ref = refs[PAGES_PER_STEP + 2]
    acc_a = jnp.zeros((PAGE, HEAD_DIM), F32)
    acc_b = jnp.zeros((PAGE, HEAD_DIM), F32)
    for p in range(CMP_STRIDE):
        xp = jnp.concatenate([pg[0, pl.ds(p, SUBLANES, stride=CMP_STRIDE), :] for pg in pages], axis=0)
        xa = (xp + pe_ref[0, p:p + 1, :]).astype(BF16)
        xb = (xp + pe_ref[0, CMP_STRIDE + p:CMP_STRIDE + p + 1, :]).astype(BF16)
        acc_a = acc_a + jnp.dot(xa, w_ref[0, p, :, 0:HEAD_DIM], preferred_element_type=F32)
        acc_b = acc_b + jnp.dot(xb, w_ref[0, p, :, HEAD_DIM:2 * HEAD_DIM], preferred_element_type=F32)
    o_ref[0, :, 0:HEAD_DIM] = acc_a
    o_ref[0, :, HEAD_DIM:2 * HEAD_DIM] = acc_b


def _cmp1(pool, pt, wcat, pe, col_block):
    nb, npg = pt.shape
    kg = npg // PAGES_PER_STEP
    return pl.pallas_call(
        _cmp1_body,
        grid_spec=pltpu.PrefetchScalarGridSpec(
            num_scalar_prefetch=1, grid=(nb, 4, kg),
            in_specs=_page_specs(HEAD_DIM, lambda b, sg, k: col_block * 4 + sg, 3)
            + [pl.BlockSpec((1, 16, HEAD_DIM, 256), lambda b, sg, k, pt: (sg // 2, 0, 0, 0)),
               pl.BlockSpec((1, CMP_LEN, HEAD_DIM), lambda b, sg, k, pt: (sg // 2, 0, 0))],
            out_specs=pl.BlockSpec((1, PAGE, 256), lambda b, sg, k, pt: (b, k, sg))),
        out_shape=jax.ShapeDtypeStruct((nb, npg * SUBLANES, 1024), F32),
        compiler_params=_cparams("arbitrary", "arbitrary", "arbitrary"),
    )(pt, *([pool] * PAGES_PER_STEP), wcat, pe)


def _gelu_tanh(x):
    return 0.5 * x * (1.0 + jnp.tanh(np.sqrt(2.0 / np.pi).astype(np.float32) * (x + 0.044715 * (x * x * x))))


def _cmp2_body(ab_ref, tail_ref, pe_ref, w1_ref, w2_ref, o_ref, s_ref, *, nc, n_tail):
    row8 = lax.broadcasted_iota(jnp.int32, (SUBLANES, 1), 0)
    row16 = lax.broadcasted_iota(jnp.int32, (CMP_STRIDE, 1), 0)
    for sg in range(4):
        slot = sg // 2
        a = ab_ref[0, :, sg * 256:sg * 256 + HEAD_DIM]
        s_ref[0:nc, :] = ab_ref[0, :, sg * 256 + HEAD_DIM:(sg + 1) * 256]
        tb = jnp.zeros((SUBLANES, HEAD_DIM), F32)
        if n_tail:
            x8 = jnp.where(row8 < n_tail, tail_ref[:, sg * HEAD_DIM:(sg + 1) * HEAD_DIM], 0.0)
            x16 = jnp.concatenate([x8, jnp.zeros((CMP_STRIDE - SUBLANES, HEAD_DIM), F32)], axis=0)
            x16 = x16 + pe_ref[slot, CMP_STRIDE:CMP_LEN, :]
            t16 = jnp.zeros((CMP_STRIDE, HEAD_DIM), F32)
            for p in range(CMP_STRIDE):
                xm = jnp.where(row16 == p, x16, 0.0).astype(BF16)
                t16 = t16 + jnp.dot(xm, w1_ref[slot, CMP_STRIDE + p].astype(BF16), preferred_element_type=F32)
            tb = jnp.sum(t16, axis=0, keepdims=True) * jnp.where(row8 == 0, 1.0, 0.0)
        s_ref[nc:nc + SUBLANES, :] = tb
        pre = a + s_ref[pl.ds(1, nc), :]
        o_ref[0, :, sg * HEAD_DIM:(sg + 1) * HEAD_DIM] = jnp.dot(
            _gelu_tanh(pre).astype(BF16), w2_ref[slot].astype(BF16), preferred_element_type=F32)


def _cmp2(ab, tail, pe, w1r, w2, n_tail, tail_col_block):
    nb, nc, _ = ab.shape
    if tail is None:
        tail = jnp.zeros((nb * SUBLANES, 512), F32)
        tail_col_block = 0
    return pl.pallas_call(
        functools.partial(_cmp2_body, nc=nc, n_tail=n_tail),
        grid=(nb,),
        in_specs=[pl.BlockSpec((1, nc, 1024), lambda b: (b, 0, 0)),
                  pl.BlockSpec((SUBLANES, 512), lambda b: (b, tail_col_block)),
                  pl.BlockSpec((2, CMP_LEN, HEAD_DIM), lambda b: (0, 0, 0)),
                  pl.BlockSpec((2, CMP_LEN, HEAD_DIM, HEAD_DIM), lambda b: (0, 0, 0, 0)),
                  pl.BlockSpec((2, HEAD_DIM, HEAD_DIM), lambda b: (0, 0, 0))],
        out_specs=pl.BlockSpec((1, nc, 512), lambda b: (b, 0, 0)),
        out_shape=jax.ShapeDtypeStruct((nb, nc, 512), F32),
        scratch_shapes=[pltpu.VMEM((nc + SUBLANES, HEAD_DIM), F32)],
        compiler_params=_cparams("arbitrary"),
    )(ab, tail, pe, w1r, w2)


def _nsa1_body(*refs, tq, q0, n_cmp, ncp, n_sel, nselp, lpad, nwb, has_tail, win_k0):
    qraw_ref, qrot_ref, cmp_ref, ov_ref = refs[:4]
    wins = refs[4:4 + nwb]
    pos = 4 + nwb
    tail_ref = None
    if has_tail:
        tail_ref = refs[pos]
        pos += 1
    ocmp_ref, owin_ref, kmask_ref = refs[pos:pos + 3]
    qt = pl.program_id(1)
    r_per_g = NSA_HEADS // NSA_KV
    qpos = q0 + qt * tq + lax.broadcasted_iota(jnp.int32, (tq, 1), 0)

    jj = lax.broadcasted_iota(jnp.int32, (1, ncp), 1)
    cmask = (jj * CMP_STRIDE + (CMP_LEN - 1) <= qpos) & (jj < n_cmp)
    blk = lax.broadcasted_iota(jnp.int32, (1, nselp), 1)
    cur = qpos // SEL_BLOCK
    forced = (blk == 0) | (blk == cur) | (blk == cur - 1)
    e_row = lax.broadcasted_iota(jnp.int32, (KSTEP // SEL_BLOCK, KSTEP), 0)
    e_col = lax.broadcasted_iota(jnp.int32, (KSTEP // SEL_BLOCK, KSTEP), 1)
    expand = jnp.where(e_col // SEL_BLOCK == e_row, 1.0, 0.0).astype(BF16)
    lane128 = lax.broadcasted_iota(jnp.int32, (1, LANES), 1)
    for g in range(NSA_KV):
        kc = cmp_ref[0, :, g * HEAD_DIM:(g + 1) * HEAD_DIM].astype(BF16)
        vc = cmp_ref[0, :, (2 + g) * HEAD_DIM:(3 + g) * HEAD_DIM].astype(BF16)
        psum = jnp.zeros((tq, ncp), F32)
        for r in range(r_per_g):
            h = g * r_per_g + r
            q = qraw_ref[:, h * HEAD_DIM:(h + 1) * HEAD_DIM].astype(BF16)
            s = jnp.where(cmask, _dot_nt(q, kc) * ATT_SCALE, NEG_INF)
            p, den = _softmax_parts(s)
            p = p / den
            psum = psum + p
            ocmp_ref[:, h * HEAD_DIM:(h + 1) * HEAD_DIM] = jnp.dot(p.astype(BF16), vc, preferred_element_type=F32)
        imp = jnp.dot(psum.astype(BF16), ov_ref[...].astype(BF16), preferred_element_type=F32)
        imp = jnp.where(forced, imp + FORCE_SCORE, imp)
        imp = jnp.where((blk <= cur) & (blk < n_sel), imp, NEG_INF)
        rank = jnp.zeros((tq, nselp), F32)
        for i in range(n_sel):
            vi = imp[:, i:i + 1]
            beats = (vi > imp) | ((vi == imp) & (blk > i))
            rank = rank + jnp.where(beats, 1.0, 0.0)
        sel = jnp.where((rank < min(SEL_TOP, n_sel)) & (blk < n_sel), 1.0, 0.0).astype(BF16)
        per = KSTEP // SEL_BLOCK
        for c in range(lpad // KSTEP):
            km = jnp.dot(sel[:, c * per:(c + 1) * per], expand, preferred_element_type=F32)
            kmask_ref[0, g, :, c * KSTEP:(c + 1) * KSTEP] = km.astype(BF16)
        if lpad % KSTEP:
            b0 = (lpad // KSTEP) * per
            km = jnp.where(lane128 < SEL_BLOCK, sel[:, b0:b0 + 1].astype(F32), 0.0)
            kmask_ref[0, g, :, (lpad // KSTEP) * KSTEP:lpad] = jnp.broadcast_to(km, (tq, LANES)).astype(BF16)

    nk = nwb * PAGE + (LANES if has_tail else 0)
    kk = lax.broadcasted_iota(jnp.int32, (1, nk), 1)
    if has_tail:
        kpos = jnp.where(kk < nwb * PAGE, win_k0 + kk, q0 + kk - nwb * PAGE)
    else:
        kpos = (qt - (nwb - 1)) * PAGE + kk
    rel = qpos - kpos
    wmask = (rel >= 0) & (rel < WINDOW) & (kpos >= 0)
    for g in range(NSA_KV):
        kparts = [w[:, g * HEAD_DIM:(g + 1) * HEAD_DIM] for w in wins]
        vparts = [w[:, (2 + g) * HEAD_DIM:(3 + g) * HEAD_DIM] for w in wins]
        if has_tail:
            zpad = jnp.zeros((LANES - SUBLANES, HEAD_DIM), F32)
            kparts += [tail_ref[:, g * HEAD_DIM:(g + 1) * HEAD_DIM], zpad]
            vparts += [tail_ref[:, (2 + g) * HEAD_DIM:(3 + g) * HEAD_DIM], zpad]
        kw = jnp.concatenate(kparts, axis=0).astype(BF16)
        vw = jnp.concatenate(vparts, axis=0).astype(BF16)
        for r in range(r_per_g):
            h = g * r_per_g + r
            q = qrot_ref[:, h * HEAD_DIM:(h + 1) * HEAD_DIM].astype(BF16)
            s = jnp.where(wmask, _dot_nt(q, kw) * ATT_SCALE, NEG_INF)
            p, den = _softmax_parts(s)
            p = p / den
            owin_ref[:, h * HEAD_DIM:(h + 1) * HEAD_DIM] = jnp.dot(p.astype(BF16), vw, preferred_element_type=F32)


def _nsa1(hp, cmp, ov, win_src, tail_src, *, nb, tq, qt_n, q0, n_cmp, n_sel, lpad, nwb, win_k0):
    n = hp.shape[0]
    ncp = cmp.shape[1]
    nselp = ov.shape[1]
    has_tail = tail_src is not None
    rows = lambda b, t: b * qt_n + t
    in_specs = [pl.BlockSpec((tq, 1024), lambda b, t: (rows(b, t), C_QRAW // 1024)),
                pl.BlockSpec((tq, 1024), lambda b, t: (rows(b, t), C_QROT // 1024)),
                pl.BlockSpec((1, ncp, 512), lambda b, t: (b, 0, 0)),
                pl.BlockSpec((ncp, nselp), lambda b, t: (0, 0))]
    if has_tail:
        in_specs += [pl.BlockSpec((PAGE, 512), lambda b, t, k=k: (b * nwb + k, 0)) for k in range(nwb)]
        in_specs += [pl.BlockSpec((SUBLANES, 512), lambda b, t: (b, C_WIN // 512))]
        args = [win_src] * nwb + [tail_src]
    else:
        in_specs += [pl.BlockSpec((PAGE, 512),
                                  lambda b, t, k=k: (b * qt_n + jnp.maximum(t - (nwb - 1) + k, 0), C_WIN // 512))
                     for k in range(nwb)]
        args = [win_src] * nwb
    return pl.pallas_call(
        functools.partial(_nsa1_body, tq=tq, q0=q0, n_cmp=n_cmp, ncp=ncp, n_sel=n_sel, nselp=nselp, lpad=lpad,
                          nwb=nwb, has_tail=has_tail, win_k0=win_k0),
        grid=(nb, qt_n),
        in_specs=in_specs,
        out_specs=[pl.BlockSpec((tq, 1024), lambda b, t: (rows(b, t), 0)),
                   pl.BlockSpec((tq, 1024), lambda b, t: (rows(b, t), 0)),
                   pl.BlockSpec((1, NSA_KV, tq, lpad), lambda b, t: (b, 0, t, 0))],
        out_shape=[jax.ShapeDtypeStruct((n, 1024), F32), jax.ShapeDtypeStruct((n, 1024), F32),
                   jax.ShapeDtypeStruct((nb, NSA_KV, qt_n * tq, lpad), BF16)],
        compiler_params=_cparams("arbitrary", "arbitrary"),
    )(hp, hp, cmp, ov, *args)


def _mattn_body(pt_ref, *refs, n_g, r_per_g, gm, tq, q0, kg_n, has_tail):
    q_ref, mask_ref = refs[:2]
    pages = refs[2:2 + PAGES_PER_STEP]
    pos = 2 + PAGES_PER_STEP
    if has_tail:
        tail_ref, tmask_ref = refs[pos:pos + 2]
        pos += 2
    o_ref, m_ref, l_ref, acc_ref = refs[pos:pos + 4]
    qt = pl.program_id(1)
    step = pl.program_id(2)
    two_pass = kg_n > 1 or has_tail
    kg = step % kg_n if two_pass else step
    qpos = q0 + qt * tq + lax.broadcasted_iota(jnp.int32, (tq, 1), 0)

    def scores(h, g, kt, kpos, mref):
        valid = (mref[0, g if gm > 1 else 0] > 0.5) & (kpos <= qpos)
        q = q_ref[:, h * HEAD_DIM:(h + 1) * HEAD_DIM].astype(BF16)
        return jnp.where(valid, _dot_nt(q, kt) * ATT_SCALE, NEG_INF)

    def main_keys(g):
        return jnp.concatenate([pg[0, :, g * HEAD_DIM:(g + 1) * HEAD_DIM] for pg in pages], axis=0).astype(BF16)

    def main_vals(g):
        return jnp.concatenate([pg[0, :, (n_g + g) * HEAD_DIM:(n_g + g + 1) * HEAD_DIM] for pg in pages],
                               axis=0).astype(BF16)

    def tail_part(g, off):
        zpad = jnp.zeros((LANES - SUBLANES, HEAD_DIM), F32)
        return jnp.concatenate([tail_ref[:, (off + g) * HEAD_DIM:(off + g + 1) * HEAD_DIM], zpad], axis=0).astype(BF16)

    kpos = kg * KSTEP + lax.broadcasted_iota(jnp.int32, (1, KSTEP), 1)
    tpos = q0 + lax.broadcasted_iota(jnp.int32, (1, LANES), 1)

    if not two_pass:
        for g in range(n_g):
            kt, vt = main_keys(g), main_vals(g)
            for r in range(r_per_g):
                h = g * r_per_g + r
                p, den = _softmax_parts(scores(h, g, kt, kpos, mask_ref))
                o_ref[:, h * HEAD_DIM:(h + 1) * HEAD_DIM] = jnp.dot((p / den).astype(BF16), vt,
                                                                   preferred_element_type=F32)
        return

    @pl.when(step == 0)
    def _():
        m_ref[...] = jnp.full(m_ref.shape, NEG_INF, F32)
        l_ref[...] = jnp.zeros(l_ref.shape, F32)
        acc_ref[...] = jnp.zeros(acc_ref.shape, F32)

    def stats(h, s):
        m_old = m_ref[h]
        m_new = jnp.maximum(m_old, jnp.max(s, axis=-1, keepdims=True))
        m_safe = jnp.where(m_new == NEG_INF, 0.0, m_new)
        l_ref[h] = jnp.exp(m_old - m_safe) * l_ref[h] + jnp.sum(jnp.exp(s - m_safe), axis=-1, keepdims=True)
        m_ref[h] = m_new

    def accumulate(h, s, vt):
        m = m_ref[h]
        m_safe = jnp.where(m == NEG_INF, 0.0, m)
        p = jnp.exp(s - m_safe) / jnp.maximum(l_ref[h], 1e-30)
        sl = slice(h * HEAD_DIM, (h + 1) * HEAD_DIM)
        acc_ref[:, sl] += jnp.dot(p.astype(BF16), vt, preferred_element_type=F32)

    @pl.when(step < kg_n)
    def _():
        for g in range(n_g):
            kt = main_keys(g)
            for r in range(r_per_g):
                h = g * r_per_g + r
                stats(h, scores(h, g, kt, kpos, mask_ref))

    if has_tail:
        @pl.when(step == kg_n - 1)
        def _():
            for g in range(n_g):
                kt = tail_part(g, 0)
                for r in range(r_per_g):
                    h = g * r_per_g + r
                    stats(h, scores(h, g, kt, tpos, tmask_ref))

    @pl.when(step >= kg_n)
    def _():
        for g in range(n_g):
            kt, vt = main_keys(g), main_vals(g)
            for r in range(r_per_g):
                h = g * r_per_g + r
                accumulate(h, scores(h, g, kt, kpos, mask_ref), vt)

    @pl.when(step == 2 * kg_n - 1)
    def _():
        if has_tail:
            for g in range(n_g):
                kt, vt = tail_part(g, 0), tail_part(g, n_g)
                for r in range(r_per_g):
                    h = g * r_per_g + r
                    accumulate(h, scores(h, g, kt, tpos, tmask_ref), vt)
        o_ref[...] = acc_ref[...]


def _mattn(hp, q_col, n_heads, mask, pool, pt, pool_col_block, tail_src, tail_col_block, *, nb, tq, qt_n, q0):
    n = hp.shape[0]
    n_g = 2
    r_per_g = n_heads // n_g
    qw = n_heads * HEAD_DIM
    gm = mask.shape[1]
    kg_n = pt.shape[1] // PAGES_PER_STEP
    has_tail = tail_src is not None
    n_steps = 2 * kg_n if (kg_n > 1 or has_tail) else 1
    in_specs = [pl.BlockSpec((tq, qw), lambda b, t, k, pt: (b * qt_n + t, q_col // qw)),
                pl.BlockSpec((1, gm, tq, KSTEP), lambda b, t, k, pt: (b, 0, t, k % kg_n))]
    in_specs += _page_specs(512, pool_col_block, 3, kg_n)
    args = [hp, mask] + [pool] * PAGES_PER_STEP
    if has_tail:
        in_specs += [pl.BlockSpec((SUBLANES, 512), lambda b, t, k, pt: (b, tail_col_block)),
                     pl.BlockSpec((1, gm, tq, LANES), lambda b, t, k, pt: (b, 0, t, kg_n * KSTEP // LANES))]
        args += [tail_src, mask]
    return pl.pallas_call(
        functools.partial(_mattn_body, n_g=n_g, r_per_g=r_per_g, gm=gm, tq=tq, q0=q0, kg_n=kg_n, has_tail=has_tail),
        grid_spec=pltpu.PrefetchScalarGridSpec(
            num_scalar_prefetch=1, grid=(nb, qt_n, n_steps),
            in_specs=in_specs,
            out_specs=pl.BlockSpec((tq, qw), lambda b, t, k, pt: (b * qt_n + t, 0)),
            scratch_shapes=[pltpu.VMEM((n_heads, tq, 1), F32), pltpu.VMEM((n_heads, tq, 1), F32),
                            pltpu.VMEM((tq, qw), F32)]),
        out_shape=jax.ShapeDtypeStruct((n, qw), F32),
        compiler_params=_cparams("arbitrary", "arbitrary", "arbitrary"),
    )(pt, *args)


def _dsa1_body(pt_ref, *refs, tq, q0, kg_n, lpad, kw, has_tail, n_keep):
    iq_ref, misc_ref = refs[:2]
    pages = refs[2:2 + PAGES_PER_STEP]
    pos = 2 + PAGES_PER_STEP
    if has_tail:
        tail_ref = refs[pos]
        pos += 1
    mask_ref, sc_ref = refs[pos:pos + 2]
    qt = pl.program_id(1)
    kg = pl.program_id(2)
    qpos = q0 + qt * tq + lax.broadcasted_iota(jnp.int32, (tq, 1), 0)
    lane128 = lax.broadcasted_iota(jnp.int32, (1, LANES), 1)

    def scores(kt, kpos):
        acc = jnp.zeros((tq, kt.shape[0]), F32)
        for h in range(IDX_HEADS):
            qi = iq_ref[:, h * IDX_DIM:(h + 1) * IDX_DIM].astype(BF16)
            w = jnp.sum(jnp.where(lane128 == MISC_IW + h, misc_ref[...], 0.0), axis=-1, keepdims=True)
            acc = acc + w * jnp.maximum(_dot_nt(qi, kt), 0.0)
        return jnp.where(kpos <= qpos, acc, NEG_INF)

    kt = jnp.concatenate([pg[0, :, 0:IDX_DIM] for pg in pages], axis=0).astype(BF16)
    kpos = kg * KSTEP + lax.broadcasted_iota(jnp.int32, (1, KSTEP), 1)
    sc_ref[kg] = scores(kt, kpos)

    @pl.when(kg == kg_n - 1)
    def _():
        nch = sc_ref.shape[0]
        if has_tail:
            zpad = jnp.zeros((KSTEP - SUBLANES, IDX_DIM), F32)
            tk = jnp.concatenate([tail_ref[:, 0:IDX_DIM], zpad], axis=0).astype(BF16)
            tl = lax.broadcasted_iota(jnp.int32, (1, KSTEP), 1)
            sc_ref[kg_n] = jnp.where(tl < LANES, scores(tk, q0 + tl), NEG_INF)
        sc = sc_ref[...] + 0.0
        bits = pltpu.bitcast(sc, jnp.int32)
        key = jnp.where(bits < 0, bits ^ jnp.int32(0x7FFFFFFF), bits)
        int_min = jnp.int32(-2 ** 31)

        def count(pred):
            return jnp.sum(jnp.sum(jnp.where(pred, 1.0, 0.0), axis=-1, keepdims=True), axis=0, keepdims=True)

        def vbit(i, thr):
            cand = thr + jnp.left_shift(jnp.int32(1), 31 - i)
            return jnp.where(count(key >= cand) >= n_keep, cand, thr)

        thr = lax.fori_loop(0, 32, vbit, jnp.full((1, tq, 1), int_min, jnp.int32))
        gt = key > thr
        tie = key == thr
        need = n_keep - count(gt)
        idx = (lax.broadcasted_iota(jnp.int32, (nch, 1, KSTEP), 0) * KSTEP
               + lax.broadcasted_iota(jnp.int32, (nch, 1, KSTEP), 2))
        nbits = int(np.ceil(np.log2(nch * KSTEP)))

        def ibit(i, c):
            cand = c + jnp.left_shift(jnp.int32(1), nbits - 1 - i)
            return jnp.where(count(tie & (idx < cand)) < need, cand, c)

        cut = lax.fori_loop(0, nbits, ibit, jnp.zeros((1, tq, 1), jnp.int32))
        sel = jnp.where(gt | (tie & (idx <= cut)), 1.0, 0.0).astype(BF16)
        for c in range(kg_n):
            mask_ref[0, 0, :, c * KSTEP:(c + 1) * KSTEP] = sel[c]
        if has_tail:
            mask_ref[0, 0, :, kg_n * KSTEP:lpad] = sel[kg_n][:, 0:lpad - kg_n * KSTEP]


def _dsa1(hp, pool, pt, kw, pool_col_block, tail_src, *, nb, tq, qt_n, q0, lpad, n_keep):
    kg_n = pt.shape[1] // PAGES_PER_STEP
    has_tail = tail_src is not None
    in_specs = [pl.BlockSpec((tq, 256), lambda b, t, k, pt: (b * qt_n + t, C_IQ // 256)),
                pl.BlockSpec((tq, LANES), lambda b, t, k, pt: (b * qt_n + t, C_MISC // LANES))]
    in_specs += _page_specs(kw, pool_col_block, 3)
    args = [hp, hp] + [pool] * PAGES_PER_STEP
    if has_tail:
        in_specs += [pl.BlockSpec((SUBLANES, LANES), lambda b, t, k, pt: (b, C_MISC // LANES))]
        args += [tail_src]
    return pl.pallas_call(
        functools.partial(_dsa1_body, tq=tq, q0=q0, kg_n=kg_n, lpad=lpad, kw=kw, has_tail=has_tail, n_keep=n_keep),
        grid_spec=pltpu.PrefetchScalarGridSpec(
            num_scalar_prefetch=1, grid=(nb, qt_n, kg_n),
            in_specs=in_specs,
            out_specs=pl.BlockSpec((1, 1, tq, lpad), lambda b, t, k, pt: (b, 0, t, 0)),
            scratch_shapes=[pltpu.VMEM((kg_n + (1 if has_tail else 0), tq, KSTEP), F32)]),
        out_shape=jax.ShapeDtypeStruct((nb, 1, qt_n * tq, lpad), BF16),
        compiler_params=_cparams("arbitrary", "arbitrary", "arbitrary"),
    )(pt, *args)


def _mix_body(ocmp_ref, osel_ref, owin_ref, odsa_ref, conv_ref, misc_ref, mg0_ref, mg1_ref, mg2_ref,
              wa_ref, wb_ref, wc_ref, z_ref):
    lane128 = lax.broadcasted_iota(jnp.int32, (1, LANES), 1)
    misc = misc_ref[...]

    def gate(kind, h):
        return jnp.sum(jnp.where(lane128 == MISC_NG + kind * NSA_HEADS + h, misc, 0.0), axis=-1, keepdims=True)

    parts = []
    for h in range(NSA_HEADS):
        sl = slice(h * HEAD_DIM, (h + 1) * HEAD_DIM)
        parts.append((gate(0, h) * ocmp_ref[:, sl] + gate(1, h) * osel_ref[:, sl]
                      + gate(2, h) * owin_ref[:, sl]).astype(BF16))
    o_nsa = jnp.concatenate(parts, axis=1)
    p_a = jnp.dot(o_nsa, wa_ref[...], preferred_element_type=F32)
    p_b = jnp.dot(odsa_ref[...].astype(BF16), wb_ref[...], preferred_element_type=F32)
    p_c = jnp.dot(conv_ref[...].astype(BF16), wc_ref[...], preferred_element_type=F32)
    z_ref[...] = (mg0_ref[...] * p_a + mg1_ref[...] * p_b + mg2_ref[...] * p_c).astype(BF16)


def _mix(hp, o_cmp, o_sel, o_win, o_dsa, conv_out, wa, wb, wc, tm):
    n = hp.shape[0]
    row = lambda w, cb=0: pl.BlockSpec((tm, w), lambda i: (i, cb))
    full = lambda a: pl.BlockSpec(a.shape, lambda i: (0, 0))
    return pl.pallas_call(
        _mix_body,
        grid=(n // tm,),
        in_specs=[row(1024), row(1024), row(1024), row(512), row(512), row(LANES, C_MISC // LANES),
                  row(D_MODEL, C_MG // D_MODEL), row(D_MODEL, C_MG // D_MODEL + 1), row(D_MODEL, C_MG // D_MODEL + 2),
                  full(wa), full(wb), full(wc)],
        out_specs=row(D_MODEL),
        out_shape=jax.ShapeDtypeStruct((n, D_MODEL), BF16),
        compiler_params=_cparams("arbitrary"),
    )(o_cmp, o_sel, o_win, o_dsa, conv_out, hp, hp, hp, hp, wa, wb, wc)


def _outln_body(z_ref, wo_ref, x_ref, g_ref, b_ref, wr_ref, rb_ref, x1_ref, gate_ref, *, tm):
    y = jnp.dot(z_ref[...], wo_ref[...], preferred_element_type=F32)
    x1 = _layernorm(ALPHA * x_ref[...] + y, g_ref[...], b_ref[...])
    x1_ref[...] = x1
    aff = _sigmoid(_dot_nt(wr_ref[...].astype(BF16), x1.astype(BF16)))
    biased = aff + rb_ref[:, 0:1]
    rows = [biased[e:e + 1, :] for e in range(N_EXPERTS)]
    best = None
    g_best = jnp.zeros((1, tm), jnp.int32)
    for g in range(N_GROUPS):
        v = rows[g * EXPERTS_PER_GROUP:(g + 1) * EXPERTS_PER_GROUP]
        score = None
        for a in range(EXPERTS_PER_GROUP):
            for c in range(a + 1, EXPERTS_PER_GROUP):
                pair = v[a] + v[c]
                score = pair if score is None else jnp.maximum(score, pair)
        if best is None:
            best = score
        else:
            better = score > best
            best = jnp.where(better, score, best)
            g_best = jnp.where(better, g, g_best)
    sel_rows = []
    for e in range(N_EXPERTS):
        g = e // EXPERTS_PER_GROUP
        rank = jnp.zeros((1, tm), F32)
        for o in range(g * EXPERTS_PER_GROUP, (g + 1) * EXPERTS_PER_GROUP):
            if o == e:
                continue
            beats = (rows[o] > rows[e]) | ((rows[o] == rows[e]) & (o < e))
            rank = rank + jnp.where(beats, 1.0, 0.0)
        sel_rows.append(jnp.where((g_best == g) & (rank < 2), aff[e:e + 1, :], 0.0))
    tot = sel_rows[0]
    for e in range(1, N_EXPERTS):
        tot = tot + sel_rows[e]
    gate_t = jnp.concatenate(sel_rows + [jnp.zeros((LANES - N_EXPERTS, tm), F32)], axis=0) / tot
    gate_ref[...] = gate_t.T


def _outln(z, wo, x, g, b, wr_t, rb, tm):
    n = z.shape[0]
    row = lambda w: pl.BlockSpec((tm, w), lambda i: (i, 0))
    full = lambda a: pl.BlockSpec(a.shape, lambda i: (0, 0))
    return pl.pallas_call(
        functools.partial(_outln_body, tm=tm),
        grid=(n // tm,),
        in_specs=[row(D_MODEL), full(wo), row(D_MODEL), full(g), full(b), full(wr_t), full(rb)],
        out_specs=[row(D_MODEL), row(LANES)],
        out_shape=[jax.ShapeDtypeStruct((n, D_MODEL), F32), jax.ShapeDtypeStruct((n, LANES), F32)],
        compiler_params=_cparams("arbitrary"),
    )(z, wo, x, g, b, wr_t, rb)


def _moe_body(x_ref, gate_ref, wg_ref, wu_ref, wd_ref, g_ref, b_ref, o_ref, xb_ref, acc_ref):
    e = pl.program_id(1)

    @pl.when(e == 0)
    def _():
        xb_ref[...] = x_ref[...].astype(BF16)
        acc_ref[...] = jnp.zeros(acc_ref.shape, F32)

    lane128 = lax.broadcasted_iota(jnp.int32, (1, LANES), 1)
    gcol = jnp.sum(jnp.where(lane128 == e, gate_ref[...], 0.0), axis=-1, keepdims=True)
    xb = xb_ref[...]
    hg = jnp.dot(xb, wg_ref[0], preferred_element_type=F32)
    hu = jnp.dot(xb, wu_ref[0], preferred_element_type=F32)
    h = (hg * _sigmoid(hg)) * hu * gcol
    acc_ref[...] += jnp.dot(h.astype(BF16), wd_ref[0], preferred_element_type=F32)

    @pl.when(e == N_EXPERTS - 1)
    def _():
        o_ref[...] = _layernorm(ALPHA * x_ref[...] + acc_ref[...], g_ref[...], b_ref[...])


def _moe(x1, gate, wg, wu, wd, g, b, tm):
    n = x1.shape[0]
    return pl.pallas_call(
        _moe_body,
        grid=(n // tm, N_EXPERTS),
        in_specs=[pl.BlockSpec((tm, D_MODEL), lambda i, e: (i, 0)),
                  pl.BlockSpec((tm, LANES), lambda i, e: (i, 0)),
                  pl.BlockSpec((1, D_MODEL, D_FF), lambda i, e: (e, 0, 0)),
                  pl.BlockSpec((1, D_MODEL, D_FF), lambda i, e: (e, 0, 0)),
                  pl.BlockSpec((1, D_FF, D_MODEL), lambda i, e: (e, 0, 0)),
                  pl.BlockSpec((1, D_MODEL), lambda i, e: (0, 0)),
                  pl.BlockSpec((1, D_MODEL), lambda i, e: (0, 0))],
        out_specs=pl.BlockSpec((tm, D_MODEL), lambda i, e: (i, 0)),
        out_shape=jax.ShapeDtypeStruct((n, D_MODEL), F32),
        scratch_shapes=[pltpu.VMEM((tm, D_MODEL), BF16), pltpu.VMEM((tm, D_MODEL), F32)],
        compiler_params=_cparams("arbitrary", "arbitrary"),
    )(x1, gate, wg, wu, wd, g, b)


def _overlap_matrix(ncp, nselp):
    cs = np.arange(ncp)[:, None] * CMP_STRIDE
    ss = np.arange(nselp)[None, :] * SEL_BLOCK
    return jnp.asarray(((cs < ss + SEL_BLOCK) & (cs + CMP_LEN > ss)).astype(np.float32))


class _Group:
    def __init__(self, nb, t_real, t_pad, q0, past_len, tm, tq):
        self.nb, self.t_real, self.t_pad, self.q0, self.past_len, self.tm, self.tq = nb, t_real, t_pad, q0, past_len, tm, tq
        self.paged = past_len > 0
        self.lp = past_len if self.paged else t_pad
        self.ltot = self.lp + (t_real if self.paged else 0)
        self.lpad = self.lp + (LANES if self.paged else 0)
        self.qt_n = t_pad // tq
        n_chunks = -(-self.ltot // CMP_STRIDE)
        self.n_cmp = n_chunks - CMP_LEN // CMP_STRIDE + 1
        self.n_sel = -(-self.ltot // SEL_BLOCK)
        self.nselp = -(-self.n_sel // LANES) * LANES
        self.n_keep = min(DSA_TOPK, self.ltot // 4)


def _mixer(gp, x, lw, caches):
    hp = _proj(x.astype(BF16), lw["w_in"], gp.tabs, lw["kinds"], gp.tm)
    nb, tq, qt_n, q0 = gp.nb, gp.tq, gp.qt_n, gp.q0
    if gp.paged:
        c_nsa, c_dsa, c_kidx, s_win, s_conv, pt = caches
        n_pool = c_nsa.shape[0]
        nsa_pool = c_nsa.reshape(n_pool, PAGE, 1024)
        dsa_pool = c_dsa.reshape(n_pool, PAGE, 512)
        kidx_pool, kidx_w, kidx_cb = c_kidx, IDX_DIM, 0
        cmp_cb, slc_cb, dsa_cb = 0, 1, 0
        tail = hp
        win_src = s_win.reshape(nb * WINDOW, 512)
        nwb, win_k0 = WINDOW // PAGE, PAST_LEN - WINDOW
        past8 = jnp.concatenate([jnp.zeros((nb, SUBLANES - 2, CONV_DIM), F32), s_conv], axis=1)
    else:
        pt = gp.pt
        nsa_pool = dsa_pool = kidx_pool = hp.reshape(nb * gp.t_pad // PAGE, PAGE, NCOL)
        kidx_w, kidx_cb = LANES, C_MISC // LANES
        cmp_cb, slc_cb, dsa_cb = C_CMP // 512, C_SLC // 512, C_DSA // 512
        tail = None
        win_src = hp
        nwb, win_k0 = WINDOW // PAGE + 1, 0
        past8 = jnp.zeros((nb, SUBLANES, CONV_DIM), F32)

    conv_out, cu = _conv(hp, past8, lw["conv_w8"], min(gp.tm, gp.t_pad), gp.t_pad // min(gp.tm, gp.t_pad))

    ab = _cmp1(nsa_pool, pt, lw["wcat"], lw["pe"], cmp_cb)
    cmp = _cmp2(ab, tail, lw["pe"], lw["w1r"], lw["phi_w2"], gp.t_real if gp.paged else 0, C_CMP // 512)
    o_cmp, o_win, kmask = _nsa1(hp, cmp, gp.ov, win_src, tail, nb=nb, tq=tq, qt_n=qt_n, q0=q0, n_cmp=gp.n_cmp,
                                n_sel=gp.n_sel, lpad=gp.lpad, nwb=nwb, win_k0=win_k0)
    o_sel = _mattn(hp, C_QROT, NSA_HEADS, kmask, nsa_pool, pt, slc_cb, tail, C_SLC // 512,
                   nb=nb, tq=tq, qt_n=qt_n, q0=q0)
    dmask = _dsa1(hp, kidx_pool, pt, kidx_w, kidx_cb, tail, nb=nb, tq=tq, qt_n=qt_n, q0=q0, lpad=gp.lpad,
                  n_keep=gp.n_keep)
    o_dsa = _mattn(hp, C_DQ, DSA_HEADS, dmask, dsa_pool, pt, dsa_cb, tail, C_DSA // 512,
                   nb=nb, tq=tq, qt_n=qt_n, q0=q0)
    z = _mix(hp, o_cmp, o_sel, o_win, o_dsa, conv_out, lw["w_a"], lw["w_b"], lw["w_c"], min(gp.tm, 256))
    return z, hp, cu


def _layer(gp, x, lw, caches):
    z, hp, cu = _mixer(gp, x, lw, caches)
    tm2 = min(gp.tm, 256)
    x1, gate = _outln(z, lw["w_o"], x, lw["ln_mix_g"], lw["ln_mix_b"], lw["wr_t"], lw["rb"], tm2)
    x2 = _moe(x1, gate, lw["w_eg"], lw["w_eu"], lw["w_ed"], lw["ln_ffn_g"], lw["ln_ffn_b"], min(gp.tm, 512))
    return x2, hp, cu


def _layer_weights(l, w_in, nsa_phi_pos, nsa_phi_w1, nsa_phi_w2, conv_w, w_br_a, w_br_b, w_br_c, w_out,
                   ln_mix_g, ln_mix_b, ln_ffn_g, ln_ffn_b, w_router, router_bias, w_e_gate, w_e_up, w_e_down):
    w1r = nsa_phi_w1[l].reshape(2, CMP_LEN, HEAD_DIM, HEAD_DIM)
    wcat = jnp.concatenate([w1r[:, :CMP_STRIDE], w1r[:, CMP_STRIDE:]], axis=-1).astype(BF16)
    return dict(
        w_in=_permute_w_in(w_in[l]), kinds=jnp.asarray(_col_kinds()),
        w1r=w1r, wcat=wcat, pe=nsa_phi_pos[l], phi_w2=nsa_phi_w2[l],
        conv_w8=jnp.concatenate([conv_w[l], jnp.zeros((SUBLANES - 3, CONV_DIM), F32)], axis=0),
        w_a=w_br_a[l].astype(BF16), w_b=w_br_b[l].astype(BF16), w_c=w_br_c[l].astype(BF16),
        w_o=w_out[l].astype(BF16),
        ln_mix_g=ln_mix_g[l][None], ln_mix_b=ln_mix_b[l][None],
        ln_ffn_g=ln_ffn_g[l][None], ln_ffn_b=ln_ffn_b[l][None],
        wr_t=w_router.T, rb=jnp.broadcast_to(router_bias[:, None], (N_EXPERTS, LANES)),
        w_eg=w_e_gate[l].astype(BF16), w_eu=w_e_up[l].astype(BF16), w_ed=w_e_down[l].astype(BF16))


def kernel(x_prompt, x_sample, cache_nsa_kv, cache_dsa_kv, cache_dsa_kidx, state_nsa_win, state_conv, page_table,
           w_in, nsa_phi_pos, nsa_phi_w1, nsa_phi_w2, conv_w, w_br_a, w_br_b, w_br_c, w_out, ln_mix_g, ln_mix_b,
           ln_ffn_g, ln_ffn_b, w_router, router_bias, w_e_gate, w_e_up, w_e_down):
    bp, tp, _ = x_prompt.shape
    bs, ts, _ = x_sample.shape
    ts_pad = SUBLANES

    gp_p = _Group(bp, tp, tp, 0, 0, tm=512, tq=128)
    gp_p.tabs = _rope_tables(jnp.arange(tp, dtype=jnp.int32))
    gp_p.pt = jnp.arange(bp * tp // PAGE, dtype=jnp.int32).reshape(bp, tp // PAGE)
    gp_p.ov = _overlap_matrix(LANES, gp_p.nselp)
    gp_s = _Group(bs, ts, ts_pad, PAST_LEN, PAST_LEN, tm=bs * ts_pad, tq=ts_pad)
    pos_s = PAST_LEN + jnp.arange(ts_pad, dtype=jnp.int32)
    gp_s.tabs = jnp.tile(_rope_tables(pos_s), (1, bs, 1))
    gp_s.ov = _overlap_matrix(PAST_LEN // CMP_STRIDE, gp_s.nselp)

    xp = x_prompt.reshape(bp * tp, D_MODEL)
    xs = jnp.concatenate([x_sample, jnp.zeros((bs, ts_pad - ts, D_MODEL), F32)], axis=1).reshape(bs * ts_pad, D_MODEL)

    outs_p = [[] for _ in range(5)]
    outs_s = [[] for _ in range(5)]
    for l in range(DEPTH):
        lw = _layer_weights(l, w_in, nsa_phi_pos, nsa_phi_w1, nsa_phi_w2, conv_w, w_br_a, w_br_b, w_br_c, w_out,
                            ln_mix_g, ln_mix_b, ln_ffn_g, ln_ffn_b, w_router, router_bias, w_e_gate, w_e_up, w_e_down)
        xp, hp_p, cu_p = _layer(gp_p, xp, lw, None)
        xs, hp_s, cu_s = _layer(gp_s, xs, lw, (cache_nsa_kv[l], cache_dsa_kv[l], cache_dsa_kidx[l],
                                                 state_nsa_win[l], state_conv[l], page_table))
        h3 = hp_p.reshape(bp, tp, NCOL)
        outs_p[0].append(h3[:, :, C_CMP:C_CMP + 1024].reshape(bp, tp, 4, NSA_KV, HEAD_DIM))
        outs_p[1].append(h3[:, :, C_DSA:C_DSA + 512].reshape(bp, tp, 2, DSA_KV, HEAD_DIM))
        outs_p[2].append(h3[:, :, C_MISC:C_MISC + IDX_DIM])
        outs_p[3].append(h3[:, tp - min(WINDOW, tp):, C_WIN:C_WIN + 512].reshape(bp, min(WINDOW, tp), 2, NSA_KV, HEAD_DIM))
        outs_p[4].append(cu_p.reshape(bp, tp, CONV_DIM)[:, tp - 2:])
        s3 = hp_s.reshape(bs, ts_pad, NCOL)[:, :ts]
        outs_s[0].append(s3[:, :, C_CMP:C_CMP + 1024].reshape(bs, ts, 4, NSA_KV, HEAD_DIM))
        outs_s[1].append(s3[:, :, C_DSA:C_DSA + 512].reshape(bs, ts, 2, DSA_KV, HEAD_DIM))
        outs_s[2].append(s3[:, :, C_MISC:C_MISC + IDX_DIM])
        win_new = s3[:, :, C_WIN:C_WIN + 512].reshape(bs, ts, 2, NSA_KV, HEAD_DIM)
        wb = state_nsa_win.shape[2]
        outs_s[3].append(jnp.concatenate([state_nsa_win[l], win_new], axis=1)[:, -wb:])
        ext = jnp.concatenate([state_conv[l], cu_s.reshape(bs, ts_pad, CONV_DIM)[:, :ts]], axis=1)
        outs_s[4].append(ext[:, -2:])
    sp = [jnp.stack(a, axis=0) for a in outs_p]
    ss = [jnp.stack(a, axis=0) for a in outs_s]
    y_p = xp.reshape(bp, tp, D_MODEL)
    y_s = xs.reshape(bs, ts_pad, D_MODEL)[:, :ts]
    return (y_p, y_s, sp[0], ss[0], sp[1], ss[1], sp[2], ss[2], sp[3], ss[3], sp[4], ss[4])
```

```python
import collections
import functools

import numpy as np
import jax
import jax.numpy as jnp
from jax import lax
from jax.experimental import pallas as pl
from jax.experimental.pallas import tpu as pltpu

F32 = jnp.float32
BF16 = jnp.bfloat16
HIGHEST = lax.Precision.HIGHEST
NEG_INF = float("-inf")

D_MODEL = 2048
DEPTH = 2
PAST_LEN = 16384
PAGE = 128
HEAD_DIM = 128
ROPE_THETA = 500000.0
NSA_HEADS = 8
NSA_KV = 2
CMP_LEN = 32
CMP_STRIDE = 16
SEL_BLOCK = 64
SEL_TOP = 16
WINDOW = 512
FORCE_SCORE = 1e4
DSA_HEADS = 4
DSA_KV = 2
IDX_HEADS = 4
IDX_DIM = 64
DSA_TOPK = 256
CONV_DIM = 512
N_EXPERTS = 16
N_GROUPS = 4
EXPERTS_PER_GROUP = 4
D_FF = 512
LN_EPS = 1e-5
ALPHA = (2 * DEPTH) ** 0.25
IN_WIDTHS = (1024, 1536, 24, 512, 512, 256, 64, 4, 1536, 6144)
ATT_SCALE = HEAD_DIM ** -0.5

LANES = 128
SUBLANES = 8
VMEM_LIMIT = 56 * 1024 * 1024

C_QRAW = 0
C_QROT = 1024
C_CMP = 2048
C_SLC = 2560
C_WIN = 3072
C_DQ = 3584
C_DSA = 4096
C_CV = 4608
C_MG = 6144
C_IQ = 12288
C_MISC = 12544
NCOL = 12800
MISC_IW = 64
MISC_NG = 68
PROJ_TN = 512
PAGES_PER_STEP = 16
KSTEP = PAGES_PER_STEP * PAGE

K_PLAIN, K_ROPE128, K_ROPE64, K_SIGMOID, K_MISC = 0, 1, 2, 3, 4


def _col_kinds():
    kinds = np.zeros(NCOL // LANES, np.int32)

    def mark(c0, n, k):
        kinds[c0 // LANES:(c0 + n) // LANES] = k

    mark(C_QROT, 1024, K_ROPE128)
    mark(C_SLC, 256, K_ROPE128)
    mark(C_WIN, 256, K_ROPE128)
    mark(C_DQ, 512, K_ROPE128)
    mark(C_DSA, 256, K_ROPE128)
    mark(C_IQ, 256, K_ROPE64)
    mark(C_MISC, 128, K_MISC)
    mark(C_MG, 6144, K_SIGMOID)
    return kinds


def _cparams(*sem):
    return pltpu.CompilerParams(dimension_semantics=sem, vmem_limit_bytes=VMEM_LIMIT)


def _sigmoid(x):
    return 1.0 / (1.0 + jnp.exp(-x))


def _layernorm(x, g, b):
    mu = jnp.mean(x, axis=-1, keepdims=True)
    xc = x - mu
    var = jnp.mean(xc * xc, axis=-1, keepdims=True)
    return xc * lax.rsqrt(var + LN_EPS) * g + b


def _dot_nt(a, b, precision=None):
    return lax.dot_general(a, b, (((1,), (1,)), ((), ())), preferred_element_type=F32, precision=precision)


def _softmax_parts(s):
    m = jnp.max(s, axis=-1, keepdims=True)
    m = jnp.where(m == NEG_INF, 0.0, m)
    p = jnp.exp(s - m)
    return p, jnp.maximum(jnp.sum(p, axis=-1, keepdims=True), 1e-30)


def _proj_body(kinds_ref, x_ref, w_ref, tab_ref, o_ref):
    nsub = PROJ_TN // LANES
    j = pl.program_id(1)
    h = jnp.dot(x_ref[...], w_ref[...], preferred_element_type=F32)
    lane = lax.broadcasted_iota(jnp.int32, (1, LANES), 1)

    def rope(hs, t0, sh):
        return (hs * tab_ref[t0] + pltpu.roll(hs, sh, 1) * tab_ref[t0 + 1]
                + pltpu.roll(hs, LANES - sh, 1) * tab_ref[t0 + 2])

    for s in range(nsub):
        kind = kinds_ref[j * nsub + s]
        hs = h[:, s * LANES:(s + 1) * LANES]
        sl = slice(s * LANES, (s + 1) * LANES)

        @pl.when(kind == K_PLAIN)
        def _():
            o_ref[:, sl] = hs

        @pl.when(kind == K_ROPE128)
        def _():
            o_ref[:, sl] = rope(hs, 0, 16)

        @pl.when(kind == K_ROPE64)
        def _():
            o_ref[:, sl] = rope(hs, 3, 8)

        @pl.when(kind == K_SIGMOID)
        def _():
            o_ref[:, sl] = _sigmoid(hs)

        @pl.when(kind == K_MISC)
        def _():
            r = rope(hs, 3, 8)
            o_ref[:, sl] = jnp.where(lane < MISC_IW, r,
                                     jnp.where(lane < MISC_NG, hs * (IDX_HEADS ** -0.5),
                                               jnp.where(lane < MISC_NG + 24, _sigmoid(hs), hs)))


def _proj(x_bf, w_bf, tabs, kinds, tm):
    n = x_bf.shape[0]
    n_tab = tabs.shape[1] // tm
    grid = (n // tm, NCOL // PROJ_TN)
    return pl.pallas_call(
        _proj_body,
        grid_spec=pltpu.PrefetchScalarGridSpec(
            num_scalar_prefetch=1, grid=grid,
            in_specs=[pl.BlockSpec((tm, D_MODEL), lambda i, j, k: (i, 0)),
                      pl.BlockSpec((D_MODEL, PROJ_TN), lambda i, j, k: (0, j)),
                      pl.BlockSpec((6, tm, LANES), lambda i, j, k: (0, i % n_tab, 0))],
            out_specs=pl.BlockSpec((tm, PROJ_TN), lambda i, j, k: (i, j))),
        out_shape=jax.ShapeDtypeStruct((n, NCOL), F32),
        compiler_params=_cparams("arbitrary", "arbitrary"),
        name="in_proj",
    )(kinds, x_bf, w_bf, tabs)


def _rope_tables(pos):
    out = []
    lane = jnp.arange(LANES)
    for d in (HEAD_DIM, IDX_DIM):
        rot = d // 4
        half = rot // 2
        inv = ROPE_THETA ** (-jnp.arange(half, dtype=F32) / half)
        ang = pos.astype(F32)[:, None] * inv[None, :]
        cos = jnp.cos(ang)
        sin = jnp.sin(ang)
        li = lane % d
        ci = jnp.take(cos, li % half, axis=1)
        si = jnp.take(sin, li % half, axis=1)
        out.append(jnp.where(li[None] < rot, ci, 1.0))
        out.append(jnp.where((li[None] >= half) & (li[None] < rot), si, 0.0))
        out.append(jnp.where(li[None] < half, -si, 0.0))
    return jnp.stack(out, axis=0)


def _permute_w_in(w):
    offs = np.cumsum((0,) + IN_WIDTHS)
    nq, nkv, ng, dq, dkv, iq, ik, iw, cv, mg = [w[:, offs[i]:offs[i + 1]] for i in range(10)]
    z = lambda n: jnp.zeros((w.shape[0], n), w.dtype)
    cols = [nq, nq, nkv, dq, dkv, cv, mg, iq, ik, iw, ng, z(LANES - 92), z(NCOL - C_MISC - LANES)]
    return jnp.concatenate(cols, axis=1).astype(BF16)


def _conv_body(cv_ref, prev_ref, past_ref, w_ref, y_ref, cu_ref, s_ref, *, tiles_per_seq, tm):
    i = pl.program_id(0)
    b = cv_ref[:, 0:CONV_DIM]
    cu = cv_ref[:, CONV_DIM:2 * CONV_DIM] * cv_ref[:, 2 * CONV_DIM:3 * CONV_DIM]
    first = (i % tiles_per_seq) == 0
    prev = prev_ref[:, CONV_DIM:2 * CONV_DIM] * prev_ref[:, 2 * CONV_DIM:3 * CONV_DIM]
    s_ref[0:SUBLANES, :] = jnp.where(first, past_ref[0], prev)
    s_ref[SUBLANES:SUBLANES + tm, :] = cu
    y = (w_ref[0:1, :] * s_ref[pl.ds(SUBLANES - 2, tm), :] + w_ref[1:2, :] * s_ref[pl.ds(SUBLANES - 1, tm), :]
         + w_ref[2:3, :] * cu)
    y_ref[...] = b * y
    cu_ref[...] = cu


def _conv(hp, past8, conv_w8, tm, tiles_per_seq):
    n = hp.shape[0]
    cvb = C_CV // (3 * CONV_DIM)
    rb = tm // SUBLANES
    return pl.pallas_call(
        functools.partial(_conv_body, tiles_per_seq=tiles_per_seq, tm=tm),
        grid=(n // tm,),
        in_specs=[pl.BlockSpec((tm, 3 * CONV_DIM), lambda i: (i, cvb)),
                  pl.BlockSpec((SUBLANES, 3 * CONV_DIM), lambda i: (jnp.maximum(i * rb - 1, 0), cvb)),
                  pl.BlockSpec((1, SUBLANES, CONV_DIM), lambda i: (i // tiles_per_seq, 0, 0)),
                  pl.BlockSpec((SUBLANES, CONV_DIM), lambda i: (0, 0))],
        out_specs=[pl.BlockSpec((tm, CONV_DIM), lambda i: (i, 0)),
                   pl.BlockSpec((tm, CONV_DIM), lambda i: (i, 0))],
        out_shape=[jax.ShapeDtypeStruct((n, CONV_DIM), F32), jax.ShapeDtypeStruct((n, CONV_DIM), F32)],
        scratch_shapes=[pltpu.VMEM((tm + SUBLANES, CONV_DIM), F32)],
        compiler_params=_cparams("arbitrary"),
        name="short_conv",
    )(hp, hp, past8, conv_w8)


_PageLayout = collections.namedtuple("_PageLayout", "il width col_block base")


def _page_specs(layout, n_lead, kg_n=None):
    def spec(k):
        def imap(*a):
            ids, pt = a[:n_lead], a[-1]
            kg = ids[-1] % kg_n if kg_n else ids[-1]
            page = pt[ids[0], kg * PAGES_PER_STEP + k]
            if layout.il:
                return (page, 0)
            cb = layout.col_block(*ids) if callable(layout.col_block) else layout.col_block
            return (page, 0, cb)
        shape = (PAGE * layout.il, LANES) if layout.il else (1, PAGE, layout.width)
        return pl.BlockSpec(shape, imap)
    return [spec(k) for k in range(PAGES_PER_STEP)]


def _page_tile(layout, ref, j, width=LANES):
    if layout.il:
        return ref[pl.ds(layout.base + j, PAGE, stride=layout.il), :]
    return ref[0, :, j * width:(j + 1) * width]


def _chunk_rows(layout, ref, j, p):
    if layout.il:
        return ref[pl.ds(p * layout.il + layout.base + j, SUBLANES, stride=CMP_STRIDE * layout.il), :]
    return ref[0, pl.ds(p, SUBLANES, stride=CMP_STRIDE), :]


def _cmp1_body(pt_ref, *refs, layout, n_inner):
    pages = refs[:PAGES_PER_STEP]
    w_ref, pe_ref = refs[PAGES_PER_STEP:PAGES_PER_STEP + 2]
    o_ref = refs[PAGES_PER_STEP + 2]
    for j in range(n_inner):
        slot = j // 2
        acc_a = jnp.zeros((PAGE, HEAD_DIM), F32)
        acc_b = jnp.zeros((PAGE, HEAD_DIM), F32)
        for p in range(CMP_STRIDE):
            xp = jnp.concatenate([_chunk_rows(layout, pg, j, p) for pg in pages], axis=0)
            xa = (xp + pe_ref[slot, p:p + 1, :]).astype(BF16)
            xb = (xp + pe_ref[slot, CMP_STRIDE + p:CMP_STRIDE + p + 1, :]).astype(BF16)
            acc_a = acc_a + jnp.dot(xa, w_ref[slot, p, :, 0:HEAD_DIM], preferred_element_type=F32)
            acc_b = acc_b + jnp.dot(xb, w_ref[slot, p, :, HEAD_DIM:2 * HEAD_DIM], preferred_element_type=F32)
        o_ref[0, :, j * 256:j * 256 + HEAD_DIM] = acc_a
        o_ref[0, :, j * 256 + HEAD_DIM:(j + 1) * 256] = acc_b


def _cmp1(pool, layout, pt, wcat, pe):
    nb, npg = pt.shape
    kg = npg // PAGES_PER_STEP
    if layout.il:
        n_inner, n_sg, wsel, osel, ow = 4, 1, (lambda sg: 0), (lambda sg: 0), 1024
        wblk = 2
    else:
        cb = layout.col_block
        layout = layout._replace(width=HEAD_DIM, col_block=lambda b, sg, k: cb * 4 + sg)
        n_inner, n_sg, wsel, osel, ow = 1, 4, (lambda sg: sg // 2), (lambda sg: sg), 256
        wblk = 1
    return pl.pallas_call(
        functools.partial(_cmp1_body, layout=layout, n_inner=n_inner),
        grid_spec=pltpu.PrefetchScalarGridSpec(
            num_scalar_prefetch=1, grid=(nb, n_sg, kg),
            in_specs=_page_specs(layout, 3)
            + [pl.BlockSpec((wblk, 16, HEAD_DIM, 256), lambda b, sg, k, pt: (wsel(sg), 0, 0, 0)),
               pl.BlockSpec((wblk, CMP_LEN, HEAD_DIM), lambda b, sg, k, pt: (wsel(sg), 0, 0))],
            out_specs=pl.BlockSpec((1, PAGE, ow), lambda b, sg, k, pt: (b, k, osel(sg)))),
        out_shape=jax.ShapeDtypeStruct((nb, npg * SUBLANES, 1024), F32),
        compiler_params=_cparams("arbitrary", "arbitrary", "arbitrary"),
        name="nsa_compress1",
    )(pt, *([pool] * PAGES_PER_STEP), wcat, pe)


def _gelu_tanh(x):
    return 0.5 * x * (1.0 + jnp.tanh(np.sqrt(2.0 / np.pi).astype(np.float32) * (x + 0.044715 * (x * x * x))))


def _cmp2_body(ab_ref, tail_ref, pe_ref, w1_ref, w2_ref, o_ref, s_ref, *, nc, n_tail):
    row8 = lax.broadcasted_iota(jnp.int32, (SUBLANES, 1), 0)
    row16 = lax.broadcasted_iota(jnp.int32, (CMP_STRIDE, 1), 0)
    for sg in range(4):
        slot = sg // 2
        a = ab_ref[0, :, sg * 256:sg * 256 + HEAD_DIM]
        s_ref[0:nc, :] = ab_ref[0, :, sg * 256 + HEAD_DIM:(sg + 1) * 256]
        tb = jnp.zeros((SUBLANES, HEAD_DIM), F32)
        if n_tail:
            x8 = jnp.where(row8 < n_tail, tail_ref[:, sg * HEAD_DIM:(sg + 1) * HEAD_DIM], 0.0)
            x16 = jnp.concatenate([x8, jnp.zeros((CMP_STRIDE - SUBLANES, HEAD_DIM), F32)], axis=0)
            x16 = x16 + pe_ref[slot, CMP_STRIDE:CMP_LEN, :]
            t16 = jnp.zeros((CMP_STRIDE, HEAD_DIM), F32)
            for p in range(CMP_STRIDE):
                xm = jnp.where(row16 == p, x16, 0.0).astype(BF16)
                t16 = t16 + jnp.dot(xm, w1_ref[slot, CMP_STRIDE + p].astype(BF16), preferred_element_type=F32)
            tb = jnp.sum(t16, axis=0, keepdims=True) * jnp.where(row8 == 0, 1.0, 0.0)
        s_ref[nc:nc + SUBLANES, :] = tb
        pre = a + s_ref[pl.ds(1, nc), :]
        o_ref[0, :, sg * HEAD_DIM:(sg + 1) * HEAD_DIM] = jnp.dot(
            _gelu_tanh(pre).astype(BF16), w2_ref[slot].astype(BF16), preferred_element_type=F32)


def _cmp2(ab, tail, pe, w1r, w2, n_tail, tail_col_block):
    nb, nc, _ = ab.shape
    if tail is None:
        tail = jnp.zeros((nb * SUBLANES, 512), F32)
        tail_col_block = 0
    return pl.pallas_call(
        functools.partial(_cmp2_body, nc=nc, n_tail=n_tail),
        grid=(nb,),
        in_specs=[pl.BlockSpec((1, nc, 1024), lambda b: (b, 0, 0)),
                  pl.BlockSpec((SUBLANES, 512), lambda b: (b, tail_col_block)),
                  pl.BlockSpec((2, CMP_LEN, HEAD_DIM), lambda b: (0, 0, 0)),
                  pl.BlockSpec((2, CMP_LEN, HEAD_DIM, HEAD_DIM), lambda b: (0, 0, 0, 0)),
                  pl.BlockSpec((2, HEAD_DIM, HEAD_DIM), lambda b: (0, 0, 0))],
        out_specs=pl.BlockSpec((1, nc, 512), lambda b: (b, 0, 0)),
        out_shape=jax.ShapeDtypeStruct((nb, nc, 512), F32),
        scratch_shapes=[pltpu.VMEM((nc + SUBLANES, HEAD_DIM), F32)],
        compiler_params=_cparams("arbitrary"),
        name="nsa_compress2",
    )(ab, tail, pe, w1r, w2)


def _nsa1_body(*refs, tq, q0, n_cmp, ncp, n_sel, nselp, lpad, nwb, has_tail, win_k0):
    qraw_ref, qrot_ref, cmp_ref, ov_ref = refs[:4]
    wins = refs[4:4 + nwb]
    pos = 4 + nwb
    tail_ref = None
    if has_tail:
        tail_ref = refs[pos]
        pos += 1
    ocmp_ref, owin_ref, kmask_ref = refs[pos:pos + 3]
    qt = pl.program_id(1)
    r_per_g = NSA_HEADS // NSA_KV
    qpos = q0 + qt * tq + lax.broadcasted_iota(jnp.int32, (tq, 1), 0)

    jj = lax.broadcasted_iota(jnp.int32, (1, ncp), 1)
    cmask = (jj * CMP_STRIDE + (CMP_LEN - 1) <= qpos) & (jj < n_cmp)
    blk = lax.broadcasted_iota(jnp.int32, (1, nselp), 1)
    cur = qpos // SEL_BLOCK
    forced = (blk == 0) | (blk == cur) | (blk == cur - 1)
    e_row = lax.broadcasted_iota(jnp.int32, (KSTEP // SEL_BLOCK, KSTEP), 0)
    e_col = lax.broadcasted_iota(jnp.int32, (KSTEP // SEL_BLOCK, KSTEP), 1)
    expand = jnp.where(e_col // SEL_BLOCK == e_row, 1.0, 0.0).astype(BF16)
    lane128 = lax.broadcasted_iota(jnp.int32, (1, LANES), 1)
    for g in range(NSA_KV):
        kc = cmp_ref[0, :, g * HEAD_DIM:(g + 1) * HEAD_DIM].astype(BF16)
        vc = cmp_ref[0, :, (2 + g) * HEAD_DIM:(3 + g) * HEAD_DIM].astype(BF16)
        psum = jnp.zeros((tq, ncp), F32)
        for r in range(r_per_g):
            h = g * r_per_g + r
            q = qraw_ref[:, h * HEAD_DIM:(h + 1) * HEAD_DIM].astype(BF16)
            s = jnp.where(cmask, _dot_nt(q, kc) * ATT_SCALE, NEG_INF)
            p, den = _softmax_parts(s)
            p = p / den
            psum = psum + p
            ocmp_ref[:, h * HEAD_DIM:(h + 1) * HEAD_DIM] = jnp.dot(p.astype(BF16), vc, preferred_element_type=F32)
        imp = jnp.dot(psum.astype(BF16), ov_ref[...].astype(BF16), preferred_element_type=F32)
        imp = jnp.where(forced, imp + FORCE_SCORE, imp)
        imp = jnp.where((blk <= cur) & (blk < n_sel), imp, NEG_INF)
        rank = jnp.zeros((tq, nselp), F32)
        for i in range(n_sel):
            vi = imp[:, i:i + 1]
            beats = (vi > imp) | ((vi == imp) & (blk > i))
            rank = rank + jnp.where(beats, 1.0, 0.0)
        sel = jnp.where((rank < min(SEL_TOP, n_sel)) & (blk < n_sel), 1.0, 0.0).astype(BF16)
        per = KSTEP // SEL_BLOCK
        for c in range(lpad // KSTEP):
            km = jnp.dot(sel[:, c * per:(c + 1) * per], expand, preferred_element_type=F32)
            kmask_ref[0, g, :, c * KSTEP:(c + 1) * KSTEP] = km.astype(BF16)
        if lpad % KSTEP:
            b0 = (lpad // KSTEP) * per
            km = jnp.where(lane128 < SEL_BLOCK, sel[:, b0:b0 + 1].astype(F32), 0.0)
            kmask_ref[0, g, :, (lpad // KSTEP) * KSTEP:lpad] = jnp.broadcast_to(km, (tq, LANES)).astype(BF16)

    nk = nwb * PAGE + (LANES if has_tail else 0)
    kk = lax.broadcasted_iota(jnp.int32, (1, nk), 1)
    if has_tail:
        kpos = jnp.where(kk < nwb * PAGE, win_k0 + kk, q0 + kk - nwb * PAGE)
    else:
        kpos = (qt - (nwb - 1)) * PAGE + kk
    rel = qpos - kpos
    wmask = (rel >= 0) & (rel < WINDOW) & (kpos >= 0)
    for g in range(NSA_KV):
        if has_tail:
            kparts = [w[pl.ds(g, PAGE, stride=4), :] for w in wins]
            vparts = [w[pl.ds(2 + g, PAGE, stride=4), :] for w in wins]
        else:
            kparts = [w[:, g * HEAD_DIM:(g + 1) * HEAD_DIM] for w in wins]
            vparts = [w[:, (2 + g) * HEAD_DIM:(3 + g) * HEAD_DIM] for w in wins]
        if has_tail:
            zpad = jnp.zeros((LANES - SUBLANES, HEAD_DIM), F32)
            kparts += [tail_ref[:, g * HEAD_DIM:(g + 1) * HEAD_DIM], zpad]
            vparts += [tail_ref[:, (2 + g) * HEAD_DIM:(3 + g) * HEAD_DIM], zpad]
        kw = jnp.concatenate(kparts, axis=0).astype(BF16)
        vw = jnp.concatenate(vparts, axis=0).astype(BF16)
        for r in range(r_per_g):
            h = g * r_per_g + r
            q = qrot_ref[:, h * HEAD_DIM:(h + 1) * HEAD_DIM].astype(BF16)
            s = jnp.where(wmask, _dot_nt(q, kw) * ATT_SCALE, NEG_INF)
            p, den = _softmax_parts(s)
            p = p / den
            owin_ref[:, h * HEAD_DIM:(h + 1) * HEAD_DIM] = jnp.dot(p.astype(BF16), vw, preferred_element_type=F32)


def _nsa1(hp, cmp, ov, win_src, tail_src, *, nb, tq, qt_n, q0, n_cmp, n_sel, lpad, nwb, win_k0, win_blk0):
    n = hp.shape[0]
    ncp = cmp.shape[1]
    nselp = ov.shape[1]
    has_tail = tail_src is not None
    rows = lambda b, t: b * qt_n + t
    in_specs = [pl.BlockSpec((tq, 1024), lambda b, t: (rows(b, t), C_QRAW // 1024)),
                pl.BlockSpec((tq, 1024), lambda b, t: (rows(b, t), C_QROT // 1024)),
                pl.BlockSpec((1, ncp, 512), lambda b, t: (b, 0, 0)),
                pl.BlockSpec((ncp, nselp), lambda b, t: (0, 0))]
    if has_tail:
        in_specs += [pl.BlockSpec((PAGE * 4, LANES), lambda b, t, k=k: (win_blk0 + b * nwb + k, 0))
                     for k in range(nwb)]
        in_specs += [pl.BlockSpec((SUBLANES, 512), lambda b, t: (b, C_WIN // 512))]
        args = [win_src] * nwb + [tail_src]
    else:
        in_specs += [pl.BlockSpec((PAGE, 512),
                                  lambda b, t, k=k: (b * qt_n + jnp.maximum(t - (nwb - 1) + k, 0), C_WIN // 512))
                     for k in range(nwb)]
        args = [win_src] * nwb
    return pl.pallas_call(
        functools.partial(_nsa1_body, tq=tq, q0=q0, n_cmp=n_cmp, ncp=ncp, n_sel=n_sel, nselp=nselp, lpad=lpad,
                          nwb=nwb, has_tail=has_tail, win_k0=win_k0),
        grid=(nb, qt_n),
        in_specs=in_specs,
        out_specs=[pl.BlockSpec((tq, 1024), lambda b, t: (rows(b, t), 0)),
                   pl.BlockSpec((tq, 1024), lambda b, t: (rows(b, t), 0)),
                   pl.BlockSpec((1, NSA_KV, tq, lpad), lambda b, t: (b, 0, t, 0))],
        out_shape=[jax.ShapeDtypeStruct((n, 1024), F32), jax.ShapeDtypeStruct((n, 1024), F32),
                   jax.ShapeDtypeStruct((nb, NSA_KV, qt_n * tq, lpad), BF16)],
        compiler_params=_cparams("arbitrary", "arbitrary"),
        name="nsa_cmp_select_window",
    )(hp, hp, cmp, ov, *args)


def _mattn_body(pt_ref, *refs, layout, n_g, r_per_g, gm, tq, q0, kg_n, has_tail):
    q_ref, mask_ref = refs[:2]
    pages = refs[2:2 + PAGES_PER_STEP]
    pos = 2 + PAGES_PER_STEP
    if has_tail:
        tail_ref, tmask_ref = refs[pos:pos + 2]
        pos += 2
    o_ref, m_ref, l_ref, acc_ref = refs[pos:pos + 4]
    qt = pl.program_id(1)
    step = pl.program_id(2)
    two_pass = kg_n > 1 or has_tail
    kg = step % kg_n if two_pass else step
    qpos = q0 + qt * tq + lax.broadcasted_iota(jnp.int32, (tq, 1), 0)

    def scores(h, g, kt, kpos, mref):
        valid = (mref[0, g if gm > 1 else 0] > 0.5) & (kpos <= qpos)
        q = q_ref[:, h * HEAD_DIM:(h + 1) * HEAD_DIM].astype(BF16)
        return jnp.where(valid, _dot_nt(q, kt) * ATT_SCALE, NEG_INF)

    def main_keys(g):
        return jnp.concatenate([_page_tile(layout, pg, g) for pg in pages], axis=0).astype(BF16)

    def main_vals(g):
        return jnp.concatenate([_page_tile(layout, pg, n_g + g) for pg in pages], axis=0).astype(BF16)

    def tail_part(g, off):
        zpad = jnp.zeros((LANES - SUBLANES, HEAD_DIM), F32)
        return jnp.concatenate([tail_ref[:, (off + g) * HEAD_DIM:(off + g + 1) * HEAD_DIM], zpad], axis=0).astype(BF16)

    kpos = kg * KSTEP + lax.broadcasted_iota(jnp.int32, (1, KSTEP), 1)
    tpos = q0 + lax.broadcasted_iota(jnp.int32, (1, LANES), 1)

    if not two_pass:
        for g in range(n_g):
            kt, vt = main_keys(g), main_vals(g)
            for r in range(r_per_g):
                h = g * r_per_g + r
                p, den = _softmax_parts(scores(h, g, kt, kpos, mask_ref))
                o_ref[:, h * HEAD_DIM:(h + 1) * HEAD_DIM] = jnp.dot((p / den).astype(BF16), vt,
                                                                   preferred_element_type=F32)
        return

    @pl.when(step == 0)
    def _():
        m_ref[...] = jnp.full(m_ref.shape, NEG_INF, F32)
        l_ref[...] = jnp.zeros(l_ref.shape, F32)
        acc_ref[...] = jnp.zeros(acc_ref.shape, F32)

    def stats(h, s):
        m_old = m_ref[h]
        m_new = jnp.maximum(m_old, jnp.max(s, axis=-1, keepdims=True))
        m_safe = jnp.where(m_new == NEG_INF, 0.0, m_new)
        l_ref[h] = jnp.exp(m_old - m_safe) * l_ref[h] + jnp.sum(jnp.exp(s - m_safe), axis=-1, keepdims=True)
        m_ref[h] = m_new

    def accumulate(h, s, vt):
        m = m_ref[h]
        m_safe = jnp.where(m == NEG_INF, 0.0, m)
        p = jnp.exp(s - m_safe) / jnp.maximum(l_ref[h], 1e-30)
        sl = slice(h * HEAD_DIM, (h + 1) * HEAD_DIM)
        acc_ref[:, sl] += jnp.dot(p.astype(BF16), vt, preferred_element_type=F32)

    @pl.when(step < kg_n)
    def _():
        for g in range(n_g):
            kt = main_keys(g)
            for r in range(r_per_g):
                h = g * r_per_g + r
                stats(h, scores(h, g, kt, kpos, mask_ref))

    if has_tail:
        @pl.when(step == kg_n - 1)
        def _():
            for g in range(n_g):
                kt = tail_part(g, 0)
                for r in range(r_per_g):
                    h = g * r_per_g + r
                    stats(h, scores(h, g, kt, tpos, tmask_ref))

    @pl.when(step >= kg_n)
    def _():
        for g in range(n_g):
            kt, vt = main_keys(g), main_vals(g)
            for r in range(r_per_g):
                h = g * r_per_g + r
                accumulate(h, scores(h, g, kt, kpos, mask_ref), vt)

    @pl.when(step == 2 * kg_n - 1)
    def _():
        if has_tail:
            for g in range(n_g):
                kt, vt = tail_part(g, 0), tail_part(g, n_g)
                for r in range(r_per_g):
                    h = g * r_per_g + r
                    accumulate(h, scores(h, g, kt, tpos, tmask_ref), vt)
        o_ref[...] = acc_ref[...]


def _mattn(hp, q_col, n_heads, mask, pool, layout, pt, tail_src, tail_col_block, *, nb, tq, qt_n, q0, name):
    n = hp.shape[0]
    n_g = 2
    r_per_g = n_heads // n_g
    qw = n_heads * HEAD_DIM
    gm = mask.shape[1]
    kg_n = pt.shape[1] // PAGES_PER_STEP
    has_tail = tail_src is not None
    n_steps = 2 * kg_n if (kg_n > 1 or has_tail) else 1
    in_specs = [pl.BlockSpec((tq, qw), lambda b, t, k, pt: (b * qt_n + t, q_col // qw)),
                pl.BlockSpec((1, gm, tq, KSTEP), lambda b, t, k, pt: (b, 0, t, k % kg_n))]
    in_specs += _page_specs(layout, 3, kg_n)
    args = [hp, mask] + [pool] * PAGES_PER_STEP
    if has_tail:
        in_specs += [pl.BlockSpec((SUBLANES, 512), lambda b, t, k, pt: (b, tail_col_block)),
                     pl.BlockSpec((1, gm, tq, LANES), lambda b, t, k, pt: (b, 0, t, kg_n * KSTEP // LANES))]
        args += [tail_src, mask]
    return pl.pallas_call(
        functools.partial(_mattn_body, layout=layout, n_g=n_g, r_per_g=r_per_g, gm=gm, tq=tq, q0=q0, kg_n=kg_n, has_tail=has_tail),
        grid_spec=pltpu.PrefetchScalarGridSpec(
            num_scalar_prefetch=1, grid=(nb, qt_n, n_steps),
            in_specs=in_specs,
            out_specs=pl.BlockSpec((tq, qw), lambda b, t, k, pt: (b * qt_n + t, 0)),
            scratch_shapes=[pltpu.VMEM((n_heads, tq, 1), F32), pltpu.VMEM((n_heads, tq, 1), F32),
                            pltpu.VMEM((tq, qw), F32)]),
        out_shape=jax.ShapeDtypeStruct((n, qw), F32),
        compiler_params=_cparams("arbitrary", "arbitrary", "arbitrary"),
        name=name,
    )(pt, *args)


def _dsa1_body(pt_ref, *refs, layout, tq, q0, kg_n, lpad, has_tail, n_keep):
    iq_ref, misc_ref = refs[:2]
    pages = refs[2:2 + PAGES_PER_STEP]
    pos = 2 + PAGES_PER_STEP
    if has_tail:
        tail_ref = refs[pos]
        pos += 1
    mask_ref, sc_ref = refs[pos:pos + 2]
    qt = pl.program_id(1)
    kg = pl.program_id(2)
    qpos = q0 + qt * tq + lax.broadcasted_iota(jnp.int32, (tq, 1), 0)
    lane128 = lax.broadcasted_iota(jnp.int32, (1, LANES), 1)

    def scores(kt, kpos):
        acc = jnp.zeros((tq, kt.shape[0]), F32)
        for h in range(IDX_HEADS):
            qi = iq_ref[:, h * IDX_DIM:(h + 1) * IDX_DIM].astype(BF16)
            w = jnp.sum(jnp.where(lane128 == MISC_IW + h, misc_ref[...], 0.0), axis=-1, keepdims=True)
            acc = acc + w * jnp.maximum(_dot_nt(qi, kt), 0.0)
        return jnp.where(kpos <= qpos, acc, NEG_INF)

    kt = jnp.concatenate([_page_tile(layout, pg, 0, IDX_DIM) for pg in pages], axis=0).astype(BF16)
    kpos = kg * KSTEP + lax.broadcasted_iota(jnp.int32, (1, KSTEP), 1)
    sc_ref[kg] = scores(kt, kpos)

    @pl.when(kg == kg_n - 1)
    def _():
        nch = sc_ref.shape[0]
        if has_tail:
            zpad = jnp.zeros((KSTEP - SUBLANES, IDX_DIM), F32)
            tk = jnp.concatenate([tail_ref[:, 0:IDX_DIM], zpad], axis=0).astype(BF16)
            tl = lax.broadcasted_iota(jnp.int32, (1, KSTEP), 1)
            sc_ref[kg_n] = jnp.where(tl < LANES, scores(tk, q0 + tl), NEG_INF)
        sc = sc_ref[...] + 0.0
        bits = pltpu.bitcast(sc, jnp.int32)
        key = jnp.where(bits < 0, bits ^ jnp.int32(0x7FFFFFFF), bits)
        int_min = jnp.int32(-2 ** 31)

        def count(pred):
            return jnp.sum(jnp.sum(jnp.where(pred, 1.0, 0.0), axis=-1, keepdims=True), axis=0, keepdims=True)

        def vbit(i, thr):
            cand = thr + jnp.left_shift(jnp.int32(1), 31 - i)
            return jnp.where(count(key >= cand) >= n_keep, cand, thr)

        thr = lax.fori_loop(0, 32, vbit, jnp.full((1, tq, 1), int_min, jnp.int32))
        gt = key > thr
        tie = key == thr
        need = n_keep - count(gt)
        idx = (lax.broadcasted_iota(jnp.int32, (nch, 1, KSTEP), 0) * KSTEP
               + lax.broadcasted_iota(jnp.int32, (nch, 1, KSTEP), 2))
        nbits = int(np.ceil(np.log2(nch * KSTEP)))

        def ibit(i, c):
            cand = c + jnp.left_shift(jnp.int32(1), nbits - 1 - i)
            return jnp.where(count(tie & (idx < cand)) < need, cand, c)

        cut = lax.fori_loop(0, nbits, ibit, jnp.zeros((1, tq, 1), jnp.int32))
        sel = jnp.where(gt | (tie & (idx <= cut)), 1.0, 0.0).astype(BF16)
        for c in range(kg_n):
            mask_ref[0, 0, :, c * KSTEP:(c + 1) * KSTEP] = sel[c]
        if has_tail:
            mask_ref[0, 0, :, kg_n * KSTEP:lpad] = sel[kg_n][:, 0:lpad - kg_n * KSTEP]


def _dsa1(hp, pool, layout, pt, tail_src, *, nb, tq, qt_n, q0, lpad, n_keep):
    kg_n = pt.shape[1] // PAGES_PER_STEP
    has_tail = tail_src is not None
    in_specs = [pl.BlockSpec((tq, 256), lambda b, t, k, pt: (b * qt_n + t, C_IQ // 256)),
                pl.BlockSpec((tq, LANES), lambda b, t, k, pt: (b * qt_n + t, C_MISC // LANES))]
    in_specs += _page_specs(layout, 3)
    args = [hp, hp] + [pool] * PAGES_PER_STEP
    if has_tail:
        in_specs += [pl.BlockSpec((SUBLANES, LANES), lambda b, t, k, pt: (b, C_MISC // LANES))]
        args += [tail_src]
    return pl.pallas_call(
        functools.partial(_dsa1_body, layout=layout, tq=tq, q0=q0, kg_n=kg_n, lpad=lpad, has_tail=has_tail,
                          n_keep=n_keep),
        grid_spec=pltpu.PrefetchScalarGridSpec(
            num_scalar_prefetch=1, grid=(nb, qt_n, kg_n),
            in_specs=in_specs,
            out_specs=pl.BlockSpec((1, 1, tq, lpad), lambda b, t, k, pt: (b, 0, t, 0)),
            scratch_shapes=[pltpu.VMEM((kg_n + (1 if has_tail else 0), tq, KSTEP), F32)]),
        out_shape=jax.ShapeDtypeStruct((nb, 1, qt_n * tq, lpad), BF16),
        compiler_params=_cparams("arbitrary", "arbitrary", "arbitrary"),
        name="dsa_indexer_topk",
    )(pt, *args)


def _mix_body(ocmp_ref, osel_ref, owin_ref, odsa_ref, conv_ref, misc_ref, mg0_ref, mg1_ref, mg2_ref,
              wa_ref, wb_ref, wc_ref, z_ref):
    lane128 = lax.broadcasted_iota(jnp.int32, (1, LANES), 1)
    misc = misc_ref[...]

    def gate(kind, h):
        return jnp.sum(jnp.where(lane128 == MISC_NG + kind * NSA_HEADS + h, misc, 0.0), axis=-1, keepdims=True)

    parts = []
    for h in range(NSA_HEADS):
        sl = slice(h * HEAD_DIM, (h + 1) * HEAD_DIM)
        parts.append((gate(0, h) * ocmp_ref[:, sl] + gate(1, h) * osel_ref[:, sl]
                      + gate(2, h) * owin_ref[:, sl]).astype(BF16))
    o_nsa = jnp.concatenate(parts, axis=1)
    p_a = jnp.dot(o_nsa, wa_ref[...], preferred_element_type=F32)
    p_b = jnp.dot(odsa_ref[...].astype(BF16), wb_ref[...], preferred_element_type=F32)
    p_c = jnp.dot(conv_ref[...].astype(BF16), wc_ref[...], preferred_element_type=F32)
    z_ref[...] = (mg0_ref[...] * p_a + mg1_ref[...] * p_b + mg2_ref[...] * p_c).astype(BF16)


def _mix(hp, o_cmp, o_sel, o_win, o_dsa, conv_out, wa, wb, wc, tm):
    n = hp.shape[0]
    row = lambda w, cb=0: pl.BlockSpec((tm, w), lambda i: (i, cb))
    full = lambda a: pl.BlockSpec(a.shape, lambda i: (0, 0))
    return pl.pallas_call(
        _mix_body,
        grid=(n // tm,),
        in_specs=[row(1024), row(1024), row(1024), row(512), row(512), row(LANES, C_MISC // LANES),
                  row(D_MODEL, C_MG // D_MODEL), row(D_MODEL, C_MG // D_MODEL + 1), row(D_MODEL, C_MG // D_MODEL + 2),
                  full(wa), full(wb), full(wc)],
        out_specs=row(D_MODEL),
        out_shape=jax.ShapeDtypeStruct((n, D_MODEL), BF16),
        compiler_params=_cparams("arbitrary"),
        name="branch_merge",
    )(o_cmp, o_sel, o_win, o_dsa, conv_out, hp, hp, hp, hp, wa, wb, wc)


def _outln_body(z_ref, wo_ref, x_ref, g_ref, b_ref, wr_ref, rb_ref, x1_ref, gate_ref, *, tm):
    y = jnp.dot(z_ref[...], wo_ref[...], preferred_element_type=F32)
    x1 = _layernorm(ALPHA * x_ref[...] + y, g_ref[...], b_ref[...])
    x1_ref[...] = x1
    aff = _sigmoid(_dot_nt(wr_ref[...].astype(BF16), x1.astype(BF16)))
    biased = aff + rb_ref[:, 0:1]
    rows = [biased[e:e + 1, :] for e in range(N_EXPERTS)]
    best = None
    g_best = jnp.zeros((1, tm), jnp.int32)
    for g in range(N_GROUPS):
        v = rows[g * EXPERTS_PER_GROUP:(g + 1) * EXPERTS_PER_GROUP]
        score = None
        for a in range(EXPERTS_PER_GROUP):
            for c in range(a + 1, EXPERTS_PER_GROUP):
                pair = v[a] + v[c]
                score = pair if score is None else jnp.maximum(score, pair)
        if best is None:
            best = score
        else:
            better = score > best
            best = jnp.where(better, score, best)
            g_best = jnp.where(better, g, g_best)
    sel_rows = []
    for e in range(N_EXPERTS):
        g = e // EXPERTS_PER_GROUP
        rank = jnp.zeros((1, tm), F32)
        for o in range(g * EXPERTS_PER_GROUP, (g + 1) * EXPERTS_PER_GROUP):
            if o == e:
                continue
            beats = (rows[o] > rows[e]) | ((rows[o] == rows[e]) & (o < e))
            rank = rank + jnp.where(beats, 1.0, 0.0)
        sel_rows.append(jnp.where((g_best == g) & (rank < 2), aff[e:e + 1, :], 0.0))
    tot = sel_rows[0]
    for e in range(1, N_EXPERTS):
        tot = tot + sel_rows[e]
    gate_t = jnp.concatenate(sel_rows + [jnp.zeros((LANES - N_EXPERTS, tm), F32)], axis=0) / tot
    gate_ref[...] = gate_t.T


def _outln(z, wo, x, g, b, wr_t, rb, tm):
    n = z.shape[0]
    row = lambda w: pl.BlockSpec((tm, w), lambda i: (i, 0))
    full = lambda a: pl.BlockSpec(a.shape, lambda i: (0, 0))
    return pl.pallas_call(
        functools.partial(_outln_body, tm=tm),
        grid=(n // tm,),
        in_specs=[row(D_MODEL), full(wo), row(D_MODEL), full(g), full(b), full(wr_t), full(rb)],
        out_specs=[row(D_MODEL), row(LANES)],
        out_shape=[jax.ShapeDtypeStruct((n, D_MODEL), F32), jax.ShapeDtypeStruct((n, LANES), F32)],
        compiler_params=_cparams("arbitrary"),
        name="out_proj_ln_router",
    )(z, wo, x, g, b, wr_t, rb)


def _moe_body(x_ref, gate_ref, wg_ref, wu_ref, wd_ref, g_ref, b_ref, o_ref, xb_ref, acc_ref):
    e = pl.program_id(1)

    @pl.when(e == 0)
    def _():
        xb_ref[...] = x_ref[...].astype(BF16)
        acc_ref[...] = jnp.zeros(acc_ref.shape, F32)

    lane128 = lax.broadcasted_iota(jnp.int32, (1, LANES), 1)
    gcol = jnp.sum(jnp.where(lane128 == e, gate_ref[...], 0.0), axis=-1, keepdims=True)
    xb = xb_ref[...]
    hg = jnp.dot(xb, wg_ref[0], preferred_element_type=F32)
    hu = jnp.dot(xb, wu_ref[0], preferred_element_type=F32)
    h = (hg * _sigmoid(hg)) * hu * gcol
    acc_ref[...] += jnp.dot(h.astype(BF16), wd_ref[0], preferred_element_type=F32)

    @pl.when(e == N_EXPERTS - 1)
    def _():
        o_ref[...] = _layernorm(ALPHA * x_ref[...] + acc_ref[...], g_ref[...], b_ref[...])


def _moe(x1, gate, wg, wu, wd, g, b, tm):
    n = x1.shape[0]
    return pl.pallas_call(
        _moe_body,
        grid=(n // tm, N_EXPERTS),
        in_specs=[pl.BlockSpec((tm, D_MODEL), lambda i, e: (i, 0)),
                  pl.BlockSpec((tm, LANES), lambda i, e: (i, 0)),
                  pl.BlockSpec((1, D_MODEL, D_FF), lambda i, e: (e, 0, 0)),
                  pl.BlockSpec((1, D_MODEL, D_FF), lambda i, e: (e, 0, 0)),
                  pl.BlockSpec((1, D_FF, D_MODEL), lambda i, e: (e, 0, 0)),
                  pl.BlockSpec((1, D_MODEL), lambda i, e: (0, 0)),
                  pl.BlockSpec((1, D_MODEL), lambda i, e: (0, 0))],
        out_specs=pl.BlockSpec((tm, D_MODEL), lambda i, e: (i, 0)),
        out_shape=jax.ShapeDtypeStruct((n, D_MODEL), F32),
        scratch_shapes=[pltpu.VMEM((tm, D_MODEL), BF16), pltpu.VMEM((tm, D_MODEL), F32)],
        compiler_params=_cparams("arbitrary", "arbitrary"),
        name="moe_ln",
    )(x1, gate, wg, wu, wd, g, b)


def _overlap_matrix(ncp, nselp):
    cs = np.arange(ncp)[:, None] * CMP_STRIDE
    ss = np.arange(nselp)[None, :] * SEL_BLOCK
    return jnp.asarray(((cs < ss + SEL_BLOCK) & (cs + CMP_LEN > ss)).astype(np.float32))


class _Group:
    def __init__(self, nb, t_real, t_pad, q0, past_len, tm, tq):
        self.nb, self.t_real, self.t_pad, self.q0, self.past_len, self.tm, self.tq = nb, t_real, t_pad, q0, past_len, tm, tq
        self.paged = past_len > 0
        self.lp = past_len if self.paged else t_pad
        self.ltot = self.lp + (t_real if self.paged else 0)
        self.lpad = self.lp + (LANES if self.paged else 0)
        self.qt_n = t_pad // tq
        n_chunks = -(-self.ltot // CMP_STRIDE)
        self.n_cmp = n_chunks - CMP_LEN // CMP_STRIDE + 1
        self.n_sel = -(-self.ltot // SEL_BLOCK)
        self.nselp = -(-self.n_sel // LANES) * LANES
        self.n_keep = min(DSA_TOPK, self.ltot // 4)


def _mixer(gp, x, lw, caches):
    hp = _proj(x.astype(BF16), lw["w_in"], gp.tabs, lw["kinds"], gp.tm)
    nb, tq, qt_n, q0 = gp.nb, gp.tq, gp.qt_n, gp.q0
    if gp.paged:
        l, c_nsa, c_dsa, c_kidx, s_win, s_conv, page_table = caches
        n_pool = c_nsa.shape[1]
        pt = page_table + l * n_pool
        nsa_pool = c_nsa.reshape(-1, LANES)
        dsa_pool = c_dsa.reshape(-1, LANES)
        kidx_pool = c_kidx.reshape(-1, PAGE, IDX_DIM)
        cmp_lay = _PageLayout(il=8, width=0, col_block=0, base=0)
        slc_lay = _PageLayout(il=8, width=0, col_block=0, base=4)
        dsa_lay = _PageLayout(il=4, width=0, col_block=0, base=0)
        kidx_lay = _PageLayout(il=0, width=IDX_DIM, col_block=0, base=0)
        tail = hp
        win_src = s_win.reshape(-1, LANES)
        nwb, win_k0, win_blk0 = WINDOW // PAGE, PAST_LEN - WINDOW, l * nb * (WINDOW // PAGE)
        past8 = jnp.concatenate([jnp.zeros((nb, SUBLANES - 2, CONV_DIM), F32), s_conv[l]], axis=1)
    else:
        pt = gp.pt
        nsa_pool = dsa_pool = kidx_pool = hp.reshape(nb * gp.t_pad // PAGE, PAGE, NCOL)
        cmp_lay = _PageLayout(il=0, width=512, col_block=C_CMP // 512, base=0)
        slc_lay = _PageLayout(il=0, width=512, col_block=C_SLC // 512, base=0)
        dsa_lay = _PageLayout(il=0, width=512, col_block=C_DSA // 512, base=0)
        kidx_lay = _PageLayout(il=0, width=LANES, col_block=C_MISC // LANES, base=0)
        tail = None
        win_src = hp
        nwb, win_k0, win_blk0 = WINDOW // PAGE + 1, 0, 0
        past8 = jnp.zeros((nb, SUBLANES, CONV_DIM), F32)

    conv_out, cu = _conv(hp, past8, lw["conv_w8"], min(gp.tm, gp.t_pad), gp.t_pad // min(gp.tm, gp.t_pad))

    ab = _cmp1(nsa_pool, cmp_lay, pt, lw["wcat"], lw["pe"])
    cmp = _cmp2(ab, tail, lw["pe"], lw["w1r"], lw["phi_w2"], gp.t_real if gp.paged else 0, C_CMP // 512)
    o_cmp, o_win, kmask = _nsa1(hp, cmp, gp.ov, win_src, tail, nb=nb, tq=tq, qt_n=qt_n, q0=q0, n_cmp=gp.n_cmp,
                                n_sel=gp.n_sel, lpad=gp.lpad, nwb=nwb, win_k0=win_k0, win_blk0=win_blk0)
    o_sel = _mattn(hp, C_QROT, NSA_HEADS, kmask, nsa_pool, slc_lay, pt, tail, C_SLC // 512,
                   nb=nb, tq=tq, qt_n=qt_n, q0=q0, name="nsa_selected_attn")
    dmask = _dsa1(hp, kidx_pool, kidx_lay, pt, tail, nb=nb, tq=tq, qt_n=qt_n, q0=q0, lpad=gp.lpad,
                  n_keep=gp.n_keep)
    o_dsa = _mattn(hp, C_DQ, DSA_HEADS, dmask, dsa_pool, dsa_lay, pt, tail, C_DSA // 512,
                   nb=nb, tq=tq, qt_n=qt_n, q0=q0, name="dsa_topk_attn")
    z = _mix(hp, o_cmp, o_sel, o_win, o_dsa, conv_out, lw["w_a"], lw["w_b"], lw["w_c"], min(gp.tm, 256))
    return z, hp, cu


def _layer(gp, x, lw, caches):
    z, hp, cu = _mixer(gp, x, lw, caches)
    tm2 = min(gp.tm, 256)
    x1, gate = _outln(z, lw["w_o"], x, lw["ln_mix_g"], lw["ln_mix_b"], lw["wr_t"], lw["rb"], tm2)
    x2 = _moe(x1, gate, lw["w_eg"], lw["w_eu"], lw["w_ed"], lw["ln_ffn_g"], lw["ln_ffn_b"], min(gp.tm, 512))
    return x2, hp, cu


def _layer_weights(l, w_in, nsa_phi_pos, nsa_phi_w1, nsa_phi_w2, conv_w, w_br_a, w_br_b, w_br_c, w_out,
                   ln_mix_g, ln_mix_b, ln_ffn_g, ln_ffn_b, w_router, router_bias, w_e_gate, w_e_up, w_e_down):
    w1r = nsa_phi_w1[l].reshape(2, CMP_LEN, HEAD_DIM, HEAD_DIM)
    wcat = jnp.concatenate([w1r[:, :CMP_STRIDE], w1r[:, CMP_STRIDE:]], axis=-1).astype(BF16)
    return dict(
        w_in=_permute_w_in(w_in[l]), kinds=jnp.asarray(_col_kinds()),
        w1r=w1r, wcat=wcat, pe=nsa_phi_pos[l], phi_w2=nsa_phi_w2[l],
        conv_w8=jnp.concatenate([conv_w[l], jnp.zeros((SUBLANES - 3, CONV_DIM), F32)], axis=0),
        w_a=w_br_a[l].astype(BF16), w_b=w_br_b[l].astype(BF16), w_c=w_br_c[l].astype(BF16),
        w_o=w_out[l].astype(BF16),
        ln_mix_g=ln_mix_g[l][None], ln_mix_b=ln_mix_b[l][None],
        ln_ffn_g=ln_ffn_g[l][None], ln_ffn_b=ln_ffn_b[l][None],
        wr_t=w_router.T, rb=jnp.broadcast_to(router_bias[:, None], (N_EXPERTS, LANES)),
        w_eg=w_e_gate[l].astype(BF16), w_eu=w_e_up[l].astype(BF16), w_ed=w_e_down[l].astype(BF16))


def kernel(x_prompt, x_sample, cache_nsa_kv, cache_dsa_kv, cache_dsa_kidx, state_nsa_win, state_conv, page_table,
           w_in, nsa_phi_pos, nsa_phi_w1, nsa_phi_w2, conv_w, w_br_a, w_br_b, w_br_c, w_out, ln_mix_g, ln_mix_b,
           ln_ffn_g, ln_ffn_b, w_router, router_bias, w_e_gate, w_e_up, w_e_down):
    bp, tp, _ = x_prompt.shape
    bs, ts, _ = x_sample.shape
    ts_pad = SUBLANES

    gp_p = _Group(bp, tp, tp, 0, 0, tm=512, tq=128)
    gp_p.tabs = _rope_tables(jnp.arange(tp, dtype=jnp.int32))
    gp_p.pt = jnp.arange(bp * tp // PAGE, dtype=jnp.int32).reshape(bp, tp // PAGE)
    gp_p.ov = _overlap_matrix(LANES, gp_p.nselp)
    gp_s = _Group(bs, ts, ts_pad, PAST_LEN, PAST_LEN, tm=bs * ts_pad, tq=ts_pad)
    pos_s = PAST_LEN + jnp.arange(ts_pad, dtype=jnp.int32)
    gp_s.tabs = jnp.tile(_rope_tables(pos_s), (1, bs, 1))
    gp_s.ov = _overlap_matrix(PAST_LEN // CMP_STRIDE, gp_s.nselp)

    xp = x_prompt.reshape(bp * tp, D_MODEL)
    xs = jnp.concatenate([x_sample, jnp.zeros((bs, ts_pad - ts, D_MODEL), F32)], axis=1).reshape(bs * ts_pad, D_MODEL)

    outs_p = [[] for _ in range(5)]
    outs_s = [[] for _ in range(5)]
    for l in range(DEPTH):
        lw = _layer_weights(l, w_in, nsa_phi_pos, nsa_phi_w1, nsa_phi_w2, conv_w, w_br_a, w_br_b, w_br_c, w_out,
                            ln_mix_g, ln_mix_b, ln_ffn_g, ln_ffn_b, w_router, router_bias, w_e_gate, w_e_up, w_e_down)
        xp, hp_p, cu_p = _layer(gp_p, xp, lw, None)
        xs, hp_s, cu_s = _layer(gp_s, xs, lw, (l, cache_nsa_kv, cache_dsa_kv, cache_dsa_kidx,
                                                 state_nsa_win, state_conv, page_table))
        h3 = hp_p.reshape(bp, tp, NCOL)
        outs_p[0].append(h3[:, :, C_CMP:C_CMP + 1024].reshape(bp, tp, 4, NSA_KV, HEAD_DIM))
        outs_p[1].append(h3[:, :, C_DSA:C_DSA + 512].reshape(bp, tp, 2, DSA_KV, HEAD_DIM))
        outs_p[2].append(h3[:, :, C_MISC:C_MISC + IDX_DIM])
        outs_p[3].append(h3[:, tp - min(WINDOW, tp):, C_WIN:C_WIN + 512].reshape(bp, min(WINDOW, tp), 2, NSA_KV, HEAD_DIM))
        outs_p[4].append(cu_p.reshape(bp, tp, CONV_DIM)[:, tp - 2:])
        s3 = hp_s.reshape(bs, ts_pad, NCOL)[:, :ts]
        outs_s[0].append(s3[:, :, C_CMP:C_CMP + 1024].reshape(bs, ts, 4, NSA_KV, HEAD_DIM))
        outs_s[1].append(s3[:, :, C_DSA:C_DSA + 512].reshape(bs, ts, 2, DSA_KV, HEAD_DIM))
        outs_s[2].append(s3[:, :, C_MISC:C_MISC + IDX_DIM])
        win_new = s3[:, :, C_WIN:C_WIN + 512].reshape(bs, ts, 2, NSA_KV, HEAD_DIM)
        wb = state_nsa_win.shape[2]
        outs_s[3].append(jnp.concatenate([state_nsa_win[l], win_new], axis=1)[:, -wb:])
        ext = jnp.concatenate([state_conv[l], cu_s.reshape(bs, ts_pad, CONV_DIM)[:, :ts]], axis=1)
        outs_s[4].append(ext[:, -2:])
    sp = [jnp.stack(a, axis=0) for a in outs_p]
    ss = [jnp.stack(a, axis=0) for a in outs_s]
    y_p = xp.reshape(bp, tp, D_MODEL)
    y_s = xs.reshape(bs, ts_pad, D_MODEL)[:, :ts]
    return (y_p, y_s, sp[0], ss[0], sp[1], ss[1], sp[2], ss[2], sp[3], ss[3], sp[4], ss[4])
```

```python
import collections
import functools

import numpy as np
import jax
import jax.numpy as jnp
from jax import lax
from jax.experimental import pallas as pl
from jax.experimental.pallas import tpu as pltpu

F32 = jnp.float32
BF16 = jnp.bfloat16
HIGHEST = lax.Precision.HIGHEST
NEG_INF = float("-inf")

D_MODEL = 2048
DEPTH = 2
PAST_LEN = 16384
PAGE = 128
HEAD_DIM = 128
ROPE_THETA = 500000.0
NSA_HEADS = 8
NSA_KV = 2
CMP_LEN = 32
CMP_STRIDE = 16
SEL_BLOCK = 64
SEL_TOP = 16
WINDOW = 512
FORCE_SCORE = 1e4
DSA_HEADS = 4
DSA_KV = 2
IDX_HEADS = 4
IDX_DIM = 64
DSA_TOPK = 256
CONV_DIM = 512
N_EXPERTS = 16
N_GROUPS = 4
EXPERTS_PER_GROUP = 4
D_FF = 512
LN_EPS = 1e-5
ALPHA = (2 * DEPTH) ** 0.25
IN_WIDTHS = (1024, 1536, 24, 512, 512, 256, 64, 4, 1536, 6144)
ATT_SCALE = HEAD_DIM ** -0.5

LANES = 128
SUBLANES = 8
VMEM_LIMIT = 56 * 1024 * 1024

C_QRAW = 0
C_QROT = 1024
C_CMP = 2048
C_SLC = 2560
C_WIN = 3072
C_DQ = 3584
C_DSA = 4096
C_CV = 4608
C_MG = 6144
C_IQ = 12288
C_MISC = 12544
NCOL = 12800
MISC_IW = 64
MISC_NG = 68
PROJ_TN = 512
PAGES_PER_STEP = 16
KSTEP = PAGES_PER_STEP * PAGE

K_PLAIN, K_ROPE128, K_ROPE64, K_SIGMOID, K_MISC = 0, 1, 2, 3, 4


def _col_kinds():
    kinds = np.zeros(NCOL // LANES, np.int32)

    def mark(c0, n, k):
        kinds[c0 // LANES:(c0 + n) // LANES] = k

    mark(C_QROT, 1024, K_ROPE128)
    mark(C_SLC, 256, K_ROPE128)
    mark(C_WIN, 256, K_ROPE128)
    mark(C_DQ, 512, K_ROPE128)
    mark(C_DSA, 256, K_ROPE128)
    mark(C_IQ, 256, K_ROPE64)
    mark(C_MISC, 128, K_MISC)
    mark(C_MG, 6144, K_SIGMOID)
    return kinds


def _cparams(*sem):
    return pltpu.CompilerParams(dimension_semantics=sem, vmem_limit_bytes=VMEM_LIMIT)


def _sigmoid(x):
    return 1.0 / (1.0 + jnp.exp(-x))


def _layernorm(x, g, b):
    mu = jnp.mean(x, axis=-1, keepdims=True)
    xc = x - mu
    var = jnp.mean(xc * xc, axis=-1, keepdims=True)
    return xc * lax.rsqrt(var + LN_EPS) * g + b


def _dot_nt(a, b, precision=None):
    return lax.dot_general(a, b, (((1,), (1,)), ((), ())), preferred_element_type=F32, precision=precision)


def _softmax_parts(s):
    m = jnp.max(s, axis=-1, keepdims=True)
    m = jnp.where(m == NEG_INF, 0.0, m)
    p = jnp.exp(s - m)
    return p, jnp.maximum(jnp.sum(p, axis=-1, keepdims=True), 1e-30)


def _proj_body(kinds_ref, x_ref, w_ref, tab_ref, o_ref):
    nsub = PROJ_TN // LANES
    j = pl.program_id(1)
    h = jnp.dot(x_ref[...], w_ref[...], preferred_element_type=F32)
    lane = lax.broadcasted_iota(jnp.int32, (1, LANES), 1)

    def rope(hs, t0, sh):
        return (hs * tab_ref[t0] + pltpu.roll(hs, sh, 1) * tab_ref[t0 + 1]
                + pltpu.roll(hs, LANES - sh, 1) * tab_ref[t0 + 2])

    for s in range(nsub):
        kind = kinds_ref[j * nsub + s]
        hs = h[:, s * LANES:(s + 1) * LANES]
        sl = slice(s * LANES, (s + 1) * LANES)

        @pl.when(kind == K_PLAIN)
        def _():
            o_ref[:, sl] = hs

        @pl.when(kind == K_ROPE128)
        def _():
            o_ref[:, sl] = rope(hs, 0, 16)

        @pl.when(kind == K_ROPE64)
        def _():
            o_ref[:, sl] = rope(hs, 3, 8)

        @pl.when(kind == K_SIGMOID)
        def _():
            o_ref[:, sl] = _sigmoid(hs)

        @pl.when(kind == K_MISC)
        def _():
            r = rope(hs, 3, 8)
            o_ref[:, sl] = jnp.where(lane < MISC_IW, r,
                                     jnp.where(lane < MISC_NG, hs * (IDX_HEADS ** -0.5),
                                               jnp.where(lane < MISC_NG + 24, _sigmoid(hs), hs)))


def _proj(x_bf, w_bf, tabs, kinds, tm):
    n = x_bf.shape[0]
    n_tab = tabs.shape[1] // tm
    grid = (n // tm, NCOL // PROJ_TN)
    return pl.pallas_call(
        _proj_body,
        grid_spec=pltpu.PrefetchScalarGridSpec(
            num_scalar_prefetch=1, grid=grid,
            in_specs=[pl.BlockSpec((tm, D_MODEL), lambda i, j, k: (i, 0)),
                      pl.BlockSpec((D_MODEL, PROJ_TN), lambda i, j, k: (0, j)),
                      pl.BlockSpec((6, tm, LANES), lambda i, j, k: (0, i % n_tab, 0))],
            out_specs=pl.BlockSpec((tm, PROJ_TN), lambda i, j, k: (i, j))),
        out_shape=jax.ShapeDtypeStruct((n, NCOL), F32),
        compiler_params=_cparams("arbitrary", "arbitrary"),
        name="in_proj",
    )(kinds, x_bf, w_bf, tabs)


def _rope_tables(pos):
    out = []
    lane = jnp.arange(LANES)
    for d in (HEAD_DIM, IDX_DIM):
        rot = d // 4
        half = rot // 2
        inv = ROPE_THETA ** (-jnp.arange(half, dtype=F32) / half)
        ang = pos.astype(F32)[:, None] * inv[None, :]
        cos = jnp.cos(ang)
        sin = jnp.sin(ang)
        li = lane % d
        ci = jnp.take(cos, li % half, axis=1)
        si = jnp.take(sin, li % half, axis=1)
        out.append(jnp.where(li[None] < rot, ci, 1.0))
        out.append(jnp.where((li[None] >= half) & (li[None] < rot), si, 0.0))
        out.append(jnp.where(li[None] < half, -si, 0.0))
    return jnp.stack(out, axis=0)


def _permute_w_in(w):
    offs = np.cumsum((0,) + IN_WIDTHS)
    nq, nkv, ng, dq, dkv, iq, ik, iw, cv, mg = [w[:, offs[i]:offs[i + 1]] for i in range(10)]
    z = lambda n: jnp.zeros((w.shape[0], n), w.dtype)
    cols = [nq, nq, nkv, dq, dkv, cv, mg, iq, ik, iw, ng, z(LANES - 92), z(NCOL - C_MISC - LANES)]
    return jnp.concatenate(cols, axis=1).astype(BF16)


def _conv_body(cv_ref, prev_ref, past_ref, w_ref, y_ref, cu_ref, s_ref, *, tiles_per_seq, tm):
    i = pl.program_id(0)
    b = cv_ref[:, 0:CONV_DIM]
    cu = cv_ref[:, CONV_DIM:2 * CONV_DIM] * cv_ref[:, 2 * CONV_DIM:3 * CONV_DIM]
    first = (i % tiles_per_seq) == 0
    prev = prev_ref[:, CONV_DIM:2 * CONV_DIM] * prev_ref[:, 2 * CONV_DIM:3 * CONV_DIM]
    s_ref[0:SUBLANES, :] = jnp.where(first, past_ref[0], prev)
    s_ref[SUBLANES:SUBLANES + tm, :] = cu
    y = (w_ref[0:1, :] * s_ref[pl.ds(SUBLANES - 2, tm), :] + w_ref[1:2, :] * s_ref[pl.ds(SUBLANES - 1, tm), :]
         + w_ref[2:3, :] * cu)
    y_ref[...] = b * y
    cu_ref[...] = cu


def _conv(hp, past8, conv_w8, tm, tiles_per_seq):
    n = hp.shape[0]
    cvb = C_CV // (3 * CONV_DIM)
    rb = tm // SUBLANES
    return pl.pallas_call(
        functools.partial(_conv_body, tiles_per_seq=tiles_per_seq, tm=tm),
        grid=(n // tm,),
        in_specs=[pl.BlockSpec((tm, 3 * CONV_DIM), lambda i: (i, cvb)),
                  pl.BlockSpec((SUBLANES, 3 * CONV_DIM), lambda i: (jnp.maximum(i * rb - 1, 0), cvb)),
                  pl.BlockSpec((1, SUBLANES, CONV_DIM), lambda i: (i // tiles_per_seq, 0, 0)),
                  pl.BlockSpec((SUBLANES, CONV_DIM), lambda i: (0, 0))],
        out_specs=[pl.BlockSpec((tm, CONV_DIM), lambda i: (i, 0)),
                   pl.BlockSpec((tm, CONV_DIM), lambda i: (i, 0))],
        out_shape=[jax.ShapeDtypeStruct((n, CONV_DIM), F32), jax.ShapeDtypeStruct((n, CONV_DIM), F32)],
        scratch_shapes=[pltpu.VMEM((tm + SUBLANES, CONV_DIM), F32)],
        compiler_params=_cparams("arbitrary"),
        name="short_conv",
    )(hp, hp, past8, conv_w8)


_PageLayout = collections.namedtuple("_PageLayout", "kind width col_block base layer nslot")


def _page_specs(layout, n_lead, kg_n=None):
    def spec(k):
        def imap(*a):
            ids, pt = a[:n_lead], a[-1]
            kg = ids[-1] % kg_n if kg_n else ids[-1]
            page = pt[ids[0], kg * PAGES_PER_STEP + k]
            if layout.kind == "cache":
                return (layout.layer, page, 0, 0, 0, 0)
            if layout.kind == "cache4":
                return (layout.layer, page, 0, 0)
            cb = layout.col_block(*ids) if callable(layout.col_block) else layout.col_block
            return (page, 0, cb)
        shape = {"cache": (1, 1, PAGE, layout.nslot, 2, HEAD_DIM), "cache4": (1, 1, PAGE, layout.width),
                 "cols": (1, PAGE, layout.width)}[layout.kind]
        return pl.BlockSpec(shape, imap)
    return [spec(k) for k in range(PAGES_PER_STEP)]


def _page_tile(layout, ref, j, width=LANES):
    if layout.kind == "cache":
        slot, g = divmod(layout.base + j, 2)
        return ref[0, 0, :, slot, g, :]
    if layout.kind == "cache4":
        return ref[0, 0, :, j * width:(j + 1) * width]
    return ref[0, :, j * width:(j + 1) * width]


def _chunk_rows(layout, ref, j, p):
    if layout.kind == "cache":
        slot, g = divmod(layout.base + j, 2)
        return ref[0, 0, pl.ds(p, SUBLANES, stride=CMP_STRIDE), slot, g, :]
    return ref[0, pl.ds(p, SUBLANES, stride=CMP_STRIDE), :]


def _cmp1_body(pt_ref, *refs, layout, n_inner):
    pages = refs[:PAGES_PER_STEP]
    w_ref, pe_ref = refs[PAGES_PER_STEP:PAGES_PER_STEP + 2]
    o_ref = refs[PAGES_PER_STEP + 2]
    for j in range(n_inner):
        slot = j // 2
        acc_a = jnp.zeros((PAGE, HEAD_DIM), F32)
        acc_b = jnp.zeros((PAGE, HEAD_DIM), F32)
        for p in range(CMP_STRIDE):
            xp = jnp.concatenate([_chunk_rows(layout, pg, j, p) for pg in pages], axis=0)
            xa = (xp + pe_ref[slot, p:p + 1, :]).astype(BF16)
            xb = (xp + pe_ref[slot, CMP_STRIDE + p:CMP_STRIDE + p + 1, :]).astype(BF16)
            acc_a = acc_a + jnp.dot(xa, w_ref[slot, p, :, 0:HEAD_DIM], preferred_element_type=F32)
            acc_b = acc_b + jnp.dot(xb, w_ref[slot, p, :, HEAD_DIM:2 * HEAD_DIM], preferred_element_type=F32)
        o_ref[0, :, j * 256:j * 256 + HEAD_DIM] = acc_a
        o_ref[0, :, j * 256 + HEAD_DIM:(j + 1) * 256] = acc_b


def _cmp1(pool, layout, pt, wcat, pe):
    nb, npg = pt.shape
    kg = npg // PAGES_PER_STEP
    if layout.kind == "cache":
        n_inner, n_sg, wsel, osel, ow = 4, 1, (lambda sg: 0), (lambda sg: 0), 1024
        wblk = 2
    else:
        cb = layout.col_block
        layout = layout._replace(width=HEAD_DIM, col_block=lambda b, sg, k: cb * 4 + sg)
        n_inner, n_sg, wsel, osel, ow = 1, 4, (lambda sg: sg // 2), (lambda sg: sg), 256
        wblk = 1
    return pl.pallas_call(
        functools.partial(_cmp1_body, layout=layout, n_inner=n_inner),
        grid_spec=pltpu.PrefetchScalarGridSpec(
            num_scalar_prefetch=1, grid=(nb, n_sg, kg),
            in_specs=_page_specs(layout, 3)
            + [pl.BlockSpec((wblk, 16, HEAD_DIM, 256), lambda b, sg, k, pt: (wsel(sg), 0, 0, 0)),
               pl.BlockSpec((wblk, CMP_LEN, HEAD_DIM), lambda b, sg, k, pt: (wsel(sg), 0, 0))],
            out_specs=pl.BlockSpec((1, PAGE, ow), lambda b, sg, k, pt: (b, k, osel(sg)))),
        out_shape=jax.ShapeDtypeStruct((nb, npg * SUBLANES, 1024), F32),
        compiler_params=_cparams("arbitrary", "arbitrary", "arbitrary"),
        name="nsa_compress1",
    )(pt, *([pool] * PAGES_PER_STEP), wcat, pe)


def _gelu_tanh(x):
    return 0.5 * x * (1.0 + jnp.tanh(np.sqrt(2.0 / np.pi).astype(np.float32) * (x + 0.044715 * (x * x * x))))


def _cmp2_body(ab_ref, tail_ref, pe_ref, w1_ref, w2_ref, o_ref, s_ref, *, nc, n_tail):
    row8 = lax.broadcasted_iota(jnp.int32, (SUBLANES, 1), 0)
    row16 = lax.broadcasted_iota(jnp.int32, (CMP_STRIDE, 1), 0)
    for sg in range(4):
        slot = sg // 2
        a = ab_ref[0, :, sg * 256:sg * 256 + HEAD_DIM]
        s_ref[0:nc, :] = ab_ref[0, :, sg * 256 + HEAD_DIM:(sg + 1) * 256]
        tb = jnp.zeros((SUBLANES, HEAD_DIM), F32)
        if n_tail:
            x8 = jnp.where(row8 < n_tail, tail_ref[:, sg * HEAD_DIM:(sg + 1) * HEAD_DIM], 0.0)
            x16 = jnp.concatenate([x8, jnp.zeros((CMP_STRIDE - SUBLANES, HEAD_DIM), F32)], axis=0)
            x16 = x16 + pe_ref[slot, CMP_STRIDE:CMP_LEN, :]
            t16 = jnp.zeros((CMP_STRIDE, HEAD_DIM), F32)
            for p in range(CMP_STRIDE):
                xm = jnp.where(row16 == p, x16, 0.0).astype(BF16)
                t16 = t16 + jnp.dot(xm, w1_ref[slot, CMP_STRIDE + p].astype(BF16), preferred_element_type=F32)
            tb = jnp.sum(t16, axis=0, keepdims=True) * jnp.where(row8 == 0, 1.0, 0.0)
        s_ref[nc:nc + SUBLANES, :] = tb
        pre = a + s_ref[pl.ds(1, nc), :]
        o_ref[0, :, sg * HEAD_DIM:(sg + 1) * HEAD_DIM] = jnp.dot(
            _gelu_tanh(pre).astype(BF16), w2_ref[slot].astype(BF16), preferred_element_type=F32)


def _cmp2(ab, tail, pe, w1r, w2, n_tail, tail_col_block):
    nb, nc, _ = ab.shape
    if tail is None:
        tail = jnp.zeros((nb * SUBLANES, 512), F32)
        tail_col_block = 0
    return pl.pallas_call(
        functools.partial(_cmp2_body, nc=nc, n_tail=n_tail),
        grid=(nb,),
        in_specs=[pl.BlockSpec((1, nc, 1024), lambda b: (b, 0, 0)),
                  pl.BlockSpec((SUBLANES, 512), lambda b: (b, tail_col_block)),
                  pl.BlockSpec((2, CMP_LEN, HEAD_DIM), lambda b: (0, 0, 0)),
                  pl.BlockSpec((2, CMP_LEN, HEAD_DIM, HEAD_DIM), lambda b: (0, 0, 0, 0)),
                  pl.BlockSpec((2, HEAD_DIM, HEAD_DIM), lambda b: (0, 0, 0))],
        out_specs=pl.BlockSpec((1, nc, 512), lambda b: (b, 0, 0)),
        out_shape=jax.ShapeDtypeStruct((nb, nc, 512), F32),
        scratch_shapes=[pltpu.VMEM((nc + SUBLANES, HEAD_DIM), F32)],
        compiler_params=_cparams("arbitrary"),
        name="nsa_compress2",
    )(ab, tail, pe, w1r, w2)


def _nsa1_body(*refs, tq, q0, n_cmp, ncp, n_sel, nselp, lpad, nwb, has_tail, win_k0):
    qraw_ref, qrot_ref, cmp_ref, ov_ref = refs[:4]
    wins = refs[4:4 + nwb]
    pos = 4 + nwb
    tail_ref = None
    if has_tail:
        tail_ref = refs[pos]
        pos += 1
    ocmp_ref, owin_ref, kmask_ref = refs[pos:pos + 3]
    qt = pl.program_id(1)
    r_per_g = NSA_HEADS // NSA_KV
    qpos = q0 + qt * tq + lax.broadcasted_iota(jnp.int32, (tq, 1), 0)

    jj = lax.broadcasted_iota(jnp.int32, (1, ncp), 1)
    cmask = (jj * CMP_STRIDE + (CMP_LEN - 1) <= qpos) & (jj < n_cmp)
    blk = lax.broadcasted_iota(jnp.int32, (1, nselp), 1)
    cur = qpos // SEL_BLOCK
    forced = (blk == 0) | (blk == cur) | (blk == cur - 1)
    e_row = lax.broadcasted_iota(jnp.int32, (KSTEP // SEL_BLOCK, KSTEP), 0)
    e_col = lax.broadcasted_iota(jnp.int32, (KSTEP // SEL_BLOCK, KSTEP), 1)
    expand = jnp.where(e_col // SEL_BLOCK == e_row, 1.0, 0.0).astype(BF16)
    lane128 = lax.broadcasted_iota(jnp.int32, (1, LANES), 1)
    for g in range(NSA_KV):
        kc = cmp_ref[0, :, g * HEAD_DIM:(g + 1) * HEAD_DIM].astype(BF16)
        vc = cmp_ref[0, :, (2 + g) * HEAD_DIM:(3 + g) * HEAD_DIM].astype(BF16)
        psum = jnp.zeros((tq, ncp), F32)
        for r in range(r_per_g):
            h = g * r_per_g + r
            q = qraw_ref[:, h * HEAD_DIM:(h + 1) * HEAD_DIM].astype(BF16)
            s = jnp.where(cmask, _dot_nt(q, kc) * ATT_SCALE, NEG_INF)
            p, den = _softmax_parts(s)
            p = p / den
            psum = psum + p
            ocmp_ref[:, h * HEAD_DIM:(h + 1) * HEAD_DIM] = jnp.dot(p.astype(BF16), vc, preferred_element_type=F32)
        imp = jnp.dot(psum.astype(BF16), ov_ref[...].astype(BF16), preferred_element_type=F32)
        imp = jnp.where(forced, imp + FORCE_SCORE, imp)
        imp = jnp.where((blk <= cur) & (blk < n_sel), imp, NEG_INF)
        rank = jnp.zeros((tq, nselp), F32)
        for i in range(n_sel):
            vi = imp[:, i:i + 1]
            beats = (vi > imp) | ((vi == imp) & (blk > i))
            rank = rank + jnp.where(beats, 1.0, 0.0)
        sel = jnp.where((rank < min(SEL_TOP, n_sel)) & (blk < n_sel), 1.0, 0.0).astype(BF16)
        per = KSTEP // SEL_BLOCK
        for c in range(lpad // KSTEP):
            km = jnp.dot(sel[:, c * per:(c + 1) * per], expand, preferred_element_type=F32)
            kmask_ref[0, g, :, c * KSTEP:(c + 1) * KSTEP] = km.astype(BF16)
        if lpad % KSTEP:
            b0 = (lpad // KSTEP) * per
            km = jnp.where(lane128 < SEL_BLOCK, sel[:, b0:b0 + 1].astype(F32), 0.0)
            kmask_ref[0, g, :, (lpad // KSTEP) * KSTEP:lpad] = jnp.broadcast_to(km, (tq, LANES)).astype(BF16)

    nk = nwb * PAGE + (LANES if has_tail else 0)
    kk = lax.broadcasted_iota(jnp.int32, (1, nk), 1)
    if has_tail:
        kpos = jnp.where(kk < nwb * PAGE, win_k0 + kk, q0 + kk - nwb * PAGE)
    else:
        kpos = (qt - (nwb - 1)) * PAGE + kk
    rel = qpos - kpos
    wmask = (rel >= 0) & (rel < WINDOW) & (kpos >= 0)
    for g in range(NSA_KV):
        if has_tail:
            kparts = [w[0, 0, :, 0, g, :] for w in wins]
            vparts = [w[0, 0, :, 1, g, :] for w in wins]
        else:
            kparts = [w[:, g * HEAD_DIM:(g + 1) * HEAD_DIM] for w in wins]
            vparts = [w[:, (2 + g) * HEAD_DIM:(3 + g) * HEAD_DIM] for w in wins]
        if has_tail:
            zpad = jnp.zeros((LANES - SUBLANES, HEAD_DIM), F32)
            kparts += [tail_ref[:, g * HEAD_DIM:(g + 1) * HEAD_DIM], zpad]
            vparts += [tail_ref[:, (2 + g) * HEAD_DIM:(3 + g) * HEAD_DIM], zpad]
        kw = jnp.concatenate(kparts, axis=0).astype(BF16)
        vw = jnp.concatenate(vparts, axis=0).astype(BF16)
        for r in range(r_per_g):
            h = g * r_per_g + r
            q = qrot_ref[:, h * HEAD_DIM:(h + 1) * HEAD_DIM].astype(BF16)
            s = jnp.where(wmask, _dot_nt(q, kw) * ATT_SCALE, NEG_INF)
            p, den = _softmax_parts(s)
            p = p / den
            owin_ref[:, h * HEAD_DIM:(h + 1) * HEAD_DIM] = jnp.dot(p.astype(BF16), vw, preferred_element_type=F32)


def _nsa1(hp, cmp, ov, win_src, tail_src, *, nb, tq, qt_n, q0, n_cmp, n_sel, lpad, nwb, win_k0, win_layer):
    n = hp.shape[0]
    ncp = cmp.shape[1]
    nselp = ov.shape[1]
    has_tail = tail_src is not None
    rows = lambda b, t: b * qt_n + t
    in_specs = [pl.BlockSpec((tq, 1024), lambda b, t: (rows(b, t), C_QRAW // 1024)),
                pl.BlockSpec((tq, 1024), lambda b, t: (rows(b, t), C_QROT // 1024)),
                pl.BlockSpec((1, ncp, 512), lambda b, t: (b, 0, 0)),
                pl.BlockSpec((ncp, nselp), lambda b, t: (0, 0))]
    if has_tail:
        in_specs += [pl.BlockSpec((1, 1, PAGE, 2, NSA_KV, HEAD_DIM), lambda b, t, k=k: (win_layer, b, k, 0, 0, 0))
                     for k in range(nwb)]
        in_specs += [pl.BlockSpec((SUBLANES, 512), lambda b, t: (b, C_WIN // 512))]
        args = [win_src] * nwb + [tail_src]
    else:
        in_specs += [pl.BlockSpec((PAGE, 512),
                                  lambda b, t, k=k: (b * qt_n + jnp.maximum(t - (nwb - 1) + k, 0), C_WIN // 512))
                     for k in range(nwb)]
        args = [win_src] * nwb
    return pl.pallas_call(
        functools.partial(_nsa1_body, tq=tq, q0=q0, n_cmp=n_cmp, ncp=ncp, n_sel=n_sel, nselp=nselp, lpad=lpad,
                          nwb=nwb, has_tail=has_tail, win_k0=win_k0),
        grid=(nb, qt_n),
        in_specs=in_specs,
        out_specs=[pl.BlockSpec((tq, 1024), lambda b, t: (rows(b, t), 0)),
                   pl.BlockSpec((tq, 1024), lambda b, t: (rows(b, t), 0)),
                   pl.BlockSpec((1, NSA_KV, tq, lpad), lambda b, t: (b, 0, t, 0))],
        out_shape=[jax.ShapeDtypeStruct((n, 1024), F32), jax.ShapeDtypeStruct((n, 1024), F32),
                   jax.ShapeDtypeStruct((nb, NSA_KV, qt_n * tq, lpad), BF16)],
        compiler_params=_cparams("arbitrary", "arbitrary"),
        name="nsa_cmp_select_window",
    )(hp, hp, cmp, ov, *args)


def _mattn_body(pt_ref, *refs, layout, n_g, r_per_g, gm, tq, q0, kg_n, has_tail):
    q_ref, mask_ref = refs[:2]
    pages = refs[2:2 + PAGES_PER_STEP]
    pos = 2 + PAGES_PER_STEP
    if has_tail:
        tail_ref, tmask_ref = refs[pos:pos + 2]
        pos += 2
    o_ref, m_ref, l_ref, acc_ref = refs[pos:pos + 4]
    qt = pl.program_id(1)
    step = pl.program_id(2)
    two_pass = kg_n > 1 or has_tail
    kg = step % kg_n if two_pass else step
    qpos = q0 + qt * tq + lax.broadcasted_iota(jnp.int32, (tq, 1), 0)

    def scores(h, g, kt, kpos, mref):
        valid = (mref[0, g if gm > 1 else 0] > 0.5) & (kpos <= qpos)
        q = q_ref[:, h * HEAD_DIM:(h + 1) * HEAD_DIM].astype(BF16)
        return jnp.where(valid, _dot_nt(q, kt) * ATT_SCALE, NEG_INF)

    def main_keys(g):
        return jnp.concatenate([_page_tile(layout, pg, g) for pg in pages], axis=0).astype(BF16)

    def main_vals(g):
        return jnp.concatenate([_page_tile(layout, pg, n_g + g) for pg in pages], axis=0).astype(BF16)

    def tail_part(g, off):
        zpad = jnp.zeros((LANES - SUBLANES, HEAD_DIM), F32)
        return jnp.concatenate([tail_ref[:, (off + g) * HEAD_DIM:(off + g + 1) * HEAD_DIM], zpad], axis=0).astype(BF16)

    kpos = kg * KSTEP + lax.broadcasted_iota(jnp.int32, (1, KSTEP), 1)
    tpos = q0 + lax.broadcasted_iota(jnp.int32, (1, LANES), 1)

    if not two_pass:
        for g in range(n_g):
            kt, vt = main_keys(g), main_vals(g)
            for r in range(r_per_g):
                h = g * r_per_g + r
                p, den = _softmax_parts(scores(h, g, kt, kpos, mask_ref))
                o_ref[:, h * HEAD_DIM:(h + 1) * HEAD_DIM] = jnp.dot((p / den).astype(BF16), vt,
                                                                   preferred_element_type=F32)
        return

    @pl.when(step == 0)
    def _():
        m_ref[...] = jnp.full(m_ref.shape, NEG_INF, F32)
        l_ref[...] = jnp.zeros(l_ref.shape, F32)
        acc_ref[...] = jnp.zeros(acc_ref.shape, F32)

    def stats(h, s):
        m_old = m_ref[h]
        m_new = jnp.maximum(m_old, jnp.max(s, axis=-1, keepdims=True))
        m_safe = jnp.where(m_new == NEG_INF, 0.0, m_new)
        l_ref[h] = jnp.exp(m_old - m_safe) * l_ref[h] + jnp.sum(jnp.exp(s - m_safe), axis=-1, keepdims=True)
        m_ref[h] = m_new

    def accumulate(h, s, vt):
        m = m_ref[h]
        m_safe = jnp.where(m == NEG_INF, 0.0, m)
        p = jnp.exp(s - m_safe) / jnp.maximum(l_ref[h], 1e-30)
        sl = slice(h * HEAD_DIM, (h + 1) * HEAD_DIM)
        acc_ref[:, sl] += jnp.dot(p.astype(BF16), vt, preferred_element_type=F32)

    @pl.when(step < kg_n)
    def _():
        for g in range(n_g):
            kt = main_keys(g)
            for r in range(r_per_g):
                h = g * r_per_g + r
                stats(h, scores(h, g, kt, kpos, mask_ref))

    if has_tail:
        @pl.when(step == kg_n - 1)
        def _():
            for g in range(n_g):
                kt = tail_part(g, 0)
                for r in range(r_per_g):
                    h = g * r_per_g + r
                    stats(h, scores(h, g, kt, tpos, tmask_ref))

    @pl.when(step >= kg_n)
    def _():
        for g in range(n_g):
            kt, vt = main_keys(g), main_vals(g)
            for r in range(r_per_g):
                h = g * r_per_g + r
                accumulate(h, scores(h, g, kt, kpos, mask_ref), vt)

    @pl.when(step == 2 * kg_n - 1)
    def _():
        if has_tail:
            for g in range(n_g):
                kt, vt = tail_part(g, 0), tail_part(g, n_g)
                for r in range(r_per_g):
                    h = g * r_per_g + r
                    accumulate(h, scores(h, g, kt, tpos, tmask_ref), vt)
        o_ref[...] = acc_ref[...]


def _mattn(hp, q_col, n_heads, mask, pool, layout, pt, tail_src, tail_col_block, *, nb, tq, qt_n, q0, name):
    n = hp.shape[0]
    n_g = 2
    r_per_g = n_heads // n_g
    qw = n_heads * HEAD_DIM
    gm = mask.shape[1]
    kg_n = pt.shape[1] // PAGES_PER_STEP
    has_tail = tail_src is not None
    n_steps = 2 * kg_n if (kg_n > 1 or has_tail) else 1
    in_specs = [pl.BlockSpec((tq, qw), lambda b, t, k, pt: (b * qt_n + t, q_col // qw)),
                pl.BlockSpec((1, gm, tq, KSTEP), lambda b, t, k, pt: (b, 0, t, k % kg_n))]
    in_specs += _page_specs(layout, 3, kg_n)
    args = [hp, mask] + [pool] * PAGES_PER_STEP
    if has_tail:
        in_specs += [pl.BlockSpec((SUBLANES, 512), lambda b, t, k, pt: (b, tail_col_block)),
                     pl.BlockSpec((1, gm, tq, LANES), lambda b, t, k, pt: (b, 0, t, kg_n * KSTEP // LANES))]
        args += [tail_src, mask]
    return pl.pallas_call(
        functools.partial(_mattn_body, layout=layout, n_g=n_g, r_per_g=r_per_g, gm=gm, tq=tq, q0=q0, kg_n=kg_n, has_tail=has_tail),
        grid_spec=pltpu.PrefetchScalarGridSpec(
            num_scalar_prefetch=1, grid=(nb, qt_n, n_steps),
            in_specs=in_specs,
            out_specs=pl.BlockSpec((tq, qw), lambda b, t, k, pt: (b * qt_n + t, 0)),
            scratch_shapes=[pltpu.VMEM((n_heads, tq, 1), F32), pltpu.VMEM((n_heads, tq, 1), F32),
                            pltpu.VMEM((tq, qw), F32)]),
        out_shape=jax.ShapeDtypeStruct((n, qw), F32),
        compiler_params=_cparams("arbitrary", "arbitrary", "arbitrary"),
        name=name,
    )(pt, *args)


PACKED_ROWS = 2 * SUBLANES


def _topk_mask_packed_t(sc_t, n_keep):
    n_keys, n_q = sc_t.shape
    i16 = jnp.int16
    bits = pltpu.bitcast(sc_t + 0.0, jnp.int32)
    key = jnp.where(bits < 0, bits ^ jnp.int32(0x7FFFFFFF), bits)
    hi = jnp.right_shift(key, 16).astype(i16)
    lo = ((key & jnp.int32(0xFFFF)) - 32768).astype(i16)
    one_b, zero_b = jnp.asarray(1, BF16), jnp.asarray(0, BF16)
    i16_min, i16_max = jnp.asarray(-32768, i16), jnp.asarray(32767, i16)
    assert n_keys % PACKED_ROWS == 0 and n_keys // PACKED_ROWS <= 256

    def colsum(x01):
        parts = [x01[i * PACKED_ROWS:(i + 1) * PACKED_ROWS, :] for i in range(n_keys // PACKED_ROWS)]
        while len(parts) > 1:
            parts = [parts[i] + parts[i + 1] for i in range(0, len(parts), 2)]
        return jnp.sum(parts[0].astype(F32), axis=0, keepdims=True)

    def search(vals, need, nbits, start):
        def body(i, thr):
            cand = thr + jnp.left_shift(jnp.int32(1), nbits - 1 - i)
            cnt = colsum(jnp.where(vals >= cand.astype(i16), one_b, zero_b))
            return jnp.where(cnt >= need, cand, thr)
        return lax.fori_loop(0, nbits, body, jnp.full((1, n_q), start, jnp.int32))

    t_hi = search(hi, float(n_keep), 16, -32768).astype(i16)
    eq = hi == t_hi
    hi_gt = jnp.where(hi > t_hi, one_b, zero_b)
    lo_m = jnp.where(eq, lo, i16_min)
    t_lo = search(lo_m, n_keep - colsum(hi_gt), 16, -32768).astype(i16)
    gt = hi_gt + jnp.where(lo_m > t_lo, one_b, zero_b)
    need = n_keep - colsum(gt)
    idx = lax.broadcasted_iota(jnp.int32, (n_keys, 1), 0).astype(i16)
    tie_idx = jnp.where(eq, jnp.where(lo == t_lo, idx, i16_max), i16_max)
    nbits = int(np.ceil(np.log2(n_keys)))

    def ibody(i, c):
        cand = c + jnp.left_shift(jnp.int32(1), nbits - 1 - i)
        cnt = colsum(jnp.where(tie_idx < cand.astype(i16), one_b, zero_b))
        return jnp.where(cnt < need, cand, c)

    cut = lax.fori_loop(0, nbits, ibody, jnp.zeros((1, n_q), jnp.int32)).astype(i16)
    return jnp.maximum(gt, jnp.where(tie_idx <= cut, one_b, zero_b))


def _dsa1_body(pt_ref, *refs, layout, tq, q0, kg_n, lpad, has_tail, n_keep):
    iq_ref, misc_ref = refs[:2]
    pages = refs[2:2 + PAGES_PER_STEP]
    pos = 2 + PAGES_PER_STEP
    if has_tail:
        tail_ref = refs[pos]
        pos += 1
    mask_ref, sc_ref = refs[pos:pos + 2]
    qt = pl.program_id(1)
    kg = pl.program_id(2)
    qpos = q0 + qt * tq + lax.broadcasted_iota(jnp.int32, (tq, 1), 0)
    lane128 = lax.broadcasted_iota(jnp.int32, (1, LANES), 1)

    if kg_n == 1 and not has_tail and tq == LANES:
        kt = jnp.concatenate([_page_tile(layout, pg, 0, IDX_DIM) for pg in pages], axis=0).astype(BF16)
        misc_t = misc_ref[...].T
        kpos_c = lax.broadcasted_iota(jnp.int32, (KSTEP, 1), 0)
        qpos_r = q0 + qt * tq + lax.broadcasted_iota(jnp.int32, (1, tq), 1)
        acc = jnp.zeros((KSTEP, tq), F32)
        for h in range(IDX_HEADS):
            qi = iq_ref[:, h * IDX_DIM:(h + 1) * IDX_DIM].astype(BF16)
            acc = acc + misc_t[MISC_IW + h:MISC_IW + h + 1, :] * jnp.maximum(_dot_nt(kt, qi), 0.0)
        sel_t = _topk_mask_packed_t(jnp.where(kpos_c <= qpos_r, acc, NEG_INF), n_keep)
        for c in range(KSTEP // LANES):
            blk = sel_t[c * LANES:(c + 1) * LANES, :].astype(F32).T
            mask_ref[0, 0, :, c * LANES:(c + 1) * LANES] = blk.astype(BF16)
        return

    def scores(kt, kpos):
        acc = jnp.zeros((tq, kt.shape[0]), F32)
        for h in range(IDX_HEADS):
            qi = iq_ref[:, h * IDX_DIM:(h + 1) * IDX_DIM].astype(BF16)
            w = jnp.sum(jnp.where(lane128 == MISC_IW + h, misc_ref[...], 0.0), axis=-1, keepdims=True)
            acc = acc + w * jnp.maximum(_dot_nt(qi, kt), 0.0)
        return jnp.where(kpos <= qpos, acc, NEG_INF)

    kt = jnp.concatenate([_page_tile(layout, pg, 0, IDX_DIM) for pg in pages], axis=0).astype(BF16)
    kpos = kg * KSTEP + lax.broadcasted_iota(jnp.int32, (1, KSTEP), 1)
    sc_ref[kg] = scores(kt, kpos)

    @pl.when(kg == kg_n - 1)
    def _():
        nch = sc_ref.shape[0]
        if has_tail:
            zpad = jnp.zeros((KSTEP - SUBLANES, IDX_DIM), F32)
            tk = jnp.concatenate([tail_ref[:, 0:IDX_DIM], zpad], axis=0).astype(BF16)
            tl = lax.broadcasted_iota(jnp.int32, (1, KSTEP), 1)
            sc_ref[kg_n] = jnp.where(tl < LANES, scores(tk, q0 + tl), NEG_INF)
        sc = sc_ref[...] + 0.0
        bits = pltpu.bitcast(sc, jnp.int32)
        key = jnp.where(bits < 0, bits ^ jnp.int32(0x7FFFFFFF), bits)
        int_min = jnp.int32(-2 ** 31)

        def count(pred):
            return jnp.sum(jnp.sum(jnp.where(pred, 1.0, 0.0), axis=-1, keepdims=True), axis=0, keepdims=True)

        def vbit(i, thr):
            cand = thr + jnp.left_shift(jnp.int32(1), 31 - i)
            return jnp.where(count(key >= cand) >= n_keep, cand, thr)

        thr = lax.fori_loop(0, 32, vbit, jnp.full((1, tq, 1), int_min, jnp.int32))
        gt = key > thr
        tie = key == thr
        need = n_keep - count(gt)
        idx = (lax.broadcasted_iota(jnp.int32, (nch, 1, KSTEP), 0) * KSTEP
               + lax.broadcasted_iota(jnp.int32, (nch, 1, KSTEP), 2))
        nbits = int(np.ceil(np.log2(nch * KSTEP)))

        def ibit(i, c):
            cand = c + jnp.left_shift(jnp.int32(1), nbits - 1 - i)
            return jnp.where(count(tie & (idx < cand)) < need, cand, c)

        cut = lax.fori_loop(0, nbits, ibit, jnp.zeros((1, tq, 1), jnp.int32))
        sel = jnp.where(gt | (tie & (idx <= cut)), 1.0, 0.0).astype(BF16)
        for c in range(kg_n):
            mask_ref[0, 0, :, c * KSTEP:(c + 1) * KSTEP] = sel[c]
        if has_tail:
            mask_ref[0, 0, :, kg_n * KSTEP:lpad] = sel[kg_n][:, 0:lpad - kg_n * KSTEP]


def _dsa1(hp, pool, layout, pt, tail_src, *, nb, tq, qt_n, q0, lpad, n_keep):
    kg_n = pt.shape[1] // PAGES_PER_STEP
    has_tail = tail_src is not None
    in_specs = [pl.BlockSpec((tq, 256), lambda b, t, k, pt: (b * qt_n + t, C_IQ // 256)),
                pl.BlockSpec((tq, LANES), lambda b, t, k, pt: (b * qt_n + t, C_MISC // LANES))]
    in_specs += _page_specs(layout, 3)
    args = [hp, hp] + [pool] * PAGES_PER_STEP
    if has_tail:
        in_specs += [pl.BlockSpec((SUBLANES, LANES), lambda b, t, k, pt: (b, C_MISC // LANES))]
        args += [tail_src]
    return pl.pallas_call(
        functools.partial(_dsa1_body, layout=layout, tq=tq, q0=q0, kg_n=kg_n, lpad=lpad, has_tail=has_tail,
                          n_keep=n_keep),
        grid_spec=pltpu.PrefetchScalarGridSpec(
            num_scalar_prefetch=1, grid=(nb, qt_n, kg_n),
            in_specs=in_specs,
            out_specs=pl.BlockSpec((1, 1, tq, lpad), lambda b, t, k, pt: (b, 0, t, 0)),
            scratch_shapes=[pltpu.VMEM((kg_n + (1 if has_tail else 0), tq, KSTEP), F32)]),
        out_shape=jax.ShapeDtypeStruct((nb, 1, qt_n * tq, lpad), BF16),
        compiler_params=_cparams("arbitrary", "arbitrary", "arbitrary"),
        name="dsa_indexer_topk",
    )(pt, *args)


def _mix_body(ocmp_ref, osel_ref, owin_ref, odsa_ref, conv_ref, misc_ref, mg0_ref, mg1_ref, mg2_ref,
              wa_ref, wb_ref, wc_ref, z_ref):
    lane128 = lax.broadcasted_iota(jnp.int32, (1, LANES), 1)
    misc = misc_ref[...]

    def gate(kind, h):
        return jnp.sum(jnp.where(lane128 == MISC_NG + kind * NSA_HEADS + h, misc, 0.0), axis=-1, keepdims=True)

    parts = []
    for h in range(NSA_HEADS):
        sl = slice(h * HEAD_DIM, (h + 1) * HEAD_DIM)
        parts.append((gate(0, h) * ocmp_ref[:, sl] + gate(1, h) * osel_ref[:, sl]
                      + gate(2, h) * owin_ref[:, sl]).astype(BF16))
    o_nsa = jnp.concatenate(parts, axis=1)
    p_a = jnp.dot(o_nsa, wa_ref[...], preferred_element_type=F32)
    p_b = jnp.dot(odsa_ref[...].astype(BF16), wb_ref[...], preferred_element_type=F32)
    p_c = jnp.dot(conv_ref[...].astype(BF16), wc_ref[...], preferred_element_type=F32)
    z_ref[...] = (mg0_ref[...] * p_a + mg1_ref[...] * p_b + mg2_ref[...] * p_c).astype(BF16)


def _mix(hp, o_cmp, o_sel, o_win, o_dsa, conv_out, wa, wb, wc, tm):
    n = hp.shape[0]
    row = lambda w, cb=0: pl.BlockSpec((tm, w), lambda i: (i, cb))
    full = lambda a: pl.BlockSpec(a.shape, lambda i: (0, 0))
    return pl.pallas_call(
        _mix_body,
        grid=(n // tm,),
        in_specs=[row(1024), row(1024), row(1024), row(512), row(512), row(LANES, C_MISC // LANES),
                  row(D_MODEL, C_MG // D_MODEL), row(D_MODEL, C_MG // D_MODEL + 1), row(D_MODEL, C_MG // D_MODEL + 2),
                  full(wa), full(wb), full(wc)],
        out_specs=row(D_MODEL),
        out_shape=jax.ShapeDtypeStruct((n, D_MODEL), BF16),
        compiler_params=_cparams("arbitrary"),
        name="branch_merge",
    )(o_cmp, o_sel, o_win, o_dsa, conv_out, hp, hp, hp, hp, wa, wb, wc)


def _outln_body(z_ref, wo_ref, x_ref, g_ref, b_ref, wr_ref, rb_ref, x1_ref, gate_ref, *, tm):
    y = jnp.dot(z_ref[...], wo_ref[...], preferred_element_type=F32)
    x1 = _layernorm(ALPHA * x_ref[...] + y, g_ref[...], b_ref[...])
    x1_ref[...] = x1
    aff = _sigmoid(_dot_nt(wr_ref[...], x1, precision=HIGHEST))
    biased = aff + rb_ref[:, 0:1]
    rows = [biased[e:e + 1, :] for e in range(N_EXPERTS)]
    best = None
    g_best = jnp.zeros((1, tm), jnp.int32)
    for g in range(N_GROUPS):
        v = rows[g * EXPERTS_PER_GROUP:(g + 1) * EXPERTS_PER_GROUP]
        score = None
        for a in range(EXPERTS_PER_GROUP):
            for c in range(a + 1, EXPERTS_PER_GROUP):
                pair = v[a] + v[c]
                score = pair if score is None else jnp.maximum(score, pair)
        if best is None:
            best = score
        else:
            better = score > best
            best = jnp.where(better, score, best)
            g_best = jnp.where(better, g, g_best)
    sel_rows = []
    for e in range(N_EXPERTS):
        g = e // EXPERTS_PER_GROUP
        rank = jnp.zeros((1, tm), F32)
        for o in range(g * EXPERTS_PER_GROUP, (g + 1) * EXPERTS_PER_GROUP):
            if o == e:
                continue
            beats = (rows[o] > rows[e]) | ((rows[o] == rows[e]) & (o < e))
            rank = rank + jnp.where(beats, 1.0, 0.0)
        sel_rows.append(jnp.where((g_best == g) & (rank < 2), aff[e:e + 1, :], 0.0))
    tot = sel_rows[0]
    for e in range(1, N_EXPERTS):
        tot = tot + sel_rows[e]
    gate_t = jnp.concatenate(sel_rows + [jnp.zeros((LANES - N_EXPERTS, tm), F32)], axis=0) / tot
    gate_ref[...] = gate_t.T


def _outln(z, wo, x, g, b, wr_t, rb, tm):
    n = z.shape[0]
    row = lambda w: pl.BlockSpec((tm, w), lambda i: (i, 0))
    full = lambda a: pl.BlockSpec(a.shape, lambda i: (0, 0))
    return pl.pallas_call(
        functools.partial(_outln_body, tm=tm),
        grid=(n // tm,),
        in_specs=[row(D_MODEL), full(wo), row(D_MODEL), full(g), full(b), full(wr_t), full(rb)],
        out_specs=[row(D_MODEL), row(LANES)],
        out_shape=[jax.ShapeDtypeStruct((n, D_MODEL), F32), jax.ShapeDtypeStruct((n, LANES), F32)],
        compiler_params=_cparams("arbitrary"),
        name="out_proj_ln_router",
    )(z, wo, x, g, b, wr_t, rb)


def _moe_body(x_ref, gate_ref, wg_ref, wu_ref, wd_ref, g_ref, b_ref, o_ref, xb_ref, acc_ref):
    e = pl.program_id(1)

    @pl.when(e == 0)
    def _():
        xb_ref[...] = x_ref[...].astype(BF16)
        acc_ref[...] = jnp.zeros(acc_ref.shape, F32)

    lane128 = lax.broadcasted_iota(jnp.int32, (1, LANES), 1)
    gcol = jnp.sum(jnp.where(lane128 == e, gate_ref[...], 0.0), axis=-1, keepdims=True)
    xb = xb_ref[...]
    hg = jnp.dot(xb, wg_ref[0], preferred_element_type=F32)
    hu = jnp.dot(xb, wu_ref[0], preferred_element_type=F32)
    h = (hg * _sigmoid(hg)) * hu * gcol
    acc_ref[...] += jnp.dot(h.astype(BF16), wd_ref[0], preferred_element_type=F32)

    @pl.when(e == N_EXPERTS - 1)
    def _():
        o_ref[...] = _layernorm(ALPHA * x_ref[...] + acc_ref[...], g_ref[...], b_ref[...])


def _moe(x1, gate, wg, wu, wd, g, b, tm):
    n = x1.shape[0]
    return pl.pallas_call(
        _moe_body,
        grid=(n // tm, N_EXPERTS),
        in_specs=[pl.BlockSpec((tm, D_MODEL), lambda i, e: (i, 0)),
                  pl.BlockSpec((tm, LANES), lambda i, e: (i, 0)),
                  pl.BlockSpec((1, D_MODEL, D_FF), lambda i, e: (e, 0, 0)),
                  pl.BlockSpec((1, D_MODEL, D_FF), lambda i, e: (e, 0, 0)),
                  pl.BlockSpec((1, D_FF, D_MODEL), lambda i, e: (e, 0, 0)),
                  pl.BlockSpec((1, D_MODEL), lambda i, e: (0, 0)),
                  pl.BlockSpec((1, D_MODEL), lambda i, e: (0, 0))],
        out_specs=pl.BlockSpec((tm, D_MODEL), lambda i, e: (i, 0)),
        out_shape=jax.ShapeDtypeStruct((n, D_MODEL), F32),
        scratch_shapes=[pltpu.VMEM((tm, D_MODEL), BF16), pltpu.VMEM((tm, D_MODEL), F32)],
        compiler_params=_cparams("arbitrary", "arbitrary"),
        name="moe_ln",
    )(x1, gate, wg, wu, wd, g, b)


def _overlap_matrix(ncp, nselp):
    cs = np.arange(ncp)[:, None] * CMP_STRIDE
    ss = np.arange(nselp)[None, :] * SEL_BLOCK
    return jnp.asarray(((cs < ss + SEL_BLOCK) & (cs + CMP_LEN > ss)).astype(np.float32))


class _Group:
    def __init__(self, nb, t_real, t_pad, q0, past_len, tm, tq):
        self.nb, self.t_real, self.t_pad, self.q0, self.past_len, self.tm, self.tq = nb, t_real, t_pad, q0, past_len, tm, tq
        self.paged = past_len > 0
        self.lp = past_len if self.paged else t_pad
        self.ltot = self.lp + (t_real if self.paged else 0)
        self.lpad = self.lp + (LANES if self.paged else 0)
        self.qt_n = t_pad // tq
        n_chunks = -(-self.ltot // CMP_STRIDE)
        self.n_cmp = n_chunks - CMP_LEN // CMP_STRIDE + 1
        self.n_sel = -(-self.ltot // SEL_BLOCK)
        self.nselp = -(-self.n_sel // LANES) * LANES
        self.n_keep = min(DSA_TOPK, self.ltot // 4)


def _mixer(gp, x, lw, caches):
    hp = _proj(x.astype(BF16), lw["w_in"], gp.tabs, lw["kinds"], gp.tm)
    nb, tq, qt_n, q0 = gp.nb, gp.tq, gp.qt_n, gp.q0
    if gp.paged:
        l, nsa_pool, dsa_pool, kidx_pool, win_src, s_conv, pt = caches
        cmp_lay = _PageLayout("cache", 0, 0, base=0, layer=l, nslot=4)
        slc_lay = _PageLayout("cache", 0, 0, base=4, layer=l, nslot=4)
        dsa_lay = _PageLayout("cache", 0, 0, base=0, layer=l, nslot=2)
        kidx_lay = _PageLayout("cache4", IDX_DIM, 0, base=0, layer=l, nslot=0)
        tail = hp
        nwb, win_k0, win_layer = WINDOW // PAGE, PAST_LEN - WINDOW, l
        past8 = jnp.concatenate([jnp.zeros((nb, SUBLANES - 2, CONV_DIM), F32), s_conv[l]], axis=1)
    else:
        pt = gp.pt
        nsa_pool = dsa_pool = kidx_pool = hp.reshape(nb * gp.t_pad // PAGE, PAGE, NCOL)
        cmp_lay = _PageLayout("cols", 512, C_CMP // 512, base=0, layer=0, nslot=0)
        slc_lay = _PageLayout("cols", 512, C_SLC // 512, base=0, layer=0, nslot=0)
        dsa_lay = _PageLayout("cols", 512, C_DSA // 512, base=0, layer=0, nslot=0)
        kidx_lay = _PageLayout("cols", LANES, C_MISC // LANES, base=0, layer=0, nslot=0)
        tail = None
        win_src = hp
        nwb, win_k0, win_layer = WINDOW // PAGE + 1, 0, 0
        past8 = jnp.zeros((nb, SUBLANES, CONV_DIM), F32)

    conv_out, cu = _conv(hp, past8, lw["conv_w8"], min(gp.tm, gp.t_pad), gp.t_pad // min(gp.tm, gp.t_pad))

    ab = _cmp1(nsa_pool, cmp_lay, pt, lw["wcat"], lw["pe"])
    cmp = _cmp2(ab, tail, lw["pe"], lw["w1r"], lw["phi_w2"], gp.t_real if gp.paged else 0, C_CMP // 512)
    o_cmp, o_win, kmask = _nsa1(hp, cmp, gp.ov, win_src, tail, nb=nb, tq=tq, qt_n=qt_n, q0=q0, n_cmp=gp.n_cmp,
                                n_sel=gp.n_sel, lpad=gp.lpad, nwb=nwb, win_k0=win_k0, win_layer=win_layer)
    o_sel = _mattn(hp, C_QROT, NSA_HEADS, kmask, nsa_pool, slc_lay, pt, tail, C_SLC // 512,
                   nb=nb, tq=tq, qt_n=qt_n, q0=q0, name="nsa_selected_attn")
    dmask = _dsa1(hp, kidx_pool, kidx_lay, pt, tail, nb=nb, tq=tq, qt_n=qt_n, q0=q0, lpad=gp.lpad,
                  n_keep=gp.n_keep)
    o_dsa = _mattn(hp, C_DQ, DSA_HEADS, dmask, dsa_pool, dsa_lay, pt, tail, C_DSA // 512,
                   nb=nb, tq=tq, qt_n=qt_n, q0=q0, name="dsa_topk_attn")
    z = _mix(hp, o_cmp, o_sel, o_win, o_dsa, conv_out, lw["w_a"], lw["w_b"], lw["w_c"], min(gp.tm, 256))
    return z, hp, cu


def _layer(gp, x, lw, caches):
    z, hp, cu = _mixer(gp, x, lw, caches)
    tm2 = min(gp.tm, 256)
    x1, gate = _outln(z, lw["w_o"], x, lw["ln_mix_g"], lw["ln_mix_b"], lw["wr_t"], lw["rb"], tm2)
    x2 = _moe(x1, gate, lw["w_eg"], lw["w_eu"], lw["w_ed"], lw["ln_ffn_g"], lw["ln_ffn_b"], min(gp.tm, 512))
    return x2, hp, cu


def _layer_weights(l, w_in, nsa_phi_pos, nsa_phi_w1, nsa_phi_w2, conv_w, w_br_a, w_br_b, w_br_c, w_out,
                   ln_mix_g, ln_mix_b, ln_ffn_g, ln_ffn_b, w_router, router_bias, w_e_gate, w_e_up, w_e_down):
    w1r = nsa_phi_w1[l].reshape(2, CMP_LEN, HEAD_DIM, HEAD_DIM)
    wcat = jnp.concatenate([w1r[:, :CMP_STRIDE], w1r[:, CMP_STRIDE:]], axis=-1).astype(BF16)
    return dict(
        w_in=_permute_w_in(w_in[l]), kinds=jnp.asarray(_col_kinds()),
        w1r=w1r, wcat=wcat, pe=nsa_phi_pos[l], phi_w2=nsa_phi_w2[l],
        conv_w8=jnp.concatenate([conv_w[l], jnp.zeros((SUBLANES - 3, CONV_DIM), F32)], axis=0),
        w_a=w_br_a[l].astype(BF16), w_b=w_br_b[l].astype(BF16), w_c=w_br_c[l].astype(BF16),
        w_o=w_out[l].astype(BF16),
        ln_mix_g=ln_mix_g[l][None], ln_mix_b=ln_mix_b[l][None],
        ln_ffn_g=ln_ffn_g[l][None], ln_ffn_b=ln_ffn_b[l][None],
        wr_t=w_router.T, rb=jnp.broadcast_to(router_bias[:, None], (N_EXPERTS, LANES)),
        w_eg=w_e_gate[l].astype(BF16), w_eu=w_e_up[l].astype(BF16), w_ed=w_e_down[l].astype(BF16))


def kernel(x_prompt, x_sample, cache_nsa_kv, cache_dsa_kv, cache_dsa_kidx, state_nsa_win, state_conv, page_table,
           w_in, nsa_phi_pos, nsa_phi_w1, nsa_phi_w2, conv_w, w_br_a, w_br_b, w_br_c, w_out, ln_mix_g, ln_mix_b,
           ln_ffn_g, ln_ffn_b, w_router, router_bias, w_e_gate, w_e_up, w_e_down):
    bp, tp, _ = x_prompt.shape
    bs, ts, _ = x_sample.shape
    ts_pad = SUBLANES

    gp_p = _Group(bp, tp, tp, 0, 0, tm=512, tq=128)
    gp_p.tabs = _rope_tables(jnp.arange(tp, dtype=jnp.int32))
    gp_p.pt = jnp.arange(bp * tp // PAGE, dtype=jnp.int32).reshape(bp, tp // PAGE)
    gp_p.ov = _overlap_matrix(LANES, gp_p.nselp)
    gp_s = _Group(bs, ts, ts_pad, PAST_LEN, PAST_LEN, tm=bs * ts_pad, tq=ts_pad)
    pos_s = PAST_LEN + jnp.arange(ts_pad, dtype=jnp.int32)
    gp_s.tabs = jnp.tile(_rope_tables(pos_s), (1, bs, 1))
    gp_s.ov = _overlap_matrix(PAST_LEN // CMP_STRIDE, gp_s.nselp)

    xp = x_prompt.reshape(bp * tp, D_MODEL)
    xs = jnp.concatenate([x_sample, jnp.zeros((bs, ts_pad - ts, D_MODEL), F32)], axis=1).reshape(bs * ts_pad, D_MODEL)

    outs_p = [[] for _ in range(5)]
    outs_s = [[] for _ in range(5)]
    for l in range(DEPTH):
        lw = _layer_weights(l, w_in, nsa_phi_pos, nsa_phi_w1, nsa_phi_w2, conv_w, w_br_a, w_br_b, w_br_c, w_out,
                            ln_mix_g, ln_mix_b, ln_ffn_g, ln_ffn_b, w_router, router_bias, w_e_gate, w_e_up, w_e_down)
        xp, hp_p, cu_p = _layer(gp_p, xp, lw, None)
        xs, hp_s, cu_s = _layer(gp_s, xs, lw, (l, cache_nsa_kv, cache_dsa_kv, cache_dsa_kidx,
                                                 state_nsa_win, state_conv, page_table))
        h3 = hp_p.reshape(bp, tp, NCOL)
        outs_p[0].append(h3[:, :, C_CMP:C_CMP + 1024].reshape(bp, tp, 4, NSA_KV, HEAD_DIM))
        outs_p[1].append(h3[:, :, C_DSA:C_DSA + 512].reshape(bp, tp, 2, DSA_KV, HEAD_DIM))
        outs_p[2].append(h3[:, :, C_MISC:C_MISC + IDX_DIM])
        outs_p[3].append(h3[:, tp - min(WINDOW, tp):, C_WIN:C_WIN + 512].reshape(bp, min(WINDOW, tp), 2, NSA_KV, HEAD_DIM))
        outs_p[4].append(cu_p.reshape(bp, tp, CONV_DIM)[:, tp - 2:])
        s3 = hp_s.reshape(bs, ts_pad, NCOL)[:, :ts]
        outs_s[0].append(s3[:, :, C_CMP:C_CMP + 1024].reshape(bs, ts, 4, NSA_KV, HEAD_DIM))
        outs_s[1].append(s3[:, :, C_DSA:C_DSA + 512].reshape(bs, ts, 2, DSA_KV, HEAD_DIM))
        outs_s[2].append(s3[:, :, C_MISC:C_MISC + IDX_DIM])
        win_new = s3[:, :, C_WIN:C_WIN + 512].reshape(bs, ts, 2, NSA_KV, HEAD_DIM)
        wb = state_nsa_win.shape[2]
        outs_s[3].append(jnp.concatenate([state_nsa_win[l], win_new], axis=1)[:, -wb:])
        ext = jnp.concatenate([state_conv[l], cu_s.reshape(bs, ts_pad, CONV_DIM)[:, :ts]], axis=1)
        outs_s[4].append(ext[:, -2:])
    sp = [jnp.stack(a, axis=0) for a in outs_p]
    ss = [jnp.stack(a, axis=0) for a in outs_s]
    y_p = xp.reshape(bp, tp, D_MODEL)
    y_s = xs.reshape(bs, ts_pad, D_MODEL)[:, :ts]
    return (y_p, y_s, sp[0], ss[0], sp[1], ss[1], sp[2], ss[2], sp[3], ss[3], sp[4], ss[4])
```

```python
import collections
import functools

import numpy as np
import jax
import jax.numpy as jnp
from jax import lax
from jax.experimental import pallas as pl
from jax.experimental.pallas import tpu as pltpu

F32 = jnp.float32
BF16 = jnp.bfloat16
HIGHEST = lax.Precision.HIGHEST
NEG_INF = float("-inf")

D_MODEL = 2048
DEPTH = 2
PAST_LEN = 16384
PAGE = 128
HEAD_DIM = 128
ROPE_THETA = 500000.0
NSA_HEADS = 8
NSA_KV = 2
CMP_LEN = 32
CMP_STRIDE = 16
SEL_BLOCK = 64
SEL_TOP = 16
WINDOW = 512
FORCE_SCORE = 1e4
DSA_HEADS = 4
DSA_KV = 2
IDX_HEADS = 4
IDX_DIM = 64
DSA_TOPK = 256
CONV_DIM = 512
N_EXPERTS = 16
N_GROUPS = 4
EXPERTS_PER_GROUP = 4
D_FF = 512
LN_EPS = 1e-5
ALPHA = (2 * DEPTH) ** 0.25
IN_WIDTHS = (1024, 1536, 24, 512, 512, 256, 64, 4, 1536, 6144)
ATT_SCALE = HEAD_DIM ** -0.5

LANES = 128
SUBLANES = 8
VMEM_LIMIT = 56 * 1024 * 1024

C_QRAW = 0
C_QROT = 1024
C_CMP = 2048
C_SLC = 2560
C_WIN = 3072
C_DQ = 3584
C_DSA = 4096
C_CV = 4608
C_MG = 6144
C_IQ = 12288
C_MISC = 12544
NCOL = 12800
MISC_IW = 64
MISC_NG = 68
PROJ_TN = 512
PAGES_PER_STEP = 16
KSTEP = PAGES_PER_STEP * PAGE

K_PLAIN, K_ROPE128, K_ROPE64, K_SIGMOID, K_MISC = 0, 1, 2, 3, 4


def _col_kinds():
    kinds = np.zeros(NCOL // LANES, np.int32)

    def mark(c0, n, k):
        kinds[c0 // LANES:(c0 + n) // LANES] = k

    mark(C_QROT, 1024, K_ROPE128)
    mark(C_SLC, 256, K_ROPE128)
    mark(C_WIN, 256, K_ROPE128)
    mark(C_DQ, 512, K_ROPE128)
    mark(C_DSA, 256, K_ROPE128)
    mark(C_IQ, 256, K_ROPE64)
    mark(C_MISC, 128, K_MISC)
    mark(C_MG, 6144, K_SIGMOID)
    return kinds


def _cparams(*sem):
    return pltpu.CompilerParams(dimension_semantics=sem, vmem_limit_bytes=VMEM_LIMIT)


def _sigmoid(x):
    return 1.0 / (1.0 + jnp.exp(-x))


def _layernorm(x, g, b):
    mu = jnp.mean(x, axis=-1, keepdims=True)
    xc = x - mu
    var = jnp.mean(xc * xc, axis=-1, keepdims=True)
    return xc * lax.rsqrt(var + LN_EPS) * g + b


def _dot_nt(a, b, precision=None):
    return lax.dot_general(a, b, (((1,), (1,)), ((), ())), preferred_element_type=F32, precision=precision)


def _softmax_parts(s):
    m = jnp.max(s, axis=-1, keepdims=True)
    m = jnp.where(m == NEG_INF, 0.0, m)
    p = jnp.exp(s - m)
    return p, jnp.maximum(jnp.sum(p, axis=-1, keepdims=True), 1e-30)


def _proj_body(kinds_ref, x_ref, wt_ref, tab_ref, o_ref, w_ref):
    nsub = PROJ_TN // LANES
    j = pl.program_id(0)

    @pl.when(pl.program_id(1) == 0)
    def _():
        for s in range(nsub):
            w_ref[:, s * LANES:(s + 1) * LANES] = wt_ref[s * LANES:(s + 1) * LANES, :].astype(F32).T.astype(BF16)

    h = jnp.dot(x_ref[...], w_ref[...], preferred_element_type=F32)
    lane = lax.broadcasted_iota(jnp.int32, (1, LANES), 1)

    def rope(hs, t0, sh):
        return (hs * tab_ref[t0] + pltpu.roll(hs, sh, 1) * tab_ref[t0 + 1]
                + pltpu.roll(hs, LANES - sh, 1) * tab_ref[t0 + 2])

    for s in range(nsub):
        kind = kinds_ref[j * nsub + s]
        hs = h[:, s * LANES:(s + 1) * LANES]
        sl = slice(s * LANES, (s + 1) * LANES)

        @pl.when(kind == K_PLAIN)
        def _():
            o_ref[:, sl] = hs

        @pl.when(kind == K_ROPE128)
        def _():
            o_ref[:, sl] = rope(hs, 0, 16)

        @pl.when(kind == K_ROPE64)
        def _():
            o_ref[:, sl] = rope(hs, 3, 8)

        @pl.when(kind == K_SIGMOID)
        def _():
            o_ref[:, sl] = _sigmoid(hs)

        @pl.when(kind == K_MISC)
        def _():
            r = rope(hs, 3, 8)
            o_ref[:, sl] = jnp.where(lane < MISC_IW, r,
                                     jnp.where(lane < MISC_NG, hs * (IDX_HEADS ** -0.5),
                                               jnp.where(lane < MISC_NG + 24, _sigmoid(hs), hs)))


def _proj(x_bf, wt_bf, tabs, kinds, tm):
    n = x_bf.shape[0]
    n_tab = tabs.shape[1] // tm
    grid = (NCOL // PROJ_TN, n // tm)
    return pl.pallas_call(
        _proj_body,
        grid_spec=pltpu.PrefetchScalarGridSpec(
            num_scalar_prefetch=1, grid=grid,
            in_specs=[pl.BlockSpec((tm, D_MODEL), lambda j, i, k: (i, 0)),
                      pl.BlockSpec((PROJ_TN, D_MODEL), lambda j, i, k: (j, 0)),
                      pl.BlockSpec((6, tm, LANES), lambda j, i, k: (0, i % n_tab, 0))],
            out_specs=pl.BlockSpec((tm, PROJ_TN), lambda j, i, k: (i, j)),
            scratch_shapes=[pltpu.VMEM((D_MODEL, PROJ_TN), BF16)]),
        out_shape=jax.ShapeDtypeStruct((n, NCOL), F32),
        compiler_params=_cparams("arbitrary", "arbitrary"),
        name="in_proj",
    )(kinds, x_bf, wt_bf, tabs)


def _rope_tables(pos):
    out = []
    lane = jnp.arange(LANES)
    for d in (HEAD_DIM, IDX_DIM):
        rot = d // 4
        half = rot // 2
        inv = ROPE_THETA ** (-jnp.arange(half, dtype=F32) / half)
        ang = pos.astype(F32)[:, None] * inv[None, :]
        cos = jnp.cos(ang)
        sin = jnp.sin(ang)
        li = lane % d
        ci = jnp.take(cos, li % half, axis=1)
        si = jnp.take(sin, li % half, axis=1)
        out.append(jnp.where(li[None] < rot, ci, 1.0))
        out.append(jnp.where((li[None] >= half) & (li[None] < rot), si, 0.0))
        out.append(jnp.where(li[None] < half, -si, 0.0))
    return jnp.stack(out, axis=0)


def _permute_w_in(w_in, l):
    wt = jnp.transpose(w_in, (2, 0, 1))[:, l, :].astype(BF16)
    offs = np.cumsum((0,) + IN_WIDTHS)
    nq, nkv, ng, dq, dkv, iq, ik, iw, cv, mg = [wt[offs[i]:offs[i + 1]] for i in range(10)]
    z = lambda n: jnp.zeros((n, wt.shape[1]), BF16)
    rows = [nq, nq, nkv, dq, dkv, cv, mg, iq, ik, iw, ng, z(LANES - 92), z(NCOL - C_MISC - LANES)]
    return jnp.concatenate(rows, axis=0)


def _conv_body(cv_ref, prev_ref, past_ref, w_ref, y_ref, cu_ref, s_ref, *, tiles_per_seq, tm):
    i = pl.program_id(0)
    b = cv_ref[:, 0:CONV_DIM]
    cu = cv_ref[:, CONV_DIM:2 * CONV_DIM] * cv_ref[:, 2 * CONV_DIM:3 * CONV_DIM]
    first = (i % tiles_per_seq) == 0
    prev = prev_ref[:, CONV_DIM:2 * CONV_DIM] * prev_ref[:, 2 * CONV_DIM:3 * CONV_DIM]
    s_ref[0:SUBLANES, :] = jnp.where(first, past_ref[0], prev)
    s_ref[SUBLANES:SUBLANES + tm, :] = cu
    y = (w_ref[0:1, :] * s_ref[pl.ds(SUBLANES - 2, tm), :] + w_ref[1:2, :] * s_ref[pl.ds(SUBLANES - 1, tm), :]
         + w_ref[2:3, :] * cu)
    y_ref[...] = b * y
    cu_ref[...] = cu


def _conv(hp, past8, conv_w8, tm, tiles_per_seq):
    n = hp.shape[0]
    cvb = C_CV // (3 * CONV_DIM)
    rb = tm // SUBLANES
    return pl.pallas_call(
        functools.partial(_conv_body, tiles_per_seq=tiles_per_seq, tm=tm),
        grid=(n // tm,),
        in_specs=[pl.BlockSpec((tm, 3 * CONV_DIM), lambda i: (i, cvb)),
                  pl.BlockSpec((SUBLANES, 3 * CONV_DIM), lambda i: (jnp.maximum(i * rb - 1, 0), cvb)),
                  pl.BlockSpec((1, SUBLANES, CONV_DIM), lambda i: (i // tiles_per_seq, 0, 0)),
                  pl.BlockSpec((SUBLANES, CONV_DIM), lambda i: (0, 0))],
        out_specs=[pl.BlockSpec((tm, CONV_DIM), lambda i: (i, 0)),
                   pl.BlockSpec((tm, CONV_DIM), lambda i: (i, 0))],
        out_shape=[jax.ShapeDtypeStruct((n, CONV_DIM), F32), jax.ShapeDtypeStruct((n, CONV_DIM), F32)],
        scratch_shapes=[pltpu.VMEM((tm + SUBLANES, CONV_DIM), F32)],
        compiler_params=_cparams("arbitrary"),
        name="short_conv",
    )(hp, hp, past8, conv_w8)


_PageLayout = collections.namedtuple("_PageLayout", "kind width col_block base layer nslot")


def _page_specs(layout, n_lead, kg_n=None):
    def spec(k):
        def imap(*a):
            ids, pt = a[:n_lead], a[-1]
            kg = ids[-1] % kg_n if kg_n else ids[-1]
            page = pt[ids[0], kg * PAGES_PER_STEP + k]
            if layout.kind == "cache":
                return (layout.layer, page, 0, layout.col_block, 0, 0)
            if layout.kind == "cache4":
                return (layout.layer, page, 0, 0)
            cb = layout.col_block(*ids) if callable(layout.col_block) else layout.col_block
            return (page, 0, cb)
        shape = {"cache": (1, 1, PAGE, layout.nslot, 2, HEAD_DIM), "cache4": (1, 1, PAGE, layout.width),
                 "cols": (1, PAGE, layout.width)}[layout.kind]
        return pl.BlockSpec(shape, imap)
    return [spec(k) for k in range(PAGES_PER_STEP)]


def _page_tile(layout, ref, j, width=LANES):
    if layout.kind == "cache":
        slot, g = divmod(layout.base + j, 2)
        return ref[0, 0, :, slot, g, :]
    if layout.kind == "cache4":
        return ref[0, 0, :, j * width:(j + 1) * width]
    return ref[0, :, j * width:(j + 1) * width]


def _chunk_rows(layout, ref, j, p):
    if layout.kind == "cache":
        slot, g = divmod(layout.base + j, 2)
        return ref[0, 0, pl.ds(p, SUBLANES, stride=CMP_STRIDE), slot, g, :]
    return ref[0, pl.ds(p, SUBLANES, stride=CMP_STRIDE), :]


def _cmp1_body(pt_ref, *refs, layout, n_inner):
    pages = refs[:PAGES_PER_STEP]
    w_ref, pe_ref = refs[PAGES_PER_STEP:PAGES_PER_STEP + 2]
    o_ref = refs[PAGES_PER_STEP + 2]
    for j in range(n_inner):
        slot = j // 2
        acc_a = jnp.zeros((PAGE, HEAD_DIM), F32)
        acc_b = jnp.zeros((PAGE, HEAD_DIM), F32)
        for p in range(CMP_STRIDE):
            xp = jnp.concatenate([_chunk_rows(layout, pg, j, p) for pg in pages], axis=0)
            xa = (xp + pe_ref[slot, p:p + 1, :]).astype(BF16)
            xb = (xp + pe_ref[slot, CMP_STRIDE + p:CMP_STRIDE + p + 1, :]).astype(BF16)
            acc_a = acc_a + jnp.dot(xa, w_ref[slot, p, :, 0:HEAD_DIM], preferred_element_type=F32)
            acc_b = acc_b + jnp.dot(xb, w_ref[slot, p, :, HEAD_DIM:2 * HEAD_DIM], preferred_element_type=F32)
        o_ref[0, :, j * 256:j * 256 + HEAD_DIM] = acc_a
        o_ref[0, :, j * 256 + HEAD_DIM:(j + 1) * 256] = acc_b


def _cmp1(pool, layout, pt, wcat, pe):
    nb, npg = pt.shape
    kg = npg // PAGES_PER_STEP
    if layout.kind == "cache":
        n_inner, n_sg, wsel, osel, ow = 4, 1, (lambda sg: 0), (lambda sg: 0), 1024
        wblk = 2
    else:
        cb = layout.col_block
        layout = layout._replace(width=HEAD_DIM, col_block=lambda b, sg, k: cb * 4 + sg)
        n_inner, n_sg, wsel, osel, ow = 1, 4, (lambda sg: sg // 2), (lambda sg: sg), 256
        wblk = 1
    return pl.pallas_call(
        functools.partial(_cmp1_body, layout=layout, n_inner=n_inner),
        grid_spec=pltpu.PrefetchScalarGridSpec(
            num_scalar_prefetch=1, grid=(nb, n_sg, kg),
            in_specs=_page_specs(layout, 3)
            + [pl.BlockSpec((wblk, 16, HEAD_DIM, 256), lambda b, sg, k, pt: (wsel(sg), 0, 0, 0)),
               pl.BlockSpec((wblk, CMP_LEN, HEAD_DIM), lambda b, sg, k, pt: (wsel(sg), 0, 0))],
            out_specs=pl.BlockSpec((1, PAGE, ow), lambda b, sg, k, pt: (b, k, osel(sg)))),
        out_shape=jax.ShapeDtypeStruct((nb, npg * SUBLANES, 1024), F32),
        compiler_params=_cparams("arbitrary", "arbitrary", "arbitrary"),
        name="nsa_compress1",
    )(pt, *([pool] * PAGES_PER_STEP), wcat, pe)


def _gelu_tanh(x):
    return 0.5 * x * (1.0 + jnp.tanh(np.sqrt(2.0 / np.pi).astype(np.float32) * (x + 0.044715 * (x * x * x))))


def _cmp2_body(ab_ref, tail_ref, pe_ref, w1_ref, w2_ref, o_ref, s_ref, *, nc, n_tail):
    row8 = lax.broadcasted_iota(jnp.int32, (SUBLANES, 1), 0)
    row16 = lax.broadcasted_iota(jnp.int32, (CMP_STRIDE, 1), 0)
    for sg in range(4):
        slot = sg // 2
        a = ab_ref[0, :, sg * 256:sg * 256 + HEAD_DIM]
        s_ref[0:nc, :] = ab_ref[0, :, sg * 256 + HEAD_DIM:(sg + 1) * 256]
        tb = jnp.zeros((SUBLANES, HEAD_DIM), F32)
        if n_tail:
            x8 = jnp.where(row8 < n_tail, tail_ref[:, sg * HEAD_DIM:(sg + 1) * HEAD_DIM], 0.0)
            x16 = jnp.concatenate([x8, jnp.zeros((CMP_STRIDE - SUBLANES, HEAD_DIM), F32)], axis=0)
            x16 = x16 + pe_ref[slot, CMP_STRIDE:CMP_LEN, :]
            t16 = jnp.zeros((CMP_STRIDE, HEAD_DIM), F32)
            for p in range(CMP_STRIDE):
                xm = jnp.where(row16 == p, x16, 0.0).astype(BF16)
                t16 = t16 + jnp.dot(xm, w1_ref[slot, CMP_STRIDE + p].astype(BF16), preferred_element_type=F32)
            tb = jnp.sum(t16, axis=0, keepdims=True) * jnp.where(row8 == 0, 1.0, 0.0)
        s_ref[nc:nc + SUBLANES, :] = tb
        pre = a + s_ref[pl.ds(1, nc), :]
        o_ref[0, :, sg * HEAD_DIM:(sg + 1) * HEAD_DIM] = jnp.dot(
            _gelu_tanh(pre).astype(BF16), w2_ref[slot].astype(BF16), preferred_element_type=F32)


def _cmp2(ab, tail, pe, w1r, w2, n_tail, tail_col_block):
    nb, nc, _ = ab.shape
    if tail is None:
        tail = jnp.zeros((nb * SUBLANES, 512), F32)
        tail_col_block = 0
    return pl.pallas_call(
        functools.partial(_cmp2_body, nc=nc, n_tail=n_tail),
        grid=(nb,),
        in_specs=[pl.BlockSpec((1, nc, 1024), lambda b: (b, 0, 0)),
                  pl.BlockSpec((SUBLANES, 512), lambda b: (b, tail_col_block)),
                  pl.BlockSpec((2, CMP_LEN, HEAD_DIM), lambda b: (0, 0, 0)),
                  pl.BlockSpec((2, CMP_LEN, HEAD_DIM, HEAD_DIM), lambda b: (0, 0, 0, 0)),
                  pl.BlockSpec((2, HEAD_DIM, HEAD_DIM), lambda b: (0, 0, 0))],
        out_specs=pl.BlockSpec((1, nc, 512), lambda b: (b, 0, 0)),
        out_shape=jax.ShapeDtypeStruct((nb, nc, 512), F32),
        scratch_shapes=[pltpu.VMEM((nc + SUBLANES, HEAD_DIM), F32)],
        compiler_params=_cparams("arbitrary"),
        name="nsa_compress2",
    )(ab, tail, pe, w1r, w2)


def _nsa1_body(*refs, tq, q0, n_cmp, ncp, n_sel, nselp, lpad, nwb, has_tail, win_k0):
    qraw_ref, qrot_ref, cmp_ref, ov_ref = refs[:4]
    wins = refs[4:4 + nwb]
    pos = 4 + nwb
    tail_ref = None
    if has_tail:
        tail_ref = refs[pos]
        pos += 1
    ocmp_ref, owin_ref, kmask_ref = refs[pos:pos + 3]
    qt = pl.program_id(1)
    r_per_g = NSA_HEADS // NSA_KV
    qpos = q0 + qt * tq + lax.broadcasted_iota(jnp.int32, (tq, 1), 0)

    jj = lax.broadcasted_iota(jnp.int32, (1, ncp), 1)
    cmask = (jj * CMP_STRIDE + (CMP_LEN - 1) <= qpos) & (jj < n_cmp)
    blk = lax.broadcasted_iota(jnp.int32, (1, nselp), 1)
    cur = qpos // SEL_BLOCK
    forced = (blk == 0) | (blk == cur) | (blk == cur - 1)
    e_row = lax.broadcasted_iota(jnp.int32, (KSTEP // SEL_BLOCK, KSTEP), 0)
    e_col = lax.broadcasted_iota(jnp.int32, (KSTEP // SEL_BLOCK, KSTEP), 1)
    expand = jnp.where(e_col // SEL_BLOCK == e_row, 1.0, 0.0).astype(BF16)
    lane128 = lax.broadcasted_iota(jnp.int32, (1, LANES), 1)
    for g in range(NSA_KV):
        kc = cmp_ref[0, :, g * HEAD_DIM:(g + 1) * HEAD_DIM].astype(BF16)
        vc = cmp_ref[0, :, (2 + g) * HEAD_DIM:(3 + g) * HEAD_DIM].astype(BF16)
        psum = jnp.zeros((tq, ncp), F32)
        for r in range(r_per_g):
            h = g * r_per_g + r
            q = qraw_ref[:, h * HEAD_DIM:(h + 1) * HEAD_DIM].astype(BF16)
            s = jnp.where(cmask, _dot_nt(q, kc) * ATT_SCALE, NEG_INF)
            p, den = _softmax_parts(s)
            p = p / den
            psum = psum + p
            ocmp_ref[:, h * HEAD_DIM:(h + 1) * HEAD_DIM] = jnp.dot(p.astype(BF16), vc, preferred_element_type=F32)
        imp = jnp.dot(psum.astype(BF16), ov_ref[...].astype(BF16), preferred_element_type=F32)
        imp = jnp.where(forced, imp + FORCE_SCORE, imp)
        imp = jnp.where((blk <= cur) & (blk < n_sel), imp, NEG_INF)
        rank = jnp.zeros((tq, nselp), F32)
        for i in range(n_sel):
            vi = imp[:, i:i + 1]
            beats = (vi > imp) | ((vi == imp) & (blk > i))
            rank = rank + jnp.where(beats, 1.0, 0.0)
        sel = jnp.where((rank < min(SEL_TOP, n_sel)) & (blk < n_sel), 1.0, 0.0).astype(BF16)
        per = KSTEP // SEL_BLOCK
        for c in range(lpad // KSTEP):
            km = jnp.dot(sel[:, c * per:(c + 1) * per], expand, preferred_element_type=F32)
            kmask_ref[0, g, :, c * KSTEP:(c + 1) * KSTEP] = km.astype(BF16)
        if lpad % KSTEP:
            b0 = (lpad // KSTEP) * per
            km = jnp.where(lane128 < SEL_BLOCK, sel[:, b0:b0 + 1].astype(F32), 0.0)
            kmask_ref[0, g, :, (lpad // KSTEP) * KSTEP:lpad] = jnp.broadcast_to(km, (tq, LANES)).astype(BF16)

    nk = nwb * PAGE + (LANES if has_tail else 0)
    kk = lax.broadcasted_iota(jnp.int32, (1, nk), 1)
    if has_tail:
        kpos = jnp.where(kk < nwb * PAGE, win_k0 + kk, q0 + kk - nwb * PAGE)
    else:
        kpos = (qt - (nwb - 1)) * PAGE + kk
    rel = qpos - kpos
    wmask = (rel >= 0) & (rel < WINDOW) & (kpos >= 0)
    for g in range(NSA_KV):
        if has_tail:
            kparts = [w[0, 0, :, 0, g, :] for w in wins]
            vparts = [w[0, 0, :, 1, g, :] for w in wins]
        else:
            kparts = [w[:, g * HEAD_DIM:(g + 1) * HEAD_DIM] for w in wins]
            vparts = [w[:, (2 + g) * HEAD_DIM:(3 + g) * HEAD_DIM] for w in wins]
        if has_tail:
            zpad = jnp.zeros((LANES - SUBLANES, HEAD_DIM), F32)
            kparts += [tail_ref[:, g * HEAD_DIM:(g + 1) * HEAD_DIM], zpad]
            vparts += [tail_ref[:, (2 + g) * HEAD_DIM:(3 + g) * HEAD_DIM], zpad]
        kw = jnp.concatenate(kparts, axis=0).astype(BF16)
        vw = jnp.concatenate(vparts, axis=0).astype(BF16)
        for r in range(r_per_g):
            h = g * r_per_g + r
            q = qrot_ref[:, h * HEAD_DIM:(h + 1) * HEAD_DIM].astype(BF16)
            s = jnp.where(wmask, _dot_nt(q, kw) * ATT_SCALE, NEG_INF)
            p, den = _softmax_parts(s)
            p = p / den
            owin_ref[:, h * HEAD_DIM:(h + 1) * HEAD_DIM] = jnp.dot(p.astype(BF16), vw, preferred_element_type=F32)


def _nsa1(hp, cmp, ov, win_src, tail_src, *, nb, tq, qt_n, q0, n_cmp, n_sel, lpad, nwb, win_k0, win_layer):
    n = hp.shape[0]
    ncp = cmp.shape[1]
    nselp = ov.shape[1]
    has_tail = tail_src is not None
    rows = lambda b, t: b * qt_n + t
    in_specs = [pl.BlockSpec((tq, 1024), lambda b, t: (rows(b, t), C_QRAW // 1024)),
                pl.BlockSpec((tq, 1024), lambda b, t: (rows(b, t), C_QROT // 1024)),
                pl.BlockSpec((1, ncp, 512), lambda b, t: (b, 0, 0)),
                pl.BlockSpec((ncp, nselp), lambda b, t: (0, 0))]
    if has_tail:
        in_specs += [pl.BlockSpec((1, 1, PAGE, 2, NSA_KV, HEAD_DIM), lambda b, t, k=k: (win_layer, b, k, 0, 0, 0))
                     for k in range(nwb)]
        in_specs += [pl.BlockSpec((SUBLANES, 512), lambda b, t: (b, C_WIN // 512))]
        args = [win_src] * nwb + [tail_src]
    else:
        in_specs += [pl.BlockSpec((PAGE, 512),
                                  lambda b, t, k=k: (b * qt_n + jnp.maximum(t - (nwb - 1) + k, 0), C_WIN // 512))
                     for k in range(nwb)]
        args = [win_src] * nwb
    return pl.pallas_call(
        functools.partial(_nsa1_body, tq=tq, q0=q0, n_cmp=n_cmp, ncp=ncp, n_sel=n_sel, nselp=nselp, lpad=lpad,
                          nwb=nwb, has_tail=has_tail, win_k0=win_k0),
        grid=(nb, qt_n),
        in_specs=in_specs,
        out_specs=[pl.BlockSpec((tq, 1024), lambda b, t: (rows(b, t), 0)),
                   pl.BlockSpec((tq, 1024), lambda b, t: (rows(b, t), 0)),
                   pl.BlockSpec((1, NSA_KV, tq, lpad), lambda b, t: (b, 0, t, 0))],
        out_shape=[jax.ShapeDtypeStruct((n, 1024), F32), jax.ShapeDtypeStruct((n, 1024), F32),
                   jax.ShapeDtypeStruct((nb, NSA_KV, qt_n * tq, lpad), BF16)],
        compiler_params=_cparams("arbitrary", "arbitrary"),
        name="nsa_cmp_select_window",
    )(hp, hp, cmp, ov, *args)


def _mattn_body(pt_ref, *refs, layout, n_g, r_per_g, gm, tq, q0, kg_n, has_tail):
    q_ref, mask_ref = refs[:2]
    pages = refs[2:2 + PAGES_PER_STEP]
    pos = 2 + PAGES_PER_STEP
    if has_tail:
        tail_ref, tmask_ref = refs[pos:pos + 2]
        pos += 2
    o_ref, m_ref, l_ref, acc_ref = refs[pos:pos + 4]
    qt = pl.program_id(1)
    step = pl.program_id(2)
    two_pass = kg_n > 1 or has_tail
    kg = step % kg_n if two_pass else step
    qpos = q0 + qt * tq + lax.broadcasted_iota(jnp.int32, (tq, 1), 0)

    def scores(h, g, kt, kpos, mref):
        valid = (mref[0, g if gm > 1 else 0] > 0.5) & (kpos <= qpos)
        q = q_ref[:, h * HEAD_DIM:(h + 1) * HEAD_DIM].astype(BF16)
        return jnp.where(valid, _dot_nt(q, kt) * ATT_SCALE, NEG_INF)

    def main_keys(g):
        return jnp.concatenate([_page_tile(layout, pg, g) for pg in pages], axis=0).astype(BF16)

    def main_vals(g):
        return jnp.concatenate([_page_tile(layout, pg, n_g + g) for pg in pages], axis=0).astype(BF16)

    def tail_part(g, off):
        zpad = jnp.zeros((LANES - SUBLANES, HEAD_DIM), F32)
        return jnp.concatenate([tail_ref[:, (off + g) * HEAD_DIM:(off + g + 1) * HEAD_DIM], zpad], axis=0).astype(BF16)

    kpos = kg * KSTEP + lax.broadcasted_iota(jnp.int32, (1, KSTEP), 1)
    tpos = q0 + lax.broadcasted_iota(jnp.int32, (1, LANES), 1)

    if not two_pass:
        for g in range(n_g):
            kt, vt = main_keys(g), main_vals(g)
            for r in range(r_per_g):
                h = g * r_per_g + r
                p, den = _softmax_parts(scores(h, g, kt, kpos, mask_ref))
                o_ref[:, h * HEAD_DIM:(h + 1) * HEAD_DIM] = jnp.dot((p / den).astype(BF16), vt,
                                                                   preferred_element_type=F32)
        return

    @pl.when(step == 0)
    def _():
        m_ref[...] = jnp.full(m_ref.shape, NEG_INF, F32)
        l_ref[...] = jnp.zeros(l_ref.shape, F32)
        acc_ref[...] = jnp.zeros(acc_ref.shape, F32)

    def stats(h, s):
        m_old = m_ref[h]
        m_new = jnp.maximum(m_old, jnp.max(s, axis=-1, keepdims=True))
        m_safe = jnp.where(m_new == NEG_INF, 0.0, m_new)
        l_ref[h] = jnp.exp(m_old - m_safe) * l_ref[h] + jnp.sum(jnp.exp(s - m_safe), axis=-1, keepdims=True)
        m_ref[h] = m_new

    def accumulate(h, s, vt):
        m = m_ref[h]
        m_safe = jnp.where(m == NEG_INF, 0.0, m)
        p = jnp.exp(s - m_safe) / jnp.maximum(l_ref[h], 1e-30)
        sl = slice(h * HEAD_DIM, (h + 1) * HEAD_DIM)
        acc_ref[:, sl] += jnp.dot(p.astype(BF16), vt, preferred_element_type=F32)

    @pl.when(step < kg_n)
    def _():
        for g in range(n_g):
            kt = main_keys(g)
            for r in range(r_per_g):
                h = g * r_per_g + r
                stats(h, scores(h, g, kt, kpos, mask_ref))

    if has_tail:
        @pl.when(step == kg_n - 1)
        def _():
            for g in range(n_g):
                kt = tail_part(g, 0)
                for r in range(r_per_g):
                    h = g * r_per_g + r
                    stats(h, scores(h, g, kt, tpos, tmask_ref))

    @pl.when(step >= kg_n)
    def _():
        for g in range(n_g):
            kt, vt = main_keys(g), main_vals(g)
            for r in range(r_per_g):
                h = g * r_per_g + r
                accumulate(h, scores(h, g, kt, kpos, mask_ref), vt)

    @pl.when(step == 2 * kg_n - 1)
    def _():
        if has_tail:
            for g in range(n_g):
                kt, vt = tail_part(g, 0), tail_part(g, n_g)
                for r in range(r_per_g):
                    h = g * r_per_g + r
                    accumulate(h, scores(h, g, kt, tpos, tmask_ref), vt)
        o_ref[...] = acc_ref[...]


def _mattn(hp, q_col, n_heads, mask, pool, layout, pt, tail_src, tail_col_block, *, nb, tq, qt_n, q0, name):
    n = hp.shape[0]
    n_g = 2
    r_per_g = n_heads // n_g
    qw = n_heads * HEAD_DIM
    gm = mask.shape[1]
    kg_n = pt.shape[1] // PAGES_PER_STEP
    has_tail = tail_src is not None
    n_steps = 2 * kg_n if (kg_n > 1 or has_tail) else 1
    in_specs = [pl.BlockSpec((tq, qw), lambda b, t, k, pt: (b * qt_n + t, q_col // qw)),
                pl.BlockSpec((1, gm, tq, KSTEP), lambda b, t, k, pt: (b, 0, t, k % kg_n))]
    in_specs += _page_specs(layout, 3, kg_n)
    args = [hp, mask] + [pool] * PAGES_PER_STEP
    if has_tail:
        in_specs += [pl.BlockSpec((SUBLANES, 512), lambda b, t, k, pt: (b, tail_col_block)),
                     pl.BlockSpec((1, gm, tq, LANES), lambda b, t, k, pt: (b, 0, t, kg_n * KSTEP // LANES))]
        args += [tail_src, mask]
    return pl.pallas_call(
        functools.partial(_mattn_body, layout=layout, n_g=n_g, r_per_g=r_per_g, gm=gm, tq=tq, q0=q0, kg_n=kg_n, has_tail=has_tail),
        grid_spec=pltpu.PrefetchScalarGridSpec(
            num_scalar_prefetch=1, grid=(nb, qt_n, n_steps),
            in_specs=in_specs,
            out_specs=pl.BlockSpec((tq, qw), lambda b, t, k, pt: (b * qt_n + t, 0)),
            scratch_shapes=[pltpu.VMEM((n_heads, tq, 1), F32), pltpu.VMEM((n_heads, tq, 1), F32),
                            pltpu.VMEM((tq, qw), F32)]),
        out_shape=jax.ShapeDtypeStruct((n, qw), F32),
        compiler_params=_cparams("arbitrary", "arbitrary", "arbitrary"),
        name=name,
    )(pt, *args)


PACKED_ROWS = 2 * SUBLANES


def _topk_mask_packed_t(sc_t, n_keep):
    n_keys, n_q = sc_t.shape
    i16 = jnp.int16
    bits = pltpu.bitcast(sc_t + 0.0, jnp.int32)
    key = jnp.where(bits < 0, bits ^ jnp.int32(0x7FFFFFFF), bits)
    hi = jnp.right_shift(key, 16).astype(i16)
    lo = ((key & jnp.int32(0xFFFF)) - 32768).astype(i16)
    one_b, zero_b = jnp.asarray(1, BF16), jnp.asarray(0, BF16)
    i16_min, i16_max = jnp.asarray(-32768, i16), jnp.asarray(32767, i16)
    assert n_keys % PACKED_ROWS == 0 and n_keys // PACKED_ROWS <= 256

    def colsum(x01):
        parts = [x01[i * PACKED_ROWS:(i + 1) * PACKED_ROWS, :] for i in range(n_keys // PACKED_ROWS)]
        while len(parts) > 1:
            parts = [parts[i] + parts[i + 1] for i in range(0, len(parts), 2)]
        return jnp.sum(parts[0].astype(F32), axis=0, keepdims=True)

    def search(vals, need, nbits, start):
        def body(i, thr):
            cand = thr + jnp.left_shift(jnp.int32(1), nbits - 1 - i)
            cnt = colsum(jnp.where(vals >= cand.astype(i16), one_b, zero_b))
            return jnp.where(cnt >= need, cand, thr)
        return lax.fori_loop(0, nbits, body, jnp.full((1, n_q), start, jnp.int32))

    t_hi = search(hi, float(n_keep), 16, -32768).astype(i16)
    eq = hi == t_hi
    hi_gt = jnp.where(hi > t_hi, one_b, zero_b)
    lo_m = jnp.where(eq, lo, i16_min)
    t_lo = search(lo_m, n_keep - colsum(hi_gt), 16, -32768).astype(i16)
    gt = hi_gt + jnp.where(lo_m > t_lo, one_b, zero_b)
    need = n_keep - colsum(gt)
    idx = lax.broadcasted_iota(jnp.int32, (n_keys, 1), 0).astype(i16)
    tie_idx = jnp.where(eq, jnp.where(lo == t_lo, idx, i16_max), i16_max)
    nbits = int(np.ceil(np.log2(n_keys)))

    def ibody(i, c):
        cand = c + jnp.left_shift(jnp.int32(1), nbits - 1 - i)
        cnt = colsum(jnp.where(tie_idx < cand.astype(i16), one_b, zero_b))
        return jnp.where(cnt < need, cand, c)

    cut = lax.fori_loop(0, nbits, ibody, jnp.zeros((1, n_q), jnp.int32)).astype(i16)
    return jnp.maximum(gt, jnp.where(tie_idx <= cut, one_b, zero_b))


def _dsa1_body(pt_ref, *refs, layout, tq, q0, kg_n, lpad, has_tail, n_keep):
    iq_ref, misc_ref = refs[:2]
    pages = refs[2:2 + PAGES_PER_STEP]
    pos = 2 + PAGES_PER_STEP
    if has_tail:
        tail_ref = refs[pos]
        pos += 1
    mask_ref, sc_ref = refs[pos:pos + 2]
    qt = pl.program_id(1)
    kg = pl.program_id(2)
    qpos = q0 + qt * tq + lax.broadcasted_iota(jnp.int32, (tq, 1), 0)
    lane128 = lax.broadcasted_iota(jnp.int32, (1, LANES), 1)

    if kg_n == 1 and not has_tail and tq == LANES:
        kt = jnp.concatenate([_page_tile(layout, pg, 0, IDX_DIM) for pg in pages], axis=0).astype(BF16)
        misc_t = misc_ref[...].T
        kpos_c = lax.broadcasted_iota(jnp.int32, (KSTEP, 1), 0)
        qpos_r = q0 + qt * tq + lax.broadcasted_iota(jnp.int32, (1, tq), 1)
        acc = jnp.zeros((KSTEP, tq), F32)
        for h in range(IDX_HEADS):
            qi = iq_ref[:, h * IDX_DIM:(h + 1) * IDX_DIM].astype(BF16)
            acc = acc + misc_t[MISC_IW + h:MISC_IW + h + 1, :] * jnp.maximum(_dot_nt(kt, qi), 0.0)
        sel_t = _topk_mask_packed_t(jnp.where(kpos_c <= qpos_r, acc, NEG_INF), n_keep)
        for c in range(KSTEP // LANES):
            blk = sel_t[c * LANES:(c + 1) * LANES, :].astype(F32).T
            mask_ref[0, 0, :, c * LANES:(c + 1) * LANES] = blk.astype(BF16)
        return

    def scores(kt, kpos):
        acc = jnp.zeros((tq, kt.shape[0]), F32)
        for h in range(IDX_HEADS):
            qi = iq_ref[:, h * IDX_DIM:(h + 1) * IDX_DIM].astype(BF16)
            w = jnp.sum(jnp.where(lane128 == MISC_IW + h, misc_ref[...], 0.0), axis=-1, keepdims=True)
            acc = acc + w * jnp.maximum(_dot_nt(qi, kt), 0.0)
        return jnp.where(kpos <= qpos, acc, NEG_INF)

    kt = jnp.concatenate([_page_tile(layout, pg, 0, IDX_DIM) for pg in pages], axis=0).astype(BF16)
    kpos = kg * KSTEP + lax.broadcasted_iota(jnp.int32, (1, KSTEP), 1)
    sc_ref[kg] = scores(kt, kpos)

    @pl.when(kg == kg_n - 1)
    def _():
        nch = sc_ref.shape[0]
        if has_tail:
            zpad = jnp.zeros((KSTEP - SUBLANES, IDX_DIM), F32)
            tk = jnp.concatenate([tail_ref[:, 0:IDX_DIM], zpad], axis=0).astype(BF16)
            tl = lax.broadcasted_iota(jnp.int32, (1, KSTEP), 1)
            sc_ref[kg_n] = jnp.where(tl < LANES, scores(tk, q0 + tl), NEG_INF)
        sc = sc_ref[...] + 0.0
        bits = pltpu.bitcast(sc, jnp.int32)
        key = jnp.where(bits < 0, bits ^ jnp.int32(0x7FFFFFFF), bits)
        int_min = jnp.int32(-2 ** 31)

        def count(pred):
            return jnp.sum(jnp.sum(jnp.where(pred, 1.0, 0.0), axis=-1, keepdims=True), axis=0, keepdims=True)

        def vbit(i, thr):
            cand = thr + jnp.left_shift(jnp.int32(1), 31 - i)
            return jnp.where(count(key >= cand) >= n_keep, cand, thr)

        thr = lax.fori_loop(0, 32, vbit, jnp.full((1, tq, 1), int_min, jnp.int32))
        gt = key > thr
        tie = key == thr
        need = n_keep - count(gt)
        idx = (lax.broadcasted_iota(jnp.int32, (nch, 1, KSTEP), 0) * KSTEP
               + lax.broadcasted_iota(jnp.int32, (nch, 1, KSTEP), 2))
        nbits = int(np.ceil(np.log2(nch * KSTEP)))

        def ibit(i, c):
            cand = c + jnp.left_shift(jnp.int32(1), nbits - 1 - i)
            return jnp.where(count(tie & (idx < cand)) < need, cand, c)

        cut = lax.fori_loop(0, nbits, ibit, jnp.zeros((1, tq, 1), jnp.int32))
        sel = jnp.where(gt | (tie & (idx <= cut)), 1.0, 0.0).astype(BF16)
        for c in range(kg_n):
            mask_ref[0, 0, :, c * KSTEP:(c + 1) * KSTEP] = sel[c]
        if has_tail:
            mask_ref[0, 0, :, kg_n * KSTEP:lpad] = sel[kg_n][:, 0:lpad - kg_n * KSTEP]


def _dsa1(hp, pool, layout, pt, tail_src, *, nb, tq, qt_n, q0, lpad, n_keep):
    kg_n = pt.shape[1] // PAGES_PER_STEP
    has_tail = tail_src is not None
    in_specs = [pl.BlockSpec((tq, 256), lambda b, t, k, pt: (b * qt_n + t, C_IQ // 256)),
                pl.BlockSpec((tq, LANES), lambda b, t, k, pt: (b * qt_n + t, C_MISC // LANES))]
    in_specs += _page_specs(layout, 3)
    args = [hp, hp] + [pool] * PAGES_PER_STEP
    if has_tail:
        in_specs += [pl.BlockSpec((SUBLANES, LANES), lambda b, t, k, pt: (b, C_MISC // LANES))]
        args += [tail_src]
    return pl.pallas_call(
        functools.partial(_dsa1_body, layout=layout, tq=tq, q0=q0, kg_n=kg_n, lpad=lpad, has_tail=has_tail,
                          n_keep=n_keep),
        grid_spec=pltpu.PrefetchScalarGridSpec(
            num_scalar_prefetch=1, grid=(nb, qt_n, kg_n),
            in_specs=in_specs,
            out_specs=pl.BlockSpec((1, 1, tq, lpad), lambda b, t, k, pt: (b, 0, t, 0)),
            scratch_shapes=[pltpu.VMEM((kg_n + (1 if has_tail else 0), tq, KSTEP), F32)]),
        out_shape=jax.ShapeDtypeStruct((nb, 1, qt_n * tq, lpad), BF16),
        compiler_params=_cparams("arbitrary", "arbitrary", "arbitrary"),
        name="dsa_indexer_topk",
    )(pt, *args)


def _mix_body(ocmp_ref, osel_ref, owin_ref, odsa_ref, conv_ref, misc_ref, mg0_ref, mg1_ref, mg2_ref,
              wa_ref, wb_ref, wc_ref, z_ref):
    lane128 = lax.broadcasted_iota(jnp.int32, (1, LANES), 1)
    misc = misc_ref[...]

    def gate(kind, h):
        return jnp.sum(jnp.where(lane128 == MISC_NG + kind * NSA_HEADS + h, misc, 0.0), axis=-1, keepdims=True)

    parts = []
    for h in range(NSA_HEADS):
        sl = slice(h * HEAD_DIM, (h + 1) * HEAD_DIM)
        parts.append((gate(0, h) * ocmp_ref[:, sl] + gate(1, h) * osel_ref[:, sl]
                      + gate(2, h) * owin_ref[:, sl]).astype(BF16))
    o_nsa = jnp.concatenate(parts, axis=1)
    p_a = jnp.dot(o_nsa, wa_ref[...], preferred_element_type=F32)
    p_b = jnp.dot(odsa_ref[...].astype(BF16), wb_ref[...], preferred_element_type=F32)
    p_c = jnp.dot(conv_ref[...].astype(BF16), wc_ref[...], preferred_element_type=F32)
    z_ref[...] = (mg0_ref[...] * p_a + mg1_ref[...] * p_b + mg2_ref[...] * p_c).astype(BF16)


def _mix(hp, o_cmp, o_sel, o_win, o_dsa, conv_out, wa, wb, wc, tm):
    n = hp.shape[0]
    row = lambda w, cb=0: pl.BlockSpec((tm, w), lambda i: (i, cb))
    full = lambda a: pl.BlockSpec(a.shape, lambda i: (0, 0))
    return pl.pallas_call(
        _mix_body,
        grid=(n // tm,),
        in_specs=[row(1024), row(1024), row(1024), row(512), row(512), row(LANES, C_MISC // LANES),
                  row(D_MODEL, C_MG // D_MODEL), row(D_MODEL, C_MG // D_MODEL + 1), row(D_MODEL, C_MG // D_MODEL + 2),
                  full(wa), full(wb), full(wc)],
        out_specs=row(D_MODEL),
        out_shape=jax.ShapeDtypeStruct((n, D_MODEL), BF16),
        compiler_params=_cparams("arbitrary"),
        name="branch_merge",
    )(o_cmp, o_sel, o_win, o_dsa, conv_out, hp, hp, hp, hp, wa, wb, wc)


def _outln_body(z_ref, wo_ref, x_ref, g_ref, b_ref, wr_ref, rb_ref, x1_ref, gate_ref, *, tm):
    y = jnp.dot(z_ref[...], wo_ref[...], preferred_element_type=F32)
    x1 = _layernorm(ALPHA * x_ref[...] + y, g_ref[...], b_ref[...])
    x1_ref[...] = x1
    x_hi = x1.astype(BF16)
    x_lo = (x1 - x_hi.astype(F32)).astype(BF16)
    w_hi = wr_ref[...].astype(BF16)
    w_lo = (wr_ref[...] - w_hi.astype(F32)).astype(BF16)
    aff = _sigmoid(_dot_nt(w_hi, x_hi) + (_dot_nt(w_hi, x_lo) + _dot_nt(w_lo, x_hi)))
    biased = aff + rb_ref[:, 0:1]
    rows = [biased[e:e + 1, :] for e in range(N_EXPERTS)]
    best = None
    g_best = jnp.zeros((1, tm), jnp.int32)
    for g in range(N_GROUPS):
        v = rows[g * EXPERTS_PER_GROUP:(g + 1) * EXPERTS_PER_GROUP]
        score = None
        for a in range(EXPERTS_PER_GROUP):
            for c in range(a + 1, EXPERTS_PER_GROUP):
                pair = v[a] + v[c]
                score = pair if score is None else jnp.maximum(score, pair)
        if best is None:
            best = score
        else:
            better = score > best
            best = jnp.where(better, score, best)
            g_best = jnp.where(better, g, g_best)
    sel_rows = []
    for e in range(N_EXPERTS):
        g = e // EXPERTS_PER_GROUP
        rank = jnp.zeros((1, tm), F32)
        for o in range(g * EXPERTS_PER_GROUP, (g + 1) * EXPERTS_PER_GROUP):
            if o == e:
                continue
            beats = (rows[o] > rows[e]) | ((rows[o] == rows[e]) & (o < e))
            rank = rank + jnp.where(beats, 1.0, 0.0)
        sel_rows.append(jnp.where((g_best == g) & (rank < 2), aff[e:e + 1, :], 0.0))
    tot = sel_rows[0]
    for e in range(1, N_EXPERTS):
        tot = tot + sel_rows[e]
    gate_t = jnp.concatenate(sel_rows + [jnp.zeros((LANES - N_EXPERTS, tm), F32)], axis=0) / tot
    gate_ref[...] = gate_t.T


def _outln(z, wo, x, g, b, wr_t, rb, tm):
    n = z.shape[0]
    row = lambda w: pl.BlockSpec((tm, w), lambda i: (i, 0))
    full = lambda a: pl.BlockSpec(a.shape, lambda i: (0, 0))
    return pl.pallas_call(
        functools.partial(_outln_body, tm=tm),
        grid=(n // tm,),
        in_specs=[row(D_MODEL), full(wo), row(D_MODEL), full(g), full(b), full(wr_t), full(rb)],
        out_specs=[row(D_MODEL), row(LANES)],
        out_shape=[jax.ShapeDtypeStruct((n, D_MODEL), F32), jax.ShapeDtypeStruct((n, LANES), F32)],
        compiler_params=_cparams("arbitrary"),
        name="out_proj_ln_router",
    )(z, wo, x, g, b, wr_t, rb)


def _moe_body(x_ref, gate_ref, wg_ref, wu_ref, wd_ref, g_ref, b_ref, o_ref, xb_ref, acc_ref):
    e = pl.program_id(1)

    @pl.when(e == 0)
    def _():
        xb_ref[...] = x_ref[...].astype(BF16)
        acc_ref[...] = jnp.zeros(acc_ref.shape, F32)

    lane128 = lax.broadcasted_iota(jnp.int32, (1, LANES), 1)
    gcol = jnp.sum(jnp.where(lane128 == e, gate_ref[...], 0.0), axis=-1, keepdims=True)
    xb = xb_ref[...]
    hg = jnp.dot(xb, wg_ref[0], preferred_element_type=F32)
    hu = jnp.dot(xb, wu_ref[0], preferred_element_type=F32)
    h = (hg * _sigmoid(hg)) * hu * gcol
    acc_ref[...] += jnp.dot(h.astype(BF16), wd_ref[0], preferred_element_type=F32)

    @pl.when(e == N_EXPERTS - 1)
    def _():
        o_ref[...] = _layernorm(ALPHA * x_ref[...] + acc_ref[...], g_ref[...], b_ref[...])


def _moe(x1, gate, wg, wu, wd, g, b, tm):
    n = x1.shape[0]
    return pl.pallas_call(
        _moe_body,
        grid=(n // tm, N_EXPERTS),
        in_specs=[pl.BlockSpec((tm, D_MODEL), lambda i, e: (i, 0)),
                  pl.BlockSpec((tm, LANES), lambda i, e: (i, 0)),
                  pl.BlockSpec((1, D_MODEL, D_FF), lambda i, e: (e, 0, 0)),
                  pl.BlockSpec((1, D_MODEL, D_FF), lambda i, e: (e, 0, 0)),
                  pl.BlockSpec((1, D_FF, D_MODEL), lambda i, e: (e, 0, 0)),
                  pl.BlockSpec((1, D_MODEL), lambda i, e: (0, 0)),
                  pl.BlockSpec((1, D_MODEL), lambda i, e: (0, 0))],
        out_specs=pl.BlockSpec((tm, D_MODEL), lambda i, e: (i, 0)),
        out_shape=jax.ShapeDtypeStruct((n, D_MODEL), F32),
        scratch_shapes=[pltpu.VMEM((tm, D_MODEL), BF16), pltpu.VMEM((tm, D_MODEL), F32)],
        compiler_params=_cparams("arbitrary", "arbitrary"),
        name="moe_ln",
    )(x1, gate, wg, wu, wd, g, b)


def _overlap_matrix(ncp, nselp):
    cs = np.arange(ncp)[:, None] * CMP_STRIDE
    ss = np.arange(nselp)[None, :] * SEL_BLOCK
    return jnp.asarray(((cs < ss + SEL_BLOCK) & (cs + CMP_LEN > ss)).astype(np.float32))


class _Group:
    def __init__(self, nb, t_real, t_pad, q0, past_len, tm, tq):
        self.nb, self.t_real, self.t_pad, self.q0, self.past_len, self.tm, self.tq = nb, t_real, t_pad, q0, past_len, tm, tq
        self.paged = past_len > 0
        self.lp = past_len if self.paged else t_pad
        self.ltot = self.lp + (t_real if self.paged else 0)
        self.lpad = self.lp + (LANES if self.paged else 0)
        self.qt_n = t_pad // tq
        n_chunks = -(-self.ltot // CMP_STRIDE)
        self.n_cmp = n_chunks - CMP_LEN // CMP_STRIDE + 1
        self.n_sel = -(-self.ltot // SEL_BLOCK)
        self.nselp = -(-self.n_sel // LANES) * LANES
        self.n_keep = min(DSA_TOPK, self.ltot // 4)


def _mixer(gp, x, lw, caches):
    hp = _proj(x.astype(BF16), lw["w_in"], gp.tabs, lw["kinds"], min(2 * gp.tm, gp.nb * gp.t_pad, gp.tabs.shape[1]))
    nb, tq, qt_n, q0 = gp.nb, gp.tq, gp.qt_n, gp.q0
    if gp.paged:
        l, nsa_pool, dsa_pool, kidx_pool, win_src, s_conv, pt = caches
        cmp_lay = _PageLayout("cache", 0, 0, base=0, layer=l, nslot=2)
        slc_lay = _PageLayout("cache", 0, 1, base=0, layer=l, nslot=2)
        dsa_lay = _PageLayout("cache", 0, 0, base=0, layer=l, nslot=2)
        kidx_lay = _PageLayout("cache4", IDX_DIM, 0, base=0, layer=l, nslot=0)
        tail = hp
        nwb, win_k0, win_layer = WINDOW // PAGE, PAST_LEN - WINDOW, l
        past8 = jnp.concatenate([jnp.zeros((nb, SUBLANES - 2, CONV_DIM), F32), s_conv[l]], axis=1)
    else:
        pt = gp.pt
        nsa_pool = dsa_pool = kidx_pool = hp.reshape(nb * gp.t_pad // PAGE, PAGE, NCOL)
        cmp_lay = _PageLayout("cols", 512, C_CMP // 512, base=0, layer=0, nslot=0)
        slc_lay = _PageLayout("cols", 512, C_SLC // 512, base=0, layer=0, nslot=0)
        dsa_lay = _PageLayout("cols", 512, C_DSA // 512, base=0, layer=0, nslot=0)
        kidx_lay = _PageLayout("cols", LANES, C_MISC // LANES, base=0, layer=0, nslot=0)
        tail = None
        win_src = hp
        nwb, win_k0, win_layer = WINDOW // PAGE + 1, 0, 0
        past8 = jnp.zeros((nb, SUBLANES, CONV_DIM), F32)

    conv_out, cu = _conv(hp, past8, lw["conv_w8"], min(gp.tm, gp.t_pad), gp.t_pad // min(gp.tm, gp.t_pad))

    ab = _cmp1(nsa_pool, cmp_lay, pt, lw["wcat"], lw["pe"])
    cmp = _cmp2(ab, tail, lw["pe"], lw["w1r"], lw["phi_w2"], gp.t_real if gp.paged else 0, C_CMP // 512)
    o_cmp, o_win, kmask = _nsa1(hp, cmp, gp.ov, win_src, tail, nb=nb, tq=tq, qt_n=qt_n, q0=q0, n_cmp=gp.n_cmp,
                                n_sel=gp.n_sel, lpad=gp.lpad, nwb=nwb, win_k0=win_k0, win_layer=win_layer)
    o_sel = _mattn(hp, C_QROT, NSA_HEADS, kmask, nsa_pool, slc_lay, pt, tail, C_SLC // 512,
                   nb=nb, tq=tq, qt_n=qt_n, q0=q0, name="nsa_selected_attn")
    dmask = _dsa1(hp, kidx_pool, kidx_lay, pt, tail, nb=nb, tq=tq, qt_n=qt_n, q0=q0, lpad=gp.lpad,
                  n_keep=gp.n_keep)
    o_dsa = _mattn(hp, C_DQ, DSA_HEADS, dmask, dsa_pool, dsa_lay, pt, tail, C_DSA // 512,
                   nb=nb, tq=tq, qt_n=qt_n, q0=q0, name="dsa_topk_attn")
    z = _mix(hp, o_cmp, o_sel, o_win, o_dsa, conv_out, lw["w_a"], lw["w_b"], lw["w_c"], min(gp.tm, 256))
    return z, hp, cu


def _layer(gp, x, lw, caches):
    z, hp, cu = _mixer(gp, x, lw, caches)
    tm2 = min(gp.tm, 256)
    x1, gate = _outln(z, lw["w_o"], x, lw["ln_mix_g"], lw["ln_mix_b"], lw["wr_t"], lw["rb"], tm2)
    x2 = _moe(x1, gate, lw["w_eg"], lw["w_eu"], lw["w_ed"], lw["ln_ffn_g"], lw["ln_ffn_b"], min(gp.tm, 512))
    return x2, hp, cu


def _layer_weights(l, w_in, nsa_phi_pos, nsa_phi_w1, nsa_phi_w2, conv_w, w_br_a, w_br_b, w_br_c, w_out,
                   ln_mix_g, ln_mix_b, ln_ffn_g, ln_ffn_b, w_router, router_bias, w_e_gate, w_e_up, w_e_down):
    w1r = nsa_phi_w1[l].reshape(2, CMP_LEN, HEAD_DIM, HEAD_DIM)
    wcat = jnp.concatenate([w1r[:, :CMP_STRIDE], w1r[:, CMP_STRIDE:]], axis=-1).astype(BF16)
    return dict(
        w_in=_permute_w_in(w_in, l), kinds=jnp.asarray(_col_kinds()),
        w1r=w1r, wcat=wcat, pe=nsa_phi_pos[l], phi_w2=nsa_phi_w2[l],
        conv_w8=jnp.concatenate([conv_w[l], jnp.zeros((SUBLANES - 3, CONV_DIM), F32)], axis=0),
        w_a=w_br_a[l].astype(BF16), w_b=w_br_b[l].astype(BF16), w_c=w_br_c[l].astype(BF16),
        w_o=w_out[l].astype(BF16),
        ln_mix_g=ln_mix_g[l][None], ln_mix_b=ln_mix_b[l][None],
        ln_ffn_g=ln_ffn_g[l][None], ln_ffn_b=ln_ffn_b[l][None],
        wr_t=w_router.T, rb=jnp.broadcast_to(router_bias[:, None], (N_EXPERTS, LANES)),
        w_eg=w_e_gate[l].astype(BF16), w_eu=w_e_up[l].astype(BF16), w_ed=w_e_down[l].astype(BF16))


def kernel(x_prompt, x_sample, cache_nsa_kv, cache_dsa_kv, cache_dsa_kidx, state_nsa_win, state_conv, page_table,
           w_in, nsa_phi_pos, nsa_phi_w1, nsa_phi_w2, conv_w, w_br_a, w_br_b, w_br_c, w_out, ln_mix_g, ln_mix_b,
           ln_ffn_g, ln_ffn_b, w_router, router_bias, w_e_gate, w_e_up, w_e_down):
    bp, tp, _ = x_prompt.shape
    bs, ts, _ = x_sample.shape
    ts_pad = SUBLANES

    gp_p = _Group(bp, tp, tp, 0, 0, tm=512, tq=128)
    gp_p.tabs = _rope_tables(jnp.arange(tp, dtype=jnp.int32))
    gp_p.pt = jnp.arange(bp * tp // PAGE, dtype=jnp.int32).reshape(bp, tp // PAGE)
    gp_p.ov = _overlap_matrix(LANES, gp_p.nselp)
    gp_s = _Group(bs, ts, ts_pad, PAST_LEN, PAST_LEN, tm=bs * ts_pad, tq=ts_pad)
    pos_s = PAST_LEN + jnp.arange(ts_pad, dtype=jnp.int32)
    gp_s.tabs = jnp.tile(_rope_tables(pos_s), (1, bs, 1))
    gp_s.ov = _overlap_matrix(PAST_LEN // CMP_STRIDE, gp_s.nselp)

    xp = x_prompt.reshape(bp * tp, D_MODEL)
    xs = jnp.concatenate([x_sample, jnp.zeros((bs, ts_pad - ts, D_MODEL), F32)], axis=1).reshape(bs * ts_pad, D_MODEL)

    outs_p = [[] for _ in range(5)]
    outs_s = [[] for _ in range(5)]
    for l in range(DEPTH):
        lw = _layer_weights(l, w_in, nsa_phi_pos, nsa_phi_w1, nsa_phi_w2, conv_w, w_br_a, w_br_b, w_br_c, w_out,
                            ln_mix_g, ln_mix_b, ln_ffn_g, ln_ffn_b, w_router, router_bias, w_e_gate, w_e_up, w_e_down)
        xp, hp_p, cu_p = _layer(gp_p, xp, lw, None)
        xs, hp_s, cu_s = _layer(gp_s, xs, lw, (l, cache_nsa_kv, cache_dsa_kv, cache_dsa_kidx,
                                                 state_nsa_win, state_conv, page_table))
        h3 = hp_p.reshape(bp, tp, NCOL)
        outs_p[0].append(h3[:, :, C_CMP:C_CMP + 1024].reshape(bp, tp, 4, NSA_KV, HEAD_DIM))
        outs_p[1].append(h3[:, :, C_DSA:C_DSA + 512].reshape(bp, tp, 2, DSA_KV, HEAD_DIM))
        outs_p[2].append(h3[:, :, C_MISC:C_MISC + IDX_DIM])
        outs_p[3].append(h3[:, tp - min(WINDOW, tp):, C_WIN:C_WIN + 512].reshape(bp, min(WINDOW, tp), 2, NSA_KV, HEAD_DIM))
        outs_p[4].append(cu_p.reshape(bp, tp, CONV_DIM)[:, tp - 2:])
        s3 = hp_s.reshape(bs, ts_pad, NCOL)[:, :ts]
        outs_s[0].append(s3[:, :, C_CMP:C_CMP + 1024].reshape(bs, ts, 4, NSA_KV, HEAD_DIM))
        outs_s[1].append(s3[:, :, C_DSA:C_DSA + 512].reshape(bs, ts, 2, DSA_KV, HEAD_DIM))
        outs_s[2].append(s3[:, :, C_MISC:C_MISC + IDX_DIM])
        win_new = s3[:, :, C_WIN:C_WIN + 512].reshape(bs, ts, 2, NSA_KV, HEAD_DIM)
        wb = state_nsa_win.shape[2]
        outs_s[3].append(jnp.concatenate([state_nsa_win[l], win_new], axis=1)[:, -wb:])
        ext = jnp.concatenate([state_conv[l], cu_s.reshape(bs, ts_pad, CONV_DIM)[:, :ts]], axis=1)
        outs_s[4].append(ext[:, -2:])
    sp = [jnp.stack(a, axis=0) for a in outs_p]
    ss = [jnp.stack(a, axis=0) for a in outs_s]
    y_p = xp.reshape(bp, tp, D_MODEL)
    y_s = xs.reshape(bs, ts_pad, D_MODEL)[:, :ts]
    return (y_p, y_s, sp[0], ss[0], sp[1], ss[1], sp[2], ss[2], sp[3], ss[3], sp[4], ss[4])
```

```python
import collections
import functools

import numpy as np
import jax
import jax.numpy as jnp
from jax import lax
from jax.experimental import pallas as pl
from jax.experimental.pallas import tpu as pltpu

F32 = jnp.float32
BF16 = jnp.bfloat16
HIGHEST = lax.Precision.HIGHEST
NEG_INF = float("-inf")

D_MODEL = 2048
DEPTH = 2
PAST_LEN = 16384
PAGE = 128
HEAD_DIM = 128
ROPE_THETA = 500000.0
NSA_HEADS = 8
NSA_KV = 2
CMP_LEN = 32
CMP_STRIDE = 16
SEL_BLOCK = 64
SEL_TOP = 16
WINDOW = 512
FORCE_SCORE = 1e4
DSA_HEADS = 4
DSA_KV = 2
IDX_HEADS = 4
IDX_DIM = 64
DSA_TOPK = 256
CONV_DIM = 512
N_EXPERTS = 16
N_GROUPS = 4
EXPERTS_PER_GROUP = 4
D_FF = 512
LN_EPS = 1e-5
ALPHA = (2 * DEPTH) ** 0.25
IN_WIDTHS = (1024, 1536, 24, 512, 512, 256, 64, 4, 1536, 6144)
ATT_SCALE = HEAD_DIM ** -0.5

LANES = 128
SUBLANES = 8
VMEM_LIMIT = 56 * 1024 * 1024

C_QRAW = 0
C_QROT = 1024
C_CMP = 2048
C_SLC = 2560
C_WIN = 3072
C_DQ = 3584
C_DSA = 4096
C_CV = 4608
C_MG = 6144
C_IQ = 12288
C_MISC = 12544
NCOL = 12800
MISC_IW = 64
MISC_NG = 68
PROJ_TN = 512
PAGES_PER_STEP = 16
KSTEP = PAGES_PER_STEP * PAGE

K_PLAIN, K_ROPE128, K_ROPE64, K_SIGMOID, K_MISC = 0, 1, 2, 3, 4


def _col_kinds():
    kinds = np.zeros(NCOL // LANES, np.int32)

    def mark(c0, n, k):
        kinds[c0 // LANES:(c0 + n) // LANES] = k

    mark(C_QROT, 1024, K_ROPE128)
    mark(C_SLC, 256, K_ROPE128)
    mark(C_WIN, 256, K_ROPE128)
    mark(C_DQ, 512, K_ROPE128)
    mark(C_DSA, 256, K_ROPE128)
    mark(C_IQ, 256, K_ROPE64)
    mark(C_MISC, 128, K_MISC)
    mark(C_MG, 6144, K_SIGMOID)
    return kinds


def _cparams(*sem):
    return pltpu.CompilerParams(dimension_semantics=sem, vmem_limit_bytes=VMEM_LIMIT)


def _sigmoid(x):
    return 1.0 / (1.0 + jnp.exp(-x))


def _layernorm(x, g, b):
    mu = jnp.mean(x, axis=-1, keepdims=True)
    xc = x - mu
    var = jnp.mean(xc * xc, axis=-1, keepdims=True)
    return xc * lax.rsqrt(var + LN_EPS) * g + b


def _dot_nt(a, b, precision=None):
    return lax.dot_general(a, b, (((1,), (1,)), ((), ())), preferred_element_type=F32, precision=precision)


def _softmax_parts(s):
    m = jnp.max(s, axis=-1, keepdims=True)
    m = jnp.where(m == NEG_INF, 0.0, m)
    p = jnp.exp(s - m)
    return p, jnp.maximum(jnp.sum(p, axis=-1, keepdims=True), 1e-30)


def _stack_heads(q_ref, g, r_per_g):
    return jnp.concatenate([q_ref[:, (g * r_per_g + r) * HEAD_DIM:(g * r_per_g + r + 1) * HEAD_DIM]
                            for r in range(r_per_g)], axis=0).astype(BF16)


def _proj_body(kinds_ref, x_ref, wt_ref, tab_ref, o_ref, w_ref):
    nsub = PROJ_TN // LANES
    j = pl.program_id(0)

    @pl.when(pl.program_id(1) == 0)
    def _():
        for s in range(nsub):
            w_ref[:, s * LANES:(s + 1) * LANES] = wt_ref[s * LANES:(s + 1) * LANES, :].astype(F32).T.astype(BF16)

    h = jnp.dot(x_ref[...], w_ref[...], preferred_element_type=F32)
    lane = lax.broadcasted_iota(jnp.int32, (1, LANES), 1)

    def rope(hs, t0, sh):
        return (hs * tab_ref[t0] + pltpu.roll(hs, sh, 1) * tab_ref[t0 + 1]
                + pltpu.roll(hs, LANES - sh, 1) * tab_ref[t0 + 2])

    for s in range(nsub):
        kind = kinds_ref[j * nsub + s]
        hs = h[:, s * LANES:(s + 1) * LANES]
        sl = slice(s * LANES, (s + 1) * LANES)

        @pl.when(kind == K_PLAIN)
        def _():
            o_ref[:, sl] = hs

        @pl.when(kind == K_ROPE128)
        def _():
            o_ref[:, sl] = rope(hs, 0, 16)

        @pl.when(kind == K_ROPE64)
        def _():
            o_ref[:, sl] = rope(hs, 3, 8)

        @pl.when(kind == K_SIGMOID)
        def _():
            o_ref[:, sl] = _sigmoid(hs)

        @pl.when(kind == K_MISC)
        def _():
            r = rope(hs, 3, 8)
            o_ref[:, sl] = jnp.where(lane < MISC_IW, r,
                                     jnp.where(lane < MISC_NG, hs * (IDX_HEADS ** -0.5),
                                               jnp.where(lane < MISC_NG + 24, _sigmoid(hs), hs)))


def _proj(x_bf, wt_bf, tabs, kinds, tm):
    n = x_bf.shape[0]
    n_tab = tabs.shape[1] // tm
    grid = (NCOL // PROJ_TN, n // tm)
    return pl.pallas_call(
        _proj_body,
        grid_spec=pltpu.PrefetchScalarGridSpec(
            num_scalar_prefetch=1, grid=grid,
            in_specs=[pl.BlockSpec((tm, D_MODEL), lambda j, i, k: (i, 0)),
                      pl.BlockSpec((PROJ_TN, D_MODEL), lambda j, i, k: (j, 0)),
                      pl.BlockSpec((6, tm, LANES), lambda j, i, k: (0, i % n_tab, 0))],
            out_specs=pl.BlockSpec((tm, PROJ_TN), lambda j, i, k: (i, j)),
            scratch_shapes=[pltpu.VMEM((D_MODEL, PROJ_TN), BF16)]),
        out_shape=jax.ShapeDtypeStruct((n, NCOL), F32),
        compiler_params=_cparams("arbitrary", "arbitrary"),
        name="in_proj",
    )(kinds, x_bf, wt_bf, tabs)


def _rope_tables(pos):
    out = []
    lane = jnp.arange(LANES)
    for d in (HEAD_DIM, IDX_DIM):
        rot = d // 4
        half = rot // 2
        inv = ROPE_THETA ** (-jnp.arange(half, dtype=F32) / half)
        ang = pos.astype(F32)[:, None] * inv[None, :]
        cos = jnp.cos(ang)
        sin = jnp.sin(ang)
        li = lane % d
        ci = jnp.take(cos, li % half, axis=1)
        si = jnp.take(sin, li % half, axis=1)
        out.append(jnp.where(li[None] < rot, ci, 1.0))
        out.append(jnp.where((li[None] >= half) & (li[None] < rot), si, 0.0))
        out.append(jnp.where(li[None] < half, -si, 0.0))
    return jnp.stack(out, axis=0)


def _permute_w_in(w_in, l):
    wt = jnp.transpose(w_in, (2, 0, 1))[:, l, :].astype(BF16)
    offs = np.cumsum((0,) + IN_WIDTHS)
    nq, nkv, ng, dq, dkv, iq, ik, iw, cv, mg = [wt[offs[i]:offs[i + 1]] for i in range(10)]
    z = lambda n: jnp.zeros((n, wt.shape[1]), BF16)
    rows = [nq, nq, nkv, dq, dkv, cv, mg, iq, ik, iw, ng, z(LANES - 92), z(NCOL - C_MISC - LANES)]
    return jnp.concatenate(rows, axis=0)


def _conv_body(cv_ref, prev_ref, past_ref, w_ref, y_ref, cu_ref, s_ref, *, tiles_per_seq, tm):
    i = pl.program_id(0)
    b = cv_ref[:, 0:CONV_DIM]
    cu = cv_ref[:, CONV_DIM:2 * CONV_DIM] * cv_ref[:, 2 * CONV_DIM:3 * CONV_DIM]
    first = (i % tiles_per_seq) == 0
    prev = prev_ref[:, CONV_DIM:2 * CONV_DIM] * prev_ref[:, 2 * CONV_DIM:3 * CONV_DIM]
    s_ref[0:SUBLANES, :] = jnp.where(first, past_ref[0], prev)
    s_ref[SUBLANES:SUBLANES + tm, :] = cu
    y = (w_ref[0:1, :] * s_ref[pl.ds(SUBLANES - 2, tm), :] + w_ref[1:2, :] * s_ref[pl.ds(SUBLANES - 1, tm), :]
         + w_ref[2:3, :] * cu)
    y_ref[...] = b * y
    cu_ref[...] = cu


def _conv(hp, past8, conv_w8, tm, tiles_per_seq):
    n = hp.shape[0]
    cvb = C_CV // (3 * CONV_DIM)
    rb = tm // SUBLANES
    return pl.pallas_call(
        functools.partial(_conv_body, tiles_per_seq=tiles_per_seq, tm=tm),
        grid=(n // tm,),
        in_specs=[pl.BlockSpec((tm, 3 * CONV_DIM), lambda i: (i, cvb)),
                  pl.BlockSpec((SUBLANES, 3 * CONV_DIM), lambda i: (jnp.maximum(i * rb - 1, 0), cvb)),
                  pl.BlockSpec((1, SUBLANES, CONV_DIM), lambda i: (i // tiles_per_seq, 0, 0)),
                  pl.BlockSpec((SUBLANES, CONV_DIM), lambda i: (0, 0))],
        out_specs=[pl.BlockSpec((tm, CONV_DIM), lambda i: (i, 0)),
                   pl.BlockSpec((tm, CONV_DIM), lambda i: (i, 0))],
        out_shape=[jax.ShapeDtypeStruct((n, CONV_DIM), F32), jax.ShapeDtypeStruct((n, CONV_DIM), F32)],
        scratch_shapes=[pltpu.VMEM((tm + SUBLANES, CONV_DIM), F32)],
        compiler_params=_cparams("arbitrary"),
        name="short_conv",
    )(hp, hp, past8, conv_w8)


_PageLayout = collections.namedtuple("_PageLayout", "kind width col_block base layer nslot")


def _page_specs(layout, n_lead, kg_n=None):
    def spec(k):
        def imap(*a):
            ids, pt = a[:n_lead], a[-1]
            kg = ids[-1] % kg_n if kg_n else ids[-1]
            page = pt[ids[0], kg * PAGES_PER_STEP + k]
            if layout.kind == "cache":
                return (layout.layer, page, 0, layout.col_block, 0, 0)
            if layout.kind == "cache4":
                return (layout.layer, page, 0, 0)
            cb = layout.col_block(*ids) if callable(layout.col_block) else layout.col_block
            return (page, 0, cb)
        shape = {"cache": (1, 1, PAGE, layout.nslot, 2, HEAD_DIM), "cache4": (1, 1, PAGE, layout.width),
                 "cols": (1, PAGE, layout.width)}[layout.kind]
        return pl.BlockSpec(shape, imap)
    return [spec(k) for k in range(PAGES_PER_STEP)]


def _page_tile(layout, ref, j, width=LANES):
    if layout.kind == "cache":
        slot, g = divmod(layout.base + j, 2)
        return ref[0, 0, :, slot, g, :]
    if layout.kind == "cache4":
        return ref[0, 0, :, j * width:(j + 1) * width]
    return ref[0, :, j * width:(j + 1) * width]


def _chunk_rows(layout, ref, j, p):
    if layout.kind == "cache":
        slot, g = divmod(layout.base + j, 2)
        return ref[0, 0, pl.ds(p, SUBLANES, stride=CMP_STRIDE), slot, g, :]
    return ref[0, pl.ds(p, SUBLANES, stride=CMP_STRIDE), :]


def _cmp1_body(pt_ref, *refs, layout, n_inner):
    pages = refs[:PAGES_PER_STEP]
    w_ref, pe_ref = refs[PAGES_PER_STEP:PAGES_PER_STEP + 2]
    o_ref = refs[PAGES_PER_STEP + 2]
    for j in range(n_inner):
        slot = j // 2
        acc_a = jnp.zeros((PAGE, HEAD_DIM), F32)
        acc_b = jnp.zeros((PAGE, HEAD_DIM), F32)
        for p in range(CMP_STRIDE):
            xp = jnp.concatenate([_chunk_rows(layout, pg, j, p) for pg in pages], axis=0)
            xa = (xp + pe_ref[slot, p:p + 1, :]).astype(BF16)
            xb = (xp + pe_ref[slot, CMP_STRIDE + p:CMP_STRIDE + p + 1, :]).astype(BF16)
            acc_a = acc_a + jnp.dot(xa, w_ref[slot, p, :, 0:HEAD_DIM], preferred_element_type=F32)
            acc_b = acc_b + jnp.dot(xb, w_ref[slot, p, :, HEAD_DIM:2 * HEAD_DIM], preferred_element_type=F32)
        o_ref[0, :, j * 256:j * 256 + HEAD_DIM] = acc_a
        o_ref[0, :, j * 256 + HEAD_DIM:(j + 1) * 256] = acc_b


def _cmp1(pool, layout, pt, wcat, pe):
    nb, npg = pt.shape
    kg = npg // PAGES_PER_STEP
    if layout.kind == "cache":
        n_inner, n_sg, wsel, osel, ow = 4, 1, (lambda sg: 0), (lambda sg: 0), 1024
        wblk = 2
    else:
        cb = layout.col_block
        layout = layout._replace(width=HEAD_DIM, col_block=lambda b, sg, k: cb * 4 + sg)
        n_inner, n_sg, wsel, osel, ow = 1, 4, (lambda sg: sg // 2), (lambda sg: sg), 256
        wblk = 1
    return pl.pallas_call(
        functools.partial(_cmp1_body, layout=layout, n_inner=n_inner),
        grid_spec=pltpu.PrefetchScalarGridSpec(
            num_scalar_prefetch=1, grid=(nb, n_sg, kg),
            in_specs=_page_specs(layout, 3)
            + [pl.BlockSpec((wblk, 16, HEAD_DIM, 256), lambda b, sg, k, pt: (wsel(sg), 0, 0, 0)),
               pl.BlockSpec((wblk, CMP_LEN, HEAD_DIM), lambda b, sg, k, pt: (wsel(sg), 0, 0))],
            out_specs=pl.BlockSpec((1, PAGE, ow), lambda b, sg, k, pt: (b, k, osel(sg)))),
        out_shape=jax.ShapeDtypeStruct((nb, npg * SUBLANES, 1024), F32),
        compiler_params=_cparams("arbitrary", "arbitrary", "arbitrary"),
        name="nsa_compress1",
    )(pt, *([pool] * PAGES_PER_STEP), wcat, pe)


def _gelu_tanh(x):
    return 0.5 * x * (1.0 + jnp.tanh(np.sqrt(2.0 / np.pi).astype(np.float32) * (x + 0.044715 * (x * x * x))))


def _cmp2_body(ab_ref, tail_ref, pe_ref, w1_ref, w2_ref, o_ref, s_ref, *, nc, n_tail):
    row8 = lax.broadcasted_iota(jnp.int32, (SUBLANES, 1), 0)
    row16 = lax.broadcasted_iota(jnp.int32, (CMP_STRIDE, 1), 0)
    for sg in range(4):
        slot = sg // 2
        a = ab_ref[0, :, sg * 256:sg * 256 + HEAD_DIM]
        s_ref[0:nc, :] = ab_ref[0, :, sg * 256 + HEAD_DIM:(sg + 1) * 256]
        tb = jnp.zeros((SUBLANES, HEAD_DIM), F32)
        if n_tail:
            x8 = jnp.where(row8 < n_tail, tail_ref[:, sg * HEAD_DIM:(sg + 1) * HEAD_DIM], 0.0)
            x16 = jnp.concatenate([x8, jnp.zeros((CMP_STRIDE - SUBLANES, HEAD_DIM), F32)], axis=0)
            x16 = x16 + pe_ref[slot, CMP_STRIDE:CMP_LEN, :]
            t16 = jnp.zeros((CMP_STRIDE, HEAD_DIM), F32)
            for p in range(CMP_STRIDE):
                xm = jnp.where(row16 == p, x16, 0.0).astype(BF16)
                t16 = t16 + jnp.dot(xm, w1_ref[slot, CMP_STRIDE + p].astype(BF16), preferred_element_type=F32)
            tb = jnp.sum(t16, axis=0, keepdims=True) * jnp.where(row8 == 0, 1.0, 0.0)
        s_ref[nc:nc + SUBLANES, :] = tb
        pre = a + s_ref[pl.ds(1, nc), :]
        o_ref[0, :, sg * HEAD_DIM:(sg + 1) * HEAD_DIM] = jnp.dot(
            _gelu_tanh(pre).astype(BF16), w2_ref[slot].astype(BF16), preferred_element_type=F32)


def _cmp2(ab, tail, pe, w1r, w2, n_tail, tail_col_block):
    nb, nc, _ = ab.shape
    if tail is None:
        tail = jnp.zeros((nb * SUBLANES, 512), F32)
        tail_col_block = 0
    return pl.pallas_call(
        functools.partial(_cmp2_body, nc=nc, n_tail=n_tail),
        grid=(nb,),
        in_specs=[pl.BlockSpec((1, nc, 1024), lambda b: (b, 0, 0)),
                  pl.BlockSpec((SUBLANES, 512), lambda b: (b, tail_col_block)),
                  pl.BlockSpec((2, CMP_LEN, HEAD_DIM), lambda b: (0, 0, 0)),
                  pl.BlockSpec((2, CMP_LEN, HEAD_DIM, HEAD_DIM), lambda b: (0, 0, 0, 0)),
                  pl.BlockSpec((2, HEAD_DIM, HEAD_DIM), lambda b: (0, 0, 0))],
        out_specs=pl.BlockSpec((1, nc, 512), lambda b: (b, 0, 0)),
        out_shape=jax.ShapeDtypeStruct((nb, nc, 512), F32),
        scratch_shapes=[pltpu.VMEM((nc + SUBLANES, HEAD_DIM), F32)],
        compiler_params=_cparams("arbitrary"),
        name="nsa_compress2",
    )(ab, tail, pe, w1r, w2)


def _nsa1_body(*refs, tq, q0, n_cmp, ncp, n_sel, nselp, lpad, nwb, has_tail, win_k0):
    qraw_ref, qrot_ref, cmp_ref, ov_ref = refs[:4]
    wins = refs[4:4 + nwb]
    pos = 4 + nwb
    tail_ref = None
    if has_tail:
        tail_ref = refs[pos]
        pos += 1
    ocmp_ref, owin_ref, kmask_ref = refs[pos:pos + 3]
    qt = pl.program_id(1)
    r_per_g = NSA_HEADS // NSA_KV
    qpos = q0 + qt * tq + lax.broadcasted_iota(jnp.int32, (tq, 1), 0)

    jj = lax.broadcasted_iota(jnp.int32, (1, ncp), 1)
    cmask = (jj * CMP_STRIDE + (CMP_LEN - 1) <= qpos) & (jj < n_cmp)
    blk = lax.broadcasted_iota(jnp.int32, (1, nselp), 1)
    cur = qpos // SEL_BLOCK
    forced = (blk == 0) | (blk == cur) | (blk == cur - 1)
    e_row = lax.broadcasted_iota(jnp.int32, (KSTEP // SEL_BLOCK, KSTEP), 0)
    e_col = lax.broadcasted_iota(jnp.int32, (KSTEP // SEL_BLOCK, KSTEP), 1)
    expand = jnp.where(e_col // SEL_BLOCK == e_row, 1.0, 0.0).astype(BF16)
    lane128 = lax.broadcasted_iota(jnp.int32, (1, LANES), 1)
    for g in range(NSA_KV):
        kc = cmp_ref[0, :, g * HEAD_DIM:(g + 1) * HEAD_DIM].astype(BF16)
        vc = cmp_ref[0, :, (2 + g) * HEAD_DIM:(3 + g) * HEAD_DIM].astype(BF16)
        q = _stack_heads(qraw_ref, g, r_per_g)
        s = _dot_nt(q, kc) * ATT_SCALE + jnp.concatenate([jnp.where(cmask, 0.0, NEG_INF)] * r_per_g, axis=0)
        p, den = _softmax_parts(s)
        p = p / den
        o = jnp.dot(p.astype(BF16), vc, preferred_element_type=F32)
        psum = jnp.zeros((tq, ncp), F32)
        for r in range(r_per_g):
            h = g * r_per_g + r
            psum = psum + p[r * tq:(r + 1) * tq]
            ocmp_ref[:, h * HEAD_DIM:(h + 1) * HEAD_DIM] = o[r * tq:(r + 1) * tq]
        imp = jnp.dot(psum.astype(BF16), ov_ref[...].astype(BF16), preferred_element_type=F32)
        imp = jnp.where(forced, imp + FORCE_SCORE, imp)
        imp = jnp.where((blk <= cur) & (blk < n_sel), imp, NEG_INF)
        rank = jnp.zeros((tq, nselp), F32)
        for i in range(n_sel):
            vi = imp[:, i:i + 1]
            beats = (vi > imp) | ((vi == imp) & (blk > i))
            rank = rank + jnp.where(beats, 1.0, 0.0)
        sel = jnp.where((rank < min(SEL_TOP, n_sel)) & (blk < n_sel), 1.0, 0.0).astype(BF16)
        per = KSTEP // SEL_BLOCK
        for c in range(lpad // KSTEP):
            km = jnp.dot(sel[:, c * per:(c + 1) * per], expand, preferred_element_type=F32)
            kmask_ref[0, g, :, c * KSTEP:(c + 1) * KSTEP] = km.astype(BF16)
        if lpad % KSTEP:
            b0 = (lpad // KSTEP) * per
            km = jnp.where(lane128 < SEL_BLOCK, sel[:, b0:b0 + 1].astype(F32), 0.0)
            kmask_ref[0, g, :, (lpad // KSTEP) * KSTEP:lpad] = jnp.broadcast_to(km, (tq, LANES)).astype(BF16)

    nk = nwb * PAGE + (LANES if has_tail else 0)
    kk = lax.broadcasted_iota(jnp.int32, (1, nk), 1)
    if has_tail:
        kpos = jnp.where(kk < nwb * PAGE, win_k0 + kk, q0 + kk - nwb * PAGE)
    else:
        kpos = (qt - (nwb - 1)) * PAGE + kk
    rel = qpos - kpos
    wmask = (rel >= 0) & (rel < WINDOW) & (kpos >= 0)
    for g in range(NSA_KV):
        if has_tail:
            kparts = [w[0, 0, :, 0, g, :] for w in wins]
            vparts = [w[0, 0, :, 1, g, :] for w in wins]
        else:
            kparts = [w[:, g * HEAD_DIM:(g + 1) * HEAD_DIM] for w in wins]
            vparts = [w[:, (2 + g) * HEAD_DIM:(3 + g) * HEAD_DIM] for w in wins]
        if has_tail:
            zpad = jnp.zeros((LANES - SUBLANES, HEAD_DIM), F32)
            kparts += [tail_ref[:, g * HEAD_DIM:(g + 1) * HEAD_DIM], zpad]
            vparts += [tail_ref[:, (2 + g) * HEAD_DIM:(3 + g) * HEAD_DIM], zpad]
        kw = jnp.concatenate(kparts, axis=0).astype(BF16)
        vw = jnp.concatenate(vparts, axis=0).astype(BF16)
        q = _stack_heads(qrot_ref, g, r_per_g)
        s = _dot_nt(q, kw) * ATT_SCALE + jnp.concatenate([jnp.where(wmask, 0.0, NEG_INF)] * r_per_g, axis=0)
        p, den = _softmax_parts(s)
        o = jnp.dot((p / den).astype(BF16), vw, preferred_element_type=F32)
        for r in range(r_per_g):
            h = g * r_per_g + r
            owin_ref[:, h * HEAD_DIM:(h + 1) * HEAD_DIM] = o[r * tq:(r + 1) * tq]


def _nsa1(hp, cmp, ov, win_src, tail_src, *, nb, tq, qt_n, q0, n_cmp, n_sel, lpad, nwb, win_k0, win_layer):
    n = hp.shape[0]
    ncp = cmp.shape[1]
    nselp = ov.shape[1]
    has_tail = tail_src is not None
    rows = lambda b, t: b * qt_n + t
    in_specs = [pl.BlockSpec((tq, 1024), lambda b, t: (rows(b, t), C_QRAW // 1024)),
                pl.BlockSpec((tq, 1024), lambda b, t: (rows(b, t), C_QROT // 1024)),
                pl.BlockSpec((1, ncp, 512), lambda b, t: (b, 0, 0)),
                pl.BlockSpec((ncp, nselp), lambda b, t: (0, 0))]
    if has_tail:
        in_specs += [pl.BlockSpec((1, 1, PAGE, 2, NSA_KV, HEAD_DIM), lambda b, t, k=k: (win_layer, b, k, 0, 0, 0))
                     for k in range(nwb)]
        in_specs += [pl.BlockSpec((SUBLANES, 512), lambda b, t: (b, C_WIN // 512))]
        args = [win_src] * nwb + [tail_src]
    else:
        in_specs += [pl.BlockSpec((PAGE, 512),
                                  lambda b, t, k=k: (b * qt_n + jnp.maximum(t - (nwb - 1) + k, 0), C_WIN // 512))
                     for k in range(nwb)]
        args = [win_src] * nwb
    return pl.pallas_call(
        functools.partial(_nsa1_body, tq=tq, q0=q0, n_cmp=n_cmp, ncp=ncp, n_sel=n_sel, nselp=nselp, lpad=lpad,
                          nwb=nwb, has_tail=has_tail, win_k0=win_k0),
        grid=(nb, qt_n),
        in_specs=in_specs,
        out_specs=[pl.BlockSpec((tq, 1024), lambda b, t: (rows(b, t), 0)),
                   pl.BlockSpec((tq, 1024), lambda b, t: (rows(b, t), 0)),
                   pl.BlockSpec((1, NSA_KV, tq, lpad), lambda b, t: (b, 0, t, 0))],
        out_shape=[jax.ShapeDtypeStruct((n, 1024), F32), jax.ShapeDtypeStruct((n, 1024), F32),
                   jax.ShapeDtypeStruct((nb, NSA_KV, qt_n * tq, lpad), BF16)],
        compiler_params=_cparams("arbitrary", "arbitrary"),
        name="nsa_cmp_select_window",
    )(hp, hp, cmp, ov, *args)


def _mattn_body(pt_ref, *refs, layout, n_g, r_per_g, gm, tq, q0, kg_n, has_tail):
    q_ref, mask_ref = refs[:2]
    pages = refs[2:2 + PAGES_PER_STEP]
    pos = 2 + PAGES_PER_STEP
    if has_tail:
        tail_ref, tmask_ref = refs[pos:pos + 2]
        pos += 2
    o_ref, m_ref, l_ref, acc_ref = refs[pos:pos + 4]
    qt = pl.program_id(1)
    step = pl.program_id(2)
    two_pass = kg_n > 1 or has_tail
    kg = step % kg_n if two_pass else step
    qpos = q0 + qt * tq + lax.broadcasted_iota(jnp.int32, (tq, 1), 0)

    def scores(g, kt, kpos, mref):
        picked = jnp.where(mref[0, g if gm > 1 else 0].astype(F32) > 0.5, 0.0, NEG_INF)
        bias = jnp.concatenate([picked + jnp.where(kpos <= qpos, 0.0, NEG_INF)] * r_per_g, axis=0)
        return _dot_nt(_stack_heads(q_ref, g, r_per_g), kt) * ATT_SCALE + bias

    def emit(g, o):
        for r in range(r_per_g):
            h = g * r_per_g + r
            o_ref[:, h * HEAD_DIM:(h + 1) * HEAD_DIM] = o[r * tq:(r + 1) * tq]

    def main_keys(g):
        return jnp.concatenate([_page_tile(layout, pg, g) for pg in pages], axis=0).astype(BF16)

    def main_vals(g):
        return jnp.concatenate([_page_tile(layout, pg, n_g + g) for pg in pages], axis=0).astype(BF16)

    def tail_part(g, off):
        zpad = jnp.zeros((LANES - SUBLANES, HEAD_DIM), F32)
        return jnp.concatenate([tail_ref[:, (off + g) * HEAD_DIM:(off + g + 1) * HEAD_DIM], zpad], axis=0).astype(BF16)

    kpos = kg * KSTEP + lax.broadcasted_iota(jnp.int32, (1, KSTEP), 1)
    tpos = q0 + lax.broadcasted_iota(jnp.int32, (1, LANES), 1)

    if not two_pass:
        for g in range(n_g):
            p, den = _softmax_parts(scores(g, main_keys(g), kpos, mask_ref))
            emit(g, jnp.dot((p / den).astype(BF16), main_vals(g), preferred_element_type=F32))
        return

    @pl.when(step == 0)
    def _():
        m_ref[...] = jnp.full(m_ref.shape, NEG_INF, F32)
        l_ref[...] = jnp.zeros(l_ref.shape, F32)
        acc_ref[...] = jnp.zeros(acc_ref.shape, F32)

    def stats(g, s):
        m_old = m_ref[g]
        m_new = jnp.maximum(m_old, jnp.max(s, axis=-1, keepdims=True))
        m_safe = jnp.where(m_new == NEG_INF, 0.0, m_new)
        l_ref[g] = jnp.exp(m_old - m_safe) * l_ref[g] + jnp.sum(jnp.exp(s - m_safe), axis=-1, keepdims=True)
        m_ref[g] = m_new

    def accumulate(g, s, vt):
        m = m_ref[g]
        m_safe = jnp.where(m == NEG_INF, 0.0, m)
        p = jnp.exp(s - m_safe) / jnp.maximum(l_ref[g], 1e-30)
        acc_ref[g] += jnp.dot(p.astype(BF16), vt, preferred_element_type=F32)

    @pl.when(step < kg_n)
    def _():
        for g in range(n_g):
            stats(g, scores(g, main_keys(g), kpos, mask_ref))

    if has_tail:
        @pl.when(step == kg_n - 1)
        def _():
            for g in range(n_g):
                stats(g, scores(g, tail_part(g, 0), tpos, tmask_ref))

    @pl.when(step >= kg_n)
    def _():
        for g in range(n_g):
            accumulate(g, scores(g, main_keys(g), kpos, mask_ref), main_vals(g))

    @pl.when(step == 2 * kg_n - 1)
    def _():
        for g in range(n_g):
            if has_tail:
                accumulate(g, scores(g, tail_part(g, 0), tpos, tmask_ref), tail_part(g, n_g))
            emit(g, acc_ref[g])


def _mattn(hp, q_col, n_heads, mask, pool, layout, pt, tail_src, tail_col_block, *, nb, tq, qt_n, q0, name):
    n = hp.shape[0]
    n_g = 2
    r_per_g = n_heads // n_g
    qw = n_heads * HEAD_DIM
    gm = mask.shape[1]
    kg_n = pt.shape[1] // PAGES_PER_STEP
    has_tail = tail_src is not None
    n_steps = 2 * kg_n if (kg_n > 1 or has_tail) else 1
    in_specs = [pl.BlockSpec((tq, qw), lambda b, t, k, pt: (b * qt_n + t, q_col // qw)),
                pl.BlockSpec((1, gm, tq, KSTEP), lambda b, t, k, pt: (b, 0, t, k % kg_n))]
    in_specs += _page_specs(layout, 3, kg_n)
    args = [hp, mask] + [pool] * PAGES_PER_STEP
    if has_tail:
        in_specs += [pl.BlockSpec((SUBLANES, 512), lambda b, t, k, pt: (b, tail_col_block)),
                     pl.BlockSpec((1, gm, tq, LANES), lambda b, t, k, pt: (b, 0, t, kg_n * KSTEP // LANES))]
        args += [tail_src, mask]
    return pl.pallas_call(
        functools.partial(_mattn_body, layout=layout, n_g=n_g, r_per_g=r_per_g, gm=gm, tq=tq, q0=q0, kg_n=kg_n, has_tail=has_tail),
        grid_spec=pltpu.PrefetchScalarGridSpec(
            num_scalar_prefetch=1, grid=(nb, qt_n, n_steps),
            in_specs=in_specs,
            out_specs=pl.BlockSpec((tq, qw), lambda b, t, k, pt: (b * qt_n + t, 0)),
            scratch_shapes=[pltpu.VMEM((n_g, r_per_g * tq, 1), F32), pltpu.VMEM((n_g, r_per_g * tq, 1), F32),
                            pltpu.VMEM((n_g, r_per_g * tq, HEAD_DIM), F32)]),
        out_shape=jax.ShapeDtypeStruct((n, qw), F32),
        compiler_params=_cparams("arbitrary", "arbitrary", "arbitrary"),
        name=name,
    )(pt, *args)


PACKED_ROWS = 2 * SUBLANES


def _topk_mask_packed_t(sc_t, n_keep):
    n_keys, n_q = sc_t.shape
    i16 = jnp.int16
    bits = pltpu.bitcast(sc_t + 0.0, jnp.int32)
    key = jnp.where(bits < 0, bits ^ jnp.int32(0x7FFFFFFF), bits)
    hi = jnp.right_shift(key, 16).astype(i16)
    lo = ((key & jnp.int32(0xFFFF)) - 32768).astype(i16)
    one_b, zero_b = jnp.asarray(1, BF16), jnp.asarray(0, BF16)
    i16_min, i16_max = jnp.asarray(-32768, i16), jnp.asarray(32767, i16)
    assert n_keys % PACKED_ROWS == 0 and n_keys // PACKED_ROWS <= 256

    def colsum(x01):
        parts = [x01[i * PACKED_ROWS:(i + 1) * PACKED_ROWS, :] for i in range(n_keys // PACKED_ROWS)]
        while len(parts) > 1:
            parts = [parts[i] + parts[i + 1] for i in range(0, len(parts), 2)]
        return jnp.sum(parts[0].astype(F32), axis=0, keepdims=True)

    def search(vals, need, nbits, start):
        def body(i, thr):
            cand = thr + jnp.left_shift(jnp.int32(1), nbits - 1 - i)
            cnt = colsum(jnp.where(vals >= cand.astype(i16), one_b, zero_b))
            return jnp.where(cnt >= need, cand, thr)
        return lax.fori_loop(0, nbits, body, jnp.full((1, n_q), start, jnp.int32))

    t_hi = search(hi, float(n_keep), 16, -32768).astype(i16)
    eq = hi == t_hi
    hi_gt = jnp.where(hi > t_hi, one_b, zero_b)
    lo_m = jnp.where(eq, lo, i16_min)
    t_lo = search(lo_m, n_keep - colsum(hi_gt), 16, -32768).astype(i16)
    gt = hi_gt + jnp.where(lo_m > t_lo, one_b, zero_b)
    need = n_keep - colsum(gt)
    idx = lax.broadcasted_iota(jnp.int32, (n_keys, 1), 0).astype(i16)
    tie_idx = jnp.where(eq, jnp.where(lo == t_lo, idx, i16_max), i16_max)
    nbits = int(np.ceil(np.log2(n_keys)))

    def ibody(i, c):
        cand = c + jnp.left_shift(jnp.int32(1), nbits - 1 - i)
        cnt = colsum(jnp.where(tie_idx < cand.astype(i16), one_b, zero_b))
        return jnp.where(cnt < need, cand, c)

    cut = lax.fori_loop(0, nbits, ibody, jnp.zeros((1, n_q), jnp.int32)).astype(i16)
    return jnp.maximum(gt, jnp.where(tie_idx <= cut, one_b, zero_b))


def _dsa1_body(pt_ref, *refs, layout, tq, q0, kg_n, lpad, has_tail, n_keep):
    iq_ref, misc_ref = refs[:2]
    pages = refs[2:2 + PAGES_PER_STEP]
    pos = 2 + PAGES_PER_STEP
    if has_tail:
        tail_ref = refs[pos]
        pos += 1
    mask_ref, sc_ref = refs[pos:pos + 2]
    qt = pl.program_id(1)
    kg = pl.program_id(2)
    qpos = q0 + qt * tq + lax.broadcasted_iota(jnp.int32, (tq, 1), 0)
    lane128 = lax.broadcasted_iota(jnp.int32, (1, LANES), 1)

    if kg_n == 1 and not has_tail and tq == LANES:
        kt = jnp.concatenate([_page_tile(layout, pg, 0, IDX_DIM) for pg in pages], axis=0).astype(BF16)
        misc_t = misc_ref[...].T
        kpos_c = lax.broadcasted_iota(jnp.int32, (KSTEP, 1), 0)
        qpos_r = q0 + qt * tq + lax.broadcasted_iota(jnp.int32, (1, tq), 1)
        acc = jnp.zeros((KSTEP, tq), F32)
        for h in range(IDX_HEADS):
            qi = iq_ref[:, h * IDX_DIM:(h + 1) * IDX_DIM].astype(BF16)
            acc = acc + misc_t[MISC_IW + h:MISC_IW + h + 1, :] * jnp.maximum(_dot_nt(kt, qi), 0.0)
        sel_t = _topk_mask_packed_t(jnp.where(kpos_c <= qpos_r, acc, NEG_INF), n_keep)
        for c in range(KSTEP // LANES):
            blk = sel_t[c * LANES:(c + 1) * LANES, :].astype(F32).T
            mask_ref[0, 0, :, c * LANES:(c + 1) * LANES] = blk.astype(BF16)
        return

    def scores(kt, kpos):
        acc = jnp.zeros((tq, kt.shape[0]), F32)
        for h in range(IDX_HEADS):
            qi = iq_ref[:, h * IDX_DIM:(h + 1) * IDX_DIM].astype(BF16)
            w = jnp.sum(jnp.where(lane128 == MISC_IW + h, misc_ref[...], 0.0), axis=-1, keepdims=True)
            acc = acc + w * jnp.maximum(_dot_nt(qi, kt), 0.0)
        return jnp.where(kpos <= qpos, acc, NEG_INF)

    kt = jnp.concatenate([_page_tile(layout, pg, 0, IDX_DIM) for pg in pages], axis=0).astype(BF16)
    kpos = kg * KSTEP + lax.broadcasted_iota(jnp.int32, (1, KSTEP), 1)
    sc_ref[kg] = scores(kt, kpos)

    @pl.when(kg == kg_n - 1)
    def _():
        nch = sc_ref.shape[0]
        if has_tail:
            zpad = jnp.zeros((KSTEP - SUBLANES, IDX_DIM), F32)
            tk = jnp.concatenate([tail_ref[:, 0:IDX_DIM], zpad], axis=0).astype(BF16)
            tl = lax.broadcasted_iota(jnp.int32, (1, KSTEP), 1)
            sc_ref[kg_n] = jnp.where(tl < LANES, scores(tk, q0 + tl), NEG_INF)
        sc = sc_ref[...] + 0.0
        bits = pltpu.bitcast(sc, jnp.int32)
        key = jnp.where(bits < 0, bits ^ jnp.int32(0x7FFFFFFF), bits)
        int_min = jnp.int32(-2 ** 31)

        def count(pred):
            return jnp.sum(jnp.sum(jnp.where(pred, 1.0, 0.0), axis=-1, keepdims=True), axis=0, keepdims=True)

        def vbit(i, thr):
            cand = thr + jnp.left_shift(jnp.int32(1), 31 - i)
            return jnp.where(count(key >= cand) >= n_keep, cand, thr)

        thr = lax.fori_loop(0, 32, vbit, jnp.full((1, tq, 1), int_min, jnp.int32))
        gt = key > thr
        tie = key == thr
        need = n_keep - count(gt)
        idx = (lax.broadcasted_iota(jnp.int32, (nch, 1, KSTEP), 0) * KSTEP
               + lax.broadcasted_iota(jnp.int32, (nch, 1, KSTEP), 2))
        nbits = int(np.ceil(np.log2(nch * KSTEP)))

        def ibit(i, c):
            cand = c + jnp.left_shift(jnp.int32(1), nbits - 1 - i)
            return jnp.where(count(tie & (idx < cand)) < need, cand, c)

        cut = lax.fori_loop(0, nbits, ibit, jnp.zeros((1, tq, 1), jnp.int32))
        sel = jnp.where(gt | (tie & (idx <= cut)), 1.0, 0.0).astype(BF16)
        for c in range(kg_n):
            mask_ref[0, 0, :, c * KSTEP:(c + 1) * KSTEP] = sel[c]
        if has_tail:
            mask_ref[0, 0, :, kg_n * KSTEP:lpad] = sel[kg_n][:, 0:lpad - kg_n * KSTEP]


def _dsa1(hp, pool, layout, pt, tail_src, *, nb, tq, qt_n, q0, lpad, n_keep):
    kg_n = pt.shape[1] // PAGES_PER_STEP
    has_tail = tail_src is not None
    in_specs = [pl.BlockSpec((tq, 256), lambda b, t, k, pt: (b * qt_n + t, C_IQ // 256)),
                pl.BlockSpec((tq, LANES), lambda b, t, k, pt: (b * qt_n + t, C_MISC // LANES))]
    in_specs += _page_specs(layout, 3)
    args = [hp, hp] + [pool] * PAGES_PER_STEP
    if has_tail:
        in_specs += [pl.BlockSpec((SUBLANES, LANES), lambda b, t, k, pt: (b, C_MISC // LANES))]
        args += [tail_src]
    return pl.pallas_call(
        functools.partial(_dsa1_body, layout=layout, tq=tq, q0=q0, kg_n=kg_n, lpad=lpad, has_tail=has_tail,
                          n_keep=n_keep),
        grid_spec=pltpu.PrefetchScalarGridSpec(
            num_scalar_prefetch=1, grid=(nb, qt_n, kg_n),
            in_specs=in_specs,
            out_specs=pl.BlockSpec((1, 1, tq, lpad), lambda b, t, k, pt: (b, 0, t, 0)),
            scratch_shapes=[pltpu.VMEM((kg_n + (1 if has_tail else 0), tq, KSTEP), F32)]),
        out_shape=jax.ShapeDtypeStruct((nb, 1, qt_n * tq, lpad), BF16),
        compiler_params=_cparams("arbitrary", "arbitrary", "arbitrary"),
        name="dsa_indexer_topk",
    )(pt, *args)


def _mix_body(ocmp_ref, osel_ref, owin_ref, odsa_ref, conv_ref, misc_ref, mg0_ref, mg1_ref, mg2_ref,
              wa_ref, wb_ref, wc_ref, z_ref):
    lane128 = lax.broadcasted_iota(jnp.int32, (1, LANES), 1)
    misc = misc_ref[...]

    def gate(kind, h):
        return jnp.sum(jnp.where(lane128 == MISC_NG + kind * NSA_HEADS + h, misc, 0.0), axis=-1, keepdims=True)

    parts = []
    for h in range(NSA_HEADS):
        sl = slice(h * HEAD_DIM, (h + 1) * HEAD_DIM)
        parts.append((gate(0, h) * ocmp_ref[:, sl] + gate(1, h) * osel_ref[:, sl]
                      + gate(2, h) * owin_ref[:, sl]).astype(BF16))
    o_nsa = jnp.concatenate(parts, axis=1)
    p_a = jnp.dot(o_nsa, wa_ref[...], preferred_element_type=F32)
    p_b = jnp.dot(odsa_ref[...].astype(BF16), wb_ref[...], preferred_element_type=F32)
    p_c = jnp.dot(conv_ref[...].astype(BF16), wc_ref[...], preferred_element_type=F32)
    z_ref[...] = (mg0_ref[...] * p_a + mg1_ref[...] * p_b + mg2_ref[...] * p_c).astype(BF16)


def _mix(hp, o_cmp, o_sel, o_win, o_dsa, conv_out, wa, wb, wc, tm):
    n = hp.shape[0]
    row = lambda w, cb=0: pl.BlockSpec((tm, w), lambda i: (i, cb))
    full = lambda a: pl.BlockSpec(a.shape, lambda i: (0, 0))
    return pl.pallas_call(
        _mix_body,
        grid=(n // tm,),
        in_specs=[row(1024), row(1024), row(1024), row(512), row(512), row(LANES, C_MISC // LANES),
                  row(D_MODEL, C_MG // D_MODEL), row(D_MODEL, C_MG // D_MODEL + 1), row(D_MODEL, C_MG // D_MODEL + 2),
                  full(wa), full(wb), full(wc)],
        out_specs=row(D_MODEL),
        out_shape=jax.ShapeDtypeStruct((n, D_MODEL), BF16),
        compiler_params=_cparams("arbitrary"),
        name="branch_merge",
    )(o_cmp, o_sel, o_win, o_dsa, conv_out, hp, hp, hp, hp, wa, wb, wc)


def _outln_body(z_ref, wo_ref, x_ref, g_ref, b_ref, wr_ref, rb_ref, x1_ref, gate_ref, *, tm):
    y = jnp.dot(z_ref[...], wo_ref[...], preferred_element_type=F32)
    x1 = _layernorm(ALPHA * x_ref[...] + y, g_ref[...], b_ref[...])
    x1_ref[...] = x1
    x_hi = x1.astype(BF16)
    x_lo = (x1 - x_hi.astype(F32)).astype(BF16)
    w_hi = wr_ref[...].astype(BF16)
    w_lo = (wr_ref[...] - w_hi.astype(F32)).astype(BF16)
    aff = _sigmoid(_dot_nt(w_hi, x_hi) + (_dot_nt(w_hi, x_lo) + _dot_nt(w_lo, x_hi)))
    biased = aff + rb_ref[:, 0:1]
    rows = [biased[e:e + 1, :] for e in range(N_EXPERTS)]
    best = None
    g_best = jnp.zeros((1, tm), jnp.int32)
    for g in range(N_GROUPS):
        v = rows[g * EXPERTS_PER_GROUP:(g + 1) * EXPERTS_PER_GROUP]
        score = None
        for a in range(EXPERTS_PER_GROUP):
            for c in range(a + 1, EXPERTS_PER_GROUP):
                pair = v[a] + v[c]
                score = pair if score is None else jnp.maximum(score, pair)
        if best is None:
            best = score
        else:
            better = score > best
            best = jnp.where(better, score, best)
            g_best = jnp.where(better, g, g_best)
    sel_rows = []
    for e in range(N_EXPERTS):
        g = e // EXPERTS_PER_GROUP
        rank = jnp.zeros((1, tm), F32)
        for o in range(g * EXPERTS_PER_GROUP, (g + 1) * EXPERTS_PER_GROUP):
            if o == e:
                continue
            beats = (rows[o] > rows[e]) | ((rows[o] == rows[e]) & (o < e))
            rank = rank + jnp.where(beats, 1.0, 0.0)
        sel_rows.append(jnp.where((g_best == g) & (rank < 2), aff[e:e + 1, :], 0.0))
    tot = sel_rows[0]
    for e in range(1, N_EXPERTS):
        tot = tot + sel_rows[e]
    gate_t = jnp.concatenate(sel_rows + [jnp.zeros((LANES - N_EXPERTS, tm), F32)], axis=0) / tot
    gate_ref[...] = gate_t.T


def _outln(z, wo, x, g, b, wr_t, rb, tm):
    n = z.shape[0]
    row = lambda w: pl.BlockSpec((tm, w), lambda i: (i, 0))
    full = lambda a: pl.BlockSpec(a.shape, lambda i: (0, 0))
    return pl.pallas_call(
        functools.partial(_outln_body, tm=tm),
        grid=(n // tm,),
        in_specs=[row(D_MODEL), full(wo), row(D_MODEL), full(g), full(b), full(wr_t), full(rb)],
        out_specs=[row(D_MODEL), row(LANES)],
        out_shape=[jax.ShapeDtypeStruct((n, D_MODEL), F32), jax.ShapeDtypeStruct((n, LANES), F32)],
        compiler_params=_cparams("arbitrary"),
        name="out_proj_ln_router",
    )(z, wo, x, g, b, wr_t, rb)


def _moe_body(x_ref, gate_ref, wg_ref, wu_ref, wd_ref, g_ref, b_ref, o_ref, xb_ref, acc_ref):
    e = pl.program_id(1)

    @pl.when(e == 0)
    def _():
        xb_ref[...] = x_ref[...].astype(BF16)
        acc_ref[...] = jnp.zeros(acc_ref.shape, F32)

    lane128 = lax.broadcasted_iota(jnp.int32, (1, LANES), 1)
    gcol = jnp.sum(jnp.where(lane128 == e, gate_ref[...], 0.0), axis=-1, keepdims=True)
    xb = xb_ref[...]
    hg = jnp.dot(xb, wg_ref[0], preferred_element_type=F32)
    hu = jnp.dot(xb, wu_ref[0], preferred_element_type=F32)
    h = (hg * _sigmoid(hg)) * hu * gcol
    acc_ref[...] += jnp.dot(h.astype(BF16), wd_ref[0], preferred_element_type=F32)

    @pl.when(e == N_EXPERTS - 1)
    def _():
        o_ref[...] = _layernorm(ALPHA * x_ref[...] + acc_ref[...], g_ref[...], b_ref[...])


def _moe(x1, gate, wg, wu, wd, g, b, tm):
    n = x1.shape[0]
    return pl.pallas_call(
        _moe_body,
        grid=(n // tm, N_EXPERTS),
        in_specs=[pl.BlockSpec((tm, D_MODEL), lambda i, e: (i, 0)),
                  pl.BlockSpec((tm, LANES), lambda i, e: (i, 0)),
                  pl.BlockSpec((1, D_MODEL, D_FF), lambda i, e: (e, 0, 0)),
                  pl.BlockSpec((1, D_MODEL, D_FF), lambda i, e: (e, 0, 0)),
                  pl.BlockSpec((1, D_FF, D_MODEL), lambda i, e: (e, 0, 0)),
                  pl.BlockSpec((1, D_MODEL), lambda i, e: (0, 0)),
                  pl.BlockSpec((1, D_MODEL), lambda i, e: (0, 0))],
        out_specs=pl.BlockSpec((tm, D_MODEL), lambda i, e: (i, 0)),
        out_shape=jax.ShapeDtypeStruct((n, D_MODEL), F32),
        scratch_shapes=[pltpu.VMEM((tm, D_MODEL), BF16), pltpu.VMEM((tm, D_MODEL), F32)],
        compiler_params=_cparams("arbitrary", "arbitrary"),
        name="moe_ln",
    )(x1, gate, wg, wu, wd, g, b)


def _overlap_matrix(ncp, nselp):
    cs = np.arange(ncp)[:, None] * CMP_STRIDE
    ss = np.arange(nselp)[None, :] * SEL_BLOCK
    return jnp.asarray(((cs < ss + SEL_BLOCK) & (cs + CMP_LEN > ss)).astype(np.float32))


class _Group:
    def __init__(self, nb, t_real, t_pad, q0, past_len, tm, tq):
        self.nb, self.t_real, self.t_pad, self.q0, self.past_len, self.tm, self.tq = nb, t_real, t_pad, q0, past_len, tm, tq
        self.paged = past_len > 0
        self.lp = past_len if self.paged else t_pad
        self.ltot = self.lp + (t_real if self.paged else 0)
        self.lpad = self.lp + (LANES if self.paged else 0)
        self.qt_n = t_pad // tq
        n_chunks = -(-self.ltot // CMP_STRIDE)
        self.n_cmp = n_chunks - CMP_LEN // CMP_STRIDE + 1
        self.n_sel = -(-self.ltot // SEL_BLOCK)
        self.nselp = -(-self.n_sel // LANES) * LANES
        self.n_keep = min(DSA_TOPK, self.ltot // 4)


def _mixer(gp, x, lw, caches):
    hp = _proj(x.astype(BF16), lw["w_in"], gp.tabs, lw["kinds"], min(2 * gp.tm, gp.nb * gp.t_pad, gp.tabs.shape[1]))
    nb, tq, qt_n, q0 = gp.nb, gp.tq, gp.qt_n, gp.q0
    if gp.paged:
        l, nsa_pool, dsa_pool, kidx_pool, win_src, s_conv, pt = caches
        cmp_lay = _PageLayout("cache", 0, 0, base=0, layer=l, nslot=2)
        slc_lay = _PageLayout("cache", 0, 1, base=0, layer=l, nslot=2)
        dsa_lay = _PageLayout("cache", 0, 0, base=0, layer=l, nslot=2)
        kidx_lay = _PageLayout("cache4", IDX_DIM, 0, base=0, layer=l, nslot=0)
        tail = hp
        nwb, win_k0, win_layer = WINDOW // PAGE, PAST_LEN - WINDOW, l
        past8 = jnp.concatenate([jnp.zeros((nb, SUBLANES - 2, CONV_DIM), F32), s_conv[l]], axis=1)
    else:
        pt = gp.pt
        nsa_pool = dsa_pool = kidx_pool = hp.reshape(nb * gp.t_pad // PAGE, PAGE, NCOL)
        cmp_lay = _PageLayout("cols", 512, C_CMP // 512, base=0, layer=0, nslot=0)
        slc_lay = _PageLayout("cols", 512, C_SLC // 512, base=0, layer=0, nslot=0)
        dsa_lay = _PageLayout("cols", 512, C_DSA // 512, base=0, layer=0, nslot=0)
        kidx_lay = _PageLayout("cols", LANES, C_MISC // LANES, base=0, layer=0, nslot=0)
        tail = None
        win_src = hp
        nwb, win_k0, win_layer = WINDOW // PAGE + 1, 0, 0
        past8 = jnp.zeros((nb, SUBLANES, CONV_DIM), F32)

    conv_out, cu = _conv(hp, past8, lw["conv_w8"], min(gp.tm, gp.t_pad), gp.t_pad // min(gp.tm, gp.t_pad))

    ab = _cmp1(nsa_pool, cmp_lay, pt, lw["wcat"], lw["pe"])
    cmp = _cmp2(ab, tail, lw["pe"], lw["w1r"], lw["phi_w2"], gp.t_real if gp.paged else 0, C_CMP // 512)
    o_cmp, o_win, kmask = _nsa1(hp, cmp, gp.ov, win_src, tail, nb=nb, tq=tq, qt_n=qt_n, q0=q0, n_cmp=gp.n_cmp,
                                n_sel=gp.n_sel, lpad=gp.lpad, nwb=nwb, win_k0=win_k0, win_layer=win_layer)
    o_sel = _mattn(hp, C_QROT, NSA_HEADS, kmask, nsa_pool, slc_lay, pt, tail, C_SLC // 512,
                   nb=nb, tq=tq, qt_n=qt_n, q0=q0, name="nsa_selected_attn")
    dmask = _dsa1(hp, kidx_pool, kidx_lay, pt, tail, nb=nb, tq=tq, qt_n=qt_n, q0=q0, lpad=gp.lpad,
                  n_keep=gp.n_keep)
    o_dsa = _mattn(hp, C_DQ, DSA_HEADS, dmask, dsa_pool, dsa_lay, pt, tail, C_DSA // 512,
                   nb=nb, tq=tq, qt_n=qt_n, q0=q0, name="dsa_topk_attn")
    z = _mix(hp, o_cmp, o_sel, o_win, o_dsa, conv_out, lw["w_a"], lw["w_b"], lw["w_c"], min(gp.tm, 256))
    return z, hp, cu


def _layer(gp, x, lw, caches):
    z, hp, cu = _mixer(gp, x, lw, caches)
    tm2 = min(gp.tm, 256)
    x1, gate = _outln(z, lw["w_o"], x, lw["ln_mix_g"], lw["ln_mix_b"], lw["wr_t"], lw["rb"], tm2)
    x2 = _moe(x1, gate, lw["w_eg"], lw["w_eu"], lw["w_ed"], lw["ln_ffn_g"], lw["ln_ffn_b"], min(gp.tm, 512))
    return x2, hp, cu


def _layer_weights(l, w_in, nsa_phi_pos, nsa_phi_w1, nsa_phi_w2, conv_w, w_br_a, w_br_b, w_br_c, w_out,
                   ln_mix_g, ln_mix_b, ln_ffn_g, ln_ffn_b, w_router, router_bias, w_e_gate, w_e_up, w_e_down):
    w1r = nsa_phi_w1[l].reshape(2, CMP_LEN, HEAD_DIM, HEAD_DIM)
    wcat = jnp.concatenate([w1r[:, :CMP_STRIDE], w1r[:, CMP_STRIDE:]], axis=-1).astype(BF16)
    return dict(
        w_in=_permute_w_in(w_in, l), kinds=jnp.asarray(_col_kinds()),
        w1r=w1r, wcat=wcat, pe=nsa_phi_pos[l], phi_w2=nsa_phi_w2[l],
        conv_w8=jnp.concatenate([conv_w[l], jnp.zeros((SUBLANES - 3, CONV_DIM), F32)], axis=0),
        w_a=w_br_a[l].astype(BF16), w_b=w_br_b[l].astype(BF16), w_c=w_br_c[l].astype(BF16),
        w_o=w_out[l].astype(BF16),
        ln_mix_g=ln_mix_g[l][None], ln_mix_b=ln_mix_b[l][None],
        ln_ffn_g=ln_ffn_g[l][None], ln_ffn_b=ln_ffn_b[l][None],
        wr_t=w_router.T, rb=jnp.broadcast_to(router_bias[:, None], (N_EXPERTS, LANES)),
        w_eg=w_e_gate[l].astype(BF16), w_eu=w_e_up[l].astype(BF16), w_ed=w_e_down[l].astype(BF16))


def kernel(x_prompt, x_sample, cache_nsa_kv, cache_dsa_kv, cache_dsa_kidx, state_nsa_win, state_conv, page_table,
           w_in, nsa_phi_pos, nsa_phi_w1, nsa_phi_w2, conv_w, w_br_a, w_br_b, w_br_c, w_out, ln_mix_g, ln_mix_b,
           ln_ffn_g, ln_ffn_b, w_router, router_bias, w_e_gate, w_e_up, w_e_down):
    bp, tp, _ = x_prompt.shape
    bs, ts, _ = x_sample.shape
    ts_pad = SUBLANES

    gp_p = _Group(bp, tp, tp, 0, 0, tm=512, tq=128)
    gp_p.tabs = _rope_tables(jnp.arange(tp, dtype=jnp.int32))
    gp_p.pt = jnp.arange(bp * tp // PAGE, dtype=jnp.int32).reshape(bp, tp // PAGE)
    gp_p.ov = _overlap_matrix(LANES, gp_p.nselp)
    gp_s = _Group(bs, ts, ts_pad, PAST_LEN, PAST_LEN, tm=bs * ts_pad, tq=ts_pad)
    pos_s = PAST_LEN + jnp.arange(ts_pad, dtype=jnp.int32)
    gp_s.tabs = jnp.tile(_rope_tables(pos_s), (1, bs, 1))
    gp_s.ov = _overlap_matrix(PAST_LEN // CMP_STRIDE, gp_s.nselp)

    xp = x_prompt.reshape(bp * tp, D_MODEL)
    xs = jnp.concatenate([x_sample, jnp.zeros((bs, ts_pad - ts, D_MODEL), F32)], axis=1).reshape(bs * ts_pad, D_MODEL)

    outs_p = [[] for _ in range(5)]
    outs_s = [[] for _ in range(5)]
    for l in range(DEPTH):
        lw = _layer_weights(l, w_in, nsa_phi_pos, nsa_phi_w1, nsa_phi_w2, conv_w, w_br_a, w_br_b, w_br_c, w_out,
                            ln_mix_g, ln_mix_b, ln_ffn_g, ln_ffn_b, w_router, router_bias, w_e_gate, w_e_up, w_e_down)
        xp, hp_p, cu_p = _layer(gp_p, xp, lw, None)
        xs, hp_s, cu_s = _layer(gp_s, xs, lw, (l, cache_nsa_kv, cache_dsa_kv, cache_dsa_kidx,
                                                 state_nsa_win, state_conv, page_table))
        h3 = hp_p.reshape(bp, tp, NCOL)
        outs_p[0].append(h3[:, :, C_CMP:C_CMP + 1024].reshape(bp, tp, 4, NSA_KV, HEAD_DIM))
        outs_p[1].append(h3[:, :, C_DSA:C_DSA + 512].reshape(bp, tp, 2, DSA_KV, HEAD_DIM))
        outs_p[2].append(h3[:, :, C_MISC:C_MISC + IDX_DIM])
        outs_p[3].append(h3[:, tp - min(WINDOW, tp):, C_WIN:C_WIN + 512].reshape(bp, min(WINDOW, tp), 2, NSA_KV, HEAD_DIM))
        outs_p[4].append(cu_p.reshape(bp, tp, CONV_DIM)[:, tp - 2:])
        s3 = hp_s.reshape(bs, ts_pad, NCOL)[:, :ts]
        outs_s[0].append(s3[:, :, C_CMP:C_CMP + 1024].reshape(bs, ts, 4, NSA_KV, HEAD_DIM))
        outs_s[1].append(s3[:, :, C_DSA:C_DSA + 512].reshape(bs, ts, 2, DSA_KV, HEAD_DIM))
        outs_s[2].append(s3[:, :, C_MISC:C_MISC + IDX_DIM])
        win_new = s3[:, :, C_WIN:C_WIN + 512].reshape(bs, ts, 2, NSA_KV, HEAD_DIM)
        wb = state_nsa_win.shape[2]
        outs_s[3].append(jnp.concatenate([state_nsa_win[l], win_new], axis=1)[:, -wb:])
        ext = jnp.concatenate([state_conv[l], cu_s.reshape(bs, ts_pad, CONV_DIM)[:, :ts]], axis=1)
        outs_s[4].append(ext[:, -2:])
    sp = [jnp.stack(a, axis=0) for a in outs_p]
    ss = [jnp.stack(a, axis=0) for a in outs_s]
    y_p = xp.reshape(bp, tp, D_MODEL)
    y_s = xs.reshape(bs, ts_pad, D_MODEL)[:, :ts]
    return (y_p, y_s, sp[0], ss[0], sp[1], ss[1], sp[2], ss[2], sp[3], ss[3], sp[4], ss[4])
```

```python
import collections
import functools

import numpy as np
import jax
import jax.numpy as jnp
from jax import lax
from jax.experimental import pallas as pl
from jax.experimental.pallas import tpu as pltpu

F32 = jnp.float32
BF16 = jnp.bfloat16
HIGHEST = lax.Precision.HIGHEST
NEG_INF = float("-inf")

D_MODEL = 2048
DEPTH = 2
PAST_LEN = 16384
PAGE = 128
HEAD_DIM = 128
ROPE_THETA = 500000.0
NSA_HEADS = 8
NSA_KV = 2
CMP_LEN = 32
CMP_STRIDE = 16
SEL_BLOCK = 64
SEL_TOP = 16
WINDOW = 512
FORCE_SCORE = 1e4
DSA_HEADS = 4
DSA_KV = 2
IDX_HEADS = 4
IDX_DIM = 64
DSA_TOPK = 256
CONV_DIM = 512
N_EXPERTS = 16
N_GROUPS = 4
EXPERTS_PER_GROUP = 4
D_FF = 512
LN_EPS = 1e-5
ALPHA = (2 * DEPTH) ** 0.25
IN_WIDTHS = (1024, 1536, 24, 512, 512, 256, 64, 4, 1536, 6144)
ATT_SCALE = HEAD_DIM ** -0.5

LANES = 128
SUBLANES = 8
VMEM_LIMIT = 56 * 1024 * 1024

C_QRAW = 0
C_QROT = 1024
C_CMP = 2048
C_SLC = 2560
C_WIN = 3072
C_DQ = 3584
C_DSA = 4096
C_CV = 4608
C_MG = 6144
C_IQ = 12288
C_MISC = 12544
NCOL = 12800
MISC_IW = 64
MISC_NG = 68
PROJ_TN = 1280
PAGES_PER_STEP = 16
KSTEP = PAGES_PER_STEP * PAGE

K_PLAIN, K_ROPE128, K_ROPE64, K_SIGMOID, K_MISC = 0, 1, 2, 3, 4


def _col_kinds():
    kinds = np.zeros(NCOL // LANES, np.int32)

    def mark(c0, n, k):
        kinds[c0 // LANES:(c0 + n) // LANES] = k

    mark(C_QROT, 1024, K_ROPE128)
    mark(C_SLC, 256, K_ROPE128)
    mark(C_WIN, 256, K_ROPE128)
    mark(C_DQ, 512, K_ROPE128)
    mark(C_DSA, 256, K_ROPE128)
    mark(C_IQ, 256, K_ROPE64)
    mark(C_MISC, 128, K_MISC)
    mark(C_MG, 6144, K_SIGMOID)
    return kinds


def _cparams(*sem):
    return pltpu.CompilerParams(dimension_semantics=sem, vmem_limit_bytes=VMEM_LIMIT)


def _sigmoid(x):
    return 1.0 / (1.0 + jnp.exp(-x))


def _layernorm(x, g, b):
    mu = jnp.mean(x, axis=-1, keepdims=True)
    xc = x - mu
    var = jnp.mean(xc * xc, axis=-1, keepdims=True)
    return xc * lax.rsqrt(var + LN_EPS) * g + b


def _dot_nt(a, b, precision=None):
    return lax.dot_general(a, b, (((1,), (1,)), ((), ())), preferred_element_type=F32, precision=precision)


def _softmax_parts(s):
    m = jnp.max(s, axis=-1, keepdims=True)
    m = jnp.where(m == NEG_INF, 0.0, m)
    p = jnp.exp(s - m)
    return p, jnp.maximum(jnp.sum(p, axis=-1, keepdims=True), 1e-30)


def _stack_heads(q_ref, g, r_per_g):
    return jnp.concatenate([q_ref[:, (g * r_per_g + r) * HEAD_DIM:(g * r_per_g + r + 1) * HEAD_DIM]
                            for r in range(r_per_g)], axis=0).astype(BF16)


def _proj_body(kinds_ref, needs_ref, x_ref, wt_ref, tab_ref, o_ref, w_ref):
    nsub = PROJ_TN // LANES
    j = pl.program_id(0)

    @pl.when(pl.program_id(1) == 0)
    def _():
        for s in range(nsub):
            w_ref[:, s * LANES:(s + 1) * LANES] = wt_ref[s * LANES:(s + 1) * LANES, :].astype(F32).T.astype(BF16)

    h = jnp.dot(x_ref[...], w_ref[...], preferred_element_type=F32)
    lane = lax.broadcasted_iota(jnp.int32, (1, LANES), 1)

    def rope(hs, t0, sh):
        return (hs * tab_ref[t0] + pltpu.roll(hs, sh, 1) * tab_ref[t0 + 1]
                + pltpu.roll(hs, LANES - sh, 1) * tab_ref[t0 + 2])

    for s in range(nsub):
        kind = kinds_ref[j * nsub + s]
        hs = h[:, s * LANES:(s + 1) * LANES]
        sl = slice(s * LANES, (s + 1) * LANES)

        @pl.when(kind == K_PLAIN)
        def _():
            o_ref[:, sl] = hs

        @pl.when(kind == K_ROPE128)
        def _():
            o_ref[:, sl] = rope(hs, 0, 16)

        @pl.when(kind == K_ROPE64)
        def _():
            o_ref[:, sl] = rope(hs, 3, 8)

        @pl.when(kind == K_SIGMOID)
        def _():
            o_ref[:, sl] = _sigmoid(hs)

        @pl.when(kind == K_MISC)
        def _():
            r = rope(hs, 3, 8)
            o_ref[:, sl] = jnp.where(lane < MISC_IW, r,
                                     jnp.where(lane < MISC_NG, hs * (IDX_HEADS ** -0.5),
                                               jnp.where(lane < MISC_NG + 24, _sigmoid(hs), hs)))


def _proj(x_bf, wt_bf, tabs, kinds, tm):
    n = x_bf.shape[0]
    n_tab = tabs.shape[1] // tm
    grid = (NCOL // PROJ_TN, n // tm)
    rotary = np.isin(_col_kinds().reshape(-1, PROJ_TN // LANES), (K_ROPE128, K_ROPE64, K_MISC)).any(axis=1)
    needs = jnp.asarray(rotary.astype(np.int32))
    return pl.pallas_call(
        _proj_body,
        grid_spec=pltpu.PrefetchScalarGridSpec(
            num_scalar_prefetch=2, grid=grid,
            in_specs=[pl.BlockSpec((tm, D_MODEL), lambda j, i, k, nd: (i, 0)),
                      pl.BlockSpec((PROJ_TN, D_MODEL), lambda j, i, k, nd: (j, 0)),
                      pl.BlockSpec((6, tm, LANES), lambda j, i, k, nd: (0, (i % n_tab) * nd[j], 0))],
            out_specs=pl.BlockSpec((tm, PROJ_TN), lambda j, i, k, nd: (i, j)),
            scratch_shapes=[pltpu.VMEM((D_MODEL, PROJ_TN), BF16)]),
        out_shape=jax.ShapeDtypeStruct((n, NCOL), F32),
        compiler_params=_cparams("arbitrary", "arbitrary"),
        name="in_proj",
    )(kinds, needs, x_bf, wt_bf, tabs)


def _rope_tables(pos):
    out = []
    lane = jnp.arange(LANES)
    for d in (HEAD_DIM, IDX_DIM):
        rot = d // 4
        half = rot // 2
        inv = ROPE_THETA ** (-jnp.arange(half, dtype=F32) / half)
        ang = pos.astype(F32)[:, None] * inv[None, :]
        cos = jnp.cos(ang)
        sin = jnp.sin(ang)
        li = lane % d
        ci = jnp.take(cos, li % half, axis=1)
        si = jnp.take(sin, li % half, axis=1)
        out.append(jnp.where(li[None] < rot, ci, 1.0))
        out.append(jnp.where((li[None] >= half) & (li[None] < rot), si, 0.0))
        out.append(jnp.where(li[None] < half, -si, 0.0))
    return jnp.stack(out, axis=0)


def _permute_w_in(w_in, l):
    wt = jnp.transpose(w_in, (2, 0, 1))[:, l, :].astype(BF16)
    offs = np.cumsum((0,) + IN_WIDTHS)
    nq, nkv, ng, dq, dkv, iq, ik, iw, cv, mg = [wt[offs[i]:offs[i + 1]] for i in range(10)]
    z = lambda n: jnp.zeros((n, wt.shape[1]), BF16)
    rows = [nq, nq, nkv, dq, dkv, cv, mg, iq, ik, iw, ng, z(LANES - 92), z(NCOL - C_MISC - LANES)]
    return jnp.concatenate(rows, axis=0)


def _conv_body(cv_ref, prev_ref, past_ref, w_ref, y_ref, cu_ref, s_ref, *, tiles_per_seq, tm):
    i = pl.program_id(0)
    b = cv_ref[:, 0:CONV_DIM]
    cu = cv_ref[:, CONV_DIM:2 * CONV_DIM] * cv_ref[:, 2 * CONV_DIM:3 * CONV_DIM]
    first = (i % tiles_per_seq) == 0
    prev = prev_ref[:, CONV_DIM:2 * CONV_DIM] * prev_ref[:, 2 * CONV_DIM:3 * CONV_DIM]
    s_ref[0:SUBLANES, :] = jnp.where(first, past_ref[0], prev)
    s_ref[SUBLANES:SUBLANES + tm, :] = cu
    y = (w_ref[0:1, :] * s_ref[pl.ds(SUBLANES - 2, tm), :] + w_ref[1:2, :] * s_ref[pl.ds(SUBLANES - 1, tm), :]
         + w_ref[2:3, :] * cu)
    y_ref[...] = b * y
    cu_ref[...] = cu


def _conv(hp, past8, conv_w8, tm, tiles_per_seq):
    n = hp.shape[0]
    cvb = C_CV // (3 * CONV_DIM)
    rb = tm // SUBLANES
    return pl.pallas_call(
        functools.partial(_conv_body, tiles_per_seq=tiles_per_seq, tm=tm),
        grid=(n // tm,),
        in_specs=[pl.BlockSpec((tm, 3 * CONV_DIM), lambda i: (i, cvb)),
                  pl.BlockSpec((SUBLANES, 3 * CONV_DIM), lambda i: (jnp.maximum(i * rb - 1, 0), cvb)),
                  pl.BlockSpec((1, SUBLANES, CONV_DIM), lambda i: (i // tiles_per_seq, 0, 0)),
                  pl.BlockSpec((SUBLANES, CONV_DIM), lambda i: (0, 0))],
        out_specs=[pl.BlockSpec((tm, CONV_DIM), lambda i: (i, 0)),
                   pl.BlockSpec((tm, CONV_DIM), lambda i: (i, 0))],
        out_shape=[jax.ShapeDtypeStruct((n, CONV_DIM), F32), jax.ShapeDtypeStruct((n, CONV_DIM), F32)],
        scratch_shapes=[pltpu.VMEM((tm + SUBLANES, CONV_DIM), F32)],
        compiler_params=_cparams("arbitrary"),
        name="short_conv",
    )(hp, hp, past8, conv_w8)


_PageLayout = collections.namedtuple("_PageLayout", "kind width col_block base layer nslot")


def _page_specs(layout, n_lead, kg_n=None):
    def spec(k):
        def imap(*a):
            ids, pt = a[:n_lead], a[-1]
            kg = ids[-1] % kg_n if kg_n else ids[-1]
            page = pt[ids[0], kg * PAGES_PER_STEP + k]
            if layout.kind == "cache":
                return (layout.layer, page, 0, layout.col_block, 0, 0)
            if layout.kind == "cache4":
                return (layout.layer, page, 0, 0)
            cb = layout.col_block(*ids) if callable(layout.col_block) else layout.col_block
            return (page, 0, cb)
        shape = {"cache": (1, 1, PAGE, layout.nslot, 2, HEAD_DIM), "cache4": (1, 1, PAGE, layout.width),
                 "cols": (1, PAGE, layout.width)}[layout.kind]
        return pl.BlockSpec(shape, imap)
    return [spec(k) for k in range(PAGES_PER_STEP)]


def _page_tile(layout, ref, j, width=LANES):
    if layout.kind == "cache":
        slot, g = divmod(layout.base + j, 2)
        return ref[0, 0, :, slot, g, :]
    if layout.kind == "cache4":
        return ref[0, 0, :, j * width:(j + 1) * width]
    return ref[0, :, j * width:(j + 1) * width]


def _chunk_rows(layout, ref, j, p):
    if layout.kind == "cache":
        slot, g = divmod(layout.base + j, 2)
        return ref[0, 0, pl.ds(p, SUBLANES, stride=CMP_STRIDE), slot, g, :]
    return ref[0, pl.ds(p, SUBLANES, stride=CMP_STRIDE), :]


def _cmp1_body(pt_ref, *refs, layout, n_inner):
    pages = refs[:PAGES_PER_STEP]
    w_ref, pe_ref = refs[PAGES_PER_STEP:PAGES_PER_STEP + 2]
    o_ref = refs[PAGES_PER_STEP + 2]
    for j in range(n_inner):
        slot = j // 2
        acc_a = jnp.zeros((PAGE, HEAD_DIM), F32)
        acc_b = jnp.zeros((PAGE, HEAD_DIM), F32)
        for p in range(CMP_STRIDE):
            xp = jnp.concatenate([_chunk_rows(layout, pg, j, p) for pg in pages], axis=0)
            xa = (xp + pe_ref[slot, p:p + 1, :]).astype(BF16)
            xb = (xp + pe_ref[slot, CMP_STRIDE + p:CMP_STRIDE + p + 1, :]).astype(BF16)
            acc_a = acc_a + jnp.dot(xa, w_ref[slot, p, :, 0:HEAD_DIM], preferred_element_type=F32)
            acc_b = acc_b + jnp.dot(xb, w_ref[slot, p, :, HEAD_DIM:2 * HEAD_DIM], preferred_element_type=F32)
        o_ref[0, :, j * 256:j * 256 + HEAD_DIM] = acc_a
        o_ref[0, :, j * 256 + HEAD_DIM:(j + 1) * 256] = acc_b


def _cmp1(pool, layout, pt, wcat, pe):
    nb, npg = pt.shape
    kg = npg // PAGES_PER_STEP
    if layout.kind == "cache":
        n_inner, n_sg, wsel, osel, ow = 4, 1, (lambda sg: 0), (lambda sg: 0), 1024
        wblk = 2
    else:
        cb = layout.col_block
        layout = layout._replace(width=HEAD_DIM, col_block=lambda b, sg, k: cb * 4 + sg)
        n_inner, n_sg, wsel, osel, ow = 1, 4, (lambda sg: sg // 2), (lambda sg: sg), 256
        wblk = 1
    return pl.pallas_call(
        functools.partial(_cmp1_body, layout=layout, n_inner=n_inner),
        grid_spec=pltpu.PrefetchScalarGridSpec(
            num_scalar_prefetch=1, grid=(nb, n_sg, kg),
            in_specs=_page_specs(layout, 3)
            + [pl.BlockSpec((wblk, 16, HEAD_DIM, 256), lambda b, sg, k, pt: (wsel(sg), 0, 0, 0)),
               pl.BlockSpec((wblk, CMP_LEN, HEAD_DIM), lambda b, sg, k, pt: (wsel(sg), 0, 0))],
            out_specs=pl.BlockSpec((1, PAGE, ow), lambda b, sg, k, pt: (b, k, osel(sg)))),
        out_shape=jax.ShapeDtypeStruct((nb, npg * SUBLANES, 1024), F32),
        compiler_params=_cparams("arbitrary", "arbitrary", "arbitrary"),
        name="nsa_compress1",
    )(pt, *([pool] * PAGES_PER_STEP), wcat, pe)


def _gelu_tanh(x):
    return 0.5 * x * (1.0 + jnp.tanh(np.sqrt(2.0 / np.pi).astype(np.float32) * (x + 0.044715 * (x * x * x))))


def _cmp2_body(ab_ref, tail_ref, pe_ref, w1_ref, w2_ref, o_ref, s_ref, *, nc, n_tail):
    row8 = lax.broadcasted_iota(jnp.int32, (SUBLANES, 1), 0)
    row16 = lax.broadcasted_iota(jnp.int32, (CMP_STRIDE, 1), 0)
    for sg in range(4):
        slot = sg // 2
        a = ab_ref[0, :, sg * 256:sg * 256 + HEAD_DIM]
        s_ref[0:nc, :] = ab_ref[0, :, sg * 256 + HEAD_DIM:(sg + 1) * 256]
        tb = jnp.zeros((SUBLANES, HEAD_DIM), F32)
        if n_tail:
            x8 = jnp.where(row8 < n_tail, tail_ref[:, sg * HEAD_DIM:(sg + 1) * HEAD_DIM], 0.0)
            x16 = jnp.concatenate([x8, jnp.zeros((CMP_STRIDE - SUBLANES, HEAD_DIM), F32)], axis=0)
            x16 = x16 + pe_ref[slot, CMP_STRIDE:CMP_LEN, :]
            t16 = jnp.zeros((CMP_STRIDE, HEAD_DIM), F32)
            for p in range(CMP_STRIDE):
                xm = jnp.where(row16 == p, x16, 0.0).astype(BF16)
                t16 = t16 + jnp.dot(xm, w1_ref[slot, CMP_STRIDE + p].astype(BF16), preferred_element_type=F32)
            tb = jnp.sum(t16, axis=0, keepdims=True) * jnp.where(row8 == 0, 1.0, 0.0)
        s_ref[nc:nc + SUBLANES, :] = tb
        pre = a + s_ref[pl.ds(1, nc), :]
        o_ref[0, :, sg * HEAD_DIM:(sg + 1) * HEAD_DIM] = jnp.dot(
            _gelu_tanh(pre).astype(BF16), w2_ref[slot].astype(BF16), preferred_element_type=F32)


def _cmp2(ab, tail, pe, w1r, w2, n_tail, tail_col_block):
    nb, nc, _ = ab.shape
    if tail is None:
        tail = jnp.zeros((nb * SUBLANES, 512), F32)
        tail_col_block = 0
    return pl.pallas_call(
        functools.partial(_cmp2_body, nc=nc, n_tail=n_tail),
        grid=(nb,),
        in_specs=[pl.BlockSpec((1, nc, 1024), lambda b: (b, 0, 0)),
                  pl.BlockSpec((SUBLANES, 512), lambda b: (b, tail_col_block)),
                  pl.BlockSpec((2, CMP_LEN, HEAD_DIM), lambda b: (0, 0, 0)),
                  pl.BlockSpec((2, CMP_LEN, HEAD_DIM, HEAD_DIM), lambda b: (0, 0, 0, 0)),
                  pl.BlockSpec((2, HEAD_DIM, HEAD_DIM), lambda b: (0, 0, 0))],
        out_specs=pl.BlockSpec((1, nc, 512), lambda b: (b, 0, 0)),
        out_shape=jax.ShapeDtypeStruct((nb, nc, 512), F32),
        scratch_shapes=[pltpu.VMEM((nc + SUBLANES, HEAD_DIM), F32)],
        compiler_params=_cparams("arbitrary"),
        name="nsa_compress2",
    )(ab, tail, pe, w1r, w2)


def _nsa1_body(*refs, tq, q0, n_cmp, ncp, n_sel, nselp, lpad, nwb, has_tail, win_k0):
    qraw_ref, qrot_ref, cmp_ref, ov_ref = refs[:4]
    wins = refs[4:4 + nwb]
    pos = 4 + nwb
    tail_ref = None
    if has_tail:
        tail_ref = refs[pos]
        pos += 1
    ocmp_ref, owin_ref, kmask_ref = refs[pos:pos + 3]
    qt = pl.program_id(1)
    r_per_g = NSA_HEADS // NSA_KV
    qpos = q0 + qt * tq + lax.broadcasted_iota(jnp.int32, (tq, 1), 0)

    jj = lax.broadcasted_iota(jnp.int32, (1, ncp), 1)
    cmask = (jj * CMP_STRIDE + (CMP_LEN - 1) <= qpos) & (jj < n_cmp)
    blk = lax.broadcasted_iota(jnp.int32, (1, nselp), 1)
    cur = qpos // SEL_BLOCK
    forced = (blk == 0) | (blk == cur) | (blk == cur - 1)
    e_row = lax.broadcasted_iota(jnp.int32, (KSTEP // SEL_BLOCK, KSTEP), 0)
    e_col = lax.broadcasted_iota(jnp.int32, (KSTEP // SEL_BLOCK, KSTEP), 1)
    expand = jnp.where(e_col // SEL_BLOCK == e_row, 1.0, 0.0).astype(BF16)
    lane128 = lax.broadcasted_iota(jnp.int32, (1, LANES), 1)
    for g in range(NSA_KV):
        kc = cmp_ref[0, :, g * HEAD_DIM:(g + 1) * HEAD_DIM].astype(BF16)
        vc = cmp_ref[0, :, (2 + g) * HEAD_DIM:(3 + g) * HEAD_DIM].astype(BF16)
        q = _stack_heads(qraw_ref, g, r_per_g)
        s = _dot_nt(q, kc) * ATT_SCALE + jnp.concatenate([jnp.where(cmask, 0.0, NEG_INF)] * r_per_g, axis=0)
        p, den = _softmax_parts(s)
        p = p / den
        o = jnp.dot(p.astype(BF16), vc, preferred_element_type=F32)
        psum = jnp.zeros((tq, ncp), F32)
        for r in range(r_per_g):
            h = g * r_per_g + r
            psum = psum + p[r * tq:(r + 1) * tq]
            ocmp_ref[:, h * HEAD_DIM:(h + 1) * HEAD_DIM] = o[r * tq:(r + 1) * tq]
        imp = jnp.dot(psum.astype(BF16), ov_ref[...].astype(BF16), preferred_element_type=F32)
        imp = jnp.where(forced, imp + FORCE_SCORE, imp)
        imp = jnp.where((blk <= cur) & (blk < n_sel), imp, NEG_INF)
        rank = jnp.zeros((tq, nselp), F32)
        for i in range(n_sel):
            vi = imp[:, i:i + 1]
            beats = (vi > imp) | ((vi == imp) & (blk > i))
            rank = rank + jnp.where(beats, 1.0, 0.0)
        sel = jnp.where((rank < min(SEL_TOP, n_sel)) & (blk < n_sel), 1.0, 0.0).astype(BF16)
        per = KSTEP // SEL_BLOCK
        for c in range(lpad // KSTEP):
            km = jnp.dot(sel[:, c * per:(c + 1) * per], expand, preferred_element_type=F32)
            kmask_ref[0, g, :, c * KSTEP:(c + 1) * KSTEP] = km.astype(BF16)
        if lpad % KSTEP:
            b0 = (lpad // KSTEP) * per
            km = jnp.where(lane128 < SEL_BLOCK, sel[:, b0:b0 + 1].astype(F32), 0.0)
            kmask_ref[0, g, :, (lpad // KSTEP) * KSTEP:lpad] = jnp.broadcast_to(km, (tq, LANES)).astype(BF16)

    nk = nwb * PAGE + (LANES if has_tail else 0)
    kk = lax.broadcasted_iota(jnp.int32, (1, nk), 1)
    if has_tail:
        kpos = jnp.where(kk < nwb * PAGE, win_k0 + kk, q0 + kk - nwb * PAGE)
    else:
        kpos = (qt - (nwb - 1)) * PAGE + kk
    rel = qpos - kpos
    wmask = (rel >= 0) & (rel < WINDOW) & (kpos >= 0)
    for g in range(NSA_KV):
        if has_tail:
            kparts = [w[0, 0, :, 0, g, :] for w in wins]
            vparts = [w[0, 0, :, 1, g, :] for w in wins]
        else:
            kparts = [w[:, g * HEAD_DIM:(g + 1) * HEAD_DIM] for w in wins]
            vparts = [w[:, (2 + g) * HEAD_DIM:(3 + g) * HEAD_DIM] for w in wins]
        if has_tail:
            zpad = jnp.zeros((LANES - SUBLANES, HEAD_DIM), F32)
            kparts += [tail_ref[:, g * HEAD_DIM:(g + 1) * HEAD_DIM], zpad]
            vparts += [tail_ref[:, (2 + g) * HEAD_DIM:(3 + g) * HEAD_DIM], zpad]
        kw = jnp.concatenate(kparts, axis=0).astype(BF16)
        vw = jnp.concatenate(vparts, axis=0).astype(BF16)
        q = _stack_heads(qrot_ref, g, r_per_g)
        s = _dot_nt(q, kw) * ATT_SCALE + jnp.concatenate([jnp.where(wmask, 0.0, NEG_INF)] * r_per_g, axis=0)
        p, den = _softmax_parts(s)
        o = jnp.dot((p / den).astype(BF16), vw, preferred_element_type=F32)
        for r in range(r_per_g):
            h = g * r_per_g + r
            owin_ref[:, h * HEAD_DIM:(h + 1) * HEAD_DIM] = o[r * tq:(r + 1) * tq]


def _nsa1(hp, cmp, ov, win_src, tail_src, *, nb, tq, qt_n, q0, n_cmp, n_sel, lpad, nwb, win_k0, win_layer):
    n = hp.shape[0]
    ncp = cmp.shape[1]
    nselp = ov.shape[1]
    has_tail = tail_src is not None
    rows = lambda b, t: b * qt_n + t
    in_specs = [pl.BlockSpec((tq, 1024), lambda b, t: (rows(b, t), C_QRAW // 1024)),
                pl.BlockSpec((tq, 1024), lambda b, t: (rows(b, t), C_QROT // 1024)),
                pl.BlockSpec((1, ncp, 512), lambda b, t: (b, 0, 0)),
                pl.BlockSpec((ncp, nselp), lambda b, t: (0, 0))]
    if has_tail:
        in_specs += [pl.BlockSpec((1, 1, PAGE, 2, NSA_KV, HEAD_DIM), lambda b, t, k=k: (win_layer, b, k, 0, 0, 0))
                     for k in range(nwb)]
        in_specs += [pl.BlockSpec((SUBLANES, 512), lambda b, t: (b, C_WIN // 512))]
        args = [win_src] * nwb + [tail_src]
    else:
        in_specs += [pl.BlockSpec((PAGE, 512),
                                  lambda b, t, k=k: (b * qt_n + jnp.maximum(t - (nwb - 1) + k, 0), C_WIN // 512))
                     for k in range(nwb)]
        args = [win_src] * nwb
    return pl.pallas_call(
        functools.partial(_nsa1_body, tq=tq, q0=q0, n_cmp=n_cmp, ncp=ncp, n_sel=n_sel, nselp=nselp, lpad=lpad,
                          nwb=nwb, has_tail=has_tail, win_k0=win_k0),
        grid=(nb, qt_n),
        in_specs=in_specs,
        out_specs=[pl.BlockSpec((tq, 1024), lambda b, t: (rows(b, t), 0)),
                   pl.BlockSpec((tq, 1024), lambda b, t: (rows(b, t), 0)),
                   pl.BlockSpec((1, NSA_KV, tq, lpad), lambda b, t: (b, 0, t, 0))],
        out_shape=[jax.ShapeDtypeStruct((n, 1024), F32), jax.ShapeDtypeStruct((n, 1024), F32),
                   jax.ShapeDtypeStruct((nb, NSA_KV, qt_n * tq, lpad), BF16)],
        compiler_params=_cparams("arbitrary", "arbitrary"),
        name="nsa_cmp_select_window",
    )(hp, hp, cmp, ov, *args)


def _mattn_body(pt_ref, *refs, layout, n_g, r_per_g, gm, tq, q0, kg_n, has_tail):
    q_ref, mask_ref = refs[:2]
    pages = refs[2:2 + PAGES_PER_STEP]
    pos = 2 + PAGES_PER_STEP
    if has_tail:
        tail_ref, tmask_ref = refs[pos:pos + 2]
        pos += 2
    o_ref, m_ref, l_ref, acc_ref = refs[pos:pos + 4]
    qt = pl.program_id(1)
    step = pl.program_id(2)
    two_pass = kg_n > 1 or has_tail
    kg = step % kg_n if two_pass else step
    qpos = q0 + qt * tq + lax.broadcasted_iota(jnp.int32, (tq, 1), 0)

    def scores(g, kt, kpos, mref):
        picked = jnp.where(mref[0, g if gm > 1 else 0].astype(F32) > 0.5, 0.0, NEG_INF)
        bias = jnp.concatenate([picked + jnp.where(kpos <= qpos, 0.0, NEG_INF)] * r_per_g, axis=0)
        return _dot_nt(_stack_heads(q_ref, g, r_per_g), kt) * ATT_SCALE + bias

    def emit(g, o):
        for r in range(r_per_g):
            h = g * r_per_g + r
            o_ref[:, h * HEAD_DIM:(h + 1) * HEAD_DIM] = o[r * tq:(r + 1) * tq]

    def main_keys(g):
        return jnp.concatenate([_page_tile(layout, pg, g) for pg in pages], axis=0).astype(BF16)

    def main_vals(g):
        return jnp.concatenate([_page_tile(layout, pg, n_g + g) for pg in pages], axis=0).astype(BF16)

    def tail_part(g, off):
        zpad = jnp.zeros((LANES - SUBLANES, HEAD_DIM), F32)
        return jnp.concatenate([tail_ref[:, (off + g) * HEAD_DIM:(off + g + 1) * HEAD_DIM], zpad], axis=0).astype(BF16)

    kpos = kg * KSTEP + lax.broadcasted_iota(jnp.int32, (1, KSTEP), 1)
    tpos = q0 + lax.broadcasted_iota(jnp.int32, (1, LANES), 1)

    if not two_pass:
        for g in range(n_g):
            p, den = _softmax_parts(scores(g, main_keys(g), kpos, mask_ref))
            emit(g, jnp.dot((p / den).astype(BF16), main_vals(g), preferred_element_type=F32))
        return

    @pl.when(step == 0)
    def _():
        m_ref[...] = jnp.full(m_ref.shape, NEG_INF, F32)
        l_ref[...] = jnp.zeros(l_ref.shape, F32)
        acc_ref[...] = jnp.zeros(acc_ref.shape, F32)

    def stats(g, s):
        m_old = m_ref[g]
        m_new = jnp.maximum(m_old, jnp.max(s, axis=-1, keepdims=True))
        m_safe = jnp.where(m_new == NEG_INF, 0.0, m_new)
        l_ref[g] = jnp.exp(m_old - m_safe) * l_ref[g] + jnp.sum(jnp.exp(s - m_safe), axis=-1, keepdims=True)
        m_ref[g] = m_new

    def accumulate(g, s, vt):
        m = m_ref[g]
        m_safe = jnp.where(m == NEG_INF, 0.0, m)
        p = jnp.exp(s - m_safe) / jnp.maximum(l_ref[g], 1e-30)
        acc_ref[g] += jnp.dot(p.astype(BF16), vt, preferred_element_type=F32)

    @pl.when(step < kg_n)
    def _():
        for g in range(n_g):
            stats(g, scores(g, main_keys(g), kpos, mask_ref))

    if has_tail:
        @pl.when(step == kg_n - 1)
        def _():
            for g in range(n_g):
                stats(g, scores(g, tail_part(g, 0), tpos, tmask_ref))

    @pl.when(step >= kg_n)
    def _():
        for g in range(n_g):
            accumulate(g, scores(g, main_keys(g), kpos, mask_ref), main_vals(g))

    @pl.when(step == 2 * kg_n - 1)
    def _():
        for g in range(n_g):
            if has_tail:
                accumulate(g, scores(g, tail_part(g, 0), tpos, tmask_ref), tail_part(g, n_g))
            emit(g, acc_ref[g])


def _mattn(hp, q_col, n_heads, mask, pool, layout, pt, tail_src, tail_col_block, *, nb, tq, qt_n, q0, name):
    n = hp.shape[0]
    n_g = 2
    r_per_g = n_heads // n_g
    qw = n_heads * HEAD_DIM
    gm = mask.shape[1]
    kg_n = pt.shape[1] // PAGES_PER_STEP
    has_tail = tail_src is not None
    n_steps = 2 * kg_n if (kg_n > 1 or has_tail) else 1
    in_specs = [pl.BlockSpec((tq, qw), lambda b, t, k, pt: (b * qt_n + t, q_col // qw)),
                pl.BlockSpec((1, gm, tq, KSTEP), lambda b, t, k, pt: (b, 0, t, k % kg_n))]
    in_specs += _page_specs(layout, 3, kg_n)
    args = [hp, mask] + [pool] * PAGES_PER_STEP
    if has_tail:
        in_specs += [pl.BlockSpec((SUBLANES, 512), lambda b, t, k, pt: (b, tail_col_block)),
                     pl.BlockSpec((1, gm, tq, LANES), lambda b, t, k, pt: (b, 0, t, kg_n * KSTEP // LANES))]
        args += [tail_src, mask]
    return pl.pallas_call(
        functools.partial(_mattn_body, layout=layout, n_g=n_g, r_per_g=r_per_g, gm=gm, tq=tq, q0=q0, kg_n=kg_n, has_tail=has_tail),
        grid_spec=pltpu.PrefetchScalarGridSpec(
            num_scalar_prefetch=1, grid=(nb, qt_n, n_steps),
            in_specs=in_specs,
            out_specs=pl.BlockSpec((tq, qw), lambda b, t, k, pt: (b * qt_n + t, 0)),
            scratch_shapes=[pltpu.VMEM((n_g, r_per_g * tq, 1), F32), pltpu.VMEM((n_g, r_per_g * tq, 1), F32),
                            pltpu.VMEM((n_g, r_per_g * tq, HEAD_DIM), F32)]),
        out_shape=jax.ShapeDtypeStruct((n, qw), F32),
        compiler_params=_cparams("arbitrary", "arbitrary", "arbitrary"),
        name=name,
    )(pt, *args)


PACKED_ROWS = 2 * SUBLANES


def _topk_mask_packed_t(sc_t, n_keep):
    n_keys, n_q = sc_t.shape
    i16 = jnp.int16
    bits = pltpu.bitcast(sc_t + 0.0, jnp.int32)
    key = jnp.where(bits < 0, bits ^ jnp.int32(0x7FFFFFFF), bits)
    hi = jnp.right_shift(key, 16).astype(i16)
    lo = ((key & jnp.int32(0xFFFF)) - 32768).astype(i16)
    one_b, zero_b = jnp.asarray(1, BF16), jnp.asarray(0, BF16)
    i16_min, i16_max = jnp.asarray(-32768, i16), jnp.asarray(32767, i16)
    assert n_keys % PACKED_ROWS == 0 and n_keys // PACKED_ROWS <= 256

    def colsum(x01):
        parts = [x01[i * PACKED_ROWS:(i + 1) * PACKED_ROWS, :] for i in range(n_keys // PACKED_ROWS)]
        while len(parts) > 1:
            parts = [parts[i] + parts[i + 1] for i in range(0, len(parts), 2)]
        return jnp.sum(parts[0].astype(F32), axis=0, keepdims=True)

    def search(vals, need, nbits, start):
        def body(i, thr):
            cand = thr + jnp.left_shift(jnp.int32(1), nbits - 1 - i)
            cnt = colsum(jnp.where(vals >= cand.astype(i16), one_b, zero_b))
            return jnp.where(cnt >= need, cand, thr)
        return lax.fori_loop(0, nbits, body, jnp.full((1, n_q), start, jnp.int32))

    t_hi = search(hi, float(n_keep), 16, -32768).astype(i16)
    eq = hi == t_hi
    hi_gt = jnp.where(hi > t_hi, one_b, zero_b)
    lo_m = jnp.where(eq, lo, i16_min)
    t_lo = search(lo_m, n_keep - colsum(hi_gt), 16, -32768).astype(i16)
    gt = hi_gt + jnp.where(lo_m > t_lo, one_b, zero_b)
    need = n_keep - colsum(gt)
    idx = lax.broadcasted_iota(jnp.int32, (n_keys, 1), 0).astype(i16)
    tie_idx = jnp.where(eq, jnp.where(lo == t_lo, idx, i16_max), i16_max)
    nbits = int(np.ceil(np.log2(n_keys)))

    def ibody(i, c):
        cand = c + jnp.left_shift(jnp.int32(1), nbits - 1 - i)
        cnt = colsum(jnp.where(tie_idx < cand.astype(i16), one_b, zero_b))
        return jnp.where(cnt < need, cand, c)

    cut = lax.fori_loop(0, nbits, ibody, jnp.zeros((1, n_q), jnp.int32)).astype(i16)
    return jnp.maximum(gt, jnp.where(tie_idx <= cut, one_b, zero_b))


def _dsa1_body(pt_ref, *refs, layout, tq, q0, kg_n, lpad, has_tail, n_keep):
    iq_ref, misc_ref = refs[:2]
    pages = refs[2:2 + PAGES_PER_STEP]
    pos = 2 + PAGES_PER_STEP
    if has_tail:
        tail_ref = refs[pos]
        pos += 1
    mask_ref, sc_ref = refs[pos:pos + 2]
    qt = pl.program_id(1)
    kg = pl.program_id(2)
    qpos = q0 + qt * tq + lax.broadcasted_iota(jnp.int32, (tq, 1), 0)
    lane128 = lax.broadcasted_iota(jnp.int32, (1, LANES), 1)

    if kg_n == 1 and not has_tail and tq == LANES:
        kt = jnp.concatenate([_page_tile(layout, pg, 0, IDX_DIM) for pg in pages], axis=0).astype(BF16)
        misc_t = misc_ref[...].T
        kpos_c = lax.broadcasted_iota(jnp.int32, (KSTEP, 1), 0)
        qpos_r = q0 + qt * tq + lax.broadcasted_iota(jnp.int32, (1, tq), 1)
        acc = jnp.zeros((KSTEP, tq), F32)
        for h in range(IDX_HEADS):
            qi = iq_ref[:, h * IDX_DIM:(h + 1) * IDX_DIM].astype(BF16)
            acc = acc + misc_t[MISC_IW + h:MISC_IW + h + 1, :] * jnp.maximum(_dot_nt(kt, qi), 0.0)
        sel_t = _topk_mask_packed_t(jnp.where(kpos_c <= qpos_r, acc, NEG_INF), n_keep)
        for c in range(KSTEP // LANES):
            blk = sel_t[c * LANES:(c + 1) * LANES, :].astype(F32).T
            mask_ref[0, 0, :, c * LANES:(c + 1) * LANES] = blk.astype(BF16)
        return

    def scores(kt, kpos):
        acc = jnp.zeros((tq, kt.shape[0]), F32)
        for h in range(IDX_HEADS):
            qi = iq_ref[:, h * IDX_DIM:(h + 1) * IDX_DIM].astype(BF16)
            w = jnp.sum(jnp.where(lane128 == MISC_IW + h, misc_ref[...], 0.0), axis=-1, keepdims=True)
            acc = acc + w * jnp.maximum(_dot_nt(qi, kt), 0.0)
        return jnp.where(kpos <= qpos, acc, NEG_INF)

    kt = jnp.concatenate([_page_tile(layout, pg, 0, IDX_DIM) for pg in pages], axis=0).astype(BF16)
    kpos = kg * KSTEP + lax.broadcasted_iota(jnp.int32, (1, KSTEP), 1)
    sc_ref[kg] = scores(kt, kpos)

    @pl.when(kg == kg_n - 1)
    def _():
        nch = sc_ref.shape[0]
        if has_tail:
            zpad = jnp.zeros((KSTEP - SUBLANES, IDX_DIM), F32)
            tk = jnp.concatenate([tail_ref[:, 0:IDX_DIM], zpad], axis=0).astype(BF16)
            tl = lax.broadcasted_iota(jnp.int32, (1, KSTEP), 1)
            sc_ref[kg_n] = jnp.where(tl < LANES, scores(tk, q0 + tl), NEG_INF)
        sc = sc_ref[...] + 0.0
        bits = pltpu.bitcast(sc, jnp.int32)
        key = jnp.where(bits < 0, bits ^ jnp.int32(0x7FFFFFFF), bits)
        int_min = jnp.int32(-2 ** 31)

        def count(pred):
            return jnp.sum(jnp.sum(jnp.where(pred, 1.0, 0.0), axis=-1, keepdims=True), axis=0, keepdims=True)

        def vbit(i, thr):
            cand = thr + jnp.left_shift(jnp.int32(1), 31 - i)
            return jnp.where(count(key >= cand) >= n_keep, cand, thr)

        thr = lax.fori_loop(0, 32, vbit, jnp.full((1, tq, 1), int_min, jnp.int32))
        gt = key > thr
        tie = key == thr
        need = n_keep - count(gt)
        idx = (lax.broadcasted_iota(jnp.int32, (nch, 1, KSTEP), 0) * KSTEP
               + lax.broadcasted_iota(jnp.int32, (nch, 1, KSTEP), 2))
        nbits = int(np.ceil(np.log2(nch * KSTEP)))

        def ibit(i, c):
            cand = c + jnp.left_shift(jnp.int32(1), nbits - 1 - i)
            return jnp.where(count(tie & (idx < cand)) < need, cand, c)

        cut = lax.fori_loop(0, nbits, ibit, jnp.zeros((1, tq, 1), jnp.int32))
        sel = jnp.where(gt | (tie & (idx <= cut)), 1.0, 0.0).astype(BF16)
        for c in range(kg_n):
            mask_ref[0, 0, :, c * KSTEP:(c + 1) * KSTEP] = sel[c]
        if has_tail:
            mask_ref[0, 0, :, kg_n * KSTEP:lpad] = sel[kg_n][:, 0:lpad - kg_n * KSTEP]


def _dsa1(hp, pool, layout, pt, tail_src, *, nb, tq, qt_n, q0, lpad, n_keep):
    kg_n = pt.shape[1] // PAGES_PER_STEP
    has_tail = tail_src is not None
    in_specs = [pl.BlockSpec((tq, 256), lambda b, t, k, pt: (b * qt_n + t, C_IQ // 256)),
                pl.BlockSpec((tq, LANES), lambda b, t, k, pt: (b * qt_n + t, C_MISC // LANES))]
    in_specs += _page_specs(layout, 3)
    args = [hp, hp] + [pool] * PAGES_PER_STEP
    if has_tail:
        in_specs += [pl.BlockSpec((SUBLANES, LANES), lambda b, t, k, pt: (b, C_MISC // LANES))]
        args += [tail_src]
    return pl.pallas_call(
        functools.partial(_dsa1_body, layout=layout, tq=tq, q0=q0, kg_n=kg_n, lpad=lpad, has_tail=has_tail,
                          n_keep=n_keep),
        grid_spec=pltpu.PrefetchScalarGridSpec(
            num_scalar_prefetch=1, grid=(nb, qt_n, kg_n),
            in_specs=in_specs,
            out_specs=pl.BlockSpec((1, 1, tq, lpad), lambda b, t, k, pt: (b, 0, t, 0)),
            scratch_shapes=[pltpu.VMEM((kg_n + (1 if has_tail else 0), tq, KSTEP), F32)]),
        out_shape=jax.ShapeDtypeStruct((nb, 1, qt_n * tq, lpad), BF16),
        compiler_params=_cparams("arbitrary", "arbitrary", "arbitrary"),
        name="dsa_indexer_topk",
    )(pt, *args)


def _mix_body(ocmp_ref, osel_ref, owin_ref, odsa_ref, conv_ref, misc_ref, mg0_ref, mg1_ref, mg2_ref,
              wa_ref, wb_ref, wc_ref, z_ref):
    lane128 = lax.broadcasted_iota(jnp.int32, (1, LANES), 1)
    misc = misc_ref[...]

    def gate(kind, h):
        return jnp.sum(jnp.where(lane128 == MISC_NG + kind * NSA_HEADS + h, misc, 0.0), axis=-1, keepdims=True)

    parts = []
    for h in range(NSA_HEADS):
        sl = slice(h * HEAD_DIM, (h + 1) * HEAD_DIM)
        parts.append((gate(0, h) * ocmp_ref[:, sl] + gate(1, h) * osel_ref[:, sl]
                      + gate(2, h) * owin_ref[:, sl]).astype(BF16))
    o_nsa = jnp.concatenate(parts, axis=1)
    p_a = jnp.dot(o_nsa, wa_ref[...], preferred_element_type=F32)
    p_b = jnp.dot(odsa_ref[...].astype(BF16), wb_ref[...], preferred_element_type=F32)
    p_c = jnp.dot(conv_ref[...].astype(BF16), wc_ref[...], preferred_element_type=F32)
    z_ref[...] = (mg0_ref[...] * p_a + mg1_ref[...] * p_b + mg2_ref[...] * p_c).astype(BF16)


def _mix(hp, o_cmp, o_sel, o_win, o_dsa, conv_out, wa, wb, wc, tm):
    n = hp.shape[0]
    row = lambda w, cb=0: pl.BlockSpec((tm, w), lambda i: (i, cb))
    full = lambda a: pl.BlockSpec(a.shape, lambda i: (0, 0))
    return pl.pallas_call(
        _mix_body,
        grid=(n // tm,),
        in_specs=[row(1024), row(1024), row(1024), row(512), row(512), row(LANES, C_MISC // LANES),
                  row(D_MODEL, C_MG // D_MODEL), row(D_MODEL, C_MG // D_MODEL + 1), row(D_MODEL, C_MG // D_MODEL + 2),
                  full(wa), full(wb), full(wc)],
        out_specs=row(D_MODEL),
        out_shape=jax.ShapeDtypeStruct((n, D_MODEL), BF16),
        compiler_params=_cparams("arbitrary"),
        name="branch_merge",
    )(o_cmp, o_sel, o_win, o_dsa, conv_out, hp, hp, hp, hp, wa, wb, wc)


def _outln_body(z_ref, wo_ref, x_ref, g_ref, b_ref, wr_ref, rb_ref, x1_ref, gate_ref, *, tm):
    y = jnp.dot(z_ref[...], wo_ref[...], preferred_element_type=F32)
    x1 = _layernorm(ALPHA * x_ref[...] + y, g_ref[...], b_ref[...])
    x1_ref[...] = x1
    x_hi = x1.astype(BF16)
    x_lo = (x1 - x_hi.astype(F32)).astype(BF16)
    w_hi = wr_ref[...].astype(BF16)
    w_lo = (wr_ref[...] - w_hi.astype(F32)).astype(BF16)
    aff = _sigmoid(_dot_nt(w_hi, x_hi) + (_dot_nt(w_hi, x_lo) + _dot_nt(w_lo, x_hi)))
    biased = aff + rb_ref[:, 0:1]
    rows = [biased[e:e + 1, :] for e in range(N_EXPERTS)]
    best = None
    g_best = jnp.zeros((1, tm), jnp.int32)
    for g in range(N_GROUPS):
        v = rows[g * EXPERTS_PER_GROUP:(g + 1) * EXPERTS_PER_GROUP]
        score = None
        for a in range(EXPERTS_PER_GROUP):
            for c in range(a + 1, EXPERTS_PER_GROUP):
                pair = v[a] + v[c]
                score = pair if score is None else jnp.maximum(score, pair)
        if best is None:
            best = score
        else:
            better = score > best
            best = jnp.where(better, score, best)
            g_best = jnp.where(better, g, g_best)
    sel_rows = []
    for e in range(N_EXPERTS):
        g = e // EXPERTS_PER_GROUP
        rank = jnp.zeros((1, tm), F32)
        for o in range(g * EXPERTS_PER_GROUP, (g + 1) * EXPERTS_PER_GROUP):
            if o == e:
                continue
            beats = (rows[o] > rows[e]) | ((rows[o] == rows[e]) & (o < e))
            rank = rank + jnp.where(beats, 1.0, 0.0)
        sel_rows.append(jnp.where((g_best == g) & (rank < 2), aff[e:e + 1, :], 0.0))
    tot = sel_rows[0]
    for e in range(1, N_EXPERTS):
        tot = tot + sel_rows[e]
    gate_t = jnp.concatenate(sel_rows + [jnp.zeros((LANES - N_EXPERTS, tm), F32)], axis=0) / tot
    gate_ref[...] = gate_t.T


def _outln(z, wo, x, g, b, wr_t, rb, tm):
    n = z.shape[0]
    row = lambda w: pl.BlockSpec((tm, w), lambda i: (i, 0))
    full = lambda a: pl.BlockSpec(a.shape, lambda i: (0, 0))
    return pl.pallas_call(
        functools.partial(_outln_body, tm=tm),
        grid=(n // tm,),
        in_specs=[row(D_MODEL), full(wo), row(D_MODEL), full(g), full(b), full(wr_t), full(rb)],
        out_specs=[row(D_MODEL), row(LANES)],
        out_shape=[jax.ShapeDtypeStruct((n, D_MODEL), F32), jax.ShapeDtypeStruct((n, LANES), F32)],
        compiler_params=_cparams("arbitrary"),
        name="out_proj_ln_router",
    )(z, wo, x, g, b, wr_t, rb)


def _moe_body(x_ref, gate_ref, wg_ref, wu_ref, wd_ref, g_ref, b_ref, o_ref, xb_ref, acc_ref):
    e = pl.program_id(1)

    @pl.when(e == 0)
    def _():
        xb_ref[...] = x_ref[...].astype(BF16)
        acc_ref[...] = jnp.zeros(acc_ref.shape, F32)

    lane128 = lax.broadcasted_iota(jnp.int32, (1, LANES), 1)
    gcol = jnp.sum(jnp.where(lane128 == e, gate_ref[...], 0.0), axis=-1, keepdims=True)
    xb = xb_ref[...]
    hg = jnp.dot(xb, wg_ref[0], preferred_element_type=F32)
    hu = jnp.dot(xb, wu_ref[0], preferred_element_type=F32)
    h = (hg * _sigmoid(hg)) * hu * gcol
    acc_ref[...] += jnp.dot(h.astype(BF16), wd_ref[0], preferred_element_type=F32)

    @pl.when(e == N_EXPERTS - 1)
    def _():
        o_ref[...] = _layernorm(ALPHA * x_ref[...] + acc_ref[...], g_ref[...], b_ref[...])


def _moe(x1, gate, wg, wu, wd, g, b, tm):
    n = x1.shape[0]
    return pl.pallas_call(
        _moe_body,
        grid=(n // tm, N_EXPERTS),
        in_specs=[pl.BlockSpec((tm, D_MODEL), lambda i, e: (i, 0)),
                  pl.BlockSpec((tm, LANES), lambda i, e: (i, 0)),
                  pl.BlockSpec((1, D_MODEL, D_FF), lambda i, e: (e, 0, 0)),
                  pl.BlockSpec((1, D_MODEL, D_FF), lambda i, e: (e, 0, 0)),
                  pl.BlockSpec((1, D_FF, D_MODEL), lambda i, e: (e, 0, 0)),
                  pl.BlockSpec((1, D_MODEL), lambda i, e: (0, 0)),
                  pl.BlockSpec((1, D_MODEL), lambda i, e: (0, 0))],
        out_specs=pl.BlockSpec((tm, D_MODEL), lambda i, e: (i, 0)),
        out_shape=jax.ShapeDtypeStruct((n, D_MODEL), F32),
        scratch_shapes=[pltpu.VMEM((tm, D_MODEL), BF16), pltpu.VMEM((tm, D_MODEL), F32)],
        compiler_params=_cparams("arbitrary", "arbitrary"),
        name="moe_ln",
    )(x1, gate, wg, wu, wd, g, b)


def _overlap_matrix(ncp, nselp):
    cs = np.arange(ncp)[:, None] * CMP_STRIDE
    ss = np.arange(nselp)[None, :] * SEL_BLOCK
    return jnp.asarray(((cs < ss + SEL_BLOCK) & (cs + CMP_LEN > ss)).astype(np.float32))


class _Group:
    def __init__(self, nb, t_real, t_pad, q0, past_len, tm, tq):
        self.nb, self.t_real, self.t_pad, self.q0, self.past_len, self.tm, self.tq = nb, t_real, t_pad, q0, past_len, tm, tq
        self.paged = past_len > 0
        self.lp = past_len if self.paged else t_pad
        self.ltot = self.lp + (t_real if self.paged else 0)
        self.lpad = self.lp + (LANES if self.paged else 0)
        self.qt_n = t_pad // tq
        n_chunks = -(-self.ltot // CMP_STRIDE)
        self.n_cmp = n_chunks - CMP_LEN // CMP_STRIDE + 1
        self.n_sel = -(-self.ltot // SEL_BLOCK)
        self.nselp = -(-self.n_sel // LANES) * LANES
        self.n_keep = min(DSA_TOPK, self.ltot // 4)


def _mixer(gp, x, lw, caches):
    hp = _proj(x.astype(BF16), lw["w_in"], gp.tabs, lw["kinds"], min(2 * gp.tm, gp.nb * gp.t_pad, gp.tabs.shape[1]))
    nb, tq, qt_n, q0 = gp.nb, gp.tq, gp.qt_n, gp.q0
    if gp.paged:
        l, nsa_pool, dsa_pool, kidx_pool, win_src, s_conv, pt = caches
        cmp_lay = _PageLayout("cache", 0, 0, base=0, layer=l, nslot=2)
        slc_lay = _PageLayout("cache", 0, 1, base=0, layer=l, nslot=2)
        dsa_lay = _PageLayout("cache", 0, 0, base=0, layer=l, nslot=2)
        kidx_lay = _PageLayout("cache4", IDX_DIM, 0, base=0, layer=l, nslot=0)
        tail = hp
        nwb, win_k0, win_layer = WINDOW // PAGE, PAST_LEN - WINDOW, l
        past8 = jnp.concatenate([jnp.zeros((nb, SUBLANES - 2, CONV_DIM), F32), s_conv[l]], axis=1)
    else:
        pt = gp.pt
        nsa_pool = dsa_pool = kidx_pool = hp.reshape(nb * gp.t_pad // PAGE, PAGE, NCOL)
        cmp_lay = _PageLayout("cols", 512, C_CMP // 512, base=0, layer=0, nslot=0)
        slc_lay = _PageLayout("cols", 512, C_SLC // 512, base=0, layer=0, nslot=0)
        dsa_lay = _PageLayout("cols", 512, C_DSA // 512, base=0, layer=0, nslot=0)
        kidx_lay = _PageLayout("cols", LANES, C_MISC // LANES, base=0, layer=0, nslot=0)
        tail = None
        win_src = hp
        nwb, win_k0, win_layer = WINDOW // PAGE + 1, 0, 0
        past8 = jnp.zeros((nb, SUBLANES, CONV_DIM), F32)

    conv_out, cu = _conv(hp, past8, lw["conv_w8"], min(gp.tm, gp.t_pad), gp.t_pad // min(gp.tm, gp.t_pad))

    ab = _cmp1(nsa_pool, cmp_lay, pt, lw["wcat"], lw["pe"])
    cmp = _cmp2(ab, tail, lw["pe"], lw["w1r"], lw["phi_w2"], gp.t_real if gp.paged else 0, C_CMP // 512)
    o_cmp, o_win, kmask = _nsa1(hp, cmp, gp.ov, win_src, tail, nb=nb, tq=tq, qt_n=qt_n, q0=q0, n_cmp=gp.n_cmp,
                                n_sel=gp.n_sel, lpad=gp.lpad, nwb=nwb, win_k0=win_k0, win_layer=win_layer)
    o_sel = _mattn(hp, C_QROT, NSA_HEADS, kmask, nsa_pool, slc_lay, pt, tail, C_SLC // 512,
                   nb=nb, tq=tq, qt_n=qt_n, q0=q0, name="nsa_selected_attn")
    dmask = _dsa1(hp, kidx_pool, kidx_lay, pt, tail, nb=nb, tq=tq, qt_n=qt_n, q0=q0, lpad=gp.lpad,
                  n_keep=gp.n_keep)
    o_dsa = _mattn(hp, C_DQ, DSA_HEADS, dmask, dsa_pool, dsa_lay, pt, tail, C_DSA // 512,
                   nb=nb, tq=tq, qt_n=qt_n, q0=q0, name="dsa_topk_attn")
    z = _mix(hp, o_cmp, o_sel, o_win, o_dsa, conv_out, lw["w_a"], lw["w_b"], lw["w_c"], min(gp.tm, 256))
    return z, hp, cu


def _layer(gp, x, lw, caches):
    z, hp, cu = _mixer(gp, x, lw, caches)
    tm2 = min(gp.tm, 256)
    x1, gate = _outln(z, lw["w_o"], x, lw["ln_mix_g"], lw["ln_mix_b"], lw["wr_t"], lw["rb"], tm2)
    x2 = _moe(x1, gate, lw["w_eg"], lw["w_eu"], lw["w_ed"], lw["ln_ffn_g"], lw["ln_ffn_b"], min(gp.tm, 512))
    return x2, hp, cu


def _layer_weights(l, w_in, nsa_phi_pos, nsa_phi_w1, nsa_phi_w2, conv_w, w_br_a, w_br_b, w_br_c, w_out,
                   ln_mix_g, ln_mix_b, ln_ffn_g, ln_ffn_b, w_router, router_bias, w_e_gate, w_e_up, w_e_down):
    w1r = nsa_phi_w1[l].reshape(2, CMP_LEN, HEAD_DIM, HEAD_DIM)
    wcat = jnp.concatenate([w1r[:, :CMP_STRIDE], w1r[:, CMP_STRIDE:]], axis=-1).astype(BF16)
    return dict(
        w_in=_permute_w_in(w_in, l), kinds=jnp.asarray(_col_kinds()),
        w1r=w1r, wcat=wcat, pe=nsa_phi_pos[l], phi_w2=nsa_phi_w2[l],
        conv_w8=jnp.concatenate([conv_w[l], jnp.zeros((SUBLANES - 3, CONV_DIM), F32)], axis=0),
        w_a=w_br_a[l].astype(BF16), w_b=w_br_b[l].astype(BF16), w_c=w_br_c[l].astype(BF16),
        w_o=w_out[l].astype(BF16),
        ln_mix_g=ln_mix_g[l][None], ln_mix_b=ln_mix_b[l][None],
        ln_ffn_g=ln_ffn_g[l][None], ln_ffn_b=ln_ffn_b[l][None],
        wr_t=w_router.T, rb=jnp.broadcast_to(router_bias[:, None], (N_EXPERTS, LANES)),
        w_eg=w_e_gate[l].astype(BF16), w_eu=w_e_up[l].astype(BF16), w_ed=w_e_down[l].astype(BF16))


def kernel(x_prompt, x_sample, cache_nsa_kv, cache_dsa_kv, cache_dsa_kidx, state_nsa_win, state_conv, page_table,
           w_in, nsa_phi_pos, nsa_phi_w1, nsa_phi_w2, conv_w, w_br_a, w_br_b, w_br_c, w_out, ln_mix_g, ln_mix_b,
           ln_ffn_g, ln_ffn_b, w_router, router_bias, w_e_gate, w_e_up, w_e_down):
    bp, tp, _ = x_prompt.shape
    bs, ts, _ = x_sample.shape
    ts_pad = SUBLANES

    gp_p = _Group(bp, tp, tp, 0, 0, tm=512, tq=128)
    gp_p.tabs = _rope_tables(jnp.arange(tp, dtype=jnp.int32))
    gp_p.pt = jnp.arange(bp * tp // PAGE, dtype=jnp.int32).reshape(bp, tp // PAGE)
    gp_p.ov = _overlap_matrix(LANES, gp_p.nselp)
    gp_s = _Group(bs, ts, ts_pad, PAST_LEN, PAST_LEN, tm=bs * ts_pad, tq=ts_pad)
    pos_s = PAST_LEN + jnp.arange(ts_pad, dtype=jnp.int32)
    gp_s.tabs = jnp.tile(_rope_tables(pos_s), (1, bs, 1))
    gp_s.ov = _overlap_matrix(PAST_LEN // CMP_STRIDE, gp_s.nselp)

    xp = x_prompt.reshape(bp * tp, D_MODEL)
    xs = jnp.concatenate([x_sample, jnp.zeros((bs, ts_pad - ts, D_MODEL), F32)], axis=1).reshape(bs * ts_pad, D_MODEL)

    outs_p = [[] for _ in range(5)]
    outs_s = [[] for _ in range(5)]
    for l in range(DEPTH):
        lw = _layer_weights(l, w_in, nsa_phi_pos, nsa_phi_w1, nsa_phi_w2, conv_w, w_br_a, w_br_b, w_br_c, w_out,
                            ln_mix_g, ln_mix_b, ln_ffn_g, ln_ffn_b, w_router, router_bias, w_e_gate, w_e_up, w_e_down)
        xp, hp_p, cu_p = _layer(gp_p, xp, lw, None)
        xs, hp_s, cu_s = _layer(gp_s, xs, lw, (l, cache_nsa_kv, cache_dsa_kv, cache_dsa_kidx,
                                                 state_nsa_win, state_conv, page_table))
        h3 = hp_p.reshape(bp, tp, NCOL)
        outs_p[0].append(h3[:, :, C_CMP:C_CMP + 1024].reshape(bp, tp, 4, NSA_KV, HEAD_DIM))
        outs_p[1].append(h3[:, :, C_DSA:C_DSA + 512].reshape(bp, tp, 2, DSA_KV, HEAD_DIM))
        outs_p[2].append(h3[:, :, C_MISC:C_MISC + IDX_DIM])
        outs_p[3].append(h3[:, tp - min(WINDOW, tp):, C_WIN:C_WIN + 512].reshape(bp, min(WINDOW, tp), 2, NSA_KV, HEAD_DIM))
        outs_p[4].append(cu_p.reshape(bp, tp, CONV_DIM)[:, tp - 2:])
        s3 = hp_s.reshape(bs, ts_pad, NCOL)[:, :ts]
        outs_s[0].append(s3[:, :, C_CMP:C_CMP + 1024].reshape(bs, ts, 4, NSA_KV, HEAD_DIM))
        outs_s[1].append(s3[:, :, C_DSA:C_DSA + 512].reshape(bs, ts, 2, DSA_KV, HEAD_DIM))
        outs_s[2].append(s3[:, :, C_MISC:C_MISC + IDX_DIM])
        win_new = s3[:, :, C_WIN:C_WIN + 512].reshape(bs, ts, 2, NSA_KV, HEAD_DIM)
        wb = state_nsa_win.shape[2]
        outs_s[3].append(jnp.concatenate([state_nsa_win[l], win_new], axis=1)[:, -wb:])
        ext = jnp.concatenate([state_conv[l], cu_s.reshape(bs, ts_pad, CONV_DIM)[:, :ts]], axis=1)
        outs_s[4].append(ext[:, -2:])
    sp = [jnp.stack(a, axis=0) for a in outs_p]
    ss = [jnp.stack(a, axis=0) for a in outs_s]
    y_p = xp.reshape(bp, tp, D_MODEL)
    y_s = xs.reshape(bs, ts_pad, D_MODEL)[:, :ts]
    return (y_p, y_s, sp[0], ss[0], sp[1], ss[1], sp[2], ss[2], sp[3], ss[3], sp[4], ss[4])
```

```python
import collections
import functools

import numpy as np
import jax
import jax.numpy as jnp
from jax import lax
from jax.experimental import pallas as pl
from jax.experimental.pallas import tpu as pltpu

F32 = jnp.float32
BF16 = jnp.bfloat16
HIGHEST = lax.Precision.HIGHEST
NEG_INF = float("-inf")

D_MODEL = 2048
DEPTH = 2
PAST_LEN = 16384
PAGE = 128
HEAD_DIM = 128
ROPE_THETA = 500000.0
NSA_HEADS = 8
NSA_KV = 2
CMP_LEN = 32
CMP_STRIDE = 16
SEL_BLOCK = 64
SEL_TOP = 16
WINDOW = 512
FORCE_SCORE = 1e4
DSA_HEADS = 4
DSA_KV = 2
IDX_HEADS = 4
IDX_DIM = 64
DSA_TOPK = 256
CONV_DIM = 512
N_EXPERTS = 16
N_GROUPS = 4
EXPERTS_PER_GROUP = 4
D_FF = 512
LN_EPS = 1e-5
ALPHA = (2 * DEPTH) ** 0.25
IN_WIDTHS = (1024, 1536, 24, 512, 512, 256, 64, 4, 1536, 6144)
ATT_SCALE = HEAD_DIM ** -0.5

LANES = 128
SUBLANES = 8
VMEM_LIMIT = 56 * 1024 * 1024

C_QRAW = 0
C_QROT = 1024
C_CMP = 2048
C_SLC = 2560
C_WIN = 3072
C_DQ = 3584
C_DSA = 4096
C_CV = 4608
C_MG = 6144
C_IQ = 12288
C_MISC = 12544
NCOL = 12800
MISC_IW = 64
MISC_NG = 68
PROJ_TN = 1280
PAGES_PER_STEP = 16
KSTEP = PAGES_PER_STEP * PAGE

K_PLAIN, K_ROPE128, K_ROPE64, K_SIGMOID, K_MISC = 0, 1, 2, 3, 4


def _col_kinds():
    kinds = np.zeros(NCOL // LANES, np.int32)

    def mark(c0, n, k):
        kinds[c0 // LANES:(c0 + n) // LANES] = k

    mark(C_QROT, 1024, K_ROPE128)
    mark(C_SLC, 256, K_ROPE128)
    mark(C_WIN, 256, K_ROPE128)
    mark(C_DQ, 512, K_ROPE128)
    mark(C_DSA, 256, K_ROPE128)
    mark(C_IQ, 256, K_ROPE64)
    mark(C_MISC, 128, K_MISC)
    mark(C_MG, 6144, K_SIGMOID)
    return kinds


def _cparams(*sem):
    return pltpu.CompilerParams(dimension_semantics=sem, vmem_limit_bytes=VMEM_LIMIT)


def _sigmoid(x):
    return 1.0 / (1.0 + jnp.exp(-x))


def _layernorm(x, g, b):
    mu = jnp.mean(x, axis=-1, keepdims=True)
    xc = x - mu
    var = jnp.mean(xc * xc, axis=-1, keepdims=True)
    return xc * lax.rsqrt(var + LN_EPS) * g + b


def _dot_nt(a, b, precision=None):
    return lax.dot_general(a, b, (((1,), (1,)), ((), ())), preferred_element_type=F32, precision=precision)


def _softmax_parts(s):
    m = jnp.max(s, axis=-1, keepdims=True)
    m = jnp.where(m == NEG_INF, 0.0, m)
    p = jnp.exp(s - m)
    return p, jnp.maximum(jnp.sum(p, axis=-1, keepdims=True), 1e-30)


def _stack_heads(q_ref, g, r_per_g):
    return jnp.concatenate([q_ref[:, (g * r_per_g + r) * HEAD_DIM:(g * r_per_g + r + 1) * HEAD_DIM]
                            for r in range(r_per_g)], axis=0).astype(BF16)


def _proj_body(kinds_ref, needs_ref, x_ref, wt_ref, tab_ref, o_ref, w_ref):
    nsub = PROJ_TN // LANES
    j = pl.program_id(0)

    @pl.when(pl.program_id(1) == 0)
    def _():
        for s in range(nsub):
            w_ref[:, s * LANES:(s + 1) * LANES] = wt_ref[s * LANES:(s + 1) * LANES, :].astype(F32).T.astype(BF16)

    h = jnp.dot(x_ref[...], w_ref[...], preferred_element_type=F32)
    lane = lax.broadcasted_iota(jnp.int32, (1, LANES), 1)

    def rope(hs, t0, sh):
        return (hs * tab_ref[t0] + pltpu.roll(hs, sh, 1) * tab_ref[t0 + 1]
                + pltpu.roll(hs, LANES - sh, 1) * tab_ref[t0 + 2])

    for s in range(nsub):
        kind = kinds_ref[j * nsub + s]
        hs = h[:, s * LANES:(s + 1) * LANES]
        sl = slice(s * LANES, (s + 1) * LANES)

        @pl.when(kind == K_PLAIN)
        def _():
            o_ref[:, sl] = hs

        @pl.when(kind == K_ROPE128)
        def _():
            o_ref[:, sl] = rope(hs, 0, 16)

        @pl.when(kind == K_ROPE64)
        def _():
            o_ref[:, sl] = rope(hs, 3, 8)

        @pl.when(kind == K_SIGMOID)
        def _():
            o_ref[:, sl] = _sigmoid(hs)

        @pl.when(kind == K_MISC)
        def _():
            r = rope(hs, 3, 8)
            o_ref[:, sl] = jnp.where(lane < MISC_IW, r,
                                     jnp.where(lane < MISC_NG, hs * (IDX_HEADS ** -0.5),
                                               jnp.where(lane < MISC_NG + 24, _sigmoid(hs), hs)))


def _proj(x_bf, wt_bf, tabs, kinds, tm):
    n = x_bf.shape[0]
    n_tab = tabs.shape[1] // tm
    grid = (NCOL // PROJ_TN, n // tm)
    rotary = np.isin(_col_kinds().reshape(-1, PROJ_TN // LANES), (K_ROPE128, K_ROPE64, K_MISC)).any(axis=1)
    needs = jnp.asarray(rotary.astype(np.int32))
    return pl.pallas_call(
        _proj_body,
        grid_spec=pltpu.PrefetchScalarGridSpec(
            num_scalar_prefetch=2, grid=grid,
            in_specs=[pl.BlockSpec((tm, D_MODEL), lambda j, i, k, nd: (i, 0)),
                      pl.BlockSpec((PROJ_TN, D_MODEL), lambda j, i, k, nd: (j, 0)),
                      pl.BlockSpec((6, tm, LANES), lambda j, i, k, nd: (0, (i % n_tab) * nd[j], 0))],
            out_specs=pl.BlockSpec((tm, PROJ_TN), lambda j, i, k, nd: (i, j)),
            scratch_shapes=[pltpu.VMEM((D_MODEL, PROJ_TN), BF16)]),
        out_shape=jax.ShapeDtypeStruct((n, NCOL), F32),
        compiler_params=_cparams("arbitrary", "arbitrary"),
        name="in_proj",
    )(kinds, needs, x_bf, wt_bf, tabs)


def _rope_tables(pos):
    out = []
    lane = jnp.arange(LANES)
    for d in (HEAD_DIM, IDX_DIM):
        rot = d // 4
        half = rot // 2
        inv = ROPE_THETA ** (-jnp.arange(half, dtype=F32) / half)
        ang = pos.astype(F32)[:, None] * inv[None, :]
        cos = jnp.cos(ang)
        sin = jnp.sin(ang)
        li = lane % d
        ci = jnp.take(cos, li % half, axis=1)
        si = jnp.take(sin, li % half, axis=1)
        out.append(jnp.where(li[None] < rot, ci, 1.0))
        out.append(jnp.where((li[None] >= half) & (li[None] < rot), si, 0.0))
        out.append(jnp.where(li[None] < half, -si, 0.0))
    return jnp.stack(out, axis=0)


def _permute_w_in(w_in, l):
    wt = jnp.transpose(w_in, (2, 0, 1))[:, l, :].astype(BF16)
    offs = np.cumsum((0,) + IN_WIDTHS)
    nq, nkv, ng, dq, dkv, iq, ik, iw, cv, mg = [wt[offs[i]:offs[i + 1]] for i in range(10)]
    z = lambda n: jnp.zeros((n, wt.shape[1]), BF16)
    rows = [nq, nq, nkv, dq, dkv, cv, mg, iq, ik, iw, ng, z(LANES - 92), z(NCOL - C_MISC - LANES)]
    return jnp.concatenate(rows, axis=0)


def _conv_body(cv_ref, prev_ref, past_ref, w_ref, y_ref, cu_ref, s_ref, *, tiles_per_seq, tm):
    i = pl.program_id(0)
    b = cv_ref[:, 0:CONV_DIM]
    cu = cv_ref[:, CONV_DIM:2 * CONV_DIM] * cv_ref[:, 2 * CONV_DIM:3 * CONV_DIM]
    first = (i % tiles_per_seq) == 0
    prev = prev_ref[:, CONV_DIM:2 * CONV_DIM] * prev_ref[:, 2 * CONV_DIM:3 * CONV_DIM]
    s_ref[0:SUBLANES, :] = jnp.where(first, past_ref[0], prev)
    s_ref[SUBLANES:SUBLANES + tm, :] = cu
    y = (w_ref[0:1, :] * s_ref[pl.ds(SUBLANES - 2, tm), :] + w_ref[1:2, :] * s_ref[pl.ds(SUBLANES - 1, tm), :]
         + w_ref[2:3, :] * cu)
    y_ref[...] = b * y
    cu_ref[...] = cu


def _conv(hp, past8, conv_w8, tm, tiles_per_seq):
    n = hp.shape[0]
    cvb = C_CV // (3 * CONV_DIM)
    rb = tm // SUBLANES
    return pl.pallas_call(
        functools.partial(_conv_body, tiles_per_seq=tiles_per_seq, tm=tm),
        grid=(n // tm,),
        in_specs=[pl.BlockSpec((tm, 3 * CONV_DIM), lambda i: (i, cvb)),
                  pl.BlockSpec((SUBLANES, 3 * CONV_DIM), lambda i: (jnp.maximum(i * rb - 1, 0), cvb)),
                  pl.BlockSpec((1, SUBLANES, CONV_DIM), lambda i: (i // tiles_per_seq, 0, 0)),
                  pl.BlockSpec((SUBLANES, CONV_DIM), lambda i: (0, 0))],
        out_specs=[pl.BlockSpec((tm, CONV_DIM), lambda i: (i, 0)),
                   pl.BlockSpec((tm, CONV_DIM), lambda i: (i, 0))],
        out_shape=[jax.ShapeDtypeStruct((n, CONV_DIM), F32), jax.ShapeDtypeStruct((n, CONV_DIM), F32)],
        scratch_shapes=[pltpu.VMEM((tm + SUBLANES, CONV_DIM), F32)],
        compiler_params=_cparams("arbitrary"),
        name="short_conv",
    )(hp, hp, past8, conv_w8)


_PageLayout = collections.namedtuple("_PageLayout", "kind width col_block base layer nslot")


def _page_specs(layout, n_lead, kg_n=None):
    def spec(k):
        def imap(*a):
            ids, pt = a[:n_lead], a[-1]
            kg = ids[-1] % kg_n if kg_n else ids[-1]
            page = pt[ids[0], kg * PAGES_PER_STEP + k]
            if layout.kind == "cache":
                return (layout.layer, page, 0, layout.col_block, 0, 0)
            if layout.kind == "cache4":
                return (layout.layer, page, 0, 0)
            cb = layout.col_block(*ids) if callable(layout.col_block) else layout.col_block
            return (page, 0, cb)
        shape = {"cache": (1, 1, PAGE, layout.nslot, 2, HEAD_DIM), "cache4": (1, 1, PAGE, layout.width),
                 "cols": (1, PAGE, layout.width)}[layout.kind]
        return pl.BlockSpec(shape, imap)
    return [spec(k) for k in range(PAGES_PER_STEP)]


def _page_tile(layout, ref, j, width=LANES):
    if layout.kind == "cache":
        slot, g = divmod(layout.base + j, 2)
        return ref[0, 0, :, slot, g, :]
    if layout.kind == "cache4":
        return ref[0, 0, :, j * width:(j + 1) * width]
    return ref[0, :, j * width:(j + 1) * width]


def _chunk_rows(layout, ref, j, p):
    if layout.kind == "cache":
        slot, g = divmod(layout.base + j, 2)
        return ref[0, 0, pl.ds(p, SUBLANES, stride=CMP_STRIDE), slot, g, :]
    return ref[0, pl.ds(p, SUBLANES, stride=CMP_STRIDE), :]


def _cmp1_body(pt_ref, *refs, layout, n_inner):
    pages = refs[:PAGES_PER_STEP]
    w_ref, pe_ref = refs[PAGES_PER_STEP:PAGES_PER_STEP + 2]
    o_ref = refs[PAGES_PER_STEP + 2]
    for j in range(n_inner):
        slot = j // 2
        acc_a = jnp.zeros((PAGE, HEAD_DIM), F32)
        acc_b = jnp.zeros((PAGE, HEAD_DIM), F32)
        for p in range(CMP_STRIDE):
            xp = jnp.concatenate([_chunk_rows(layout, pg, j, p) for pg in pages], axis=0)
            xa = (xp + pe_ref[slot, p:p + 1, :]).astype(BF16)
            xb = (xp + pe_ref[slot, CMP_STRIDE + p:CMP_STRIDE + p + 1, :]).astype(BF16)
            acc_a = acc_a + jnp.dot(xa, w_ref[slot, p, :, 0:HEAD_DIM], preferred_element_type=F32)
            acc_b = acc_b + jnp.dot(xb, w_ref[slot, p, :, HEAD_DIM:2 * HEAD_DIM], preferred_element_type=F32)
        o_ref[0, :, j * 256:j * 256 + HEAD_DIM] = acc_a
        o_ref[0, :, j * 256 + HEAD_DIM:(j + 1) * 256] = acc_b


def _cmp1(pool, layout, pt, wcat, pe):
    nb, npg = pt.shape
    kg = npg // PAGES_PER_STEP
    if layout.kind == "cache":
        n_inner, n_sg, wsel, osel, ow = 4, 1, (lambda sg: 0), (lambda sg: 0), 1024
        wblk = 2
    else:
        cb = layout.col_block
        layout = layout._replace(width=HEAD_DIM, col_block=lambda b, sg, k: cb * 4 + sg)
        n_inner, n_sg, wsel, osel, ow = 1, 4, (lambda sg: sg // 2), (lambda sg: sg), 256
        wblk = 1
    return pl.pallas_call(
        functools.partial(_cmp1_body, layout=layout, n_inner=n_inner),
        grid_spec=pltpu.PrefetchScalarGridSpec(
            num_scalar_prefetch=1, grid=(nb, n_sg, kg),
            in_specs=_page_specs(layout, 3)
            + [pl.BlockSpec((wblk, 16, HEAD_DIM, 256), lambda b, sg, k, pt: (wsel(sg), 0, 0, 0)),
               pl.BlockSpec((wblk, CMP_LEN, HEAD_DIM), lambda b, sg, k, pt: (wsel(sg), 0, 0))],
            out_specs=pl.BlockSpec((1, PAGE, ow), lambda b, sg, k, pt: (b, k, osel(sg)))),
        out_shape=jax.ShapeDtypeStruct((nb, npg * SUBLANES, 1024), F32),
        compiler_params=_cparams("arbitrary", "arbitrary", "arbitrary"),
        name="nsa_compress1",
    )(pt, *([pool] * PAGES_PER_STEP), wcat, pe)


def _gelu_tanh(x):
    return 0.5 * x * (1.0 + jnp.tanh(np.sqrt(2.0 / np.pi).astype(np.float32) * (x + 0.044715 * (x * x * x))))


def _cmp2_body(ab_ref, tail_ref, pe_ref, w1_ref, w2_ref, o_ref, s_ref, *, nc, n_tail):
    row8 = lax.broadcasted_iota(jnp.int32, (SUBLANES, 1), 0)
    row16 = lax.broadcasted_iota(jnp.int32, (CMP_STRIDE, 1), 0)
    for sg in range(4):
        slot = sg // 2
        a = ab_ref[0, :, sg * 256:sg * 256 + HEAD_DIM]
        s_ref[0:nc, :] = ab_ref[0, :, sg * 256 + HEAD_DIM:(sg + 1) * 256]
        tb = jnp.zeros((SUBLANES, HEAD_DIM), F32)
        if n_tail:
            x8 = jnp.where(row8 < n_tail, tail_ref[:, sg * HEAD_DIM:(sg + 1) * HEAD_DIM], 0.0)
            x16 = jnp.concatenate([x8, jnp.zeros((CMP_STRIDE - SUBLANES, HEAD_DIM), F32)], axis=0)
            x16 = x16 + pe_ref[slot, CMP_STRIDE:CMP_LEN, :]
            t16 = jnp.zeros((CMP_STRIDE, HEAD_DIM), F32)
            for p in range(CMP_STRIDE):
                xm = jnp.where(row16 == p, x16, 0.0).astype(BF16)
                t16 = t16 + jnp.dot(xm, w1_ref[slot, CMP_STRIDE + p].astype(BF16), preferred_element_type=F32)
            tb = jnp.sum(t16, axis=0, keepdims=True) * jnp.where(row8 == 0, 1.0, 0.0)
        s_ref[nc:nc + SUBLANES, :] = tb
        pre = a + s_ref[pl.ds(1, nc), :]
        o_ref[0, :, sg * HEAD_DIM:(sg + 1) * HEAD_DIM] = jnp.dot(
            _gelu_tanh(pre).astype(BF16), w2_ref[slot].astype(BF16), preferred_element_type=F32)


def _cmp2(ab, tail, pe, w1r, w2, n_tail, tail_col_block):
    nb, nc, _ = ab.shape
    if tail is None:
        tail = jnp.zeros((nb * SUBLANES, 512), F32)
        tail_col_block = 0
    return pl.pallas_call(
        functools.partial(_cmp2_body, nc=nc, n_tail=n_tail),
        grid=(nb,),
        in_specs=[pl.BlockSpec((1, nc, 1024), lambda b: (b, 0, 0)),
                  pl.BlockSpec((SUBLANES, 512), lambda b: (b, tail_col_block)),
                  pl.BlockSpec((2, CMP_LEN, HEAD_DIM), lambda b: (0, 0, 0)),
                  pl.BlockSpec((2, CMP_LEN, HEAD_DIM, HEAD_DIM), lambda b: (0, 0, 0, 0)),
                  pl.BlockSpec((2, HEAD_DIM, HEAD_DIM), lambda b: (0, 0, 0))],
        out_specs=pl.BlockSpec((1, nc, 512), lambda b: (b, 0, 0)),
        out_shape=jax.ShapeDtypeStruct((nb, nc, 512), F32),
        scratch_shapes=[pltpu.VMEM((nc + SUBLANES, HEAD_DIM), F32)],
        compiler_params=_cparams("arbitrary"),
        name="nsa_compress2",
    )(ab, tail, pe, w1r, w2)


def _nsa1_body(*refs, tq, q0, n_cmp, ncp, n_sel, nselp, lpad, nwb, has_tail, win_k0):
    qraw_ref, qrot_ref, cmp_ref, ov_ref = refs[:4]
    wins = refs[4:4 + nwb]
    pos = 4 + nwb
    tail_ref = None
    if has_tail:
        tail_ref = refs[pos]
        pos += 1
    ocmp_ref, owin_ref, kmask_ref = refs[pos:pos + 3]
    qt = pl.program_id(1)
    r_per_g = NSA_HEADS // NSA_KV
    qpos = q0 + qt * tq + lax.broadcasted_iota(jnp.int32, (tq, 1), 0)

    jj = lax.broadcasted_iota(jnp.int32, (1, ncp), 1)
    cmask = (jj * CMP_STRIDE + (CMP_LEN - 1) <= qpos) & (jj < n_cmp)
    blk = lax.broadcasted_iota(jnp.int32, (1, nselp), 1)
    cur = qpos // SEL_BLOCK
    forced = (blk == 0) | (blk == cur) | (blk == cur - 1)
    e_row = lax.broadcasted_iota(jnp.int32, (KSTEP // SEL_BLOCK, KSTEP), 0)
    e_col = lax.broadcasted_iota(jnp.int32, (KSTEP // SEL_BLOCK, KSTEP), 1)
    expand = jnp.where(e_col // SEL_BLOCK == e_row, 1.0, 0.0).astype(BF16)
    lane128 = lax.broadcasted_iota(jnp.int32, (1, LANES), 1)
    for g in range(NSA_KV):
        kc = cmp_ref[0, :, g * HEAD_DIM:(g + 1) * HEAD_DIM].astype(BF16)
        vc = cmp_ref[0, :, (2 + g) * HEAD_DIM:(3 + g) * HEAD_DIM].astype(BF16)
        q = _stack_heads(qraw_ref, g, r_per_g)
        s = _dot_nt(q, kc) * ATT_SCALE + jnp.concatenate([jnp.where(cmask, 0.0, NEG_INF)] * r_per_g, axis=0)
        p, den = _softmax_parts(s)
        p = p / den
        o = jnp.dot(p.astype(BF16), vc, preferred_element_type=F32)
        psum = jnp.zeros((tq, ncp), F32)
        for r in range(r_per_g):
            h = g * r_per_g + r
            psum = psum + p[r * tq:(r + 1) * tq]
            ocmp_ref[:, h * HEAD_DIM:(h + 1) * HEAD_DIM] = o[r * tq:(r + 1) * tq]
        imp = jnp.dot(psum.astype(BF16), ov_ref[...].astype(BF16), preferred_element_type=F32)
        imp = jnp.where(forced, imp + FORCE_SCORE, imp)
        imp = jnp.where((blk <= cur) & (blk < n_sel), imp, NEG_INF)
        rank = jnp.zeros((tq, nselp), F32)
        for i in range(n_sel):
            vi = imp[:, i:i + 1]
            beats = (vi > imp) | ((vi == imp) & (blk > i))
            rank = rank + jnp.where(beats, 1.0, 0.0)
        sel = jnp.where((rank < min(SEL_TOP, n_sel)) & (blk < n_sel), 1.0, 0.0).astype(BF16)
        per = KSTEP // SEL_BLOCK
        for c in range(lpad // KSTEP):
            km = jnp.dot(sel[:, c * per:(c + 1) * per], expand, preferred_element_type=F32)
            kmask_ref[0, g, :, c * KSTEP:(c + 1) * KSTEP] = km.astype(BF16)
        if lpad % KSTEP:
            b0 = (lpad // KSTEP) * per
            km = jnp.where(lane128 < SEL_BLOCK, sel[:, b0:b0 + 1].astype(F32), 0.0)
            kmask_ref[0, g, :, (lpad // KSTEP) * KSTEP:lpad] = jnp.broadcast_to(km, (tq, LANES)).astype(BF16)

    nk = nwb * PAGE + (LANES if has_tail else 0)
    kk = lax.broadcasted_iota(jnp.int32, (1, nk), 1)
    if has_tail:
        kpos = jnp.where(kk < nwb * PAGE, win_k0 + kk, q0 + kk - nwb * PAGE)
    else:
        kpos = (qt - (nwb - 1)) * PAGE + kk
    rel = qpos - kpos
    wmask = (rel >= 0) & (rel < WINDOW) & (kpos >= 0)
    for g in range(NSA_KV):
        if has_tail:
            kparts = [w[0, 0, :, 0, g, :] for w in wins]
            vparts = [w[0, 0, :, 1, g, :] for w in wins]
        else:
            kparts = [w[:, g * HEAD_DIM:(g + 1) * HEAD_DIM] for w in wins]
            vparts = [w[:, (2 + g) * HEAD_DIM:(3 + g) * HEAD_DIM] for w in wins]
        if has_tail:
            zpad = jnp.zeros((LANES - SUBLANES, HEAD_DIM), F32)
            kparts += [tail_ref[:, g * HEAD_DIM:(g + 1) * HEAD_DIM], zpad]
            vparts += [tail_ref[:, (2 + g) * HEAD_DIM:(3 + g) * HEAD_DIM], zpad]
        kw = jnp.concatenate(kparts, axis=0).astype(BF16)
        vw = jnp.concatenate(vparts, axis=0).astype(BF16)
        q = _stack_heads(qrot_ref, g, r_per_g)
        s = _dot_nt(q, kw) * ATT_SCALE + jnp.concatenate([jnp.where(wmask, 0.0, NEG_INF)] * r_per_g, axis=0)
        p, den = _softmax_parts(s)
        o = jnp.dot((p / den).astype(BF16), vw, preferred_element_type=F32)
        for r in range(r_per_g):
            h = g * r_per_g + r
            owin_ref[:, h * HEAD_DIM:(h + 1) * HEAD_DIM] = o[r * tq:(r + 1) * tq]


def _nsa1(hp, cmp, ov, win_src, tail_src, *, nb, tq, qt_n, q0, n_cmp, n_sel, lpad, nwb, win_k0, win_layer):
    n = hp.shape[0]
    ncp = cmp.shape[1]
    nselp = ov.shape[1]
    has_tail = tail_src is not None
    rows = lambda b, t: b * qt_n + t
    in_specs = [pl.BlockSpec((tq, 1024), lambda b, t: (rows(b, t), C_QRAW // 1024)),
                pl.BlockSpec((tq, 1024), lambda b, t: (rows(b, t), C_QROT // 1024)),
                pl.BlockSpec((1, ncp, 512), lambda b, t: (b, 0, 0)),
                pl.BlockSpec((ncp, nselp), lambda b, t: (0, 0))]
    if has_tail:
        in_specs += [pl.BlockSpec((1, 1, PAGE, 2, NSA_KV, HEAD_DIM), lambda b, t, k=k: (win_layer, b, k, 0, 0, 0))
                     for k in range(nwb)]
        in_specs += [pl.BlockSpec((SUBLANES, 512), lambda b, t: (b, C_WIN // 512))]
        args = [win_src] * nwb + [tail_src]
    else:
        in_specs += [pl.BlockSpec((PAGE, 512),
                                  lambda b, t, k=k: (b * qt_n + jnp.maximum(t - (nwb - 1) + k, 0), C_WIN // 512))
                     for k in range(nwb)]
        args = [win_src] * nwb
    return pl.pallas_call(
        functools.partial(_nsa1_body, tq=tq, q0=q0, n_cmp=n_cmp, ncp=ncp, n_sel=n_sel, nselp=nselp, lpad=lpad,
                          nwb=nwb, has_tail=has_tail, win_k0=win_k0),
        grid=(nb, qt_n),
        in_specs=in_specs,
        out_specs=[pl.BlockSpec((tq, 1024), lambda b, t: (rows(b, t), 0)),
                   pl.BlockSpec((tq, 1024), lambda b, t: (rows(b, t), 0)),
                   pl.BlockSpec((1, NSA_KV, tq, lpad), lambda b, t: (b, 0, t, 0))],
        out_shape=[jax.ShapeDtypeStruct((n, 1024), F32), jax.ShapeDtypeStruct((n, 1024), F32),
                   jax.ShapeDtypeStruct((nb, NSA_KV, qt_n * tq, lpad), BF16)],
        compiler_params=_cparams("arbitrary", "arbitrary"),
        name="nsa_cmp_select_window",
    )(hp, hp, cmp, ov, *args)


def _mattn_body(pt_ref, *refs, layout, n_g, r_per_g, gm, tq, q0, kg_n, has_tail):
    q_ref, mask_ref = refs[:2]
    pages = refs[2:2 + PAGES_PER_STEP]
    pos = 2 + PAGES_PER_STEP
    if has_tail:
        tail_ref, tmask_ref = refs[pos:pos + 2]
        pos += 2
    o_ref, s_ref, v_ref = refs[pos:pos + 3]
    qt = pl.program_id(1)
    step = pl.program_id(2)
    two_pass = kg_n > 1 or has_tail
    kg = step
    qpos = q0 + qt * tq + lax.broadcasted_iota(jnp.int32, (tq, 1), 0)

    def scores(g, kt, kpos, mref):
        picked = jnp.where(mref[0, g if gm > 1 else 0].astype(F32) > 0.5, 0.0, NEG_INF)
        bias = jnp.concatenate([picked + jnp.where(kpos <= qpos, 0.0, NEG_INF)] * r_per_g, axis=0)
        return _dot_nt(_stack_heads(q_ref, g, r_per_g), kt) * ATT_SCALE + bias

    def emit(g, o):
        for r in range(r_per_g):
            h = g * r_per_g + r
            o_ref[:, h * HEAD_DIM:(h + 1) * HEAD_DIM] = o[r * tq:(r + 1) * tq]

    def main_keys(g):
        return jnp.concatenate([_page_tile(layout, pg, g) for pg in pages], axis=0).astype(BF16)

    def main_vals(g):
        return jnp.concatenate([_page_tile(layout, pg, n_g + g) for pg in pages], axis=0).astype(BF16)

    def tail_part(g, off):
        zpad = jnp.zeros((LANES - SUBLANES, HEAD_DIM), F32)
        return jnp.concatenate([tail_ref[:, (off + g) * HEAD_DIM:(off + g + 1) * HEAD_DIM], zpad], axis=0).astype(BF16)

    kpos = kg * KSTEP + lax.broadcasted_iota(jnp.int32, (1, KSTEP), 1)
    tpos = q0 + lax.broadcasted_iota(jnp.int32, (1, LANES), 1)

    if not two_pass:
        for g in range(n_g):
            p, den = _softmax_parts(scores(g, main_keys(g), kpos, mask_ref))
            emit(g, jnp.dot((p / den).astype(BF16), main_vals(g), preferred_element_type=F32))
        return

    for g in range(n_g):
        s_ref[g, kg] = scores(g, main_keys(g), kpos, mask_ref)
        v_ref[g, pl.ds(pl.multiple_of(kg * KSTEP, KSTEP), KSTEP), :] = main_vals(g)

    @pl.when(step == kg_n - 1)
    def _():
        for g in range(n_g):
            chunks = [s_ref[g, c] for c in range(kg_n)]
            vals = [v_ref[g, c * KSTEP:(c + 1) * KSTEP, :] for c in range(kg_n)]
            if has_tail:
                chunks.append(scores(g, tail_part(g, 0), tpos, tmask_ref))
                vals.append(tail_part(g, n_g))
            m = chunks[0].max(axis=-1, keepdims=True)
            for c in chunks[1:]:
                m = jnp.maximum(m, c.max(axis=-1, keepdims=True))
            m = jnp.where(m == NEG_INF, 0.0, m)
            exps = [jnp.exp(c - m) for c in chunks]
            den = exps[0].sum(axis=-1, keepdims=True)
            for e in exps[1:]:
                den = den + e.sum(axis=-1, keepdims=True)
            den = jnp.maximum(den, 1e-30)
            o = jnp.zeros((r_per_g * tq, HEAD_DIM), F32)
            for e, v in zip(exps, vals):
                o = o + jnp.dot((e / den).astype(BF16), v, preferred_element_type=F32)
            emit(g, o)


def _mattn(hp, q_col, n_heads, mask, pool, layout, pt, tail_src, tail_col_block, *, nb, tq, qt_n, q0, name):
    n = hp.shape[0]
    n_g = 2
    r_per_g = n_heads // n_g
    qw = n_heads * HEAD_DIM
    gm = mask.shape[1]
    kg_n = pt.shape[1] // PAGES_PER_STEP
    has_tail = tail_src is not None
    buffered = kg_n > 1 or has_tail
    in_specs = [pl.BlockSpec((tq, qw), lambda b, t, k, pt: (b * qt_n + t, q_col // qw)),
                pl.BlockSpec((1, gm, tq, KSTEP), lambda b, t, k, pt: (b, 0, t, k))]
    in_specs += _page_specs(layout, 3)
    args = [hp, mask] + [pool] * PAGES_PER_STEP
    if has_tail:
        in_specs += [pl.BlockSpec((SUBLANES, 512), lambda b, t, k, pt: (b, tail_col_block)),
                     pl.BlockSpec((1, gm, tq, LANES), lambda b, t, k, pt: (b, 0, t, kg_n * KSTEP // LANES))]
        args += [tail_src, mask]
    return pl.pallas_call(
        functools.partial(_mattn_body, layout=layout, n_g=n_g, r_per_g=r_per_g, gm=gm, tq=tq, q0=q0, kg_n=kg_n, has_tail=has_tail),
        grid_spec=pltpu.PrefetchScalarGridSpec(
            num_scalar_prefetch=1, grid=(nb, qt_n, kg_n),
            in_specs=in_specs,
            out_specs=pl.BlockSpec((tq, qw), lambda b, t, k, pt: (b * qt_n + t, 0)),
            scratch_shapes=[pltpu.VMEM((n_g, kg_n, r_per_g * tq, KSTEP) if buffered else (1, 1, SUBLANES, LANES), F32),
                            pltpu.VMEM((n_g, kg_n * KSTEP, HEAD_DIM) if buffered else (1, 2 * SUBLANES, LANES), BF16)]),
        out_shape=jax.ShapeDtypeStruct((n, qw), F32),
        compiler_params=_cparams("arbitrary", "arbitrary", "arbitrary"),
        name=name,
    )(pt, *args)


PACKED_ROWS = 2 * SUBLANES


def _topk_mask_packed_t(sc_t, n_keep):
    n_keys, n_q = sc_t.shape
    i16 = jnp.int16
    bits = pltpu.bitcast(sc_t + 0.0, jnp.int32)
    key = jnp.where(bits < 0, bits ^ jnp.int32(0x7FFFFFFF), bits)
    hi = jnp.right_shift(key, 16).astype(i16)
    lo = ((key & jnp.int32(0xFFFF)) - 32768).astype(i16)
    one_b, zero_b = jnp.asarray(1, BF16), jnp.asarray(0, BF16)
    i16_min, i16_max = jnp.asarray(-32768, i16), jnp.asarray(32767, i16)
    assert n_keys % PACKED_ROWS == 0 and n_keys // PACKED_ROWS <= 256

    def colsum(x01):
        parts = [x01[i * PACKED_ROWS:(i + 1) * PACKED_ROWS, :] for i in range(n_keys // PACKED_ROWS)]
        while len(parts) > 1:
            parts = [parts[i] + parts[i + 1] for i in range(0, len(parts), 2)]
        return jnp.sum(parts[0].astype(F32), axis=0, keepdims=True)

    def search(vals, need, nbits, start):
        def body(i, thr):
            cand = thr + jnp.left_shift(jnp.int32(1), nbits - 1 - i)
            cnt = colsum(jnp.where(vals >= cand.astype(i16), one_b, zero_b))
            return jnp.where(cnt >= need, cand, thr)
        return lax.fori_loop(0, nbits, body, jnp.full((1, n_q), start, jnp.int32))

    t_hi = search(hi, float(n_keep), 16, -32768).astype(i16)
    eq = hi == t_hi
    hi_gt = jnp.where(hi > t_hi, one_b, zero_b)
    lo_m = jnp.where(eq, lo, i16_min)
    t_lo = search(lo_m, n_keep - colsum(hi_gt), 16, -32768).astype(i16)
    gt = hi_gt + jnp.where(lo_m > t_lo, one_b, zero_b)
    need = n_keep - colsum(gt)
    idx = lax.broadcasted_iota(jnp.int32, (n_keys, 1), 0).astype(i16)
    tie_idx = jnp.where(eq, jnp.where(lo == t_lo, idx, i16_max), i16_max)
    nbits = int(np.ceil(np.log2(n_keys)))

    def ibody(i, c):
        cand = c + jnp.left_shift(jnp.int32(1), nbits - 1 - i)
        cnt = colsum(jnp.where(tie_idx < cand.astype(i16), one_b, zero_b))
        return jnp.where(cnt < need, cand, c)

    cut = lax.fori_loop(0, nbits, ibody, jnp.zeros((1, n_q), jnp.int32)).astype(i16)
    return jnp.maximum(gt, jnp.where(tie_idx <= cut, one_b, zero_b))


def _dsa1_body(pt_ref, *refs, layout, tq, q0, kg_n, lpad, has_tail, n_keep):
    iq_ref, misc_ref = refs[:2]
    pages = refs[2:2 + PAGES_PER_STEP]
    pos = 2 + PAGES_PER_STEP
    if has_tail:
        tail_ref = refs[pos]
        pos += 1
    mask_ref, sc_ref = refs[pos:pos + 2]
    qt = pl.program_id(1)
    kg = pl.program_id(2)
    qpos = q0 + qt * tq + lax.broadcasted_iota(jnp.int32, (tq, 1), 0)
    lane128 = lax.broadcasted_iota(jnp.int32, (1, LANES), 1)

    if kg_n == 1 and not has_tail and tq == LANES:
        kt = jnp.concatenate([_page_tile(layout, pg, 0, IDX_DIM) for pg in pages], axis=0).astype(BF16)
        misc_t = misc_ref[...].T
        kpos_c = lax.broadcasted_iota(jnp.int32, (KSTEP, 1), 0)
        qpos_r = q0 + qt * tq + lax.broadcasted_iota(jnp.int32, (1, tq), 1)
        acc = jnp.zeros((KSTEP, tq), F32)
        for h in range(IDX_HEADS):
            qi = iq_ref[:, h * IDX_DIM:(h + 1) * IDX_DIM].astype(BF16)
            acc = acc + misc_t[MISC_IW + h:MISC_IW + h + 1, :] * jnp.maximum(_dot_nt(kt, qi), 0.0)
        sel_t = _topk_mask_packed_t(jnp.where(kpos_c <= qpos_r, acc, NEG_INF), n_keep)
        for c in range(KSTEP // LANES):
            blk = sel_t[c * LANES:(c + 1) * LANES, :].astype(F32).T
            mask_ref[0, 0, :, c * LANES:(c + 1) * LANES] = blk.astype(BF16)
        return

    def scores(kt, kpos):
        acc = jnp.zeros((tq, kt.shape[0]), F32)
        for h in range(IDX_HEADS):
            qi = iq_ref[:, h * IDX_DIM:(h + 1) * IDX_DIM].astype(BF16)
            w = jnp.sum(jnp.where(lane128 == MISC_IW + h, misc_ref[...], 0.0), axis=-1, keepdims=True)
            acc = acc + w * jnp.maximum(_dot_nt(qi, kt), 0.0)
        return jnp.where(kpos <= qpos, acc, NEG_INF)

    kt = jnp.concatenate([_page_tile(layout, pg, 0, IDX_DIM) for pg in pages], axis=0).astype(BF16)
    kpos = kg * KSTEP + lax.broadcasted_iota(jnp.int32, (1, KSTEP), 1)
    sc_ref[kg] = scores(kt, kpos)

    @pl.when(kg == kg_n - 1)
    def _():
        nch = sc_ref.shape[0]
        if has_tail:
            zpad = jnp.zeros((KSTEP - SUBLANES, IDX_DIM), F32)
            tk = jnp.concatenate([tail_ref[:, 0:IDX_DIM], zpad], axis=0).astype(BF16)
            tl = lax.broadcasted_iota(jnp.int32, (1, KSTEP), 1)
            sc_ref[kg_n] = jnp.where(tl < LANES, scores(tk, q0 + tl), NEG_INF)
        sc = sc_ref[...] + 0.0
        bits = pltpu.bitcast(sc, jnp.int32)
        key = jnp.where(bits < 0, bits ^ jnp.int32(0x7FFFFFFF), bits)
        int_min = jnp.int32(-2 ** 31)

        def count(pred):
            return jnp.sum(jnp.sum(jnp.where(pred, 1.0, 0.0), axis=-1, keepdims=True), axis=0, keepdims=True)

        def vbit(i, thr):
            cand = thr + jnp.left_shift(jnp.int32(1), 31 - i)
            return jnp.where(count(key >= cand) >= n_keep, cand, thr)

        thr = lax.fori_loop(0, 32, vbit, jnp.full((1, tq, 1), int_min, jnp.int32))
        gt = key > thr
        tie = key == thr
        need = n_keep - count(gt)
        idx = (lax.broadcasted_iota(jnp.int32, (nch, 1, KSTEP), 0) * KSTEP
               + lax.broadcasted_iota(jnp.int32, (nch, 1, KSTEP), 2))
        nbits = int(np.ceil(np.log2(nch * KSTEP)))

        def ibit(i, c):
            cand = c + jnp.left_shift(jnp.int32(1), nbits - 1 - i)
            return jnp.where(count(tie & (idx < cand)) < need, cand, c)

        cut = lax.fori_loop(0, nbits, ibit, jnp.zeros((1, tq, 1), jnp.int32))
        sel = jnp.where(gt | (tie & (idx <= cut)), 1.0, 0.0).astype(BF16)
        for c in range(kg_n):
            mask_ref[0, 0, :, c * KSTEP:(c + 1) * KSTEP] = sel[c]
        if has_tail:
            mask_ref[0, 0, :, kg_n * KSTEP:lpad] = sel[kg_n][:, 0:lpad - kg_n * KSTEP]


def _dsa1(hp, pool, layout, pt, tail_src, *, nb, tq, qt_n, q0, lpad, n_keep):
    kg_n = pt.shape[1] // PAGES_PER_STEP
    has_tail = tail_src is not None
    in_specs = [pl.BlockSpec((tq, 256), lambda b, t, k, pt: (b * qt_n + t, C_IQ // 256)),
                pl.BlockSpec((tq, LANES), lambda b, t, k, pt: (b * qt_n + t, C_MISC // LANES))]
    in_specs += _page_specs(layout, 3)
    args = [hp, hp] + [pool] * PAGES_PER_STEP
    if has_tail:
        in_specs += [pl.BlockSpec((SUBLANES, LANES), lambda b, t, k, pt: (b, C_MISC // LANES))]
        args += [tail_src]
    return pl.pallas_call(
        functools.partial(_dsa1_body, layout=layout, tq=tq, q0=q0, kg_n=kg_n, lpad=lpad, has_tail=has_tail,
                          n_keep=n_keep),
        grid_spec=pltpu.PrefetchScalarGridSpec(
            num_scalar_prefetch=1, grid=(nb, qt_n, kg_n),
            in_specs=in_specs,
            out_specs=pl.BlockSpec((1, 1, tq, lpad), lambda b, t, k, pt: (b, 0, t, 0)),
            scratch_shapes=[pltpu.VMEM((kg_n + (1 if has_tail else 0), tq, KSTEP), F32)]),
        out_shape=jax.ShapeDtypeStruct((nb, 1, qt_n * tq, lpad), BF16),
        compiler_params=_cparams("arbitrary", "arbitrary", "arbitrary"),
        name="dsa_indexer_topk",
    )(pt, *args)


def _mix_body(ocmp_ref, osel_ref, owin_ref, odsa_ref, conv_ref, misc_ref, mg0_ref, mg1_ref, mg2_ref,
              wa_ref, wb_ref, wc_ref, z_ref):
    lane128 = lax.broadcasted_iota(jnp.int32, (1, LANES), 1)
    misc = misc_ref[...]

    def gate(kind, h):
        return jnp.sum(jnp.where(lane128 == MISC_NG + kind * NSA_HEADS + h, misc, 0.0), axis=-1, keepdims=True)

    parts = []
    for h in range(NSA_HEADS):
        sl = slice(h * HEAD_DIM, (h + 1) * HEAD_DIM)
        parts.append((gate(0, h) * ocmp_ref[:, sl] + gate(1, h) * osel_ref[:, sl]
                      + gate(2, h) * owin_ref[:, sl]).astype(BF16))
    o_nsa = jnp.concatenate(parts, axis=1)
    p_a = jnp.dot(o_nsa, wa_ref[...], preferred_element_type=F32)
    p_b = jnp.dot(odsa_ref[...].astype(BF16), wb_ref[...], preferred_element_type=F32)
    p_c = jnp.dot(conv_ref[...].astype(BF16), wc_ref[...], preferred_element_type=F32)
    z_ref[...] = (mg0_ref[...] * p_a + mg1_ref[...] * p_b + mg2_ref[...] * p_c).astype(BF16)


def _mix(hp, o_cmp, o_sel, o_win, o_dsa, conv_out, wa, wb, wc, tm):
    n = hp.shape[0]
    row = lambda w, cb=0: pl.BlockSpec((tm, w), lambda i: (i, cb))
    full = lambda a: pl.BlockSpec(a.shape, lambda i: (0, 0))
    return pl.pallas_call(
        _mix_body,
        grid=(n // tm,),
        in_specs=[row(1024), row(1024), row(1024), row(512), row(512), row(LANES, C_MISC // LANES),
                  row(D_MODEL, C_MG // D_MODEL), row(D_MODEL, C_MG // D_MODEL + 1), row(D_MODEL, C_MG // D_MODEL + 2),
                  full(wa), full(wb), full(wc)],
        out_specs=row(D_MODEL),
        out_shape=jax.ShapeDtypeStruct((n, D_MODEL), BF16),
        compiler_params=_cparams("arbitrary"),
        name="branch_merge",
    )(o_cmp, o_sel, o_win, o_dsa, conv_out, hp, hp, hp, hp, wa, wb, wc)


def _outln_body(z_ref, wo_ref, x_ref, g_ref, b_ref, wr_ref, rb_ref, x1_ref, gate_ref, *, tm):
    y = jnp.dot(z_ref[...], wo_ref[...], preferred_element_type=F32)
    x1 = _layernorm(ALPHA * x_ref[...] + y, g_ref[...], b_ref[...])
    x1_ref[...] = x1
    x_hi = x1.astype(BF16)
    x_lo = (x1 - x_hi.astype(F32)).astype(BF16)
    w_hi = wr_ref[...].astype(BF16)
    w_lo = (wr_ref[...] - w_hi.astype(F32)).astype(BF16)
    aff = _sigmoid(_dot_nt(w_hi, x_hi) + (_dot_nt(w_hi, x_lo) + _dot_nt(w_lo, x_hi)))
    biased = aff + rb_ref[:, 0:1]
    rows = [biased[e:e + 1, :] for e in range(N_EXPERTS)]
    best = None
    g_best = jnp.zeros((1, tm), jnp.int32)
    for g in range(N_GROUPS):
        v = rows[g * EXPERTS_PER_GROUP:(g + 1) * EXPERTS_PER_GROUP]
        score = None
        for a in range(EXPERTS_PER_GROUP):
            for c in range(a + 1, EXPERTS_PER_GROUP):
                pair = v[a] + v[c]
                score = pair if score is None else jnp.maximum(score, pair)
        if best is None:
            best = score
        else:
            better = score > best
            best = jnp.where(better, score, best)
            g_best = jnp.where(better, g, g_best)
    sel_rows = []
    for e in range(N_EXPERTS):
        g = e // EXPERTS_PER_GROUP
        rank = jnp.zeros((1, tm), F32)
        for o in range(g * EXPERTS_PER_GROUP, (g + 1) * EXPERTS_PER_GROUP):
            if o == e:
                continue
            beats = (rows[o] > rows[e]) | ((rows[o] == rows[e]) & (o < e))
            rank = rank + jnp.where(beats, 1.0, 0.0)
        sel_rows.append(jnp.where((g_best == g) & (rank < 2), aff[e:e + 1, :], 0.0))
    tot = sel_rows[0]
    for e in range(1, N_EXPERTS):
        tot = tot + sel_rows[e]
    gate_t = jnp.concatenate(sel_rows + [jnp.zeros((LANES - N_EXPERTS, tm), F32)], axis=0) / tot
    gate_ref[...] = gate_t.T


def _outln(z, wo, x, g, b, wr_t, rb, tm):
    n = z.shape[0]
    row = lambda w: pl.BlockSpec((tm, w), lambda i: (i, 0))
    full = lambda a: pl.BlockSpec(a.shape, lambda i: (0, 0))
    return pl.pallas_call(
        functools.partial(_outln_body, tm=tm),
        grid=(n // tm,),
        in_specs=[row(D_MODEL), full(wo), row(D_MODEL), full(g), full(b), full(wr_t), full(rb)],
        out_specs=[row(D_MODEL), row(LANES)],
        out_shape=[jax.ShapeDtypeStruct((n, D_MODEL), F32), jax.ShapeDtypeStruct((n, LANES), F32)],
        compiler_params=_cparams("arbitrary"),
        name="out_proj_ln_router",
    )(z, wo, x, g, b, wr_t, rb)


def _moe_body(x_ref, gate_ref, wg_ref, wu_ref, wd_ref, g_ref, b_ref, o_ref, xb_ref, acc_ref):
    e = pl.program_id(1)

    @pl.when(e == 0)
    def _():
        xb_ref[...] = x_ref[...].astype(BF16)
        acc_ref[...] = jnp.zeros(acc_ref.shape, F32)

    lane128 = lax.broadcasted_iota(jnp.int32, (1, LANES), 1)
    gcol = jnp.sum(jnp.where(lane128 == e, gate_ref[...], 0.0), axis=-1, keepdims=True)
    xb = xb_ref[...]
    hg = jnp.dot(xb, wg_ref[0], preferred_element_type=F32)
    hu = jnp.dot(xb, wu_ref[0], preferred_element_type=F32)
    h = (hg * _sigmoid(hg)) * hu * gcol
    acc_ref[...] += jnp.dot(h.astype(BF16), wd_ref[0], preferred_element_type=F32)

    @pl.when(e == N_EXPERTS - 1)
    def _():
        o_ref[...] = _layernorm(ALPHA * x_ref[...] + acc_ref[...], g_ref[...], b_ref[...])


def _moe(x1, gate, wg, wu, wd, g, b, tm):
    n = x1.shape[0]
    return pl.pallas_call(
        _moe_body,
        grid=(n // tm, N_EXPERTS),
        in_specs=[pl.BlockSpec((tm, D_MODEL), lambda i, e: (i, 0)),
                  pl.BlockSpec((tm, LANES), lambda i, e: (i, 0)),
                  pl.BlockSpec((1, D_MODEL, D_FF), lambda i, e: (e, 0, 0)),
                  pl.BlockSpec((1, D_MODEL, D_FF), lambda i, e: (e, 0, 0)),
                  pl.BlockSpec((1, D_FF, D_MODEL), lambda i, e: (e, 0, 0)),
                  pl.BlockSpec((1, D_MODEL), lambda i, e: (0, 0)),
                  pl.BlockSpec((1, D_MODEL), lambda i, e: (0, 0))],
        out_specs=pl.BlockSpec((tm, D_MODEL), lambda i, e: (i, 0)),
        out_shape=jax.ShapeDtypeStruct((n, D_MODEL), F32),
        scratch_shapes=[pltpu.VMEM((tm, D_MODEL), BF16), pltpu.VMEM((tm, D_MODEL), F32)],
        compiler_params=_cparams("arbitrary", "arbitrary"),
        name="moe_ln",
    )(x1, gate, wg, wu, wd, g, b)


def _overlap_matrix(ncp, nselp):
    cs = np.arange(ncp)[:, None] * CMP_STRIDE
    ss = np.arange(nselp)[None, :] * SEL_BLOCK
    return jnp.asarray(((cs < ss + SEL_BLOCK) & (cs + CMP_LEN > ss)).astype(np.float32))


class _Group:
    def __init__(self, nb, t_real, t_pad, q0, past_len, tm, tq):
        self.nb, self.t_real, self.t_pad, self.q0, self.past_len, self.tm, self.tq = nb, t_real, t_pad, q0, past_len, tm, tq
        self.paged = past_len > 0
        self.lp = past_len if self.paged else t_pad
        self.ltot = self.lp + (t_real if self.paged else 0)
        self.lpad = self.lp + (LANES if self.paged else 0)
        self.qt_n = t_pad // tq
        n_chunks = -(-self.ltot // CMP_STRIDE)
        self.n_cmp = n_chunks - CMP_LEN // CMP_STRIDE + 1
        self.n_sel = -(-self.ltot // SEL_BLOCK)
        self.nselp = -(-self.n_sel // LANES) * LANES
        self.n_keep = min(DSA_TOPK, self.ltot // 4)


def _mixer(gp, x, lw, caches):
    hp = _proj(x.astype(BF16), lw["w_in"], gp.tabs, lw["kinds"], min(2 * gp.tm, gp.nb * gp.t_pad, gp.tabs.shape[1]))
    nb, tq, qt_n, q0 = gp.nb, gp.tq, gp.qt_n, gp.q0
    if gp.paged:
        l, nsa_pool, dsa_pool, kidx_pool, win_src, s_conv, pt = caches
        cmp_lay = _PageLayout("cache", 0, 0, base=0, layer=l, nslot=2)
        slc_lay = _PageLayout("cache", 0, 1, base=0, layer=l, nslot=2)
        dsa_lay = _PageLayout("cache", 0, 0, base=0, layer=l, nslot=2)
        kidx_lay = _PageLayout("cache4", IDX_DIM, 0, base=0, layer=l, nslot=0)
        tail = hp
        nwb, win_k0, win_layer = WINDOW // PAGE, PAST_LEN - WINDOW, l
        past8 = jnp.concatenate([jnp.zeros((nb, SUBLANES - 2, CONV_DIM), F32), s_conv[l]], axis=1)
    else:
        pt = gp.pt
        nsa_pool = dsa_pool = kidx_pool = hp.reshape(nb * gp.t_pad // PAGE, PAGE, NCOL)
        cmp_lay = _PageLayout("cols", 512, C_CMP // 512, base=0, layer=0, nslot=0)
        slc_lay = _PageLayout("cols", 512, C_SLC // 512, base=0, layer=0, nslot=0)
        dsa_lay = _PageLayout("cols", 512, C_DSA // 512, base=0, layer=0, nslot=0)
        kidx_lay = _PageLayout("cols", LANES, C_MISC // LANES, base=0, layer=0, nslot=0)
        tail = None
        win_src = hp
        nwb, win_k0, win_layer = WINDOW // PAGE + 1, 0, 0
        past8 = jnp.zeros((nb, SUBLANES, CONV_DIM), F32)

    conv_out, cu = _conv(hp, past8, lw["conv_w8"], min(gp.tm, gp.t_pad), gp.t_pad // min(gp.tm, gp.t_pad))

    ab = _cmp1(nsa_pool, cmp_lay, pt, lw["wcat"], lw["pe"])
    cmp = _cmp2(ab, tail, lw["pe"], lw["w1r"], lw["phi_w2"], gp.t_real if gp.paged else 0, C_CMP // 512)
    o_cmp, o_win, kmask = _nsa1(hp, cmp, gp.ov, win_src, tail, nb=nb, tq=tq, qt_n=qt_n, q0=q0, n_cmp=gp.n_cmp,
                                n_sel=gp.n_sel, lpad=gp.lpad, nwb=nwb, win_k0=win_k0, win_layer=win_layer)
    o_sel = _mattn(hp, C_QROT, NSA_HEADS, kmask, nsa_pool, slc_lay, pt, tail, C_SLC // 512,
                   nb=nb, tq=tq, qt_n=qt_n, q0=q0, name="nsa_selected_attn")
    dmask = _dsa1(hp, kidx_pool, kidx_lay, pt, tail, nb=nb, tq=tq, qt_n=qt_n, q0=q0, lpad=gp.lpad,
                  n_keep=gp.n_keep)
    o_dsa = _mattn(hp, C_DQ, DSA_HEADS, dmask, dsa_pool, dsa_lay, pt, tail, C_DSA // 512,
                   nb=nb, tq=tq, qt_n=qt_n, q0=q0, name="dsa_topk_attn")
    z = _mix(hp, o_cmp, o_sel, o_win, o_dsa, conv_out, lw["w_a"], lw["w_b"], lw["w_c"], min(gp.tm, 256))
    return z, hp, cu


def _layer(gp, x, lw, caches):
    z, hp, cu = _mixer(gp, x, lw, caches)
    tm2 = min(gp.tm, 256)
    x1, gate = _outln(z, lw["w_o"], x, lw["ln_mix_g"], lw["ln_mix_b"], lw["wr_t"], lw["rb"], tm2)
    x2 = _moe(x1, gate, lw["w_eg"], lw["w_eu"], lw["w_ed"], lw["ln_ffn_g"], lw["ln_ffn_b"], min(gp.tm, 512))
    return x2, hp, cu


def _layer_weights(l, w_in, nsa_phi_pos, nsa_phi_w1, nsa_phi_w2, conv_w, w_br_a, w_br_b, w_br_c, w_out,
                   ln_mix_g, ln_mix_b, ln_ffn_g, ln_ffn_b, w_router, router_bias, w_e_gate, w_e_up, w_e_down):
    w1r = nsa_phi_w1[l].reshape(2, CMP_LEN, HEAD_DIM, HEAD_DIM)
    wcat = jnp.concatenate([w1r[:, :CMP_STRIDE], w1r[:, CMP_STRIDE:]], axis=-1).astype(BF16)
    return dict(
        w_in=_permute_w_in(w_in, l), kinds=jnp.asarray(_col_kinds()),
        w1r=w1r, wcat=wcat, pe=nsa_phi_pos[l], phi_w2=nsa_phi_w2[l],
        conv_w8=jnp.concatenate([conv_w[l], jnp.zeros((SUBLANES - 3, CONV_DIM), F32)], axis=0),
        w_a=w_br_a[l].astype(BF16), w_b=w_br_b[l].astype(BF16), w_c=w_br_c[l].astype(BF16),
        w_o=w_out[l].astype(BF16),
        ln_mix_g=ln_mix_g[l][None], ln_mix_b=ln_mix_b[l][None],
        ln_ffn_g=ln_ffn_g[l][None], ln_ffn_b=ln_ffn_b[l][None],
        wr_t=w_router.T, rb=jnp.broadcast_to(router_bias[:, None], (N_EXPERTS, LANES)),
        w_eg=w_e_gate[l].astype(BF16), w_eu=w_e_up[l].astype(BF16), w_ed=w_e_down[l].astype(BF16))


def kernel(x_prompt, x_sample, cache_nsa_kv, cache_dsa_kv, cache_dsa_kidx, state_nsa_win, state_conv, page_table,
           w_in, nsa_phi_pos, nsa_phi_w1, nsa_phi_w2, conv_w, w_br_a, w_br_b, w_br_c, w_out, ln_mix_g, ln_mix_b,
           ln_ffn_g, ln_ffn_b, w_router, router_bias, w_e_gate, w_e_up, w_e_down):
    bp, tp, _ = x_prompt.shape
    bs, ts, _ = x_sample.shape
    ts_pad = SUBLANES

    gp_p = _Group(bp, tp, tp, 0, 0, tm=512, tq=128)
    gp_p.tabs = _rope_tables(jnp.arange(tp, dtype=jnp.int32))
    gp_p.pt = jnp.arange(bp * tp // PAGE, dtype=jnp.int32).reshape(bp, tp // PAGE)
    gp_p.ov = _overlap_matrix(LANES, gp_p.nselp)
    gp_s = _Group(bs, ts, ts_pad, PAST_LEN, PAST_LEN, tm=bs * ts_pad, tq=ts_pad)
    pos_s = PAST_LEN + jnp.arange(ts_pad, dtype=jnp.int32)
    gp_s.tabs = jnp.tile(_rope_tables(pos_s), (1, bs, 1))
    gp_s.ov = _overlap_matrix(PAST_LEN // CMP_STRIDE, gp_s.nselp)

    xp = x_prompt.reshape(bp * tp, D_MODEL)
    xs = jnp.concatenate([x_sample, jnp.zeros((bs, ts_pad - ts, D_MODEL), F32)], axis=1).reshape(bs * ts_pad, D_MODEL)

    outs_p = [[] for _ in range(5)]
    outs_s = [[] for _ in range(5)]
    for l in range(DEPTH):
        lw = _layer_weights(l, w_in, nsa_phi_pos, nsa_phi_w1, nsa_phi_w2, conv_w, w_br_a, w_br_b, w_br_c, w_out,
                            ln_mix_g, ln_mix_b, ln_ffn_g, ln_ffn_b, w_router, router_bias, w_e_gate, w_e_up, w_e_down)
        xp, hp_p, cu_p = _layer(gp_p, xp, lw, None)
        xs, hp_s, cu_s = _layer(gp_s, xs, lw, (l, cache_nsa_kv, cache_dsa_kv, cache_dsa_kidx,
                                                 state_nsa_win, state_conv, page_table))
        h3 = hp_p.reshape(bp, tp, NCOL)
        outs_p[0].append(h3[:, :, C_CMP:C_CMP + 1024].reshape(bp, tp, 4, NSA_KV, HEAD_DIM))
        outs_p[1].append(h3[:, :, C_DSA:C_DSA + 512].reshape(bp, tp, 2, DSA_KV, HEAD_DIM))
        outs_p[2].append(h3[:, :, C_MISC:C_MISC + IDX_DIM])
        outs_p[3].append(h3[:, tp - min(WINDOW, tp):, C_WIN:C_WIN + 512].reshape(bp, min(WINDOW, tp), 2, NSA_KV, HEAD_DIM))
        outs_p[4].append(cu_p.reshape(bp, tp, CONV_DIM)[:, tp - 2:])
        s3 = hp_s.reshape(bs, ts_pad, NCOL)[:, :ts]
        outs_s[0].append(s3[:, :, C_CMP:C_CMP + 1024].reshape(bs, ts, 4, NSA_KV, HEAD_DIM))
        outs_s[1].append(s3[:, :, C_DSA:C_DSA + 512].reshape(bs, ts, 2, DSA_KV, HEAD_DIM))
        outs_s[2].append(s3[:, :, C_MISC:C_MISC + IDX_DIM])
        win_new = s3[:, :, C_WIN:C_WIN + 512].reshape(bs, ts, 2, NSA_KV, HEAD_DIM)
        wb = state_nsa_win.shape[2]
        outs_s[3].append(jnp.concatenate([state_nsa_win[l], win_new], axis=1)[:, -wb:])
        ext = jnp.concatenate([state_conv[l], cu_s.reshape(bs, ts_pad, CONV_DIM)[:, :ts]], axis=1)
        outs_s[4].append(ext[:, -2:])
    sp = [jnp.stack(a, axis=0) for a in outs_p]
    ss = [jnp.stack(a, axis=0) for a in outs_s]
    y_p = xp.reshape(bp, tp, D_MODEL)
    y_s = xs.reshape(bs, ts_pad, D_MODEL)[:, :ts]
    return (y_p, y_s, sp[0], ss[0], sp[1], ss[1], sp[2], ss[2], sp[3], ss[3], sp[4], ss[4])
```

```python
import collections
import functools

import numpy as np
import jax
import jax.numpy as jnp
from jax import lax
from jax.experimental import pallas as pl
from jax.experimental.pallas import tpu as pltpu

F32 = jnp.float32
BF16 = jnp.bfloat16
HIGHEST = lax.Precision.HIGHEST
NEG_INF = float("-inf")

D_MODEL = 2048
DEPTH = 2
PAST_LEN = 16384
PAGE = 128
HEAD_DIM = 128
ROPE_THETA = 500000.0
NSA_HEADS = 8
NSA_KV = 2
CMP_LEN = 32
CMP_STRIDE = 16
SEL_BLOCK = 64
SEL_TOP = 16
WINDOW = 512
FORCE_SCORE = 1e4
DSA_HEADS = 4
DSA_KV = 2
IDX_HEADS = 4
IDX_DIM = 64
DSA_TOPK = 256
CONV_DIM = 512
N_EXPERTS = 16
N_GROUPS = 4
EXPERTS_PER_GROUP = 4
D_FF = 512
LN_EPS = 1e-5
ALPHA = (2 * DEPTH) ** 0.25
IN_WIDTHS = (1024, 1536, 24, 512, 512, 256, 64, 4, 1536, 6144)
ATT_SCALE = HEAD_DIM ** -0.5

LANES = 128
SUBLANES = 8
VMEM_LIMIT = 56 * 1024 * 1024

C_QRAW = 0
C_QROT = 1024
C_CMP = 2048
C_SLC = 2560
C_WIN = 3072
C_DQ = 3584
C_DSA = 4096
C_CV = 4608
C_MG = 6144
C_IQ = 12288
C_MISC = 12544
NCOL = 12800
MISC_IW = 64
MISC_NG = 68
PROJ_TN = 1280
PAGES_PER_STEP = 16
KSTEP = PAGES_PER_STEP * PAGE
CAUSAL_CHUNKS = 4

K_PLAIN, K_ROPE128, K_ROPE64, K_SIGMOID, K_MISC = 0, 1, 2, 3, 4


def _col_kinds():
    kinds = np.zeros(NCOL // LANES, np.int32)

    def mark(c0, n, k):
        kinds[c0 // LANES:(c0 + n) // LANES] = k

    mark(C_QROT, 1024, K_ROPE128)
    mark(C_SLC, 256, K_ROPE128)
    mark(C_WIN, 256, K_ROPE128)
    mark(C_DQ, 512, K_ROPE128)
    mark(C_DSA, 256, K_ROPE128)
    mark(C_IQ, 256, K_ROPE64)
    mark(C_MISC, 128, K_MISC)
    mark(C_MG, 6144, K_SIGMOID)
    return kinds


def _cparams(*sem):
    return pltpu.CompilerParams(dimension_semantics=sem, vmem_limit_bytes=VMEM_LIMIT)


def _sigmoid(x):
    return 1.0 / (1.0 + jnp.exp(-x))


def _layernorm(x, g, b):
    mu = jnp.mean(x, axis=-1, keepdims=True)
    xc = x - mu
    var = jnp.mean(xc * xc, axis=-1, keepdims=True)
    return xc * lax.rsqrt(var + LN_EPS) * g + b


def _dot_nt(a, b, precision=None):
    return lax.dot_general(a, b, (((1,), (1,)), ((), ())), preferred_element_type=F32, precision=precision)


def _softmax_parts(s):
    m = jnp.max(s, axis=-1, keepdims=True)
    m = jnp.where(m == NEG_INF, 0.0, m)
    p = jnp.exp(s - m)
    return p, jnp.maximum(jnp.sum(p, axis=-1, keepdims=True), 1e-30)


def _stack_heads(q_ref, g, r_per_g):
    return jnp.concatenate([q_ref[:, (g * r_per_g + r) * HEAD_DIM:(g * r_per_g + r + 1) * HEAD_DIM]
                            for r in range(r_per_g)], axis=0).astype(BF16)


def _proj_body(kinds_ref, needs_ref, x_ref, wt_ref, tab_ref, o_ref, w_ref):
    nsub = PROJ_TN // LANES
    j = pl.program_id(0)

    @pl.when(pl.program_id(1) == 0)
    def _():
        for s in range(nsub):
            w_ref[:, s * LANES:(s + 1) * LANES] = wt_ref[s * LANES:(s + 1) * LANES, :].astype(F32).T.astype(BF16)

    h = jnp.dot(x_ref[...], w_ref[...], preferred_element_type=F32)
    lane = lax.broadcasted_iota(jnp.int32, (1, LANES), 1)

    def rope(hs, t0, sh):
        return (hs * tab_ref[t0] + pltpu.roll(hs, sh, 1) * tab_ref[t0 + 1]
                + pltpu.roll(hs, LANES - sh, 1) * tab_ref[t0 + 2])

    for s in range(nsub):
        kind = kinds_ref[j * nsub + s]
        hs = h[:, s * LANES:(s + 1) * LANES]
        sl = slice(s * LANES, (s + 1) * LANES)

        @pl.when(kind == K_PLAIN)
        def _():
            o_ref[:, sl] = hs

        @pl.when(kind == K_ROPE128)
        def _():
            o_ref[:, sl] = rope(hs, 0, 16)

        @pl.when(kind == K_ROPE64)
        def _():
            o_ref[:, sl] = rope(hs, 3, 8)

        @pl.when(kind == K_SIGMOID)
        def _():
            o_ref[:, sl] = _sigmoid(hs)

        @pl.when(kind == K_MISC)
        def _():
            r = rope(hs, 3, 8)
            o_ref[:, sl] = jnp.where(lane < MISC_IW, r,
                                     jnp.where(lane < MISC_NG, hs * (IDX_HEADS ** -0.5),
                                               jnp.where(lane < MISC_NG + 24, _sigmoid(hs), hs)))


def _proj(x_bf, wt_bf, tabs, kinds, tm):
    n = x_bf.shape[0]
    n_tab = tabs.shape[1] // tm
    grid = (NCOL // PROJ_TN, n // tm)
    rotary = np.isin(_col_kinds().reshape(-1, PROJ_TN // LANES), (K_ROPE128, K_ROPE64, K_MISC)).any(axis=1)
    needs = jnp.asarray(rotary.astype(np.int32))
    return pl.pallas_call(
        _proj_body,
        grid_spec=pltpu.PrefetchScalarGridSpec(
            num_scalar_prefetch=2, grid=grid,
            in_specs=[pl.BlockSpec((tm, D_MODEL), lambda j, i, k, nd: (i, 0)),
                      pl.BlockSpec((PROJ_TN, D_MODEL), lambda j, i, k, nd: (j, 0)),
                      pl.BlockSpec((6, tm, LANES), lambda j, i, k, nd: (0, (i % n_tab) * nd[j], 0))],
            out_specs=pl.BlockSpec((tm, PROJ_TN), lambda j, i, k, nd: (i, j)),
            scratch_shapes=[pltpu.VMEM((D_MODEL, PROJ_TN), BF16)]),
        out_shape=jax.ShapeDtypeStruct((n, NCOL), F32),
        compiler_params=_cparams("arbitrary", "arbitrary"),
        name="in_proj",
    )(kinds, needs, x_bf, wt_bf, tabs)


def _rope_tables(pos):
    out = []
    lane = jnp.arange(LANES)
    for d in (HEAD_DIM, IDX_DIM):
        rot = d // 4
        half = rot // 2
        inv = ROPE_THETA ** (-jnp.arange(half, dtype=F32) / half)
        ang = pos.astype(F32)[:, None] * inv[None, :]
        cos = jnp.cos(ang)
        sin = jnp.sin(ang)
        li = lane % d
        ci = jnp.take(cos, li % half, axis=1)
        si = jnp.take(sin, li % half, axis=1)
        out.append(jnp.where(li[None] < rot, ci, 1.0))
        out.append(jnp.where((li[None] >= half) & (li[None] < rot), si, 0.0))
        out.append(jnp.where(li[None] < half, -si, 0.0))
    return jnp.stack(out, axis=0)


def _permute_w_in(w_in, l):
    wt = jnp.transpose(w_in, (2, 0, 1))[:, l, :].astype(BF16)
    offs = np.cumsum((0,) + IN_WIDTHS)
    nq, nkv, ng, dq, dkv, iq, ik, iw, cv, mg = [wt[offs[i]:offs[i + 1]] for i in range(10)]
    z = lambda n: jnp.zeros((n, wt.shape[1]), BF16)
    rows = [nq, nq, nkv, dq, dkv, cv, mg, iq, ik, iw, ng, z(LANES - 92), z(NCOL - C_MISC - LANES)]
    return jnp.concatenate(rows, axis=0)


def _conv_body(cv_ref, prev_ref, past_ref, w_ref, y_ref, cu_ref, s_ref, *, tiles_per_seq, tm):
    i = pl.program_id(0)
    b = cv_ref[:, 0:CONV_DIM]
    cu = cv_ref[:, CONV_DIM:2 * CONV_DIM] * cv_ref[:, 2 * CONV_DIM:3 * CONV_DIM]
    first = (i % tiles_per_seq) == 0
    prev = prev_ref[:, CONV_DIM:2 * CONV_DIM] * prev_ref[:, 2 * CONV_DIM:3 * CONV_DIM]
    s_ref[0:SUBLANES, :] = jnp.where(first, past_ref[0], prev)
    s_ref[SUBLANES:SUBLANES + tm, :] = cu
    y = (w_ref[0:1, :] * s_ref[pl.ds(SUBLANES - 2, tm), :] + w_ref[1:2, :] * s_ref[pl.ds(SUBLANES - 1, tm), :]
         + w_ref[2:3, :] * cu)
    y_ref[...] = b * y
    cu_ref[...] = cu


def _conv(hp, past8, conv_w8, tm, tiles_per_seq):
    n = hp.shape[0]
    cvb = C_CV // (3 * CONV_DIM)
    rb = tm // SUBLANES
    return pl.pallas_call(
        functools.partial(_conv_body, tiles_per_seq=tiles_per_seq, tm=tm),
        grid=(n // tm,),
        in_specs=[pl.BlockSpec((tm, 3 * CONV_DIM), lambda i: (i, cvb)),
                  pl.BlockSpec((SUBLANES, 3 * CONV_DIM), lambda i: (jnp.maximum(i * rb - 1, 0), cvb)),
                  pl.BlockSpec((1, SUBLANES, CONV_DIM), lambda i: (i // tiles_per_seq, 0, 0)),
                  pl.BlockSpec((SUBLANES, CONV_DIM), lambda i: (0, 0))],
        out_specs=[pl.BlockSpec((tm, CONV_DIM), lambda i: (i, 0)),
                   pl.BlockSpec((tm, CONV_DIM), lambda i: (i, 0))],
        out_shape=[jax.ShapeDtypeStruct((n, CONV_DIM), F32), jax.ShapeDtypeStruct((n, CONV_DIM), F32)],
        scratch_shapes=[pltpu.VMEM((tm + SUBLANES, CONV_DIM), F32)],
        compiler_params=_cparams("arbitrary"),
        name="short_conv",
    )(hp, hp, past8, conv_w8)


_PageLayout = collections.namedtuple("_PageLayout", "kind width col_block base layer nslot")


def _page_specs(layout, n_lead, kg_n=None):
    def spec(k):
        def imap(*a):
            ids, pt = a[:n_lead], a[-1]
            kg = ids[-1] % kg_n if kg_n else ids[-1]
            page = pt[ids[0], kg * PAGES_PER_STEP + k]
            if layout.kind == "cache":
                return (layout.layer, page, 0, layout.col_block, 0, 0)
            if layout.kind == "cache4t":
                return (layout.layer, page, 0, 0)
            cb = layout.col_block(*ids) if callable(layout.col_block) else layout.col_block
            return (page, 0, cb)
        shape = {"cache": (1, 1, PAGE, layout.nslot, 2, HEAD_DIM), "cache4t": (1, 1, layout.width, PAGE),
                 "cols": (1, PAGE, layout.width)}[layout.kind]
        return pl.BlockSpec(shape, imap)
    return [spec(k) for k in range(PAGES_PER_STEP)]


def _page_tile(layout, ref, j, width=LANES):
    if layout.kind == "cache":
        slot, g = divmod(layout.base + j, 2)
        return ref[0, 0, :, slot, g, :]
    return ref[0, :, j * width:(j + 1) * width]


def _chunk_rows(layout, ref, j, p):
    if layout.kind == "cache":
        slot, g = divmod(layout.base + j, 2)
        return ref[0, 0, pl.ds(p, SUBLANES, stride=CMP_STRIDE), slot, g, :]
    return ref[0, pl.ds(p, SUBLANES, stride=CMP_STRIDE), :]


def _cmp1_body(pt_ref, *refs, layout, n_inner):
    pages = refs[:PAGES_PER_STEP]
    w_ref, pe_ref = refs[PAGES_PER_STEP:PAGES_PER_STEP + 2]
    o_ref = refs[PAGES_PER_STEP + 2]
    for j in range(n_inner):
        slot = j // 2
        acc_a = jnp.zeros((PAGE, HEAD_DIM), F32)
        acc_b = jnp.zeros((PAGE, HEAD_DIM), F32)
        for p in range(CMP_STRIDE):
            xp = jnp.concatenate([_chunk_rows(layout, pg, j, p) for pg in pages], axis=0)
            xa = (xp + pe_ref[slot, p:p + 1, :]).astype(BF16)
            xb = (xp + pe_ref[slot, CMP_STRIDE + p:CMP_STRIDE + p + 1, :]).astype(BF16)
            acc_a = acc_a + jnp.dot(xa, w_ref[slot, p, :, 0:HEAD_DIM], preferred_element_type=F32)
            acc_b = acc_b + jnp.dot(xb, w_ref[slot, p, :, HEAD_DIM:2 * HEAD_DIM], preferred_element_type=F32)
        o_ref[0, :, j * 256:j * 256 + HEAD_DIM] = acc_a
        o_ref[0, :, j * 256 + HEAD_DIM:(j + 1) * 256] = acc_b


def _cmp1(pool, layout, pt, wcat, pe):
    nb, npg = pt.shape
    kg = npg // PAGES_PER_STEP
    if layout.kind == "cache":
        n_inner, n_sg, wsel, osel, ow = 4, 1, (lambda sg: 0), (lambda sg: 0), 1024
        wblk = 2
    else:
        cb = layout.col_block
        layout = layout._replace(width=HEAD_DIM, col_block=lambda b, sg, k: cb * 4 + sg)
        n_inner, n_sg, wsel, osel, ow = 1, 4, (lambda sg: sg // 2), (lambda sg: sg), 256
        wblk = 1
    return pl.pallas_call(
        functools.partial(_cmp1_body, layout=layout, n_inner=n_inner),
        grid_spec=pltpu.PrefetchScalarGridSpec(
            num_scalar_prefetch=1, grid=(nb, n_sg, kg),
            in_specs=_page_specs(layout, 3)
            + [pl.BlockSpec((wblk, 16, HEAD_DIM, 256), lambda b, sg, k, pt: (wsel(sg), 0, 0, 0)),
               pl.BlockSpec((wblk, CMP_LEN, HEAD_DIM), lambda b, sg, k, pt: (wsel(sg), 0, 0))],
            out_specs=pl.BlockSpec((1, PAGE, ow), lambda b, sg, k, pt: (b, k, osel(sg)))),
        out_shape=jax.ShapeDtypeStruct((nb, npg * SUBLANES, 1024), F32),
        compiler_params=_cparams("arbitrary", "arbitrary", "arbitrary"),
        name="nsa_compress1",
    )(pt, *([pool] * PAGES_PER_STEP), wcat, pe)


def _gelu_tanh(x):
    return 0.5 * x * (1.0 + jnp.tanh(np.sqrt(2.0 / np.pi).astype(np.float32) * (x + 0.044715 * (x * x * x))))


def _cmp2_body(ab_ref, tail_ref, pe_ref, w1_ref, w2_ref, o_ref, s_ref, *, nc, n_tail):
    row8 = lax.broadcasted_iota(jnp.int32, (SUBLANES, 1), 0)
    row16 = lax.broadcasted_iota(jnp.int32, (CMP_STRIDE, 1), 0)
    for sg in range(4):
        slot = sg // 2
        a = ab_ref[0, :, sg * 256:sg * 256 + HEAD_DIM]
        s_ref[0:nc, :] = ab_ref[0, :, sg * 256 + HEAD_DIM:(sg + 1) * 256]
        tb = jnp.zeros((SUBLANES, HEAD_DIM), F32)
        if n_tail:
            x8 = jnp.where(row8 < n_tail, tail_ref[:, sg * HEAD_DIM:(sg + 1) * HEAD_DIM], 0.0)
            x16 = jnp.concatenate([x8, jnp.zeros((CMP_STRIDE - SUBLANES, HEAD_DIM), F32)], axis=0)
            x16 = x16 + pe_ref[slot, CMP_STRIDE:CMP_LEN, :]
            t16 = jnp.zeros((CMP_STRIDE, HEAD_DIM), F32)
            for p in range(CMP_STRIDE):
                xm = jnp.where(row16 == p, x16, 0.0).astype(BF16)
                t16 = t16 + jnp.dot(xm, w1_ref[slot, CMP_STRIDE + p].astype(BF16), preferred_element_type=F32)
            tb = jnp.sum(t16, axis=0, keepdims=True) * jnp.where(row8 == 0, 1.0, 0.0)
        s_ref[nc:nc + SUBLANES, :] = tb
        pre = a + s_ref[pl.ds(1, nc), :]
        o_ref[0, :, sg * HEAD_DIM:(sg + 1) * HEAD_DIM] = jnp.dot(
            _gelu_tanh(pre).astype(BF16), w2_ref[slot].astype(BF16), preferred_element_type=F32)


def _cmp2(ab, tail, pe, w1r, w2, n_tail, tail_col_block):
    nb, nc, _ = ab.shape
    if tail is None:
        tail = jnp.zeros((nb * SUBLANES, 512), F32)
        tail_col_block = 0
    return pl.pallas_call(
        functools.partial(_cmp2_body, nc=nc, n_tail=n_tail),
        grid=(nb,),
        in_specs=[pl.BlockSpec((1, nc, 1024), lambda b: (b, 0, 0)),
                  pl.BlockSpec((SUBLANES, 512), lambda b: (b, tail_col_block)),
                  pl.BlockSpec((2, CMP_LEN, HEAD_DIM), lambda b: (0, 0, 0)),
                  pl.BlockSpec((2, CMP_LEN, HEAD_DIM, HEAD_DIM), lambda b: (0, 0, 0, 0)),
                  pl.BlockSpec((2, HEAD_DIM, HEAD_DIM), lambda b: (0, 0, 0))],
        out_specs=pl.BlockSpec((1, nc, 512), lambda b: (b, 0, 0)),
        out_shape=jax.ShapeDtypeStruct((nb, nc, 512), F32),
        scratch_shapes=[pltpu.VMEM((nc + SUBLANES, HEAD_DIM), F32)],
        compiler_params=_cparams("arbitrary"),
        name="nsa_compress2",
    )(ab, tail, pe, w1r, w2)


def _nsa1_body(*refs, tq, q0, n_cmp, ncp, n_sel, nselp, lpad, nwb, has_tail, win_k0):
    qraw_ref, qrot_ref, cmp_ref, ov_ref = refs[:4]
    wins = refs[4:4 + nwb]
    pos = 4 + nwb
    tail_ref = None
    if has_tail:
        tail_ref = refs[pos]
        pos += 1
    ocmp_ref, owin_ref, kmask_ref = refs[pos:pos + 3]
    qt = pl.program_id(1)
    r_per_g = NSA_HEADS // NSA_KV
    qpos = q0 + qt * tq + lax.broadcasted_iota(jnp.int32, (tq, 1), 0)

    jj = lax.broadcasted_iota(jnp.int32, (1, ncp), 1)
    cmask = (jj * CMP_STRIDE + (CMP_LEN - 1) <= qpos) & (jj < n_cmp)
    blk = lax.broadcasted_iota(jnp.int32, (1, nselp), 1)
    cur = qpos // SEL_BLOCK
    forced = (blk == 0) | (blk == cur) | (blk == cur - 1)
    e_row = lax.broadcasted_iota(jnp.int32, (KSTEP // SEL_BLOCK, KSTEP), 0)
    e_col = lax.broadcasted_iota(jnp.int32, (KSTEP // SEL_BLOCK, KSTEP), 1)
    expand = jnp.where(e_col // SEL_BLOCK == e_row, 1.0, 0.0).astype(BF16)
    lane128 = lax.broadcasted_iota(jnp.int32, (1, LANES), 1)
    for g in range(NSA_KV):
        kc = cmp_ref[0, :, g * HEAD_DIM:(g + 1) * HEAD_DIM].astype(BF16)
        vc = cmp_ref[0, :, (2 + g) * HEAD_DIM:(3 + g) * HEAD_DIM].astype(BF16)
        q = _stack_heads(qraw_ref, g, r_per_g)
        s = _dot_nt(q, kc) * ATT_SCALE + jnp.concatenate([jnp.where(cmask, 0.0, NEG_INF)] * r_per_g, axis=0)
        p, den = _softmax_parts(s)
        p = p / den
        o = jnp.dot(p.astype(BF16), vc, preferred_element_type=F32)
        psum = jnp.zeros((tq, ncp), F32)
        for r in range(r_per_g):
            h = g * r_per_g + r
            psum = psum + p[r * tq:(r + 1) * tq]
            ocmp_ref[:, h * HEAD_DIM:(h + 1) * HEAD_DIM] = o[r * tq:(r + 1) * tq]
        imp = jnp.dot(psum.astype(BF16), ov_ref[...].astype(BF16), preferred_element_type=F32)
        imp = jnp.where(forced, imp + FORCE_SCORE, imp)
        imp = jnp.where((blk <= cur) & (blk < n_sel), imp, NEG_INF)
        rank = jnp.zeros((tq, nselp), F32)
        for i in range(n_sel):
            vi = imp[:, i:i + 1]
            rank = rank + jnp.where(vi > imp, 1.0, jnp.where(vi == imp, jnp.where(blk > i, 1.0, 0.0), 0.0))
        sel = jnp.where((rank < min(SEL_TOP, n_sel)) & (blk < n_sel), 1.0, 0.0).astype(BF16)
        per = KSTEP // SEL_BLOCK
        for c in range(lpad // KSTEP):
            km = jnp.dot(sel[:, c * per:(c + 1) * per], expand, preferred_element_type=F32)
            kmask_ref[0, g, :, c * KSTEP:(c + 1) * KSTEP] = km.astype(BF16)
        if lpad % KSTEP:
            b0 = (lpad // KSTEP) * per
            km = jnp.where(lane128 < SEL_BLOCK, sel[:, b0:b0 + 1].astype(F32), 0.0)
            kmask_ref[0, g, :, (lpad // KSTEP) * KSTEP:lpad] = jnp.broadcast_to(km, (tq, LANES)).astype(BF16)

    nk = nwb * PAGE + (LANES if has_tail else 0)
    kk = lax.broadcasted_iota(jnp.int32, (1, nk), 1)
    if has_tail:
        kpos = jnp.where(kk < nwb * PAGE, win_k0 + kk, q0 + kk - nwb * PAGE)
    else:
        kpos = (qt - (nwb - 1)) * PAGE + kk
    rel = qpos - kpos
    wmask = (rel >= 0) & (rel < WINDOW) & (kpos >= 0)
    for g in range(NSA_KV):
        if has_tail:
            kparts = [w[0, 0, :, 0, g, :] for w in wins]
            vparts = [w[0, 0, :, 1, g, :] for w in wins]
        else:
            kparts = [w[:, g * HEAD_DIM:(g + 1) * HEAD_DIM] for w in wins]
            vparts = [w[:, (2 + g) * HEAD_DIM:(3 + g) * HEAD_DIM] for w in wins]
        if has_tail:
            zpad = jnp.zeros((LANES - SUBLANES, HEAD_DIM), F32)
            kparts += [tail_ref[:, g * HEAD_DIM:(g + 1) * HEAD_DIM], zpad]
            vparts += [tail_ref[:, (2 + g) * HEAD_DIM:(3 + g) * HEAD_DIM], zpad]
        kw = jnp.concatenate(kparts, axis=0).astype(BF16)
        vw = jnp.concatenate(vparts, axis=0).astype(BF16)
        q = _stack_heads(qrot_ref, g, r_per_g)
        s = _dot_nt(q, kw) * ATT_SCALE + jnp.concatenate([jnp.where(wmask, 0.0, NEG_INF)] * r_per_g, axis=0)
        p, den = _softmax_parts(s)
        o = jnp.dot((p / den).astype(BF16), vw, preferred_element_type=F32)
        for r in range(r_per_g):
            h = g * r_per_g + r
            owin_ref[:, h * HEAD_DIM:(h + 1) * HEAD_DIM] = o[r * tq:(r + 1) * tq]


def _nsa1(hp, cmp, ov, win_src, tail_src, *, nb, tq, qt_n, q0, n_cmp, n_sel, lpad, nwb, win_k0, win_layer):
    n = hp.shape[0]
    ncp = cmp.shape[1]
    nselp = ov.shape[1]
    has_tail = tail_src is not None
    rows = lambda b, t: b * qt_n + t
    in_specs = [pl.BlockSpec((tq, 1024), lambda b, t: (rows(b, t), C_QRAW // 1024)),
                pl.BlockSpec((tq, 1024), lambda b, t: (rows(b, t), C_QROT // 1024)),
                pl.BlockSpec((1, ncp, 512), lambda b, t: (b, 0, 0)),
                pl.BlockSpec((ncp, nselp), lambda b, t: (0, 0))]
    if has_tail:
        in_specs += [pl.BlockSpec((1, 1, PAGE, 2, NSA_KV, HEAD_DIM), lambda b, t, k=k: (win_layer, b, k, 0, 0, 0))
                     for k in range(nwb)]
        in_specs += [pl.BlockSpec((SUBLANES, 512), lambda b, t: (b, C_WIN // 512))]
        args = [win_src] * nwb + [tail_src]
    else:
        in_specs += [pl.BlockSpec((PAGE, 512),
                                  lambda b, t, k=k: (b * qt_n + jnp.maximum(t - (nwb - 1) + k, 0), C_WIN // 512))
                     for k in range(nwb)]
        args = [win_src] * nwb
    return pl.pallas_call(
        functools.partial(_nsa1_body, tq=tq, q0=q0, n_cmp=n_cmp, ncp=ncp, n_sel=n_sel, nselp=nselp, lpad=lpad,
                          nwb=nwb, has_tail=has_tail, win_k0=win_k0),
        grid=(nb, qt_n),
        in_specs=in_specs,
        out_specs=[pl.BlockSpec((tq, 1024), lambda b, t: (rows(b, t), 0)),
                   pl.BlockSpec((tq, 1024), lambda b, t: (rows(b, t), 0)),
                   pl.BlockSpec((1, NSA_KV, tq, lpad), lambda b, t: (b, 0, t, 0))],
        out_shape=[jax.ShapeDtypeStruct((n, 1024), F32), jax.ShapeDtypeStruct((n, 1024), F32),
                   jax.ShapeDtypeStruct((nb, NSA_KV, qt_n * tq, lpad), BF16)],
        compiler_params=_cparams("arbitrary", "arbitrary"),
        name="nsa_cmp_select_window",
    )(hp, hp, cmp, ov, *args)


def _mattn_body(pt_ref, *refs, layout, n_g, r_per_g, gm, tq, q0, kg_n, has_tail):
    q_ref, mask_ref = refs[:2]
    pages = refs[2:2 + PAGES_PER_STEP]
    pos = 2 + PAGES_PER_STEP
    if has_tail:
        tail_ref, tmask_ref = refs[pos:pos + 2]
        pos += 2
    o_ref, s_ref, v_ref, acc_ref = refs[pos:pos + 4]
    qt = pl.program_id(1)
    step = pl.program_id(2)
    two_pass = kg_n > 1 or has_tail
    kg = step
    qpos = q0 + qt * tq + lax.broadcasted_iota(jnp.int32, (tq, 1), 0)

    def scores(g, kt, kpos, mref):
        picked = jnp.where(mref[0, g if gm > 1 else 0].astype(F32) > 0.5, 0.0, NEG_INF)
        bias = jnp.concatenate([picked + jnp.where(kpos <= qpos, 0.0, NEG_INF)] * r_per_g, axis=0)
        return _dot_nt(_stack_heads(q_ref, g, r_per_g), kt) * ATT_SCALE + bias

    def emit(g, o):
        for r in range(r_per_g):
            h = g * r_per_g + r
            o_ref[:, h * HEAD_DIM:(h + 1) * HEAD_DIM] = o[r * tq:(r + 1) * tq]

    def main_keys(g):
        return jnp.concatenate([_page_tile(layout, pg, g) for pg in pages], axis=0).astype(BF16)

    def main_vals(g):
        return jnp.concatenate([_page_tile(layout, pg, n_g + g) for pg in pages], axis=0).astype(BF16)

    def tail_part(g, off):
        zpad = jnp.zeros((LANES - SUBLANES, HEAD_DIM), F32)
        return jnp.concatenate([tail_ref[:, (off + g) * HEAD_DIM:(off + g + 1) * HEAD_DIM], zpad], axis=0).astype(BF16)

    kpos = kg * KSTEP + lax.broadcasted_iota(jnp.int32, (1, KSTEP), 1)
    tpos = q0 + lax.broadcasted_iota(jnp.int32, (1, LANES), 1)

    if not two_pass:
        ch = KSTEP // CAUSAL_CHUNKS
        ppc = PAGES_PER_STEP // CAUSAL_CHUNKS
        c_last = (q0 + qt * tq + tq - 1) // ch
        for g in range(n_g):
            qg = _stack_heads(q_ref, g, r_per_g)
            for c in range(CAUSAL_CHUNKS):
                sl = slice(c * ch, (c + 1) * ch)

                @pl.when(c <= c_last)
                def _():
                    kt = jnp.concatenate([_page_tile(layout, pages[c * ppc + k], g) for k in range(ppc)],
                                         axis=0).astype(BF16)
                    picked = jnp.where(mask_ref[0, g if gm > 1 else 0, :, sl].astype(F32) > 0.5, 0.0, NEG_INF)
                    bias = jnp.concatenate([picked + jnp.where(kpos[:, sl] <= qpos, 0.0, NEG_INF)] * r_per_g, axis=0)
                    s_ref[0, 0, :, sl] = _dot_nt(qg, kt) * ATT_SCALE + bias

                @pl.when(c > c_last)
                def _():
                    s_ref[0, 0, :, sl] = jnp.full((r_per_g * tq, ch), NEG_INF, F32)

            p, den = _softmax_parts(s_ref[0, 0])
            pn = (p / den).astype(BF16)
            acc_ref[...] = jnp.zeros(acc_ref.shape, F32)
            for c in range(CAUSAL_CHUNKS):
                @pl.when(c <= c_last)
                def _():
                    vt = jnp.concatenate([_page_tile(layout, pages[c * ppc + k], n_g + g) for k in range(ppc)],
                                         axis=0).astype(BF16)
                    acc_ref[...] += jnp.dot(pn[:, c * ch:(c + 1) * ch], vt, preferred_element_type=F32)
            emit(g, acc_ref[...])
        return

    for g in range(n_g):
        s_ref[g, kg] = scores(g, main_keys(g), kpos, mask_ref)
        v_ref[g, pl.ds(pl.multiple_of(kg * KSTEP, KSTEP), KSTEP), :] = main_vals(g)

    @pl.when(step == kg_n - 1)
    def _():
        for g in range(n_g):
            chunks = [s_ref[g, c] for c in range(kg_n)]
            vals = [v_ref[g, c * KSTEP:(c + 1) * KSTEP, :] for c in range(kg_n)]
            if has_tail:
                chunks.append(scores(g, tail_part(g, 0), tpos, tmask_ref))
                vals.append(tail_part(g, n_g))
            m = chunks[0].max(axis=-1, keepdims=True)
            for c in chunks[1:]:
                m = jnp.maximum(m, c.max(axis=-1, keepdims=True))
            m = jnp.where(m == NEG_INF, 0.0, m)
            exps = [jnp.exp(c - m) for c in chunks]
            den = exps[0].sum(axis=-1, keepdims=True)
            for e in exps[1:]:
                den = den + e.sum(axis=-1, keepdims=True)
            den = jnp.maximum(den, 1e-30)
            o = jnp.zeros((r_per_g * tq, HEAD_DIM), F32)
            for e, v in zip(exps, vals):
                o = o + jnp.dot((e / den).astype(BF16), v, preferred_element_type=F32)
            emit(g, o)


def _mattn(hp, q_col, n_heads, mask, pool, layout, pt, tail_src, tail_col_block, *, nb, tq, qt_n, q0, name):
    n = hp.shape[0]
    n_g = 2
    r_per_g = n_heads // n_g
    qw = n_heads * HEAD_DIM
    gm = mask.shape[1]
    kg_n = pt.shape[1] // PAGES_PER_STEP
    has_tail = tail_src is not None
    buffered = kg_n > 1 or has_tail
    in_specs = [pl.BlockSpec((tq, qw), lambda b, t, k, pt: (b * qt_n + t, q_col // qw)),
                pl.BlockSpec((1, gm, tq, KSTEP), lambda b, t, k, pt: (b, 0, t, k))]
    in_specs += _page_specs(layout, 3)
    args = [hp, mask] + [pool] * PAGES_PER_STEP
    if has_tail:
        in_specs += [pl.BlockSpec((SUBLANES, 512), lambda b, t, k, pt: (b, tail_col_block)),
                     pl.BlockSpec((1, gm, tq, LANES), lambda b, t, k, pt: (b, 0, t, kg_n * KSTEP // LANES))]
        args += [tail_src, mask]
    return pl.pallas_call(
        functools.partial(_mattn_body, layout=layout, n_g=n_g, r_per_g=r_per_g, gm=gm, tq=tq, q0=q0, kg_n=kg_n, has_tail=has_tail),
        grid_spec=pltpu.PrefetchScalarGridSpec(
            num_scalar_prefetch=1, grid=(nb, qt_n, kg_n),
            in_specs=in_specs,
            out_specs=pl.BlockSpec((tq, qw), lambda b, t, k, pt: (b * qt_n + t, 0)),
            scratch_shapes=[pltpu.VMEM((n_g, kg_n, r_per_g * tq, KSTEP) if buffered else (1, 1, r_per_g * tq, KSTEP), F32),
                            pltpu.VMEM((n_g, kg_n * KSTEP, HEAD_DIM) if buffered else (1, 2 * SUBLANES, LANES), BF16),
                            pltpu.VMEM((r_per_g * tq, HEAD_DIM), F32)]),
        out_shape=jax.ShapeDtypeStruct((n, qw), F32),
        compiler_params=_cparams("arbitrary", "arbitrary", "arbitrary"),
        name=name,
    )(pt, *args)


PACKED_ROWS = 2 * SUBLANES


def _topk_mask_packed_t(sc_t, n_keep):
    n_keys, n_q = sc_t.shape
    i16 = jnp.int16
    bits = pltpu.bitcast(sc_t + 0.0, jnp.int32)
    key = jnp.where(bits < 0, bits ^ jnp.int32(0x7FFFFFFF), bits)
    hi = jnp.right_shift(key, 16).astype(i16)
    lo = ((key & jnp.int32(0xFFFF)) - 32768).astype(i16)
    one_b, zero_b = jnp.asarray(1, BF16), jnp.asarray(0, BF16)
    i16_min, i16_max = jnp.asarray(-32768, i16), jnp.asarray(32767, i16)
    assert n_keys % PACKED_ROWS == 0 and n_keys // PACKED_ROWS <= 256

    def colsum(x01):
        parts = [x01[i * PACKED_ROWS:(i + 1) * PACKED_ROWS, :] for i in range(n_keys // PACKED_ROWS)]
        while len(parts) > 1:
            parts = [parts[i] + parts[i + 1] for i in range(0, len(parts), 2)]
        return jnp.sum(parts[0].astype(F32), axis=0, keepdims=True)

    def search(vals, need, nbits, start):
        def body(i, thr):
            cand = thr + jnp.left_shift(jnp.int32(1), nbits - 1 - i)
            cnt = colsum(jnp.where(vals >= cand.astype(i16), one_b, zero_b))
            return jnp.where(cnt >= need, cand, thr)
        return lax.fori_loop(0, nbits, body, jnp.full((1, n_q), start, jnp.int32))

    t_hi = search(hi, float(n_keep), 16, -32768).astype(i16)
    eq = hi == t_hi
    hi_gt = jnp.where(hi > t_hi, one_b, zero_b)
    lo_m = jnp.where(eq, lo, i16_min)
    t_lo = search(lo_m, n_keep - colsum(hi_gt), 16, -32768).astype(i16)
    gt = hi_gt + jnp.where(lo_m > t_lo, one_b, zero_b)
    need = n_keep - colsum(gt)
    idx = lax.broadcasted_iota(jnp.int32, (n_keys, 1), 0).astype(i16)
    tie_idx = jnp.where(eq, jnp.where(lo == t_lo, idx, i16_max), i16_max)
    nbits = int(np.ceil(np.log2(n_keys)))

    def ibody(i, c):
        cand = c + jnp.left_shift(jnp.int32(1), nbits - 1 - i)
        cnt = colsum(jnp.where(tie_idx < cand.astype(i16), one_b, zero_b))
        return jnp.where(cnt < need, cand, c)

    n_tie = colsum(jnp.where(tie_idx < i16_max, one_b, zero_b))
    cut = lax.cond(jnp.max(n_tie - need) > 0.0,
                   lambda: lax.fori_loop(0, nbits, ibody, jnp.zeros((1, n_q), jnp.int32)),
                   lambda: jnp.full((1, n_q), n_keys - 1, jnp.int32)).astype(i16)
    return jnp.maximum(gt, jnp.where(tie_idx <= cut, one_b, zero_b))


def _dsa1_body(pt_ref, *refs, layout, tq, q0, kg_n, lpad, has_tail, n_keep):
    iq_ref, misc_ref = refs[:2]
    pages = refs[2:2 + PAGES_PER_STEP]
    pos = 2 + PAGES_PER_STEP
    if has_tail:
        tail_ref = refs[pos]
        pos += 1
    mask_ref, sc_ref = refs[pos:pos + 2]
    qt = pl.program_id(1)
    kg = pl.program_id(2)
    qpos = q0 + qt * tq + lax.broadcasted_iota(jnp.int32, (tq, 1), 0)
    lane128 = lax.broadcasted_iota(jnp.int32, (1, LANES), 1)

    if kg_n == 1 and not has_tail and tq == LANES:
        kt = jnp.concatenate([_page_tile(layout, pg, 0, IDX_DIM) for pg in pages], axis=0).astype(BF16)
        misc_t = misc_ref[...].T
        kpos_c = lax.broadcasted_iota(jnp.int32, (KSTEP, 1), 0)
        qpos_r = q0 + qt * tq + lax.broadcasted_iota(jnp.int32, (1, tq), 1)
        acc = jnp.zeros((KSTEP, tq), F32)
        for h in range(IDX_HEADS):
            qi = iq_ref[:, h * IDX_DIM:(h + 1) * IDX_DIM].astype(BF16)
            acc = acc + misc_t[MISC_IW + h:MISC_IW + h + 1, :] * jnp.maximum(_dot_nt(kt, qi), 0.0)
        sel_t = _topk_mask_packed_t(jnp.where(kpos_c <= qpos_r, acc, NEG_INF), n_keep)
        for c in range(KSTEP // LANES):
            blk = sel_t[c * LANES:(c + 1) * LANES, :].astype(F32).T
            mask_ref[0, 0, :, c * LANES:(c + 1) * LANES] = blk.astype(BF16)
        return

    def scores(kt, kpos, keys_on_lanes=False):
        acc = jnp.zeros((tq, kpos.shape[1]), F32)
        for h in range(IDX_HEADS):
            qi = iq_ref[:, h * IDX_DIM:(h + 1) * IDX_DIM].astype(BF16)
            w = jnp.sum(jnp.where(lane128 == MISC_IW + h, misc_ref[...], 0.0), axis=-1, keepdims=True)
            qk = jnp.dot(qi, kt, preferred_element_type=F32) if keys_on_lanes else _dot_nt(qi, kt)
            acc = acc + w * jnp.maximum(qk, 0.0)
        return jnp.where(kpos <= qpos, acc, NEG_INF)

    kpos = kg * KSTEP + lax.broadcasted_iota(jnp.int32, (1, KSTEP), 1)
    if layout.kind == "cache4t":
        kt = jnp.concatenate([pg[0, 0] for pg in pages], axis=1).astype(BF16)
        sc_ref[kg] = scores(kt, kpos, keys_on_lanes=True)
    else:
        kt = jnp.concatenate([_page_tile(layout, pg, 0, IDX_DIM) for pg in pages], axis=0).astype(BF16)
        sc_ref[kg] = scores(kt, kpos)

    @pl.when(kg == kg_n - 1)
    def _():
        nch = sc_ref.shape[0]
        if has_tail:
            zpad = jnp.zeros((KSTEP - SUBLANES, IDX_DIM), F32)
            tk = jnp.concatenate([tail_ref[:, 0:IDX_DIM], zpad], axis=0).astype(BF16)
            tl = lax.broadcasted_iota(jnp.int32, (1, KSTEP), 1)
            sc_ref[kg_n] = jnp.where(tl < LANES, scores(tk, q0 + tl), NEG_INF)
        sc = sc_ref[...] + 0.0
        bits = pltpu.bitcast(sc, jnp.int32)
        key = jnp.where(bits < 0, bits ^ jnp.int32(0x7FFFFFFF), bits)
        int_min = jnp.int32(-2 ** 31)

        def count(pred):
            return jnp.sum(jnp.sum(jnp.where(pred, 1.0, 0.0), axis=-1, keepdims=True), axis=0, keepdims=True)

        def vbit(i, thr):
            cand = thr + jnp.left_shift(jnp.int32(1), 31 - i)
            return jnp.where(count(key >= cand) >= n_keep, cand, thr)

        thr = lax.fori_loop(0, 32, vbit, jnp.full((1, tq, 1), int_min, jnp.int32))
        gt = key > thr
        tie = key == thr
        need = n_keep - count(gt)
        idx = (lax.broadcasted_iota(jnp.int32, (nch, 1, KSTEP), 0) * KSTEP
               + lax.broadcasted_iota(jnp.int32, (nch, 1, KSTEP), 2))
        nbits = int(np.ceil(np.log2(nch * KSTEP)))

        def ibit(i, c):
            cand = c + jnp.left_shift(jnp.int32(1), nbits - 1 - i)
            return jnp.where(count(tie & (idx < cand)) < need, cand, c)

        cut = lax.fori_loop(0, nbits, ibit, jnp.zeros((1, tq, 1), jnp.int32))
        sel = jnp.where(gt | (tie & (idx <= cut)), 1.0, 0.0).astype(BF16)
        for c in range(kg_n):
            mask_ref[0, 0, :, c * KSTEP:(c + 1) * KSTEP] = sel[c]
        if has_tail:
            mask_ref[0, 0, :, kg_n * KSTEP:lpad] = sel[kg_n][:, 0:lpad - kg_n * KSTEP]


def _dsa1(hp, pool, layout, pt, tail_src, *, nb, tq, qt_n, q0, lpad, n_keep):
    kg_n = pt.shape[1] // PAGES_PER_STEP
    has_tail = tail_src is not None
    in_specs = [pl.BlockSpec((tq, 256), lambda b, t, k, pt: (b * qt_n + t, C_IQ // 256)),
                pl.BlockSpec((tq, LANES), lambda b, t, k, pt: (b * qt_n + t, C_MISC // LANES))]
    in_specs += _page_specs(layout, 3)
    args = [hp, hp] + [pool] * PAGES_PER_STEP
    if has_tail:
        in_specs += [pl.BlockSpec((SUBLANES, LANES), lambda b, t, k, pt: (b, C_MISC // LANES))]
        args += [tail_src]
    return pl.pallas_call(
        functools.partial(_dsa1_body, layout=layout, tq=tq, q0=q0, kg_n=kg_n, lpad=lpad, has_tail=has_tail,
                          n_keep=n_keep),
        grid_spec=pltpu.PrefetchScalarGridSpec(
            num_scalar_prefetch=1, grid=(nb, qt_n, kg_n),
            in_specs=in_specs,
            out_specs=pl.BlockSpec((1, 1, tq, lpad), lambda b, t, k, pt: (b, 0, t, 0)),
            scratch_shapes=[pltpu.VMEM((kg_n + (1 if has_tail else 0), tq, KSTEP), F32)]),
        out_shape=jax.ShapeDtypeStruct((nb, 1, qt_n * tq, lpad), BF16),
        compiler_params=_cparams("arbitrary", "arbitrary", "arbitrary"),
        name="dsa_indexer_topk",
    )(pt, *args)


def _mix_body(ocmp_ref, osel_ref, owin_ref, odsa_ref, conv_ref, misc_ref, mg0_ref, mg1_ref, mg2_ref,
              wa_ref, wb_ref, wc_ref, z_ref):
    lane128 = lax.broadcasted_iota(jnp.int32, (1, LANES), 1)
    misc = misc_ref[...]

    def gate(kind, h):
        return jnp.sum(jnp.where(lane128 == MISC_NG + kind * NSA_HEADS + h, misc, 0.0), axis=-1, keepdims=True)

    parts = []
    for h in range(NSA_HEADS):
        sl = slice(h * HEAD_DIM, (h + 1) * HEAD_DIM)
        parts.append((gate(0, h) * ocmp_ref[:, sl] + gate(1, h) * osel_ref[:, sl]
                      + gate(2, h) * owin_ref[:, sl]).astype(BF16))
    o_nsa = jnp.concatenate(parts, axis=1)
    p_a = jnp.dot(o_nsa, wa_ref[...], preferred_element_type=F32)
    p_b = jnp.dot(odsa_ref[...].astype(BF16), wb_ref[...], preferred_element_type=F32)
    p_c = jnp.dot(conv_ref[...].astype(BF16), wc_ref[...], preferred_element_type=F32)
    z_ref[...] = (mg0_ref[...] * p_a + mg1_ref[...] * p_b + mg2_ref[...] * p_c).astype(BF16)


def _mix(hp, o_cmp, o_sel, o_win, o_dsa, conv_out, wa, wb, wc, tm):
    n = hp.shape[0]
    row = lambda w, cb=0: pl.BlockSpec((tm, w), lambda i: (i, cb))
    full = lambda a: pl.BlockSpec(a.shape, lambda i: (0, 0))
    return pl.pallas_call(
        _mix_body,
        grid=(n // tm,),
        in_specs=[row(1024), row(1024), row(1024), row(512), row(512), row(LANES, C_MISC // LANES),
                  row(D_MODEL, C_MG // D_MODEL), row(D_MODEL, C_MG // D_MODEL + 1), row(D_MODEL, C_MG // D_MODEL + 2),
                  full(wa), full(wb), full(wc)],
        out_specs=row(D_MODEL),
        out_shape=jax.ShapeDtypeStruct((n, D_MODEL), BF16),
        compiler_params=_cparams("arbitrary"),
        name="branch_merge",
    )(o_cmp, o_sel, o_win, o_dsa, conv_out, hp, hp, hp, hp, wa, wb, wc)


def _outln_body(z_ref, wo_ref, x_ref, g_ref, b_ref, wr_ref, rb_ref, x1_ref, gate_ref, *, tm):
    y = jnp.dot(z_ref[...], wo_ref[...], preferred_element_type=F32)
    x1 = _layernorm(ALPHA * x_ref[...] + y, g_ref[...], b_ref[...])
    x1_ref[...] = x1
    x_hi = x1.astype(BF16)
    x_lo = (x1 - x_hi.astype(F32)).astype(BF16)
    w_hi = wr_ref[...].astype(BF16)
    w_lo = (wr_ref[...] - w_hi.astype(F32)).astype(BF16)
    aff = _sigmoid(_dot_nt(w_hi, x_hi) + (_dot_nt(w_hi, x_lo) + _dot_nt(w_lo, x_hi)))
    biased = aff + rb_ref[:, 0:1]
    rows = [biased[e:e + 1, :] for e in range(N_EXPERTS)]
    best = None
    g_best = jnp.zeros((1, tm), jnp.int32)
    for g in range(N_GROUPS):
        v = rows[g * EXPERTS_PER_GROUP:(g + 1) * EXPERTS_PER_GROUP]
        score = None
        for a in range(EXPERTS_PER_GROUP):
            for c in range(a + 1, EXPERTS_PER_GROUP):
                pair = v[a] + v[c]
                score = pair if score is None else jnp.maximum(score, pair)
        if best is None:
            best = score
        else:
            better = score > best
            best = jnp.where(better, score, best)
            g_best = jnp.where(better, g, g_best)
    sel_rows = []
    for e in range(N_EXPERTS):
        g = e // EXPERTS_PER_GROUP
        rank = jnp.zeros((1, tm), F32)
        for o in range(g * EXPERTS_PER_GROUP, (g + 1) * EXPERTS_PER_GROUP):
            if o == e:
                continue
            beats = (rows[o] > rows[e]) | ((rows[o] == rows[e]) & (o < e))
            rank = rank + jnp.where(beats, 1.0, 0.0)
        sel_rows.append(jnp.where((g_best == g) & (rank < 2), aff[e:e + 1, :], 0.0))
    tot = sel_rows[0]
    for e in range(1, N_EXPERTS):
        tot = tot + sel_rows[e]
    gate_t = jnp.concatenate(sel_rows + [jnp.zeros((LANES - N_EXPERTS, tm), F32)], axis=0) / tot
    gate_ref[...] = gate_t.T


def _outln(z, wo, x, g, b, wr_t, rb, tm):
    n = z.shape[0]
    row = lambda w: pl.BlockSpec((tm, w), lambda i: (i, 0))
    full = lambda a: pl.BlockSpec(a.shape, lambda i: (0, 0))
    return pl.pallas_call(
        functools.partial(_outln_body, tm=tm),
        grid=(n // tm,),
        in_specs=[row(D_MODEL), full(wo), row(D_MODEL), full(g), full(b), full(wr_t), full(rb)],
        out_specs=[row(D_MODEL), row(LANES)],
        out_shape=[jax.ShapeDtypeStruct((n, D_MODEL), F32), jax.ShapeDtypeStruct((n, LANES), F32)],
        compiler_params=_cparams("arbitrary"),
        name="out_proj_ln_router",
    )(z, wo, x, g, b, wr_t, rb)


def _moe_body(x_ref, gate_ref, wg_ref, wu_ref, wd_ref, g_ref, b_ref, o_ref, xb_ref, acc_ref):
    e = pl.program_id(1)

    @pl.when(e == 0)
    def _():
        xb_ref[...] = x_ref[...].astype(BF16)
        acc_ref[...] = jnp.zeros(acc_ref.shape, F32)

    lane128 = lax.broadcasted_iota(jnp.int32, (1, LANES), 1)
    gcol = jnp.sum(jnp.where(lane128 == e, gate_ref[...], 0.0), axis=-1, keepdims=True)
    xb = xb_ref[...]
    hg = jnp.dot(xb, wg_ref[0], preferred_element_type=F32)
    hu = jnp.dot(xb, wu_ref[0], preferred_element_type=F32)
    h = (hg * _sigmoid(hg)) * hu * gcol
    acc_ref[...] += jnp.dot(h.astype(BF16), wd_ref[0], preferred_element_type=F32)

    @pl.when(e == N_EXPERTS - 1)
    def _():
        o_ref[...] = _layernorm(ALPHA * x_ref[...] + acc_ref[...], g_ref[...], b_ref[...])


def _moe(x1, gate, wg, wu, wd, g, b, tm):
    n = x1.shape[0]
    return pl.pallas_call(
        _moe_body,
        grid=(n // tm, N_EXPERTS),
        in_specs=[pl.BlockSpec((tm, D_MODEL), lambda i, e: (i, 0)),
                  pl.BlockSpec((tm, LANES), lambda i, e: (i, 0)),
                  pl.BlockSpec((1, D_MODEL, D_FF), lambda i, e: (e, 0, 0)),
                  pl.BlockSpec((1, D_MODEL, D_FF), lambda i, e: (e, 0, 0)),
                  pl.BlockSpec((1, D_FF, D_MODEL), lambda i, e: (e, 0, 0)),
                  pl.BlockSpec((1, D_MODEL), lambda i, e: (0, 0)),
                  pl.BlockSpec((1, D_MODEL), lambda i, e: (0, 0))],
        out_specs=pl.BlockSpec((tm, D_MODEL), lambda i, e: (i, 0)),
        out_shape=jax.ShapeDtypeStruct((n, D_MODEL), F32),
        scratch_shapes=[pltpu.VMEM((tm, D_MODEL), BF16), pltpu.VMEM((tm, D_MODEL), F32)],
        compiler_params=_cparams("arbitrary", "arbitrary"),
        name="moe_ln",
    )(x1, gate, wg, wu, wd, g, b)


def _overlap_matrix(ncp, nselp):
    cs = np.arange(ncp)[:, None] * CMP_STRIDE
    ss = np.arange(nselp)[None, :] * SEL_BLOCK
    return jnp.asarray(((cs < ss + SEL_BLOCK) & (cs + CMP_LEN > ss)).astype(np.float32))


class _Group:
    def __init__(self, nb, t_real, t_pad, q0, past_len, tm, tq):
        self.nb, self.t_real, self.t_pad, self.q0, self.past_len, self.tm, self.tq = nb, t_real, t_pad, q0, past_len, tm, tq
        self.paged = past_len > 0
        self.lp = past_len if self.paged else t_pad
        self.ltot = self.lp + (t_real if self.paged else 0)
        self.lpad = self.lp + (LANES if self.paged else 0)
        self.qt_n = t_pad // tq
        n_chunks = -(-self.ltot // CMP_STRIDE)
        self.n_cmp = n_chunks - CMP_LEN // CMP_STRIDE + 1
        self.n_sel = -(-self.ltot // SEL_BLOCK)
        self.nselp = -(-self.n_sel // LANES) * LANES
        self.n_keep = min(DSA_TOPK, self.ltot // 4)


def _mixer(gp, x, lw, caches):
    hp = _proj(x.astype(BF16), lw["w_in"], gp.tabs, lw["kinds"], min(2 * gp.tm, gp.nb * gp.t_pad, gp.tabs.shape[1]))
    nb, tq, qt_n, q0 = gp.nb, gp.tq, gp.qt_n, gp.q0
    if gp.paged:
        l, nsa_pool, dsa_pool, kidx_pool, win_src, s_conv, pt = caches
        cmp_lay = _PageLayout("cache", 0, 0, base=0, layer=l, nslot=2)
        slc_lay = _PageLayout("cache", 0, 1, base=0, layer=l, nslot=2)
        dsa_lay = _PageLayout("cache", 0, 0, base=0, layer=l, nslot=2)
        kidx_pool = jnp.transpose(kidx_pool, (0, 1, 3, 2))
        kidx_lay = _PageLayout("cache4t", IDX_DIM, 0, base=0, layer=l, nslot=0)
        tail = hp
        nwb, win_k0, win_layer = WINDOW // PAGE, PAST_LEN - WINDOW, l
        past8 = jnp.concatenate([jnp.zeros((nb, SUBLANES - 2, CONV_DIM), F32), s_conv[l]], axis=1)
    else:
        pt = gp.pt
        nsa_pool = dsa_pool = kidx_pool = hp.reshape(nb * gp.t_pad // PAGE, PAGE, NCOL)
        cmp_lay = _PageLayout("cols", 512, C_CMP // 512, base=0, layer=0, nslot=0)
        slc_lay = _PageLayout("cols", 512, C_SLC // 512, base=0, layer=0, nslot=0)
        dsa_lay = _PageLayout("cols", 512, C_DSA // 512, base=0, layer=0, nslot=0)
        kidx_lay = _PageLayout("cols", LANES, C_MISC // LANES, base=0, layer=0, nslot=0)
        tail = None
        win_src = hp
        nwb, win_k0, win_layer = WINDOW // PAGE + 1, 0, 0
        past8 = jnp.zeros((nb, SUBLANES, CONV_DIM), F32)

    conv_out, cu = _conv(hp, past8, lw["conv_w8"], min(gp.tm, gp.t_pad), gp.t_pad // min(gp.tm, gp.t_pad))

    ab = _cmp1(nsa_pool, cmp_lay, pt, lw["wcat"], lw["pe"])
    cmp = _cmp2(ab, tail, lw["pe"], lw["w1r"], lw["phi_w2"], gp.t_real if gp.paged else 0, C_CMP // 512)
    o_cmp, o_win, kmask = _nsa1(hp, cmp, gp.ov, win_src, tail, nb=nb, tq=tq, qt_n=qt_n, q0=q0, n_cmp=gp.n_cmp,
                                n_sel=gp.n_sel, lpad=gp.lpad, nwb=nwb, win_k0=win_k0, win_layer=win_layer)
    o_sel = _mattn(hp, C_QROT, NSA_HEADS, kmask, nsa_pool, slc_lay, pt, tail, C_SLC // 512,
                   nb=nb, tq=tq, qt_n=qt_n, q0=q0, name="nsa_selected_attn")
    dmask = _dsa1(hp, kidx_pool, kidx_lay, pt, tail, nb=nb, tq=tq, qt_n=qt_n, q0=q0, lpad=gp.lpad,
                  n_keep=gp.n_keep)
    o_dsa = _mattn(hp, C_DQ, DSA_HEADS, dmask, dsa_pool, dsa_lay, pt, tail, C_DSA // 512,
                   nb=nb, tq=tq, qt_n=qt_n, q0=q0, name="dsa_topk_attn")
    z = _mix(hp, o_cmp, o_sel, o_win, o_dsa, conv_out, lw["w_a"], lw["w_b"], lw["w_c"], min(gp.tm, 256))
    return z, hp, cu


def _layer(gp, x, lw, caches):
    z, hp, cu = _mixer(gp, x, lw, caches)
    tm2 = min(gp.tm, 256)
    x1, gate = _outln(z, lw["w_o"], x, lw["ln_mix_g"], lw["ln_mix_b"], lw["wr_t"], lw["rb"], tm2)
    x2 = _moe(x1, gate, lw["w_eg"], lw["w_eu"], lw["w_ed"], lw["ln_ffn_g"], lw["ln_ffn_b"], min(gp.tm, 512))
    return x2, hp, cu


def _layer_weights(l, w_in, nsa_phi_pos, nsa_phi_w1, nsa_phi_w2, conv_w, w_br_a, w_br_b, w_br_c, w_out,
                   ln_mix_g, ln_mix_b, ln_ffn_g, ln_ffn_b, w_router, router_bias, w_e_gate, w_e_up, w_e_down):
    w1r = nsa_phi_w1[l].reshape(2, CMP_LEN, HEAD_DIM, HEAD_DIM)
    wcat = jnp.concatenate([w1r[:, :CMP_STRIDE], w1r[:, CMP_STRIDE:]], axis=-1).astype(BF16)
    return dict(
        w_in=_permute_w_in(w_in, l), kinds=jnp.asarray(_col_kinds()),
        w1r=w1r, wcat=wcat, pe=nsa_phi_pos[l], phi_w2=nsa_phi_w2[l],
        conv_w8=jnp.concatenate([conv_w[l], jnp.zeros((SUBLANES - 3, CONV_DIM), F32)], axis=0),
        w_a=w_br_a[l].astype(BF16), w_b=w_br_b[l].astype(BF16), w_c=w_br_c[l].astype(BF16),
        w_o=w_out[l].astype(BF16),
        ln_mix_g=ln_mix_g[l][None], ln_mix_b=ln_mix_b[l][None],
        ln_ffn_g=ln_ffn_g[l][None], ln_ffn_b=ln_ffn_b[l][None],
        wr_t=w_router.T, rb=jnp.broadcast_to(router_bias[:, None], (N_EXPERTS, LANES)),
        w_eg=w_e_gate[l].astype(BF16), w_eu=w_e_up[l].astype(BF16), w_ed=w_e_down[l].astype(BF16))


def kernel(x_prompt, x_sample, cache_nsa_kv, cache_dsa_kv, cache_dsa_kidx, state_nsa_win, state_conv, page_table,
           w_in, nsa_phi_pos, nsa_phi_w1, nsa_phi_w2, conv_w, w_br_a, w_br_b, w_br_c, w_out, ln_mix_g, ln_mix_b,
           ln_ffn_g, ln_ffn_b, w_router, router_bias, w_e_gate, w_e_up, w_e_down):
    bp, tp, _ = x_prompt.shape
    bs, ts, _ = x_sample.shape
    ts_pad = SUBLANES

    gp_p = _Group(bp, tp, tp, 0, 0, tm=512, tq=128)
    gp_p.tabs = _rope_tables(jnp.arange(tp, dtype=jnp.int32))
    gp_p.pt = jnp.arange(bp * tp // PAGE, dtype=jnp.int32).reshape(bp, tp // PAGE)
    gp_p.ov = _overlap_matrix(LANES, gp_p.nselp)
    gp_s = _Group(bs, ts, ts_pad, PAST_LEN, PAST_LEN, tm=bs * ts_pad, tq=ts_pad)
    pos_s = PAST_LEN + jnp.arange(ts_pad, dtype=jnp.int32)
    gp_s.tabs = jnp.tile(_rope_tables(pos_s), (1, bs, 1))
    gp_s.ov = _overlap_matrix(PAST_LEN // CMP_STRIDE, gp_s.nselp)

    xp = x_prompt.reshape(bp * tp, D_MODEL)
    xs = jnp.concatenate([x_sample, jnp.zeros((bs, ts_pad - ts, D_MODEL), F32)], axis=1).reshape(bs * ts_pad, D_MODEL)

    outs_p = [[] for _ in range(5)]
    outs_s = [[] for _ in range(5)]
    for l in range(DEPTH):
        lw = _layer_weights(l, w_in, nsa_phi_pos, nsa_phi_w1, nsa_phi_w2, conv_w, w_br_a, w_br_b, w_br_c, w_out,
                            ln_mix_g, ln_mix_b, ln_ffn_g, ln_ffn_b, w_router, router_bias, w_e_gate, w_e_up, w_e_down)
        xp, hp_p, cu_p = _layer(gp_p, xp, lw, None)
        xs, hp_s, cu_s = _layer(gp_s, xs, lw, (l, cache_nsa_kv, cache_dsa_kv, cache_dsa_kidx,
                                                 state_nsa_win, state_conv, page_table))
        h3 = hp_p.reshape(bp, tp, NCOL)
        outs_p[0].append(h3[:, :, C_CMP:C_CMP + 1024].reshape(bp, tp, 4, NSA_KV, HEAD_DIM))
        outs_p[1].append(h3[:, :, C_DSA:C_DSA + 512].reshape(bp, tp, 2, DSA_KV, HEAD_DIM))
        outs_p[2].append(h3[:, :, C_MISC:C_MISC + IDX_DIM])
        outs_p[3].append(h3[:, tp - min(WINDOW, tp):, C_WIN:C_WIN + 512].reshape(bp, min(WINDOW, tp), 2, NSA_KV, HEAD_DIM))
        outs_p[4].append(cu_p.reshape(bp, tp, CONV_DIM)[:, tp - 2:])
        s3 = hp_s.reshape(bs, ts_pad, NCOL)[:, :ts]
        outs_s[0].append(s3[:, :, C_CMP:C_CMP + 1024].reshape(bs, ts, 4, NSA_KV, HEAD_DIM))
        outs_s[1].append(s3[:, :, C_DSA:C_DSA + 512].reshape(bs, ts, 2, DSA_KV, HEAD_DIM))
        outs_s[2].append(s3[:, :, C_MISC:C_MISC + IDX_DIM])
        win_new = s3[:, :, C_WIN:C_WIN + 512].reshape(bs, ts, 2, NSA_KV, HEAD_DIM)
        wb = state_nsa_win.shape[2]
        outs_s[3].append(jnp.concatenate([state_nsa_win[l], win_new], axis=1)[:, -wb:])
        ext = jnp.concatenate([state_conv[l], cu_s.reshape(bs, ts_pad, CONV_DIM)[:, :ts]], axis=1)
        outs_s[4].append(ext[:, -2:])
    sp = [jnp.stack(a, axis=0) for a in outs_p]
    ss = [jnp.stack(a, axis=0) for a in outs_s]
    y_p = xp.reshape(bp, tp, D_MODEL)
    y_s = xs.reshape(bs, ts_pad, D_MODEL)[:, :ts]
    return (y_p, y_s, sp[0], ss[0], sp[1], ss[1], sp[2], ss[2], sp[3], ss[3], sp[4], ss[4])
```

```python
import collections
import functools

import numpy as np
import jax
import jax.numpy as jnp
from jax import lax
from jax.experimental import pallas as pl
from jax.experimental.pallas import tpu as pltpu

F32 = jnp.float32
BF16 = jnp.bfloat16
HIGHEST = lax.Precision.HIGHEST
NEG_INF = float("-inf")

D_MODEL = 2048
DEPTH = 2
PAST_LEN = 16384
PAGE = 128
HEAD_DIM = 128
ROPE_THETA = 500000.0
NSA_HEADS = 8
NSA_KV = 2
CMP_LEN = 32
CMP_STRIDE = 16
SEL_BLOCK = 64
SEL_TOP = 16
WINDOW = 512
FORCE_SCORE = 1e4
DSA_HEADS = 4
DSA_KV = 2
IDX_HEADS = 4
IDX_DIM = 64
DSA_TOPK = 256
CONV_DIM = 512
N_EXPERTS = 16
N_GROUPS = 4
EXPERTS_PER_GROUP = 4
D_FF = 512
LN_EPS = 1e-5
ALPHA = (2 * DEPTH) ** 0.25
IN_WIDTHS = (1024, 1536, 24, 512, 512, 256, 64, 4, 1536, 6144)
ATT_SCALE = HEAD_DIM ** -0.5

LANES = 128
SUBLANES = 8
VMEM_LIMIT = 56 * 1024 * 1024

C_QRAW = 0
C_QROT = 1024
C_CMP = 2048
C_SLC = 2560
C_WIN = 3072
C_DQ = 3584
C_DSA = 4096
C_CV = 4608
C_MG = 6144
C_IQ = 12288
C_MISC = 12544
NCOL = 12800
MISC_IW = 64
MISC_NG = 68
PROJ_TN = 1280
PAGES_PER_STEP = 16
KSTEP = PAGES_PER_STEP * PAGE

K_PLAIN, K_ROPE128, K_ROPE64, K_SIGMOID, K_MISC = 0, 1, 2, 3, 4


def _col_kinds():
    kinds = np.zeros(NCOL // LANES, np.int32)

    def mark(c0, n, k):
        kinds[c0 // LANES:(c0 + n) // LANES] = k

    mark(C_QROT, 1024, K_ROPE128)
    mark(C_SLC, 256, K_ROPE128)
    mark(C_WIN, 256, K_ROPE128)
    mark(C_DQ, 512, K_ROPE128)
    mark(C_DSA, 256, K_ROPE128)
    mark(C_IQ, 256, K_ROPE64)
    mark(C_MISC, 128, K_MISC)
    mark(C_MG, 6144, K_SIGMOID)
    return kinds


def _cparams(*sem):
    return pltpu.CompilerParams(dimension_semantics=sem, vmem_limit_bytes=VMEM_LIMIT)


def _sigmoid(x):
    return 1.0 / (1.0 + jnp.exp(-x))


def _layernorm(x, g, b):
    mu = jnp.mean(x, axis=-1, keepdims=True)
    xc = x - mu
    var = jnp.mean(xc * xc, axis=-1, keepdims=True)
    return xc * lax.rsqrt(var + LN_EPS) * g + b


def _dot_nt(a, b, precision=None):
    return lax.dot_general(a, b, (((1,), (1,)), ((), ())), preferred_element_type=F32, precision=precision)


def _softmax_parts(s):
    m = jnp.max(s, axis=-1, keepdims=True)
    m = jnp.where(m == NEG_INF, 0.0, m)
    p = jnp.exp(s - m)
    return p, jnp.maximum(jnp.sum(p, axis=-1, keepdims=True), 1e-30)


def _stack_heads(q_ref, g, r_per_g):
    return jnp.concatenate([q_ref[:, (g * r_per_g + r) * HEAD_DIM:(g * r_per_g + r + 1) * HEAD_DIM]
                            for r in range(r_per_g)], axis=0).astype(BF16)


def _proj_body(kinds_ref, needs_ref, x_ref, wt_ref, tab_ref, o_ref, w_ref):
    nsub = PROJ_TN // LANES
    j = pl.program_id(0)

    @pl.when(pl.program_id(1) == 0)
    def _():
        for s in range(nsub):
            w_ref[:, s * LANES:(s + 1) * LANES] = wt_ref[s * LANES:(s + 1) * LANES, :].astype(F32).T.astype(BF16)

    h = jnp.dot(x_ref[...], w_ref[...], preferred_element_type=F32)
    lane = lax.broadcasted_iota(jnp.int32, (1, LANES), 1)

    def rope(hs, t0, sh):
        return (hs * tab_ref[t0] + pltpu.roll(hs, sh, 1) * tab_ref[t0 + 1]
                + pltpu.roll(hs, LANES - sh, 1) * tab_ref[t0 + 2])

    for s in range(nsub):
        kind = kinds_ref[j * nsub + s]
        hs = h[:, s * LANES:(s + 1) * LANES]
        sl = slice(s * LANES, (s + 1) * LANES)

        @pl.when(kind == K_PLAIN)
        def _():
            o_ref[:, sl] = hs

        @pl.when(kind == K_ROPE128)
        def _():
            o_ref[:, sl] = rope(hs, 0, 16)

        @pl.when(kind == K_ROPE64)
        def _():
            o_ref[:, sl] = rope(hs, 3, 8)

        @pl.when(kind == K_SIGMOID)
        def _():
            o_ref[:, sl] = _sigmoid(hs)

        @pl.when(kind == K_MISC)
        def _():
            r = rope(hs, 3, 8)
            o_ref[:, sl] = jnp.where(lane < MISC_IW, r,
                                     jnp.where(lane < MISC_NG, hs * (IDX_HEADS ** -0.5),
                                               jnp.where(lane < MISC_NG + 24, _sigmoid(hs), hs)))


def _proj(x_bf, wt_bf, tabs, kinds, tm):
    n = x_bf.shape[0]
    n_tab = tabs.shape[1] // tm
    grid = (NCOL // PROJ_TN, n // tm)
    rotary = np.isin(_col_kinds().reshape(-1, PROJ_TN // LANES), (K_ROPE128, K_ROPE64, K_MISC)).any(axis=1)
    needs = jnp.asarray(rotary.astype(np.int32))
    return pl.pallas_call(
        _proj_body,
        grid_spec=pltpu.PrefetchScalarGridSpec(
            num_scalar_prefetch=2, grid=grid,
            in_specs=[pl.BlockSpec((tm, D_MODEL), lambda j, i, k, nd: (i, 0)),
                      pl.BlockSpec((PROJ_TN, D_MODEL), lambda j, i, k, nd: (j, 0)),
                      pl.BlockSpec((6, tm, LANES), lambda j, i, k, nd: (0, (i % n_tab) * nd[j], 0))],
            out_specs=pl.BlockSpec((tm, PROJ_TN), lambda j, i, k, nd: (i, j)),
            scratch_shapes=[pltpu.VMEM((D_MODEL, PROJ_TN), BF16)]),
        out_shape=jax.ShapeDtypeStruct((n, NCOL), F32),
        compiler_params=_cparams("arbitrary", "arbitrary"),
        name="in_proj",
    )(kinds, needs, x_bf, wt_bf, tabs)


def _rope_tables(pos):
    out = []
    lane = jnp.arange(LANES)
    for d in (HEAD_DIM, IDX_DIM):
        rot = d // 4
        half = rot // 2
        inv = ROPE_THETA ** (-jnp.arange(half, dtype=F32) / half)
        ang = pos.astype(F32)[:, None] * inv[None, :]
        cos = jnp.cos(ang)
        sin = jnp.sin(ang)
        li = lane % d
        ci = jnp.take(cos, li % half, axis=1)
        si = jnp.take(sin, li % half, axis=1)
        out.append(jnp.where(li[None] < rot, ci, 1.0))
        out.append(jnp.where((li[None] >= half) & (li[None] < rot), si, 0.0))
        out.append(jnp.where(li[None] < half, -si, 0.0))
    return jnp.stack(out, axis=0)


def _permute_w_in(w_in, l):
    wt = jnp.transpose(w_in, (2, 0, 1))[:, l, :].astype(BF16)
    offs = np.cumsum((0,) + IN_WIDTHS)
    nq, nkv, ng, dq, dkv, iq, ik, iw, cv, mg = [wt[offs[i]:offs[i + 1]] for i in range(10)]
    z = lambda n: jnp.zeros((n, wt.shape[1]), BF16)
    rows = [nq, nq, nkv, dq, dkv, cv, mg, iq, ik, iw, ng, z(LANES - 92), z(NCOL - C_MISC - LANES)]
    return jnp.concatenate(rows, axis=0)


def _conv_body(cv_ref, prev_ref, past_ref, w_ref, y_ref, cu_ref, s_ref, *, tiles_per_seq, tm):
    i = pl.program_id(0)
    b = cv_ref[:, 0:CONV_DIM]
    cu = cv_ref[:, CONV_DIM:2 * CONV_DIM] * cv_ref[:, 2 * CONV_DIM:3 * CONV_DIM]
    first = (i % tiles_per_seq) == 0
    prev = prev_ref[:, CONV_DIM:2 * CONV_DIM] * prev_ref[:, 2 * CONV_DIM:3 * CONV_DIM]
    s_ref[0:SUBLANES, :] = jnp.where(first, past_ref[0], prev)
    s_ref[SUBLANES:SUBLANES + tm, :] = cu
    y = (w_ref[0:1, :] * s_ref[pl.ds(SUBLANES - 2, tm), :] + w_ref[1:2, :] * s_ref[pl.ds(SUBLANES - 1, tm), :]
         + w_ref[2:3, :] * cu)
    y_ref[...] = b * y
    cu_ref[...] = cu


def _conv(hp, past8, conv_w8, tm, tiles_per_seq):
    n = hp.shape[0]
    cvb = C_CV // (3 * CONV_DIM)
    rb = tm // SUBLANES
    return pl.pallas_call(
        functools.partial(_conv_body, tiles_per_seq=tiles_per_seq, tm=tm),
        grid=(n // tm,),
        in_specs=[pl.BlockSpec((tm, 3 * CONV_DIM), lambda i: (i, cvb)),
                  pl.BlockSpec((SUBLANES, 3 * CONV_DIM), lambda i: (jnp.maximum(i * rb - 1, 0), cvb)),
                  pl.BlockSpec((1, SUBLANES, CONV_DIM), lambda i: (i // tiles_per_seq, 0, 0)),
                  pl.BlockSpec((SUBLANES, CONV_DIM), lambda i: (0, 0))],
        out_specs=[pl.BlockSpec((tm, CONV_DIM), lambda i: (i, 0)),
                   pl.BlockSpec((tm, CONV_DIM), lambda i: (i, 0))],
        out_shape=[jax.ShapeDtypeStruct((n, CONV_DIM), F32), jax.ShapeDtypeStruct((n, CONV_DIM), F32)],
        scratch_shapes=[pltpu.VMEM((tm + SUBLANES, CONV_DIM), F32)],
        compiler_params=_cparams("arbitrary"),
        name="short_conv",
    )(hp, hp, past8, conv_w8)


_PageLayout = collections.namedtuple("_PageLayout", "kind width col_block base layer nslot")


def _page_specs(layout, n_lead, kg_n=None):
    def spec(k):
        def imap(*a):
            ids, pt = a[:n_lead], a[-1]
            kg = ids[-1] % kg_n if kg_n else ids[-1]
            page = pt[ids[0], kg * PAGES_PER_STEP + k]
            if layout.kind == "cache":
                return (layout.layer, page, 0, layout.col_block, 0, 0)
            if layout.kind == "cache4t":
                return (layout.layer, page, 0, 0)
            cb = layout.col_block(*ids) if callable(layout.col_block) else layout.col_block
            return (page, 0, cb)
        shape = {"cache": (1, 1, PAGE, layout.nslot, 2, HEAD_DIM), "cache4t": (1, 1, layout.width, PAGE),
                 "cols": (1, PAGE, layout.width)}[layout.kind]
        return pl.BlockSpec(shape, imap)
    return [spec(k) for k in range(PAGES_PER_STEP)]


def _page_tile(layout, ref, j, width=LANES):
    if layout.kind == "cache":
        slot, g = divmod(layout.base + j, 2)
        return ref[0, 0, :, slot, g, :]
    return ref[0, :, j * width:(j + 1) * width]


def _chunk_rows(layout, ref, j, p):
    if layout.kind == "cache":
        slot, g = divmod(layout.base + j, 2)
        return ref[0, 0, pl.ds(p, SUBLANES, stride=CMP_STRIDE), slot, g, :]
    return ref[0, pl.ds(p, SUBLANES, stride=CMP_STRIDE), :]


def _cmp1_body(pt_ref, *refs, layout, n_inner):
    pages = refs[:PAGES_PER_STEP]
    w_ref, pe_ref = refs[PAGES_PER_STEP:PAGES_PER_STEP + 2]
    o_ref = refs[PAGES_PER_STEP + 2]
    for j in range(n_inner):
        slot = j // 2
        acc_a = jnp.zeros((PAGE, HEAD_DIM), F32)
        acc_b = jnp.zeros((PAGE, HEAD_DIM), F32)
        for p in range(CMP_STRIDE):
            xp = jnp.concatenate([_chunk_rows(layout, pg, j, p) for pg in pages], axis=0)
            xa = (xp + pe_ref[slot, p:p + 1, :]).astype(BF16)
            xb = (xp + pe_ref[slot, CMP_STRIDE + p:CMP_STRIDE + p + 1, :]).astype(BF16)
            acc_a = acc_a + jnp.dot(xa, w_ref[slot, p, :, 0:HEAD_DIM], preferred_element_type=F32)
            acc_b = acc_b + jnp.dot(xb, w_ref[slot, p, :, HEAD_DIM:2 * HEAD_DIM], preferred_element_type=F32)
        o_ref[0, :, j * 256:j * 256 + HEAD_DIM] = acc_a
        o_ref[0, :, j * 256 + HEAD_DIM:(j + 1) * 256] = acc_b


def _cmp1(pool, layout, pt, wcat, pe):
    nb, npg = pt.shape
    kg = npg // PAGES_PER_STEP
    if layout.kind == "cache":
        n_inner, n_sg, wsel, osel, ow = 4, 1, (lambda sg: 0), (lambda sg: 0), 1024
        wblk = 2
    else:
        cb = layout.col_block
        layout = layout._replace(width=HEAD_DIM, col_block=lambda b, sg, k: cb * 4 + sg)
        n_inner, n_sg, wsel, osel, ow = 1, 4, (lambda sg: sg // 2), (lambda sg: sg), 256
        wblk = 1
    return pl.pallas_call(
        functools.partial(_cmp1_body, layout=layout, n_inner=n_inner),
        grid_spec=pltpu.PrefetchScalarGridSpec(
            num_scalar_prefetch=1, grid=(nb, n_sg, kg),
            in_specs=_page_specs(layout, 3)
            + [pl.BlockSpec((wblk, 16, HEAD_DIM, 256), lambda b, sg, k, pt: (wsel(sg), 0, 0, 0)),
               pl.BlockSpec((wblk, CMP_LEN, HEAD_DIM), lambda b, sg, k, pt: (wsel(sg), 0, 0))],
            out_specs=pl.BlockSpec((1, PAGE, ow), lambda b, sg, k, pt: (b, k, osel(sg)))),
        out_shape=jax.ShapeDtypeStruct((nb, npg * SUBLANES, 1024), F32),
        compiler_params=_cparams("arbitrary", "arbitrary", "arbitrary"),
        name="nsa_compress1",
    )(pt, *([pool] * PAGES_PER_STEP), wcat, pe)


def _gelu_tanh(x):
    return 0.5 * x * (1.0 + jnp.tanh(np.sqrt(2.0 / np.pi).astype(np.float32) * (x + 0.044715 * (x * x * x))))


def _cmp2_body(ab_ref, tail_ref, pe_ref, w1_ref, w2_ref, o_ref, s_ref, *, nc, n_tail):
    row8 = lax.broadcasted_iota(jnp.int32, (SUBLANES, 1), 0)
    row16 = lax.broadcasted_iota(jnp.int32, (CMP_STRIDE, 1), 0)
    for sg in range(4):
        slot = sg // 2
        a = ab_ref[0, :, sg * 256:sg * 256 + HEAD_DIM]
        s_ref[0:nc, :] = ab_ref[0, :, sg * 256 + HEAD_DIM:(sg + 1) * 256]
        tb = jnp.zeros((SUBLANES, HEAD_DIM), F32)
        if n_tail:
            x8 = jnp.where(row8 < n_tail, tail_ref[:, sg * HEAD_DIM:(sg + 1) * HEAD_DIM], 0.0)
            x16 = jnp.concatenate([x8, jnp.zeros((CMP_STRIDE - SUBLANES, HEAD_DIM), F32)], axis=0)
            x16 = x16 + pe_ref[slot, CMP_STRIDE:CMP_LEN, :]
            t16 = jnp.zeros((CMP_STRIDE, HEAD_DIM), F32)
            for p in range(CMP_STRIDE):
                xm = jnp.where(row16 == p, x16, 0.0).astype(BF16)
                t16 = t16 + jnp.dot(xm, w1_ref[slot, CMP_STRIDE + p].astype(BF16), preferred_element_type=F32)
            tb = jnp.sum(t16, axis=0, keepdims=True) * jnp.where(row8 == 0, 1.0, 0.0)
        s_ref[nc:nc + SUBLANES, :] = tb
        pre = a + s_ref[pl.ds(1, nc), :]
        o_ref[0, :, sg * HEAD_DIM:(sg + 1) * HEAD_DIM] = jnp.dot(
            _gelu_tanh(pre).astype(BF16), w2_ref[slot].astype(BF16), preferred_element_type=F32)


def _cmp2(ab, tail, pe, w1r, w2, n_tail, tail_col_block):
    nb, nc, _ = ab.shape
    if tail is None:
        tail = jnp.zeros((nb * SUBLANES, 512), F32)
        tail_col_block = 0
    return pl.pallas_call(
        functools.partial(_cmp2_body, nc=nc, n_tail=n_tail),
        grid=(nb,),
        in_specs=[pl.BlockSpec((1, nc, 1024), lambda b: (b, 0, 0)),
                  pl.BlockSpec((SUBLANES, 512), lambda b: (b, tail_col_block)),
                  pl.BlockSpec((2, CMP_LEN, HEAD_DIM), lambda b: (0, 0, 0)),
                  pl.BlockSpec((2, CMP_LEN, HEAD_DIM, HEAD_DIM), lambda b: (0, 0, 0, 0)),
                  pl.BlockSpec((2, HEAD_DIM, HEAD_DIM), lambda b: (0, 0, 0))],
        out_specs=pl.BlockSpec((1, nc, 512), lambda b: (b, 0, 0)),
        out_shape=jax.ShapeDtypeStruct((nb, nc, 512), F32),
        scratch_shapes=[pltpu.VMEM((nc + SUBLANES, HEAD_DIM), F32)],
        compiler_params=_cparams("arbitrary"),
        name="nsa_compress2",
    )(ab, tail, pe, w1r, w2)


def _nsa1_body(*refs, tq, q0, n_cmp, ncp, n_sel, nselp, lpad, nwb, has_tail, win_k0):
    qraw_ref, qrot_ref, cmp_ref, ov_ref = refs[:4]
    wins = refs[4:4 + nwb]
    pos = 4 + nwb
    tail_ref = None
    if has_tail:
        tail_ref = refs[pos]
        pos += 1
    ocmp_ref, owin_ref, kmask_ref = refs[pos:pos + 3]
    qt = pl.program_id(1)
    r_per_g = NSA_HEADS // NSA_KV
    qpos = q0 + qt * tq + lax.broadcasted_iota(jnp.int32, (tq, 1), 0)

    jj = lax.broadcasted_iota(jnp.int32, (1, ncp), 1)
    cmask = (jj * CMP_STRIDE + (CMP_LEN - 1) <= qpos) & (jj < n_cmp)
    blk = lax.broadcasted_iota(jnp.int32, (1, nselp), 1)
    cur = qpos // SEL_BLOCK
    forced = (blk == 0) | (blk == cur) | (blk == cur - 1)
    e_row = lax.broadcasted_iota(jnp.int32, (KSTEP // SEL_BLOCK, KSTEP), 0)
    e_col = lax.broadcasted_iota(jnp.int32, (KSTEP // SEL_BLOCK, KSTEP), 1)
    expand = jnp.where(e_col // SEL_BLOCK == e_row, 1.0, 0.0).astype(BF16)
    lane128 = lax.broadcasted_iota(jnp.int32, (1, LANES), 1)
    for g in range(NSA_KV):
        kc = cmp_ref[0, :, g * HEAD_DIM:(g + 1) * HEAD_DIM].astype(BF16)
        vc = cmp_ref[0, :, (2 + g) * HEAD_DIM:(3 + g) * HEAD_DIM].astype(BF16)
        q = _stack_heads(qraw_ref, g, r_per_g)
        s = _dot_nt(q, kc) * ATT_SCALE + jnp.concatenate([jnp.where(cmask, 0.0, NEG_INF)] * r_per_g, axis=0)
        p, den = _softmax_parts(s)
        p = p / den
        o = jnp.dot(p.astype(BF16), vc, preferred_element_type=F32)
        psum = jnp.zeros((tq, ncp), F32)
        for r in range(r_per_g):
            h = g * r_per_g + r
            psum = psum + p[r * tq:(r + 1) * tq]
            ocmp_ref[:, h * HEAD_DIM:(h + 1) * HEAD_DIM] = o[r * tq:(r + 1) * tq]
        imp = jnp.dot(psum.astype(BF16), ov_ref[...].astype(BF16), preferred_element_type=F32)
        imp = jnp.where(forced, imp + FORCE_SCORE, imp)
        imp = jnp.where((blk <= cur) & (blk < n_sel), imp, NEG_INF)
        rank = jnp.zeros((tq, nselp), F32)
        for i in range(n_sel):
            vi = imp[:, i:i + 1]
            rank = rank + jnp.where(vi > imp, 1.0, jnp.where(vi == imp, jnp.where(blk > i, 1.0, 0.0), 0.0))
        sel = jnp.where((rank < min(SEL_TOP, n_sel)) & (blk < n_sel), 1.0, 0.0).astype(BF16)
        per = KSTEP // SEL_BLOCK
        for c in range(lpad // KSTEP):
            km = jnp.dot(sel[:, c * per:(c + 1) * per], expand, preferred_element_type=F32)
            kmask_ref[0, g, :, c * KSTEP:(c + 1) * KSTEP] = km.astype(BF16)
        if lpad % KSTEP:
            b0 = (lpad // KSTEP) * per
            km = jnp.where(lane128 < SEL_BLOCK, sel[:, b0:b0 + 1].astype(F32), 0.0)
            kmask_ref[0, g, :, (lpad // KSTEP) * KSTEP:lpad] = jnp.broadcast_to(km, (tq, LANES)).astype(BF16)

    nk = nwb * PAGE + (LANES if has_tail else 0)
    kk = lax.broadcasted_iota(jnp.int32, (1, nk), 1)
    if has_tail:
        kpos = jnp.where(kk < nwb * PAGE, win_k0 + kk, q0 + kk - nwb * PAGE)
    else:
        kpos = (qt - (nwb - 1)) * PAGE + kk
    rel = qpos - kpos
    wmask = (rel >= 0) & (rel < WINDOW) & (kpos >= 0)
    for g in range(NSA_KV):
        if has_tail:
            kparts = [w[0, 0, :, 0, g, :] for w in wins]
            vparts = [w[0, 0, :, 1, g, :] for w in wins]
        else:
            kparts = [w[:, g * HEAD_DIM:(g + 1) * HEAD_DIM] for w in wins]
            vparts = [w[:, (2 + g) * HEAD_DIM:(3 + g) * HEAD_DIM] for w in wins]
        if has_tail:
            zpad = jnp.zeros((LANES - SUBLANES, HEAD_DIM), F32)
            kparts += [tail_ref[:, g * HEAD_DIM:(g + 1) * HEAD_DIM], zpad]
            vparts += [tail_ref[:, (2 + g) * HEAD_DIM:(3 + g) * HEAD_DIM], zpad]
        kw = jnp.concatenate(kparts, axis=0).astype(BF16)
        vw = jnp.concatenate(vparts, axis=0).astype(BF16)
        q = _stack_heads(qrot_ref, g, r_per_g)
        s = _dot_nt(q, kw) * ATT_SCALE + jnp.concatenate([jnp.where(wmask, 0.0, NEG_INF)] * r_per_g, axis=0)
        p, den = _softmax_parts(s)
        o = jnp.dot((p / den).astype(BF16), vw, preferred_element_type=F32)
        for r in range(r_per_g):
            h = g * r_per_g + r
            owin_ref[:, h * HEAD_DIM:(h + 1) * HEAD_DIM] = o[r * tq:(r + 1) * tq]


def _nsa1(hp, cmp, ov, win_src, tail_src, *, nb, tq, qt_n, q0, n_cmp, n_sel, lpad, nwb, win_k0, win_layer):
    n = hp.shape[0]
    ncp = cmp.shape[1]
    nselp = ov.shape[1]
    has_tail = tail_src is not None
    rows = lambda b, t: b * qt_n + t
    in_specs = [pl.BlockSpec((tq, 1024), lambda b, t: (rows(b, t), C_QRAW // 1024)),
                pl.BlockSpec((tq, 1024), lambda b, t: (rows(b, t), C_QROT // 1024)),
                pl.BlockSpec((1, ncp, 512), lambda b, t: (b, 0, 0)),
                pl.BlockSpec((ncp, nselp), lambda b, t: (0, 0))]
    if has_tail:
        in_specs += [pl.BlockSpec((1, 1, PAGE, 2, NSA_KV, HEAD_DIM), lambda b, t, k=k: (win_layer, b, k, 0, 0, 0))
                     for k in range(nwb)]
        in_specs += [pl.BlockSpec((SUBLANES, 512), lambda b, t: (b, C_WIN // 512))]
        args = [win_src] * nwb + [tail_src]
    else:
        in_specs += [pl.BlockSpec((PAGE, 512),
                                  lambda b, t, k=k: (b * qt_n + jnp.maximum(t - (nwb - 1) + k, 0), C_WIN // 512))
                     for k in range(nwb)]
        args = [win_src] * nwb
    return pl.pallas_call(
        functools.partial(_nsa1_body, tq=tq, q0=q0, n_cmp=n_cmp, ncp=ncp, n_sel=n_sel, nselp=nselp, lpad=lpad,
                          nwb=nwb, has_tail=has_tail, win_k0=win_k0),
        grid=(nb, qt_n),
        in_specs=in_specs,
        out_specs=[pl.BlockSpec((tq, 1024), lambda b, t: (rows(b, t), 0)),
                   pl.BlockSpec((tq, 1024), lambda b, t: (rows(b, t), 0)),
                   pl.BlockSpec((1, NSA_KV, tq, lpad), lambda b, t: (b, 0, t, 0))],
        out_shape=[jax.ShapeDtypeStruct((n, 1024), F32), jax.ShapeDtypeStruct((n, 1024), F32),
                   jax.ShapeDtypeStruct((nb, NSA_KV, qt_n * tq, lpad), BF16)],
        compiler_params=_cparams("arbitrary", "arbitrary"),
        name="nsa_cmp_select_window",
    )(hp, hp, cmp, ov, *args)


def _mattn_body(pt_ref, *refs, layout, n_g, r_per_g, gm, tq, q0, kg_n, has_tail):
    q_ref, mask_ref = refs[:2]
    pages = refs[2:2 + PAGES_PER_STEP]
    pos = 2 + PAGES_PER_STEP
    if has_tail:
        tail_ref, tmask_ref = refs[pos:pos + 2]
        pos += 2
    o_ref, s_ref, v_ref = refs[pos:pos + 3]
    qt = pl.program_id(1)
    step = pl.program_id(2)
    two_pass = kg_n > 1 or has_tail
    kg = step
    qpos = q0 + qt * tq + lax.broadcasted_iota(jnp.int32, (tq, 1), 0)

    def scores(g, kt, kpos, mref):
        picked = jnp.where(mref[0, g if gm > 1 else 0].astype(F32) > 0.5, 0.0, NEG_INF)
        bias = jnp.concatenate([picked + jnp.where(kpos <= qpos, 0.0, NEG_INF)] * r_per_g, axis=0)
        return _dot_nt(_stack_heads(q_ref, g, r_per_g), kt) * ATT_SCALE + bias

    def emit(g, o):
        for r in range(r_per_g):
            h = g * r_per_g + r
            o_ref[:, h * HEAD_DIM:(h + 1) * HEAD_DIM] = o[r * tq:(r + 1) * tq]

    def main_keys(g):
        return jnp.concatenate([_page_tile(layout, pg, g) for pg in pages], axis=0).astype(BF16)

    def main_vals(g):
        return jnp.concatenate([_page_tile(layout, pg, n_g + g) for pg in pages], axis=0).astype(BF16)

    def tail_part(g, off):
        zpad = jnp.zeros((LANES - SUBLANES, HEAD_DIM), F32)
        return jnp.concatenate([tail_ref[:, (off + g) * HEAD_DIM:(off + g + 1) * HEAD_DIM], zpad], axis=0).astype(BF16)

    kpos = kg * KSTEP + lax.broadcasted_iota(jnp.int32, (1, KSTEP), 1)
    tpos = q0 + lax.broadcasted_iota(jnp.int32, (1, LANES), 1)

    if not two_pass:
        for g in range(n_g):
            p, den = _softmax_parts(scores(g, main_keys(g), kpos, mask_ref))
            emit(g, jnp.dot((p / den).astype(BF16), main_vals(g), preferred_element_type=F32))
        return

    for g in range(n_g):
        s_ref[g, kg] = scores(g, main_keys(g), kpos, mask_ref)
        v_ref[g, pl.ds(pl.multiple_of(kg * KSTEP, KSTEP), KSTEP), :] = main_vals(g)

    @pl.when(step == kg_n - 1)
    def _():
        for g in range(n_g):
            chunks = [s_ref[g, c] for c in range(kg_n)]
            vals = [v_ref[g, c * KSTEP:(c + 1) * KSTEP, :] for c in range(kg_n)]
            if has_tail:
                chunks.append(scores(g, tail_part(g, 0), tpos, tmask_ref))
                vals.append(tail_part(g, n_g))
            m = chunks[0].max(axis=-1, keepdims=True)
            for c in chunks[1:]:
                m = jnp.maximum(m, c.max(axis=-1, keepdims=True))
            m = jnp.where(m == NEG_INF, 0.0, m)
            exps = [jnp.exp(c - m) for c in chunks]
            den = exps[0].sum(axis=-1, keepdims=True)
            for e in exps[1:]:
                den = den + e.sum(axis=-1, keepdims=True)
            den = jnp.maximum(den, 1e-30)
            o = jnp.zeros((r_per_g * tq, HEAD_DIM), F32)
            for e, v in zip(exps, vals):
                o = o + jnp.dot((e / den).astype(BF16), v, preferred_element_type=F32)
            emit(g, o)


def _mattn(hp, q_col, n_heads, mask, pool, layout, pt, tail_src, tail_col_block, *, nb, tq, qt_n, q0, name):
    n = hp.shape[0]
    n_g = 2
    r_per_g = n_heads // n_g
    qw = n_heads * HEAD_DIM
    gm = mask.shape[1]
    kg_n = pt.shape[1] // PAGES_PER_STEP
    has_tail = tail_src is not None
    buffered = kg_n > 1 or has_tail
    in_specs = [pl.BlockSpec((tq, qw), lambda b, t, k, pt: (b * qt_n + t, q_col // qw)),
                pl.BlockSpec((1, gm, tq, KSTEP), lambda b, t, k, pt: (b, 0, t, k))]
    in_specs += _page_specs(layout, 3)
    args = [hp, mask] + [pool] * PAGES_PER_STEP
    if has_tail:
        in_specs += [pl.BlockSpec((SUBLANES, 512), lambda b, t, k, pt: (b, tail_col_block)),
                     pl.BlockSpec((1, gm, tq, LANES), lambda b, t, k, pt: (b, 0, t, kg_n * KSTEP // LANES))]
        args += [tail_src, mask]
    return pl.pallas_call(
        functools.partial(_mattn_body, layout=layout, n_g=n_g, r_per_g=r_per_g, gm=gm, tq=tq, q0=q0, kg_n=kg_n, has_tail=has_tail),
        grid_spec=pltpu.PrefetchScalarGridSpec(
            num_scalar_prefetch=1, grid=(nb, qt_n, kg_n),
            in_specs=in_specs,
            out_specs=pl.BlockSpec((tq, qw), lambda b, t, k, pt: (b * qt_n + t, 0)),
            scratch_shapes=[pltpu.VMEM((n_g, kg_n, r_per_g * tq, KSTEP) if buffered else (1, 1, SUBLANES, LANES), F32),
                            pltpu.VMEM((n_g, kg_n * KSTEP, HEAD_DIM) if buffered else (1, 2 * SUBLANES, LANES), BF16)]),
        out_shape=jax.ShapeDtypeStruct((n, qw), F32),
        compiler_params=_cparams("arbitrary", "arbitrary", "arbitrary"),
        name=name,
    )(pt, *args)


PACKED_ROWS = 2 * SUBLANES


def _topk_mask_packed_t(sc_t, n_keep):
    n_keys, n_q = sc_t.shape
    i16 = jnp.int16
    bits = pltpu.bitcast(sc_t + 0.0, jnp.int32)
    key = jnp.where(bits < 0, bits ^ jnp.int32(0x7FFFFFFF), bits)
    hi = jnp.right_shift(key, 16).astype(i16)
    lo = ((key & jnp.int32(0xFFFF)) - 32768).astype(i16)
    one_b, zero_b = jnp.asarray(1, BF16), jnp.asarray(0, BF16)
    i16_min, i16_max = jnp.asarray(-32768, i16), jnp.asarray(32767, i16)
    assert n_keys % PACKED_ROWS == 0 and n_keys // PACKED_ROWS <= 256

    def colsum(x01):
        parts = [x01[i * PACKED_ROWS:(i + 1) * PACKED_ROWS, :] for i in range(n_keys // PACKED_ROWS)]
        while len(parts) > 1:
            parts = [parts[i] + parts[i + 1] for i in range(0, len(parts), 2)]
        return jnp.sum(parts[0].astype(F32), axis=0, keepdims=True)

    def search(vals, need, nbits, start):
        def body(i, thr):
            cand = thr + jnp.left_shift(jnp.int32(1), nbits - 1 - i)
            cnt = colsum(jnp.where(vals >= cand.astype(i16), one_b, zero_b))
            return jnp.where(cnt >= need, cand, thr)
        return lax.fori_loop(0, nbits, body, jnp.full((1, n_q), start, jnp.int32))

    t_hi = search(hi, float(n_keep), 16, -32768).astype(i16)
    eq = hi == t_hi
    hi_gt = jnp.where(hi > t_hi, one_b, zero_b)
    lo_m = jnp.where(eq, lo, i16_min)
    t_lo = search(lo_m, n_keep - colsum(hi_gt), 16, -32768).astype(i16)
    gt = hi_gt + jnp.where(lo_m > t_lo, one_b, zero_b)
    need = n_keep - colsum(gt)
    idx = lax.broadcasted_iota(jnp.int32, (n_keys, 1), 0).astype(i16)
    tie_idx = jnp.where(eq, jnp.where(lo == t_lo, idx, i16_max), i16_max)
    nbits = int(np.ceil(np.log2(n_keys)))

    def ibody(i, c):
        cand = c + jnp.left_shift(jnp.int32(1), nbits - 1 - i)
        cnt = colsum(jnp.where(tie_idx < cand.astype(i16), one_b, zero_b))
        return jnp.where(cnt < need, cand, c)

    cut = lax.fori_loop(0, nbits, ibody, jnp.zeros((1, n_q), jnp.int32)).astype(i16)
    return jnp.maximum(gt, jnp.where(tie_idx <= cut, one_b, zero_b))


def _dsa1_body(pt_ref, *refs, layout, tq, q0, kg_n, lpad, has_tail, n_keep):
    iq_ref, misc_ref = refs[:2]
    pages = refs[2:2 + PAGES_PER_STEP]
    pos = 2 + PAGES_PER_STEP
    if has_tail:
        tail_ref = refs[pos]
        pos += 1
    mask_ref, sc_ref = refs[pos:pos + 2]
    qt = pl.program_id(1)
    kg = pl.program_id(2)
    qpos = q0 + qt * tq + lax.broadcasted_iota(jnp.int32, (tq, 1), 0)
    lane128 = lax.broadcasted_iota(jnp.int32, (1, LANES), 1)

    if kg_n == 1 and not has_tail and tq == LANES:
        kt = jnp.concatenate([_page_tile(layout, pg, 0, IDX_DIM) for pg in pages], axis=0).astype(BF16)
        misc_t = misc_ref[...].T
        kpos_c = lax.broadcasted_iota(jnp.int32, (KSTEP, 1), 0)
        qpos_r = q0 + qt * tq + lax.broadcasted_iota(jnp.int32, (1, tq), 1)
        acc = jnp.zeros((KSTEP, tq), F32)
        for h in range(IDX_HEADS):
            qi = iq_ref[:, h * IDX_DIM:(h + 1) * IDX_DIM].astype(BF16)
            acc = acc + misc_t[MISC_IW + h:MISC_IW + h + 1, :] * jnp.maximum(_dot_nt(kt, qi), 0.0)
        sel_t = _topk_mask_packed_t(jnp.where(kpos_c <= qpos_r, acc, NEG_INF), n_keep)
        for c in range(KSTEP // LANES):
            blk = sel_t[c * LANES:(c + 1) * LANES, :].astype(F32).T
            mask_ref[0, 0, :, c * LANES:(c + 1) * LANES] = blk.astype(BF16)
        return

    def scores(kt, kpos, keys_on_lanes=False):
        acc = jnp.zeros((tq, kpos.shape[1]), F32)
        for h in range(IDX_HEADS):
            qi = iq_ref[:, h * IDX_DIM:(h + 1) * IDX_DIM].astype(BF16)
            w = jnp.sum(jnp.where(lane128 == MISC_IW + h, misc_ref[...], 0.0), axis=-1, keepdims=True)
            qk = jnp.dot(qi, kt, preferred_element_type=F32) if keys_on_lanes else _dot_nt(qi, kt)
            acc = acc + w * jnp.maximum(qk, 0.0)
        return jnp.where(kpos <= qpos, acc, NEG_INF)

    kpos = kg * KSTEP + lax.broadcasted_iota(jnp.int32, (1, KSTEP), 1)
    if layout.kind == "cache4t":
        kt = jnp.concatenate([pg[0, 0] for pg in pages], axis=1).astype(BF16)
        sc_ref[kg] = scores(kt, kpos, keys_on_lanes=True)
    else:
        kt = jnp.concatenate([_page_tile(layout, pg, 0, IDX_DIM) for pg in pages], axis=0).astype(BF16)
        sc_ref[kg] = scores(kt, kpos)

    @pl.when(kg == kg_n - 1)
    def _():
        nch = sc_ref.shape[0]
        if has_tail:
            zpad = jnp.zeros((KSTEP - SUBLANES, IDX_DIM), F32)
            tk = jnp.concatenate([tail_ref[:, 0:IDX_DIM], zpad], axis=0).astype(BF16)
            tl = lax.broadcasted_iota(jnp.int32, (1, KSTEP), 1)
            sc_ref[kg_n] = jnp.where(tl < LANES, scores(tk, q0 + tl), NEG_INF)
        sc = sc_ref[...] + 0.0
        bits = pltpu.bitcast(sc, jnp.int32)
        key = jnp.where(bits < 0, bits ^ jnp.int32(0x7FFFFFFF), bits)
        int_min = jnp.int32(-2 ** 31)

        def count(pred):
            return jnp.sum(jnp.sum(jnp.where(pred, 1.0, 0.0), axis=-1, keepdims=True), axis=0, keepdims=True)

        def vbit(i, thr):
            cand = thr + jnp.left_shift(jnp.int32(1), 31 - i)
            return jnp.where(count(key >= cand) >= n_keep, cand, thr)

        thr = lax.fori_loop(0, 32, vbit, jnp.full((1, tq, 1), int_min, jnp.int32))
        gt = key > thr
        tie = key == thr
        need = n_keep - count(gt)
        idx = (lax.broadcasted_iota(jnp.int32, (nch, 1, KSTEP), 0) * KSTEP
               + lax.broadcasted_iota(jnp.int32, (nch, 1, KSTEP), 2))
        nbits = int(np.ceil(np.log2(nch * KSTEP)))

        def ibit(i, c):
            cand = c + jnp.left_shift(jnp.int32(1), nbits - 1 - i)
            return jnp.where(count(tie & (idx < cand)) < need, cand, c)

        cut = lax.fori_loop(0, nbits, ibit, jnp.zeros((1, tq, 1), jnp.int32))
        sel = jnp.where(gt | (tie & (idx <= cut)), 1.0, 0.0).astype(BF16)
        for c in range(kg_n):
            mask_ref[0, 0, :, c * KSTEP:(c + 1) * KSTEP] = sel[c]
        if has_tail:
            mask_ref[0, 0, :, kg_n * KSTEP:lpad] = sel[kg_n][:, 0:lpad - kg_n * KSTEP]


def _dsa1(hp, pool, layout, pt, tail_src, *, nb, tq, qt_n, q0, lpad, n_keep):
    kg_n = pt.shape[1] // PAGES_PER_STEP
    has_tail = tail_src is not None
    in_specs = [pl.BlockSpec((tq, 256), lambda b, t, k, pt: (b * qt_n + t, C_IQ // 256)),
                pl.BlockSpec((tq, LANES), lambda b, t, k, pt: (b * qt_n + t, C_MISC // LANES))]
    in_specs += _page_specs(layout, 3)
    args = [hp, hp] + [pool] * PAGES_PER_STEP
    if has_tail:
        in_specs += [pl.BlockSpec((SUBLANES, LANES), lambda b, t, k, pt: (b, C_MISC // LANES))]
        args += [tail_src]
    return pl.pallas_call(
        functools.partial(_dsa1_body, layout=layout, tq=tq, q0=q0, kg_n=kg_n, lpad=lpad, has_tail=has_tail,
                          n_keep=n_keep),
        grid_spec=pltpu.PrefetchScalarGridSpec(
            num_scalar_prefetch=1, grid=(nb, qt_n, kg_n),
            in_specs=in_specs,
            out_specs=pl.BlockSpec((1, 1, tq, lpad), lambda b, t, k, pt: (b, 0, t, 0)),
            scratch_shapes=[pltpu.VMEM((kg_n + (1 if has_tail else 0), tq, KSTEP), F32)]),
        out_shape=jax.ShapeDtypeStruct((nb, 1, qt_n * tq, lpad), BF16),
        compiler_params=_cparams("arbitrary", "arbitrary", "arbitrary"),
        name="dsa_indexer_topk",
    )(pt, *args)


def _mix_body(ocmp_ref, osel_ref, owin_ref, odsa_ref, conv_ref, misc_ref, mg0_ref, mg1_ref, mg2_ref,
              wa_ref, wb_ref, wc_ref, z_ref):
    lane128 = lax.broadcasted_iota(jnp.int32, (1, LANES), 1)
    misc = misc_ref[...]

    def gate(kind, h):
        return jnp.sum(jnp.where(lane128 == MISC_NG + kind * NSA_HEADS + h, misc, 0.0), axis=-1, keepdims=True)

    parts = []
    for h in range(NSA_HEADS):
        sl = slice(h * HEAD_DIM, (h + 1) * HEAD_DIM)
        parts.append((gate(0, h) * ocmp_ref[:, sl] + gate(1, h) * osel_ref[:, sl]
                      + gate(2, h) * owin_ref[:, sl]).astype(BF16))
    o_nsa = jnp.concatenate(parts, axis=1)
    p_a = jnp.dot(o_nsa, wa_ref[...], preferred_element_type=F32)
    p_b = jnp.dot(odsa_ref[...].astype(BF16), wb_ref[...], preferred_element_type=F32)
    p_c = jnp.dot(conv_ref[...].astype(BF16), wc_ref[...], preferred_element_type=F32)
    z_ref[...] = (mg0_ref[...] * p_a + mg1_ref[...] * p_b + mg2_ref[...] * p_c).astype(BF16)


def _mix(hp, o_cmp, o_sel, o_win, o_dsa, conv_out, wa, wb, wc, tm):
    n = hp.shape[0]
    row = lambda w, cb=0: pl.BlockSpec((tm, w), lambda i: (i, cb))
    full = lambda a: pl.BlockSpec(a.shape, lambda i: (0, 0))
    return pl.pallas_call(
        _mix_body,
        grid=(n // tm,),
        in_specs=[row(1024), row(1024), row(1024), row(512), row(512), row(LANES, C_MISC // LANES),
                  row(D_MODEL, C_MG // D_MODEL), row(D_MODEL, C_MG // D_MODEL + 1), row(D_MODEL, C_MG // D_MODEL + 2),
                  full(wa), full(wb), full(wc)],
        out_specs=row(D_MODEL),
        out_shape=jax.ShapeDtypeStruct((n, D_MODEL), BF16),
        compiler_params=_cparams("arbitrary"),
        name="branch_merge",
    )(o_cmp, o_sel, o_win, o_dsa, conv_out, hp, hp, hp, hp, wa, wb, wc)


def _outln_body(z_ref, wo_ref, x_ref, g_ref, b_ref, wr_ref, rb_ref, x1_ref, gate_ref, *, tm):
    y = jnp.dot(z_ref[...], wo_ref[...], preferred_element_type=F32)
    x1 = _layernorm(ALPHA * x_ref[...] + y, g_ref[...], b_ref[...])
    x1_ref[...] = x1
    x_hi = x1.astype(BF16)
    x_lo = (x1 - x_hi.astype(F32)).astype(BF16)
    w_hi = wr_ref[...].astype(BF16)
    w_lo = (wr_ref[...] - w_hi.astype(F32)).astype(BF16)
    aff = _sigmoid(_dot_nt(w_hi, x_hi) + (_dot_nt(w_hi, x_lo) + _dot_nt(w_lo, x_hi)))
    biased = aff + rb_ref[:, 0:1]
    rows = [biased[e:e + 1, :] for e in range(N_EXPERTS)]
    best = None
    g_best = jnp.zeros((1, tm), jnp.int32)
    for g in range(N_GROUPS):
        v = rows[g * EXPERTS_PER_GROUP:(g + 1) * EXPERTS_PER_GROUP]
        score = None
        for a in range(EXPERTS_PER_GROUP):
            for c in range(a + 1, EXPERTS_PER_GROUP):
                pair = v[a] + v[c]
                score = pair if score is None else jnp.maximum(score, pair)
        if best is None:
            best = score
        else:
            better = score > best
            best = jnp.where(better, score, best)
            g_best = jnp.where(better, g, g_best)
    sel_rows = []
    for e in range(N_EXPERTS):
        g = e // EXPERTS_PER_GROUP
        rank = jnp.zeros((1, tm), F32)
        for o in range(g * EXPERTS_PER_GROUP, (g + 1) * EXPERTS_PER_GROUP):
            if o == e:
                continue
            beats = (rows[o] > rows[e]) | ((rows[o] == rows[e]) & (o < e))
            rank = rank + jnp.where(beats, 1.0, 0.0)
        sel_rows.append(jnp.where((g_best == g) & (rank < 2), aff[e:e + 1, :], 0.0))
    tot = sel_rows[0]
    for e in range(1, N_EXPERTS):
        tot = tot + sel_rows[e]
    gate_t = jnp.concatenate(sel_rows + [jnp.zeros((LANES - N_EXPERTS, tm), F32)], axis=0) / tot
    gate_ref[...] = gate_t.T


def _outln(z, wo, x, g, b, wr_t, rb, tm):
    n = z.shape[0]
    row = lambda w: pl.BlockSpec((tm, w), lambda i: (i, 0))
    full = lambda a: pl.BlockSpec(a.shape, lambda i: (0, 0))
    return pl.pallas_call(
        functools.partial(_outln_body, tm=tm),
        grid=(n // tm,),
        in_specs=[row(D_MODEL), full(wo), row(D_MODEL), full(g), full(b), full(wr_t), full(rb)],
        out_specs=[row(D_MODEL), row(LANES)],
        out_shape=[jax.ShapeDtypeStruct((n, D_MODEL), F32), jax.ShapeDtypeStruct((n, LANES), F32)],
        compiler_params=_cparams("arbitrary"),
        name="out_proj_ln_router",
    )(z, wo, x, g, b, wr_t, rb)


def _moe_body(x_ref, gate_ref, wg_ref, wu_ref, wd_ref, g_ref, b_ref, o_ref, xb_ref, acc_ref):
    e = pl.program_id(1)

    @pl.when(e == 0)
    def _():
        xb_ref[...] = x_ref[...].astype(BF16)
        acc_ref[...] = jnp.zeros(acc_ref.shape, F32)

    lane128 = lax.broadcasted_iota(jnp.int32, (1, LANES), 1)
    gcol = jnp.sum(jnp.where(lane128 == e, gate_ref[...], 0.0), axis=-1, keepdims=True)
    xb = xb_ref[...]
    hg = jnp.dot(xb, wg_ref[0], preferred_element_type=F32)
    hu = jnp.dot(xb, wu_ref[0], preferred_element_type=F32)
    h = (hg * _sigmoid(hg)) * hu * gcol
    acc_ref[...] += jnp.dot(h.astype(BF16), wd_ref[0], preferred_element_type=F32)

    @pl.when(e == N_EXPERTS - 1)
    def _():
        o_ref[...] = _layernorm(ALPHA * x_ref[...] + acc_ref[...], g_ref[...], b_ref[...])


def _moe(x1, gate, wg, wu, wd, g, b, tm):
    n = x1.shape[0]
    return pl.pallas_call(
        _moe_body,
        grid=(n // tm, N_EXPERTS),
        in_specs=[pl.BlockSpec((tm, D_MODEL), lambda i, e: (i, 0)),
                  pl.BlockSpec((tm, LANES), lambda i, e: (i, 0)),
                  pl.BlockSpec((1, D_MODEL, D_FF), lambda i, e: (e, 0, 0)),
                  pl.BlockSpec((1, D_MODEL, D_FF), lambda i, e: (e, 0, 0)),
                  pl.BlockSpec((1, D_FF, D_MODEL), lambda i, e: (e, 0, 0)),
                  pl.BlockSpec((1, D_MODEL), lambda i, e: (0, 0)),
                  pl.BlockSpec((1, D_MODEL), lambda i, e: (0, 0))],
        out_specs=pl.BlockSpec((tm, D_MODEL), lambda i, e: (i, 0)),
        out_shape=jax.ShapeDtypeStruct((n, D_MODEL), F32),
        scratch_shapes=[pltpu.VMEM((tm, D_MODEL), BF16), pltpu.VMEM((tm, D_MODEL), F32)],
        compiler_params=_cparams("arbitrary", "arbitrary"),
        name="moe_ln",
    )(x1, gate, wg, wu, wd, g, b)


def _overlap_matrix(ncp, nselp):
    cs = np.arange(ncp)[:, None] * CMP_STRIDE
    ss = np.arange(nselp)[None, :] * SEL_BLOCK
    return jnp.asarray(((cs < ss + SEL_BLOCK) & (cs + CMP_LEN > ss)).astype(np.float32))


class _Group:
    def __init__(self, nb, t_real, t_pad, q0, past_len, tm, tq):
        self.nb, self.t_real, self.t_pad, self.q0, self.past_len, self.tm, self.tq = nb, t_real, t_pad, q0, past_len, tm, tq
        self.paged = past_len > 0
        self.lp = past_len if self.paged else t_pad
        self.ltot = self.lp + (t_real if self.paged else 0)
        self.lpad = self.lp + (LANES if self.paged else 0)
        self.qt_n = t_pad // tq
        n_chunks = -(-self.ltot // CMP_STRIDE)
        self.n_cmp = n_chunks - CMP_LEN // CMP_STRIDE + 1
        self.n_sel = -(-self.ltot // SEL_BLOCK)
        self.nselp = -(-self.n_sel // LANES) * LANES
        self.n_keep = min(DSA_TOPK, self.ltot // 4)


def _mixer(gp, x, lw, caches):
    hp = _proj(x.astype(BF16), lw["w_in"], gp.tabs, lw["kinds"], min(2 * gp.tm, gp.nb * gp.t_pad, gp.tabs.shape[1]))
    nb, tq, qt_n, q0 = gp.nb, gp.tq, gp.qt_n, gp.q0
    if gp.paged:
        l, nsa_pool, dsa_pool, kidx_pool, win_src, s_conv, pt = caches
        cmp_lay = _PageLayout("cache", 0, 0, base=0, layer=l, nslot=2)
        slc_lay = _PageLayout("cache", 0, 1, base=0, layer=l, nslot=2)
        dsa_lay = _PageLayout("cache", 0, 0, base=0, layer=l, nslot=2)
        kidx_pool = jnp.transpose(kidx_pool, (0, 1, 3, 2))
        kidx_lay = _PageLayout("cache4t", IDX_DIM, 0, base=0, layer=l, nslot=0)
        tail = hp
        nwb, win_k0, win_layer = WINDOW // PAGE, PAST_LEN - WINDOW, l
        past8 = jnp.concatenate([jnp.zeros((nb, SUBLANES - 2, CONV_DIM), F32), s_conv[l]], axis=1)
    else:
        pt = gp.pt
        nsa_pool = dsa_pool = kidx_pool = hp.reshape(nb * gp.t_pad // PAGE, PAGE, NCOL)
        cmp_lay = _PageLayout("cols", 512, C_CMP // 512, base=0, layer=0, nslot=0)
        slc_lay = _PageLayout("cols", 512, C_SLC // 512, base=0, layer=0, nslot=0)
        dsa_lay = _PageLayout("cols", 512, C_DSA // 512, base=0, layer=0, nslot=0)
        kidx_lay = _PageLayout("cols", LANES, C_MISC // LANES, base=0, layer=0, nslot=0)
        tail = None
        win_src = hp
        nwb, win_k0, win_layer = WINDOW // PAGE + 1, 0, 0
        past8 = jnp.zeros((nb, SUBLANES, CONV_DIM), F32)

    conv_out, cu = _conv(hp, past8, lw["conv_w8"], min(gp.tm, gp.t_pad), gp.t_pad // min(gp.tm, gp.t_pad))

    ab = _cmp1(nsa_pool, cmp_lay, pt, lw["wcat"], lw["pe"])
    cmp = _cmp2(ab, tail, lw["pe"], lw["w1r"], lw["phi_w2"], gp.t_real if gp.paged else 0, C_CMP // 512)
    o_cmp, o_win, kmask = _nsa1(hp, cmp, gp.ov, win_src, tail, nb=nb, tq=tq, qt_n=qt_n, q0=q0, n_cmp=gp.n_cmp,
                                n_sel=gp.n_sel, lpad=gp.lpad, nwb=nwb, win_k0=win_k0, win_layer=win_layer)
    o_sel = _mattn(hp, C_QROT, NSA_HEADS, kmask, nsa_pool, slc_lay, pt, tail, C_SLC // 512,
                   nb=nb, tq=tq, qt_n=qt_n, q0=q0, name="nsa_selected_attn")
    dmask = _dsa1(hp, kidx_pool, kidx_lay, pt, tail, nb=nb, tq=tq, qt_n=qt_n, q0=q0, lpad=gp.lpad,
                  n_keep=gp.n_keep)
    o_dsa = _mattn(hp, C_DQ, DSA_HEADS, dmask, dsa_pool, dsa_lay, pt, tail, C_DSA // 512,
                   nb=nb, tq=tq, qt_n=qt_n, q0=q0, name="dsa_topk_attn")
    z = _mix(hp, o_cmp, o_sel, o_win, o_dsa, conv_out, lw["w_a"], lw["w_b"], lw["w_c"], min(gp.tm, 256))
    return z, hp, cu


def _layer(gp, x, lw, caches):
    z, hp, cu = _mixer(gp, x, lw, caches)
    tm2 = min(gp.tm, 256)
    x1, gate = _outln(z, lw["w_o"], x, lw["ln_mix_g"], lw["ln_mix_b"], lw["wr_t"], lw["rb"], tm2)
    x2 = _moe(x1, gate, lw["w_eg"], lw["w_eu"], lw["w_ed"], lw["ln_ffn_g"], lw["ln_ffn_b"], min(gp.tm, 512))
    return x2, hp, cu


def _layer_weights(l, w_in, nsa_phi_pos, nsa_phi_w1, nsa_phi_w2, conv_w, w_br_a, w_br_b, w_br_c, w_out,
                   ln_mix_g, ln_mix_b, ln_ffn_g, ln_ffn_b, w_router, router_bias, w_e_gate, w_e_up, w_e_down):
    w1r = nsa_phi_w1[l].reshape(2, CMP_LEN, HEAD_DIM, HEAD_DIM)
    wcat = jnp.concatenate([w1r[:, :CMP_STRIDE], w1r[:, CMP_STRIDE:]], axis=-1).astype(BF16)
    return dict(
        w_in=_permute_w_in(w_in, l), kinds=jnp.asarray(_col_kinds()),
        w1r=w1r, wcat=wcat, pe=nsa_phi_pos[l], phi_w2=nsa_phi_w2[l],
        conv_w8=jnp.concatenate([conv_w[l], jnp.zeros((SUBLANES - 3, CONV_DIM), F32)], axis=0),
        w_a=w_br_a[l].astype(BF16), w_b=w_br_b[l].astype(BF16), w_c=w_br_c[l].astype(BF16),
        w_o=w_out[l].astype(BF16),
        ln_mix_g=ln_mix_g[l][None], ln_mix_b=ln_mix_b[l][None],
        ln_ffn_g=ln_ffn_g[l][None], ln_ffn_b=ln_ffn_b[l][None],
        wr_t=w_router.T, rb=jnp.broadcast_to(router_bias[:, None], (N_EXPERTS, LANES)),
        w_eg=w_e_gate[l].astype(BF16), w_eu=w_e_up[l].astype(BF16), w_ed=w_e_down[l].astype(BF16))


def kernel(x_prompt, x_sample, cache_nsa_kv, cache_dsa_kv, cache_dsa_kidx, state_nsa_win, state_conv, page_table,
           w_in, nsa_phi_pos, nsa_phi_w1, nsa_phi_w2, conv_w, w_br_a, w_br_b, w_br_c, w_out, ln_mix_g, ln_mix_b,
           ln_ffn_g, ln_ffn_b, w_router, router_bias, w_e_gate, w_e_up, w_e_down):
    bp, tp, _ = x_prompt.shape
    bs, ts, _ = x_sample.shape
    ts_pad = SUBLANES

    gp_p = _Group(bp, tp, tp, 0, 0, tm=512, tq=128)
    gp_p.tabs = _rope_tables(jnp.arange(tp, dtype=jnp.int32))
    gp_p.pt = jnp.arange(bp * tp // PAGE, dtype=jnp.int32).reshape(bp, tp // PAGE)
    gp_p.ov = _overlap_matrix(LANES, gp_p.nselp)
    gp_s = _Group(bs, ts, ts_pad, PAST_LEN, PAST_LEN, tm=bs * ts_pad, tq=ts_pad)
    pos_s = PAST_LEN + jnp.arange(ts_pad, dtype=jnp.int32)
    gp_s.tabs = jnp.tile(_rope_tables(pos_s), (1, bs, 1))
    gp_s.ov = _overlap_matrix(PAST_LEN // CMP_STRIDE, gp_s.nselp)

    xp = x_prompt.reshape(bp * tp, D_MODEL)
    xs = jnp.concatenate([x_sample, jnp.zeros((bs, ts_pad - ts, D_MODEL), F32)], axis=1).reshape(bs * ts_pad, D_MODEL)

    outs_p = [[] for _ in range(5)]
    outs_s = [[] for _ in range(5)]
    for l in range(DEPTH):
        lw = _layer_weights(l, w_in, nsa_phi_pos, nsa_phi_w1, nsa_phi_w2, conv_w, w_br_a, w_br_b, w_br_c, w_out,
                            ln_mix_g, ln_mix_b, ln_ffn_g, ln_ffn_b, w_router, router_bias, w_e_gate, w_e_up, w_e_down)
        xp, hp_p, cu_p = _layer(gp_p, xp, lw, None)
        xs, hp_s, cu_s = _layer(gp_s, xs, lw, (l, cache_nsa_kv, cache_dsa_kv, cache_dsa_kidx,
                                                 state_nsa_win, state_conv, page_table))
        h3 = hp_p.reshape(bp, tp, NCOL)
        outs_p[0].append(h3[:, :, C_CMP:C_CMP + 1024].reshape(bp, tp, 4, NSA_KV, HEAD_DIM))
        outs_p[1].append(h3[:, :, C_DSA:C_DSA + 512].reshape(bp, tp, 2, DSA_KV, HEAD_DIM))
        outs_p[2].append(h3[:, :, C_MISC:C_MISC + IDX_DIM])
        outs_p[3].append(h3[:, tp - min(WINDOW, tp):, C_WIN:C_WIN + 512].reshape(bp, min(WINDOW, tp), 2, NSA_KV, HEAD_DIM))
        outs_p[4].append(cu_p.reshape(bp, tp, CONV_DIM)[:, tp - 2:])
        s3 = hp_s.reshape(bs, ts_pad, NCOL)[:, :ts]
        outs_s[0].append(s3[:, :, C_CMP:C_CMP + 1024].reshape(bs, ts, 4, NSA_KV, HEAD_DIM))
        outs_s[1].append(s3[:, :, C_DSA:C_DSA + 512].reshape(bs, ts, 2, DSA_KV, HEAD_DIM))
        outs_s[2].append(s3[:, :, C_MISC:C_MISC + IDX_DIM])
        win_new = s3[:, :, C_WIN:C_WIN + 512].reshape(bs, ts, 2, NSA_KV, HEAD_DIM)
        wb = state_nsa_win.shape[2]
        outs_s[3].append(jnp.concatenate([state_nsa_win[l], win_new], axis=1)[:, -wb:])
        ext = jnp.concatenate([state_conv[l], cu_s.reshape(bs, ts_pad, CONV_DIM)[:, :ts]], axis=1)
        outs_s[4].append(ext[:, -2:])
    sp = [jnp.stack(a, axis=0) for a in outs_p]
    ss = [jnp.stack(a, axis=0) for a in outs_s]
    y_p = xp.reshape(bp, tp, D_MODEL)
    y_s = xs.reshape(bs, ts_pad, D_MODEL)[:, :ts]
    return (y_p, y_s, sp[0], ss[0], sp[1], ss[1], sp[2], ss[2], sp[3], ss[3], sp[4], ss[4])
```

```python
import collections
import functools

import numpy as np
import jax
import jax.numpy as jnp
from jax import lax
from jax.experimental import pallas as pl
from jax.experimental.pallas import tpu as pltpu

F32 = jnp.float32
BF16 = jnp.bfloat16
NEG_INF = float("-inf")

D_MODEL = 2048
DEPTH = 2
PAST_LEN = 16384
PAGE = 128
HEAD_DIM = 128
ROPE_THETA = 500000.0
NSA_HEADS = 8
NSA_KV = 2
CMP_LEN = 32
CMP_STRIDE = 16
SEL_BLOCK = 64
SEL_TOP = 16
WINDOW = 512
FORCE_SCORE = 1e4
DSA_HEADS = 4
DSA_KV = 2
IDX_HEADS = 4
IDX_DIM = 64
DSA_TOPK = 256
CONV_DIM = 512
N_EXPERTS = 16
N_GROUPS = 4
EXPERTS_PER_GROUP = 4
D_FF = 512
LN_EPS = 1e-5
ALPHA = (2 * DEPTH) ** 0.25
IN_WIDTHS = (1024, 1536, 24, 512, 512, 256, 64, 4, 1536, 6144)
ATT_SCALE = HEAD_DIM ** -0.5

LANES = 128
SUBLANES = 8
VMEM_LIMIT = 56 * 1024 * 1024

C_QRAW = 0
C_QROT = 1024
C_CMP = 2048
C_SLC = 2560
C_WIN = 3072
C_DQ = 3584
C_DSA = 4096
C_CV = 4608
C_MG = 6144
C_IQ = 12288
C_MISC = 12544
NCOL = 12800
MISC_IW = 64
MISC_NG = 68
PROJ_TN = 1280
PAGES_PER_STEP = 16
KSTEP = PAGES_PER_STEP * PAGE

K_PLAIN, K_ROPE128, K_ROPE64, K_SIGMOID, K_MISC = 0, 1, 2, 3, 4


def _col_kinds():
    kinds = np.zeros(NCOL // LANES, np.int32)

    def mark(c0, n, k):
        kinds[c0 // LANES:(c0 + n) // LANES] = k

    mark(C_QROT, 1024, K_ROPE128)
    mark(C_SLC, 256, K_ROPE128)
    mark(C_WIN, 256, K_ROPE128)
    mark(C_DQ, 512, K_ROPE128)
    mark(C_DSA, 256, K_ROPE128)
    mark(C_IQ, 256, K_ROPE64)
    mark(C_MISC, 128, K_MISC)
    mark(C_MG, 6144, K_SIGMOID)
    return kinds


def _cparams(*sem):
    return pltpu.CompilerParams(dimension_semantics=sem, vmem_limit_bytes=VMEM_LIMIT)


def _sigmoid(x):
    return 1.0 / (1.0 + jnp.exp(-x))


def _layernorm(x, g, b):
    mu = jnp.mean(x, axis=-1, keepdims=True)
    xc = x - mu
    var = jnp.mean(xc * xc, axis=-1, keepdims=True)
    return xc * lax.rsqrt(var + LN_EPS) * g + b


def _dot_nt(a, b, precision=None):
    return lax.dot_general(a, b, (((1,), (1,)), ((), ())), preferred_element_type=F32, precision=precision)


def _softmax_parts(s):
    m = jnp.max(s, axis=-1, keepdims=True)
    m = jnp.where(m == NEG_INF, 0.0, m)
    p = jnp.exp(s - m)
    return p, jnp.maximum(jnp.sum(p, axis=-1, keepdims=True), 1e-30)


def _stack_heads(q_ref, g, r_per_g):
    return jnp.concatenate([q_ref[:, (g * r_per_g + r) * HEAD_DIM:(g * r_per_g + r + 1) * HEAD_DIM]
                            for r in range(r_per_g)], axis=0).astype(BF16)


def _proj_body(kinds_ref, needs_ref, x_ref, wt_ref, tab_ref, o_ref, w_ref):
    nsub = PROJ_TN // LANES
    j = pl.program_id(0)

    @pl.when(pl.program_id(1) == 0)
    def _():
        for s in range(nsub):
            w_ref[:, s * LANES:(s + 1) * LANES] = wt_ref[s * LANES:(s + 1) * LANES, :].astype(F32).T.astype(BF16)

    h = jnp.dot(x_ref[...], w_ref[...], preferred_element_type=F32)
    lane = lax.broadcasted_iota(jnp.int32, (1, LANES), 1)

    def rope(hs, t0, sh):
        return (hs * tab_ref[t0] + pltpu.roll(hs, sh, 1) * tab_ref[t0 + 1]
                + pltpu.roll(hs, LANES - sh, 1) * tab_ref[t0 + 2])

    for s in range(nsub):
        kind = kinds_ref[j * nsub + s]
        hs = h[:, s * LANES:(s + 1) * LANES]
        sl = slice(s * LANES, (s + 1) * LANES)

        @pl.when(kind == K_PLAIN)
        def _():
            o_ref[:, sl] = hs

        @pl.when(kind == K_ROPE128)
        def _():
            o_ref[:, sl] = rope(hs, 0, 16)

        @pl.when(kind == K_ROPE64)
        def _():
            o_ref[:, sl] = rope(hs, 3, 8)

        @pl.when(kind == K_SIGMOID)
        def _():
            o_ref[:, sl] = _sigmoid(hs)

        @pl.when(kind == K_MISC)
        def _():
            r = rope(hs, 3, 8)
            o_ref[:, sl] = jnp.where(lane < MISC_IW, r,
                                     jnp.where(lane < MISC_NG, hs * (IDX_HEADS ** -0.5),
                                               jnp.where(lane < MISC_NG + 24, _sigmoid(hs), hs)))


def _proj(x_bf, wt_bf, tabs, kinds, tm):
    n = x_bf.shape[0]
    n_tab = tabs.shape[1] // tm
    grid = (NCOL // PROJ_TN, n // tm)
    rotary = np.isin(_col_kinds().reshape(-1, PROJ_TN // LANES), (K_ROPE128, K_ROPE64, K_MISC)).any(axis=1)
    needs = jnp.asarray(rotary.astype(np.int32))
    return pl.pallas_call(
        _proj_body,
        grid_spec=pltpu.PrefetchScalarGridSpec(
            num_scalar_prefetch=2, grid=grid,
            in_specs=[pl.BlockSpec((tm, D_MODEL), lambda j, i, k, nd: (i, 0)),
                      pl.BlockSpec((PROJ_TN, D_MODEL), lambda j, i, k, nd: (j, 0)),
                      pl.BlockSpec((6, tm, LANES), lambda j, i, k, nd: (0, (i % n_tab) * nd[j], 0))],
            out_specs=pl.BlockSpec((tm, PROJ_TN), lambda j, i, k, nd: (i, j)),
            scratch_shapes=[pltpu.VMEM((D_MODEL, PROJ_TN), BF16)]),
        out_shape=jax.ShapeDtypeStruct((n, NCOL), F32),
        compiler_params=_cparams("arbitrary", "arbitrary"),
        name="in_proj",
    )(kinds, needs, x_bf, wt_bf, tabs)


def _rope_tables(pos):
    out = []
    lane = jnp.arange(LANES)
    for d in (HEAD_DIM, IDX_DIM):
        rot = d // 4
        half = rot // 2
        inv = ROPE_THETA ** (-jnp.arange(half, dtype=F32) / half)
        ang = pos.astype(F32)[:, None] * inv[None, :]
        cos = jnp.cos(ang)
        sin = jnp.sin(ang)
        li = lane % d
        ci = jnp.take(cos, li % half, axis=1)
        si = jnp.take(sin, li % half, axis=1)
        out.append(jnp.where(li[None] < rot, ci, 1.0))
        out.append(jnp.where((li[None] >= half) & (li[None] < rot), si, 0.0))
        out.append(jnp.where(li[None] < half, -si, 0.0))
    return jnp.stack(out, axis=0)


def _permute_w_in(w_in, l):
    wt = jnp.transpose(w_in, (2, 0, 1))[:, l, :].astype(BF16)
    offs = np.cumsum((0,) + IN_WIDTHS)
    nq, nkv, ng, dq, dkv, iq, ik, iw, cv, mg = [wt[offs[i]:offs[i + 1]] for i in range(10)]
    z = lambda n: jnp.zeros((n, wt.shape[1]), BF16)
    rows = [nq, nq, nkv, dq, dkv, cv, mg, iq, ik, iw, ng, z(LANES - 92), z(NCOL - C_MISC - LANES)]
    return jnp.concatenate(rows, axis=0)


def _conv_body(cv_ref, prev_ref, past_ref, w_ref, y_ref, cu_ref, s_ref, *, tiles_per_seq, tm):
    i = pl.program_id(0)
    b = cv_ref[:, 0:CONV_DIM]
    cu = cv_ref[:, CONV_DIM:2 * CONV_DIM] * cv_ref[:, 2 * CONV_DIM:3 * CONV_DIM]
    first = (i % tiles_per_seq) == 0
    prev = prev_ref[:, CONV_DIM:2 * CONV_DIM] * prev_ref[:, 2 * CONV_DIM:3 * CONV_DIM]
    s_ref[0:SUBLANES, :] = jnp.where(first, past_ref[0], prev)
    s_ref[SUBLANES:SUBLANES + tm, :] = cu
    y = (w_ref[0:1, :] * s_ref[pl.ds(SUBLANES - 2, tm), :] + w_ref[1:2, :] * s_ref[pl.ds(SUBLANES - 1, tm), :]
         + w_ref[2:3, :] * cu)
    y_ref[...] = b * y
    cu_ref[...] = cu


def _conv(hp, past8, conv_w8, tm, tiles_per_seq):
    n = hp.shape[0]
    cvb = C_CV // (3 * CONV_DIM)
    rb = tm // SUBLANES
    return pl.pallas_call(
        functools.partial(_conv_body, tiles_per_seq=tiles_per_seq, tm=tm),
        grid=(n // tm,),
        in_specs=[pl.BlockSpec((tm, 3 * CONV_DIM), lambda i: (i, cvb)),
                  pl.BlockSpec((SUBLANES, 3 * CONV_DIM), lambda i: (jnp.maximum(i * rb - 1, 0), cvb)),
                  pl.BlockSpec((1, SUBLANES, CONV_DIM), lambda i: (i // tiles_per_seq, 0, 0)),
                  pl.BlockSpec((SUBLANES, CONV_DIM), lambda i: (0, 0))],
        out_specs=[pl.BlockSpec((tm, CONV_DIM), lambda i: (i, 0)),
                   pl.BlockSpec((tm, CONV_DIM), lambda i: (i, 0))],
        out_shape=[jax.ShapeDtypeStruct((n, CONV_DIM), F32), jax.ShapeDtypeStruct((n, CONV_DIM), F32)],
        scratch_shapes=[pltpu.VMEM((tm + SUBLANES, CONV_DIM), F32)],
        compiler_params=_cparams("arbitrary"),
        name="short_conv",
    )(hp, hp, past8, conv_w8)


_PageLayout = collections.namedtuple("_PageLayout", "kind width col_block base layer nslot")


def _page_specs(layout, n_lead, kg_n=None):
    def spec(k):
        def imap(*a):
            ids, pt = a[:n_lead], a[-1]
            kg = ids[-1] % kg_n if kg_n else ids[-1]
            page = pt[ids[0], kg * PAGES_PER_STEP + k]
            if layout.kind == "cache":
                return (layout.layer, page, 0, layout.col_block, 0, 0)
            if layout.kind == "cache4t":
                return (layout.layer, page, 0, 0)
            cb = layout.col_block(*ids) if callable(layout.col_block) else layout.col_block
            return (page, 0, cb)
        shape = {"cache": (1, 1, PAGE, layout.nslot, 2, HEAD_DIM), "cache4t": (1, 1, layout.width, PAGE),
                 "cols": (1, PAGE, layout.width)}[layout.kind]
        return pl.BlockSpec(shape, imap)
    return [spec(k) for k in range(PAGES_PER_STEP)]


def _page_tile(layout, ref, j, width=LANES):
    if layout.kind == "cache":
        slot, g = divmod(layout.base + j, 2)
        return ref[0, 0, :, slot, g, :]
    return ref[0, :, j * width:(j + 1) * width]


def _chunk_rows(layout, ref, j, p):
    return ref[0, pl.ds(p, SUBLANES, stride=CMP_STRIDE), :]


def _cmp1_body(pt_ref, *refs, layout, n_inner):
    pages = refs[:PAGES_PER_STEP]
    w_ref, pe_ref = refs[PAGES_PER_STEP:PAGES_PER_STEP + 2]
    o_ref, s_ref = refs[PAGES_PER_STEP + 2:PAGES_PER_STEP + 4]
    if layout.kind == "cache":
        for slot in range(2):
            acc_a = jnp.zeros((2 * PAGE, HEAD_DIM), F32)
            acc_b = jnp.zeros((2 * PAGE, HEAD_DIM), F32)
            for p in range(CMP_STRIDE):
                xp = jnp.concatenate(
                    [pg[0, 0, pl.ds(p, SUBLANES, stride=CMP_STRIDE), slot, :, :].reshape(2 * SUBLANES, HEAD_DIM)
                     for pg in pages], axis=0)
                xa = (xp + pe_ref[slot, p:p + 1, :]).astype(BF16)
                xb = (xp + pe_ref[slot, CMP_STRIDE + p:CMP_STRIDE + p + 1, :]).astype(BF16)
                acc_a = acc_a + jnp.dot(xa, w_ref[slot, p, :, 0:HEAD_DIM], preferred_element_type=F32)
                acc_b = acc_b + jnp.dot(xb, w_ref[slot, p, :, HEAD_DIM:2 * HEAD_DIM], preferred_element_type=F32)
            for half, acc in enumerate((acc_a, acc_b)):
                s_ref[...] = acc
                for g in range(2):
                    c0 = (slot * 2 + g) * 256 + half * HEAD_DIM
                    o_ref[0, :, c0:c0 + HEAD_DIM] = s_ref[pl.ds(g, PAGE, stride=2), :]
        return
    for j in range(n_inner):
        slot = j // 2
        acc_a = jnp.zeros((PAGE, HEAD_DIM), F32)
        acc_b = jnp.zeros((PAGE, HEAD_DIM), F32)
        for p in range(CMP_STRIDE):
            xp = jnp.concatenate([_chunk_rows(layout, pg, j, p) for pg in pages], axis=0)
            xa = (xp + pe_ref[slot, p:p + 1, :]).astype(BF16)
            xb = (xp + pe_ref[slot, CMP_STRIDE + p:CMP_STRIDE + p + 1, :]).astype(BF16)
            acc_a = acc_a + jnp.dot(xa, w_ref[slot, p, :, 0:HEAD_DIM], preferred_element_type=F32)
            acc_b = acc_b + jnp.dot(xb, w_ref[slot, p, :, HEAD_DIM:2 * HEAD_DIM], preferred_element_type=F32)
        o_ref[0, :, j * 256:j * 256 + HEAD_DIM] = acc_a
        o_ref[0, :, j * 256 + HEAD_DIM:(j + 1) * 256] = acc_b


def _cmp1(pool, layout, pt, wcat, pe):
    nb, npg = pt.shape
    kg = npg // PAGES_PER_STEP
    if layout.kind == "cache":
        n_inner, n_sg, wsel, osel, ow = 4, 1, (lambda sg: 0), (lambda sg: 0), 1024
        wblk = 2
    else:
        cb = layout.col_block
        layout = layout._replace(width=HEAD_DIM, col_block=lambda b, sg, k: cb * 4 + sg)
        n_inner, n_sg, wsel, osel, ow = 1, 4, (lambda sg: sg // 2), (lambda sg: sg), 256
        wblk = 1
    return pl.pallas_call(
        functools.partial(_cmp1_body, layout=layout, n_inner=n_inner),
        grid_spec=pltpu.PrefetchScalarGridSpec(
            num_scalar_prefetch=1, grid=(nb, n_sg, kg),
            in_specs=_page_specs(layout, 3)
            + [pl.BlockSpec((wblk, 16, HEAD_DIM, 256), lambda b, sg, k, pt: (wsel(sg), 0, 0, 0)),
               pl.BlockSpec((wblk, CMP_LEN, HEAD_DIM), lambda b, sg, k, pt: (wsel(sg), 0, 0))],
            out_specs=pl.BlockSpec((1, PAGE, ow), lambda b, sg, k, pt: (b, k, osel(sg))),
            scratch_shapes=[pltpu.VMEM((2 * PAGE, HEAD_DIM), F32)]),
        out_shape=jax.ShapeDtypeStruct((nb, npg * SUBLANES, 1024), F32),
        compiler_params=_cparams("arbitrary", "arbitrary", "arbitrary"),
        name="nsa_compress1",
    )(pt, *([pool] * PAGES_PER_STEP), wcat, pe)


def _gelu_tanh(x):
    return 0.5 * x * (1.0 + jnp.tanh(np.sqrt(2.0 / np.pi).astype(np.float32) * (x + 0.044715 * (x * x * x))))


def _cmp2_body(ab_ref, tail_ref, pe_ref, w1_ref, w2_ref, o_ref, s_ref, *, nc, n_tail):
    row8 = lax.broadcasted_iota(jnp.int32, (SUBLANES, 1), 0)
    row16 = lax.broadcasted_iota(jnp.int32, (CMP_STRIDE, 1), 0)
    for sg in range(4):
        slot = sg // 2
        a = ab_ref[0, :, sg * 256:sg * 256 + HEAD_DIM]
        s_ref[0:nc, :] = ab_ref[0, :, sg * 256 + HEAD_DIM:(sg + 1) * 256]
        tb = jnp.zeros((SUBLANES, HEAD_DIM), F32)
        if n_tail:
            x8 = jnp.where(row8 < n_tail, tail_ref[:, sg * HEAD_DIM:(sg + 1) * HEAD_DIM], 0.0)
            x16 = jnp.concatenate([x8, jnp.zeros((CMP_STRIDE - SUBLANES, HEAD_DIM), F32)], axis=0)
            x16 = x16 + pe_ref[slot, CMP_STRIDE:CMP_LEN, :]
            t16 = jnp.zeros((CMP_STRIDE, HEAD_DIM), F32)
            for p in range(CMP_STRIDE):
                xm = jnp.where(row16 == p, x16, 0.0).astype(BF16)
                t16 = t16 + jnp.dot(xm, w1_ref[slot, CMP_STRIDE + p].astype(BF16), preferred_element_type=F32)
            tb = jnp.sum(t16, axis=0, keepdims=True) * jnp.where(row8 == 0, 1.0, 0.0)
        s_ref[nc:nc + SUBLANES, :] = tb
        pre = a + s_ref[pl.ds(1, nc), :]
        o_ref[0, :, sg * HEAD_DIM:(sg + 1) * HEAD_DIM] = jnp.dot(
            _gelu_tanh(pre).astype(BF16), w2_ref[slot].astype(BF16), preferred_element_type=F32)


def _cmp2(ab, tail, pe, w1r, w2, n_tail, tail_col_block):
    nb, nc, _ = ab.shape
    if tail is None:
        tail = jnp.zeros((nb * SUBLANES, 512), F32)
        tail_col_block = 0
    return pl.pallas_call(
        functools.partial(_cmp2_body, nc=nc, n_tail=n_tail),
        grid=(nb,),
        in_specs=[pl.BlockSpec((1, nc, 1024), lambda b: (b, 0, 0)),
                  pl.BlockSpec((SUBLANES, 512), lambda b: (b, tail_col_block)),
                  pl.BlockSpec((2, CMP_LEN, HEAD_DIM), lambda b: (0, 0, 0)),
                  pl.BlockSpec((2, CMP_LEN, HEAD_DIM, HEAD_DIM), lambda b: (0, 0, 0, 0)),
                  pl.BlockSpec((2, HEAD_DIM, HEAD_DIM), lambda b: (0, 0, 0))],
        out_specs=pl.BlockSpec((1, nc, 512), lambda b: (b, 0, 0)),
        out_shape=jax.ShapeDtypeStruct((nb, nc, 512), F32),
        scratch_shapes=[pltpu.VMEM((nc + SUBLANES, HEAD_DIM), F32)],
        compiler_params=_cparams("arbitrary"),
        name="nsa_compress2",
    )(ab, tail, pe, w1r, w2)


def _nsa1_body(*refs, tq, q0, n_cmp, ncp, n_sel, nselp, lpad, nwb, has_tail, win_k0):
    qraw_ref, qrot_ref, cmp_ref, ov_ref = refs[:4]
    wins = refs[4:4 + nwb]
    pos = 4 + nwb
    tail_ref = None
    if has_tail:
        tail_ref = refs[pos]
        pos += 1
    ocmp_ref, owin_ref, kmask_ref = refs[pos:pos + 3]
    qt = pl.program_id(1)
    r_per_g = NSA_HEADS // NSA_KV
    qpos = q0 + qt * tq + lax.broadcasted_iota(jnp.int32, (tq, 1), 0)

    jj = lax.broadcasted_iota(jnp.int32, (1, ncp), 1)
    cmask = (jj * CMP_STRIDE + (CMP_LEN - 1) <= qpos) & (jj < n_cmp)
    blk = lax.broadcasted_iota(jnp.int32, (1, nselp), 1)
    cur = qpos // SEL_BLOCK
    forced = (blk == 0) | (blk == cur) | (blk == cur - 1)
    e_row = lax.broadcasted_iota(jnp.int32, (KSTEP // SEL_BLOCK, KSTEP), 0)
    e_col = lax.broadcasted_iota(jnp.int32, (KSTEP // SEL_BLOCK, KSTEP), 1)
    expand = jnp.where(e_col // SEL_BLOCK == e_row, 1.0, 0.0).astype(BF16)
    lane128 = lax.broadcasted_iota(jnp.int32, (1, LANES), 1)
    for g in range(NSA_KV):
        kc = cmp_ref[0, :, g * HEAD_DIM:(g + 1) * HEAD_DIM].astype(BF16)
        vc = cmp_ref[0, :, (2 + g) * HEAD_DIM:(3 + g) * HEAD_DIM].astype(BF16)
        q = _stack_heads(qraw_ref, g, r_per_g)
        s = _dot_nt(q, kc) * ATT_SCALE + jnp.concatenate([jnp.where(cmask, 0.0, NEG_INF)] * r_per_g, axis=0)
        p, den = _softmax_parts(s)
        p = p / den
        o = jnp.dot(p.astype(BF16), vc, preferred_element_type=F32)
        psum = jnp.zeros((tq, ncp), F32)
        for r in range(r_per_g):
            h = g * r_per_g + r
            psum = psum + p[r * tq:(r + 1) * tq]
            ocmp_ref[:, h * HEAD_DIM:(h + 1) * HEAD_DIM] = o[r * tq:(r + 1) * tq]
        imp = jnp.dot(psum.astype(BF16), ov_ref[...].astype(BF16), preferred_element_type=F32)
        imp = jnp.where(forced, imp + FORCE_SCORE, imp)
        imp = jnp.where((blk <= cur) & (blk < n_sel), imp, NEG_INF)
        rank = jnp.zeros((tq, nselp), F32)
        for i in range(n_sel):
            vi = imp[:, i:i + 1]
            rank = rank + jnp.where(vi > imp, 1.0, jnp.where(vi == imp, jnp.where(blk > i, 1.0, 0.0), 0.0))
        sel = jnp.where((rank < min(SEL_TOP, n_sel)) & (blk < n_sel), 1.0, 0.0).astype(BF16)
        per = KSTEP // SEL_BLOCK
        for c in range(lpad // KSTEP):
            km = jnp.dot(sel[:, c * per:(c + 1) * per], expand, preferred_element_type=F32)
            kmask_ref[0, g, :, c * KSTEP:(c + 1) * KSTEP] = km.astype(BF16)
        if lpad % KSTEP:
            b0 = (lpad // KSTEP) * per
            km = jnp.where(lane128 < SEL_BLOCK, sel[:, b0:b0 + 1].astype(F32), 0.0)
            kmask_ref[0, g, :, (lpad // KSTEP) * KSTEP:lpad] = jnp.broadcast_to(km, (tq, LANES)).astype(BF16)

    nk = nwb * PAGE + (LANES if has_tail else 0)
    kk = lax.broadcasted_iota(jnp.int32, (1, nk), 1)
    if has_tail:
        kpos = jnp.where(kk < nwb * PAGE, win_k0 + kk, q0 + kk - nwb * PAGE)
    else:
        kpos = (qt - (nwb - 1)) * PAGE + kk
    rel = qpos - kpos
    wmask = (rel >= 0) & (rel < WINDOW) & (kpos >= 0)
    for g in range(NSA_KV):
        if has_tail:
            kparts = [w[0, 0, :, 0, g, :] for w in wins]
            vparts = [w[0, 0, :, 1, g, :] for w in wins]
        else:
            kparts = [w[:, g * HEAD_DIM:(g + 1) * HEAD_DIM] for w in wins]
            vparts = [w[:, (2 + g) * HEAD_DIM:(3 + g) * HEAD_DIM] for w in wins]
        if has_tail:
            zpad = jnp.zeros((LANES - SUBLANES, HEAD_DIM), F32)
            kparts += [tail_ref[:, g * HEAD_DIM:(g + 1) * HEAD_DIM], zpad]
            vparts += [tail_ref[:, (2 + g) * HEAD_DIM:(3 + g) * HEAD_DIM], zpad]
        kw = jnp.concatenate(kparts, axis=0).astype(BF16)
        vw = jnp.concatenate(vparts, axis=0).astype(BF16)
        q = _stack_heads(qrot_ref, g, r_per_g)
        s = _dot_nt(q, kw) * ATT_SCALE + jnp.concatenate([jnp.where(wmask, 0.0, NEG_INF)] * r_per_g, axis=0)
        p, den = _softmax_parts(s)
        o = jnp.dot((p / den).astype(BF16), vw, preferred_element_type=F32)
        for r in range(r_per_g):
            h = g * r_per_g + r
            owin_ref[:, h * HEAD_DIM:(h + 1) * HEAD_DIM] = o[r * tq:(r + 1) * tq]


def _nsa1(hp, cmp, ov, win_src, tail_src, *, nb, tq, qt_n, q0, n_cmp, n_sel, lpad, nwb, win_k0, win_layer):
    n = hp.shape[0]
    ncp = cmp.shape[1]
    nselp = ov.shape[1]
    has_tail = tail_src is not None
    rows = lambda b, t: b * qt_n + t
    in_specs = [pl.BlockSpec((tq, 1024), lambda b, t: (rows(b, t), C_QRAW // 1024)),
                pl.BlockSpec((tq, 1024), lambda b, t: (rows(b, t), C_QROT // 1024)),
                pl.BlockSpec((1, ncp, 512), lambda b, t: (b, 0, 0)),
                pl.BlockSpec((ncp, nselp), lambda b, t: (0, 0))]
    if has_tail:
        in_specs += [pl.BlockSpec((1, 1, PAGE, 2, NSA_KV, HEAD_DIM), lambda b, t, k=k: (win_layer, b, k, 0, 0, 0))
                     for k in range(nwb)]
        in_specs += [pl.BlockSpec((SUBLANES, 512), lambda b, t: (b, C_WIN // 512))]
        args = [win_src] * nwb + [tail_src]
    else:
        in_specs += [pl.BlockSpec((PAGE, 512),
                                  lambda b, t, k=k: (b * qt_n + jnp.maximum(t - (nwb - 1) + k, 0), C_WIN // 512))
                     for k in range(nwb)]
        args = [win_src] * nwb
    return pl.pallas_call(
        functools.partial(_nsa1_body, tq=tq, q0=q0, n_cmp=n_cmp, ncp=ncp, n_sel=n_sel, nselp=nselp, lpad=lpad,
                          nwb=nwb, has_tail=has_tail, win_k0=win_k0),
        grid=(nb, qt_n),
        in_specs=in_specs,
        out_specs=[pl.BlockSpec((tq, 1024), lambda b, t: (rows(b, t), 0)),
                   pl.BlockSpec((tq, 1024), lambda b, t: (rows(b, t), 0)),
                   pl.BlockSpec((1, NSA_KV, tq, lpad), lambda b, t: (b, 0, t, 0))],
        out_shape=[jax.ShapeDtypeStruct((n, 1024), F32), jax.ShapeDtypeStruct((n, 1024), F32),
                   jax.ShapeDtypeStruct((nb, NSA_KV, qt_n * tq, lpad), BF16)],
        compiler_params=_cparams("arbitrary", "arbitrary"),
        name="nsa_cmp_select_window",
    )(hp, hp, cmp, ov, *args)


def _mattn_body(pt_ref, *refs, layout, n_g, r_per_g, gm, tq, q0, kg_n, has_tail):
    q_ref, mask_ref = refs[:2]
    pages = refs[2:2 + PAGES_PER_STEP]
    pos = 2 + PAGES_PER_STEP
    if has_tail:
        tail_ref, tmask_ref = refs[pos:pos + 2]
        pos += 2
    o_ref, s_ref, v_ref = refs[pos:pos + 3]
    qt = pl.program_id(1)
    step = pl.program_id(2)
    two_pass = kg_n > 1 or has_tail
    kg = step
    qpos = q0 + qt * tq + lax.broadcasted_iota(jnp.int32, (tq, 1), 0)

    def scores(g, kt, kpos, mref):
        picked = jnp.where(mref[0, g if gm > 1 else 0].astype(F32) > 0.5, 0.0, NEG_INF)
        bias = jnp.concatenate([picked + jnp.where(kpos <= qpos, 0.0, NEG_INF)] * r_per_g, axis=0)
        return _dot_nt(_stack_heads(q_ref, g, r_per_g), kt) * ATT_SCALE + bias

    def emit(g, o):
        for r in range(r_per_g):
            h = g * r_per_g + r
            o_ref[:, h * HEAD_DIM:(h + 1) * HEAD_DIM] = o[r * tq:(r + 1) * tq]

    def main_keys(g):
        return jnp.concatenate([_page_tile(layout, pg, g) for pg in pages], axis=0).astype(BF16)

    def main_vals(g):
        return jnp.concatenate([_page_tile(layout, pg, n_g + g) for pg in pages], axis=0).astype(BF16)

    def tail_part(g, off):
        zpad = jnp.zeros((LANES - SUBLANES, HEAD_DIM), F32)
        return jnp.concatenate([tail_ref[:, (off + g) * HEAD_DIM:(off + g + 1) * HEAD_DIM], zpad], axis=0).astype(BF16)

    kpos = kg * KSTEP + lax.broadcasted_iota(jnp.int32, (1, KSTEP), 1)
    tpos = q0 + lax.broadcasted_iota(jnp.int32, (1, LANES), 1)

    if not two_pass:
        for g in range(n_g):
            p, den = _softmax_parts(scores(g, main_keys(g), kpos, mask_ref))
            emit(g, jnp.dot((p / den).astype(BF16), main_vals(g), preferred_element_type=F32))
        return

    for g in range(n_g):
        s_ref[g, kg] = scores(g, main_keys(g), kpos, mask_ref)
        v_ref[g, pl.ds(pl.multiple_of(kg * KSTEP, KSTEP), KSTEP), :] = main_vals(g)

    @pl.when(step == kg_n - 1)
    def _():
        for g in range(n_g):
            chunks = [s_ref[g, c] for c in range(kg_n)]
            vals = [v_ref[g, c * KSTEP:(c + 1) * KSTEP, :] for c in range(kg_n)]
            if has_tail:
                chunks.append(scores(g, tail_part(g, 0), tpos, tmask_ref))
                vals.append(tail_part(g, n_g))
            m = chunks[0].max(axis=-1, keepdims=True)
            for c in chunks[1:]:
                m = jnp.maximum(m, c.max(axis=-1, keepdims=True))
            m = jnp.where(m == NEG_INF, 0.0, m)
            exps = [jnp.exp(c - m) for c in chunks]
            den = exps[0].sum(axis=-1, keepdims=True)
            for e in exps[1:]:
                den = den + e.sum(axis=-1, keepdims=True)
            den = jnp.maximum(den, 1e-30)
            o = jnp.zeros((r_per_g * tq, HEAD_DIM), F32)
            for e, v in zip(exps, vals):
                o = o + jnp.dot((e / den).astype(BF16), v, preferred_element_type=F32)
            emit(g, o)


def _mattn(hp, q_col, n_heads, mask, pool, layout, pt, tail_src, tail_col_block, *, nb, tq, qt_n, q0, name):
    n = hp.shape[0]
    n_g = 2
    r_per_g = n_heads // n_g
    qw = n_heads * HEAD_DIM
    gm = mask.shape[1]
    kg_n = pt.shape[1] // PAGES_PER_STEP
    has_tail = tail_src is not None
    buffered = kg_n > 1 or has_tail
    in_specs = [pl.BlockSpec((tq, qw), lambda b, t, k, pt: (b * qt_n + t, q_col // qw)),
                pl.BlockSpec((1, gm, tq, KSTEP), lambda b, t, k, pt: (b, 0, t, k))]
    in_specs += _page_specs(layout, 3)
    args = [hp, mask] + [pool] * PAGES_PER_STEP
    if has_tail:
        in_specs += [pl.BlockSpec((SUBLANES, 512), lambda b, t, k, pt: (b, tail_col_block)),
                     pl.BlockSpec((1, gm, tq, LANES), lambda b, t, k, pt: (b, 0, t, kg_n * KSTEP // LANES))]
        args += [tail_src, mask]
    return pl.pallas_call(
        functools.partial(_mattn_body, layout=layout, n_g=n_g, r_per_g=r_per_g, gm=gm, tq=tq, q0=q0, kg_n=kg_n, has_tail=has_tail),
        grid_spec=pltpu.PrefetchScalarGridSpec(
            num_scalar_prefetch=1, grid=(nb, qt_n, kg_n),
            in_specs=in_specs,
            out_specs=pl.BlockSpec((tq, qw), lambda b, t, k, pt: (b * qt_n + t, 0)),
            scratch_shapes=[pltpu.VMEM((n_g, kg_n, r_per_g * tq, KSTEP) if buffered else (1, 1, SUBLANES, LANES), F32),
                            pltpu.VMEM((n_g, kg_n * KSTEP, HEAD_DIM) if buffered else (1, 2 * SUBLANES, LANES), BF16)]),
        out_shape=jax.ShapeDtypeStruct((n, qw), F32),
        compiler_params=_cparams("arbitrary", "arbitrary", "arbitrary"),
        name=name,
    )(pt, *args)


PACKED_ROWS = 2 * SUBLANES


def _topk_mask_packed_t(sc_t, n_keep):
    n_keys, n_q = sc_t.shape
    i16 = jnp.int16
    bits = pltpu.bitcast(sc_t + 0.0, jnp.int32)
    key = jnp.where(bits < 0, bits ^ jnp.int32(0x7FFFFFFF), bits)
    hi = jnp.right_shift(key, 16).astype(i16)
    lo = ((key & jnp.int32(0xFFFF)) - 32768).astype(i16)
    one_b, zero_b = jnp.asarray(1, BF16), jnp.asarray(0, BF16)
    i16_min, i16_max = jnp.asarray(-32768, i16), jnp.asarray(32767, i16)
    assert n_keys % PACKED_ROWS == 0 and n_keys // PACKED_ROWS <= 256

    def colsum(x01):
        parts = [x01[i * PACKED_ROWS:(i + 1) * PACKED_ROWS, :] for i in range(n_keys // PACKED_ROWS)]
        while len(parts) > 1:
            parts = [parts[i] + parts[i + 1] for i in range(0, len(parts), 2)]
        return jnp.sum(parts[0].astype(F32), axis=0, keepdims=True)

    def search(vals, need, nbits, start):
        def body(i, thr):
            cand = thr + jnp.left_shift(jnp.int32(1), nbits - 1 - i)
            cnt = colsum(jnp.where(vals >= cand.astype(i16), one_b, zero_b))
            return jnp.where(cnt >= need, cand, thr)
        return lax.fori_loop(0, nbits, body, jnp.full((1, n_q), start, jnp.int32))

    t_hi = search(hi, float(n_keep), 16, -32768).astype(i16)
    eq = hi == t_hi
    hi_gt = jnp.where(hi > t_hi, one_b, zero_b)
    lo_m = jnp.where(eq, lo, i16_min)
    t_lo = search(lo_m, n_keep - colsum(hi_gt), 16, -32768).astype(i16)
    gt = hi_gt + jnp.where(lo_m > t_lo, one_b, zero_b)
    need = n_keep - colsum(gt)
    idx = lax.broadcasted_iota(jnp.int32, (n_keys, 1), 0).astype(i16)
    tie_idx = jnp.where(eq, jnp.where(lo == t_lo, idx, i16_max), i16_max)
    nbits = int(np.ceil(np.log2(n_keys)))

    def ibody(i, c):
        cand = c + jnp.left_shift(jnp.int32(1), nbits - 1 - i)
        cnt = colsum(jnp.where(tie_idx < cand.astype(i16), one_b, zero_b))
        return jnp.where(cnt < need, cand, c)

    cut = lax.fori_loop(0, nbits, ibody, jnp.zeros((1, n_q), jnp.int32)).astype(i16)
    return jnp.maximum(gt, jnp.where(tie_idx <= cut, one_b, zero_b))


def _dsa1_body(pt_ref, *refs, layout, tq, q0, kg_n, lpad, has_tail, n_keep):
    iq_ref, misc_ref = refs[:2]
    pages = refs[2:2 + PAGES_PER_STEP]
    pos = 2 + PAGES_PER_STEP
    if has_tail:
        tail_ref = refs[pos]
        pos += 1
    mask_ref, sc_ref = refs[pos:pos + 2]
    qt = pl.program_id(1)
    kg = pl.program_id(2)
    qpos = q0 + qt * tq + lax.broadcasted_iota(jnp.int32, (tq, 1), 0)
    lane128 = lax.broadcasted_iota(jnp.int32, (1, LANES), 1)

    if kg_n == 1 and not has_tail and tq == LANES:
        kt = jnp.concatenate([_page_tile(layout, pg, 0, IDX_DIM) for pg in pages], axis=0).astype(BF16)
        misc_t = misc_ref[...].T
        kpos_c = lax.broadcasted_iota(jnp.int32, (KSTEP, 1), 0)
        qpos_r = q0 + qt * tq + lax.broadcasted_iota(jnp.int32, (1, tq), 1)
        acc = jnp.zeros((KSTEP, tq), F32)
        for h in range(IDX_HEADS):
            qi = iq_ref[:, h * IDX_DIM:(h + 1) * IDX_DIM].astype(BF16)
            acc = acc + misc_t[MISC_IW + h:MISC_IW + h + 1, :] * jnp.maximum(_dot_nt(kt, qi), 0.0)
        sel_t = _topk_mask_packed_t(jnp.where(kpos_c <= qpos_r, acc, NEG_INF), n_keep)
        for c in range(KSTEP // LANES):
            blk = sel_t[c * LANES:(c + 1) * LANES, :].astype(F32).T
            mask_ref[0, 0, :, c * LANES:(c + 1) * LANES] = blk.astype(BF16)
        return

    def scores(kt, kpos, keys_on_lanes=False):
        acc = jnp.zeros((tq, kpos.shape[1]), F32)
        for h in range(IDX_HEADS):
            qi = iq_ref[:, h * IDX_DIM:(h + 1) * IDX_DIM].astype(BF16)
            w = jnp.sum(jnp.where(lane128 == MISC_IW + h, misc_ref[...], 0.0), axis=-1, keepdims=True)
            qk = jnp.dot(qi, kt, preferred_element_type=F32) if keys_on_lanes else _dot_nt(qi, kt)
            acc = acc + w * jnp.maximum(qk, 0.0)
        return jnp.where(kpos <= qpos, acc, NEG_INF)

    kpos = kg * KSTEP + lax.broadcasted_iota(jnp.int32, (1, KSTEP), 1)
    if layout.kind == "cache4t":
        kt = jnp.concatenate([pg[0, 0] for pg in pages], axis=1).astype(BF16)
        sc_ref[kg] = scores(kt, kpos, keys_on_lanes=True)
    else:
        kt = jnp.concatenate([_page_tile(layout, pg, 0, IDX_DIM) for pg in pages], axis=0).astype(BF16)
        sc_ref[kg] = scores(kt, kpos)

    @pl.when(kg == kg_n - 1)
    def _():
        nch = sc_ref.shape[0]
        if has_tail:
            zpad = jnp.zeros((KSTEP - SUBLANES, IDX_DIM), F32)
            tk = jnp.concatenate([tail_ref[:, 0:IDX_DIM], zpad], axis=0).astype(BF16)
            tl = lax.broadcasted_iota(jnp.int32, (1, KSTEP), 1)
            sc_ref[kg_n] = jnp.where(tl < LANES, scores(tk, q0 + tl), NEG_INF)
        sc = sc_ref[...] + 0.0
        bits = pltpu.bitcast(sc, jnp.int32)
        key = jnp.where(bits < 0, bits ^ jnp.int32(0x7FFFFFFF), bits)
        int_min = jnp.int32(-2 ** 31)

        def count(pred):
            return jnp.sum(jnp.sum(jnp.where(pred, 1.0, 0.0), axis=-1, keepdims=True), axis=0, keepdims=True)

        def vbit(i, thr):
            cand = thr + jnp.left_shift(jnp.int32(1), 31 - i)
            return jnp.where(count(key >= cand) >= n_keep, cand, thr)

        thr = lax.fori_loop(0, 32, vbit, jnp.full((1, tq, 1), int_min, jnp.int32))
        gt = key > thr
        tie = key == thr
        need = n_keep - count(gt)
        idx = (lax.broadcasted_iota(jnp.int32, (nch, 1, KSTEP), 0) * KSTEP
               + lax.broadcasted_iota(jnp.int32, (nch, 1, KSTEP), 2))
        nbits = int(np.ceil(np.log2(nch * KSTEP)))

        def ibit(i, c):
            cand = c + jnp.left_shift(jnp.int32(1), nbits - 1 - i)
            return jnp.where(count(tie & (idx < cand)) < need, cand, c)

        cut = lax.fori_loop(0, nbits, ibit, jnp.zeros((1, tq, 1), jnp.int32))
        sel = jnp.where(gt | (tie & (idx <= cut)), 1.0, 0.0).astype(BF16)
        for c in range(kg_n):
            mask_ref[0, 0, :, c * KSTEP:(c + 1) * KSTEP] = sel[c]
        if has_tail:
            mask_ref[0, 0, :, kg_n * KSTEP:lpad] = sel[kg_n][:, 0:lpad - kg_n * KSTEP]


def _dsa1(hp, pool, layout, pt, tail_src, *, nb, tq, qt_n, q0, lpad, n_keep):
    kg_n = pt.shape[1] // PAGES_PER_STEP
    has_tail = tail_src is not None
    in_specs = [pl.BlockSpec((tq, 256), lambda b, t, k, pt: (b * qt_n + t, C_IQ // 256)),
                pl.BlockSpec((tq, LANES), lambda b, t, k, pt: (b * qt_n + t, C_MISC // LANES))]
    in_specs += _page_specs(layout, 3)
    args = [hp, hp] + [pool] * PAGES_PER_STEP
    if has_tail:
        in_specs += [pl.BlockSpec((SUBLANES, LANES), lambda b, t, k, pt: (b, C_MISC // LANES))]
        args += [tail_src]
    return pl.pallas_call(
        functools.partial(_dsa1_body, layout=layout, tq=tq, q0=q0, kg_n=kg_n, lpad=lpad, has_tail=has_tail,
                          n_keep=n_keep),
        grid_spec=pltpu.PrefetchScalarGridSpec(
            num_scalar_prefetch=1, grid=(nb, qt_n, kg_n),
            in_specs=in_specs,
            out_specs=pl.BlockSpec((1, 1, tq, lpad), lambda b, t, k, pt: (b, 0, t, 0)),
            scratch_shapes=[pltpu.VMEM((kg_n + (1 if has_tail else 0), tq, KSTEP), F32)]),
        out_shape=jax.ShapeDtypeStruct((nb, 1, qt_n * tq, lpad), BF16),
        compiler_params=_cparams("arbitrary", "arbitrary", "arbitrary"),
        name="dsa_indexer_topk",
    )(pt, *args)


def _mix_body(ocmp_ref, osel_ref, owin_ref, odsa_ref, conv_ref, misc_ref, mg0_ref, mg1_ref, mg2_ref,
              wa_ref, wb_ref, wc_ref, z_ref):
    lane128 = lax.broadcasted_iota(jnp.int32, (1, LANES), 1)
    misc = misc_ref[...]

    def gate(kind, h):
        return jnp.sum(jnp.where(lane128 == MISC_NG + kind * NSA_HEADS + h, misc, 0.0), axis=-1, keepdims=True)

    parts = []
    for h in range(NSA_HEADS):
        sl = slice(h * HEAD_DIM, (h + 1) * HEAD_DIM)
        parts.append((gate(0, h) * ocmp_ref[:, sl] + gate(1, h) * osel_ref[:, sl]
                      + gate(2, h) * owin_ref[:, sl]).astype(BF16))
    o_nsa = jnp.concatenate(parts, axis=1)
    p_a = jnp.dot(o_nsa, wa_ref[...], preferred_element_type=F32)
    p_b = jnp.dot(odsa_ref[...].astype(BF16), wb_ref[...], preferred_element_type=F32)
    p_c = jnp.dot(conv_ref[...].astype(BF16), wc_ref[...], preferred_element_type=F32)
    z_ref[...] = (mg0_ref[...] * p_a + mg1_ref[...] * p_b + mg2_ref[...] * p_c).astype(BF16)


def _mix(hp, o_cmp, o_sel, o_win, o_dsa, conv_out, wa, wb, wc, tm):
    n = hp.shape[0]
    row = lambda w, cb=0: pl.BlockSpec((tm, w), lambda i: (i, cb))
    full = lambda a: pl.BlockSpec(a.shape, lambda i: (0, 0))
    return pl.pallas_call(
        _mix_body,
        grid=(n // tm,),
        in_specs=[row(1024), row(1024), row(1024), row(512), row(512), row(LANES, C_MISC // LANES),
                  row(D_MODEL, C_MG // D_MODEL), row(D_MODEL, C_MG // D_MODEL + 1), row(D_MODEL, C_MG // D_MODEL + 2),
                  full(wa), full(wb), full(wc)],
        out_specs=row(D_MODEL),
        out_shape=jax.ShapeDtypeStruct((n, D_MODEL), BF16),
        compiler_params=_cparams("arbitrary"),
        name="branch_merge",
    )(o_cmp, o_sel, o_win, o_dsa, conv_out, hp, hp, hp, hp, wa, wb, wc)


def _outln_body(z_ref, wo_ref, x_ref, g_ref, b_ref, wr_ref, rb_ref, x1_ref, gate_ref, *, tm):
    y = jnp.dot(z_ref[...], wo_ref[...], preferred_element_type=F32)
    x1 = _layernorm(ALPHA * x_ref[...] + y, g_ref[...], b_ref[...])
    x1_ref[...] = x1
    x_hi = x1.astype(BF16)
    x_lo = (x1 - x_hi.astype(F32)).astype(BF16)
    w_hi = wr_ref[...].astype(BF16)
    w_lo = (wr_ref[...] - w_hi.astype(F32)).astype(BF16)
    aff = _sigmoid(_dot_nt(w_hi, x_hi) + (_dot_nt(w_hi, x_lo) + _dot_nt(w_lo, x_hi)))
    biased = aff + rb_ref[:, 0:1]
    rows = [biased[e:e + 1, :] for e in range(N_EXPERTS)]
    best = None
    g_best = jnp.zeros((1, tm), jnp.int32)
    for g in range(N_GROUPS):
        v = rows[g * EXPERTS_PER_GROUP:(g + 1) * EXPERTS_PER_GROUP]
        score = None
        for a in range(EXPERTS_PER_GROUP):
            for c in range(a + 1, EXPERTS_PER_GROUP):
                pair = v[a] + v[c]
                score = pair if score is None else jnp.maximum(score, pair)
        if best is None:
            best = score
        else:
            better = score > best
            best = jnp.where(better, score, best)
            g_best = jnp.where(better, g, g_best)
    sel_rows = []
    for e in range(N_EXPERTS):
        g = e // EXPERTS_PER_GROUP
        rank = jnp.zeros((1, tm), F32)
        for o in range(g * EXPERTS_PER_GROUP, (g + 1) * EXPERTS_PER_GROUP):
            if o == e:
                continue
            beats = (rows[o] > rows[e]) | ((rows[o] == rows[e]) & (o < e))
            rank = rank + jnp.where(beats, 1.0, 0.0)
        sel_rows.append(jnp.where((g_best == g) & (rank < 2), aff[e:e + 1, :], 0.0))
    tot = sel_rows[0]
    for e in range(1, N_EXPERTS):
        tot = tot + sel_rows[e]
    gate_t = jnp.concatenate(sel_rows + [jnp.zeros((LANES - N_EXPERTS, tm), F32)], axis=0) / tot
    gate_ref[...] = gate_t.T


def _outln(z, wo, x, g, b, wr_t, rb, tm):
    n = z.shape[0]
    row = lambda w: pl.BlockSpec((tm, w), lambda i: (i, 0))
    full = lambda a: pl.BlockSpec(a.shape, lambda i: (0, 0))
    return pl.pallas_call(
        functools.partial(_outln_body, tm=tm),
        grid=(n // tm,),
        in_specs=[row(D_MODEL), full(wo), row(D_MODEL), full(g), full(b), full(wr_t), full(rb)],
        out_specs=[row(D_MODEL), row(LANES)],
        out_shape=[jax.ShapeDtypeStruct((n, D_MODEL), F32), jax.ShapeDtypeStruct((n, LANES), F32)],
        compiler_params=_cparams("arbitrary"),
        name="out_proj_ln_router",
    )(z, wo, x, g, b, wr_t, rb)


def _moe_body(x_ref, gate_ref, wg_ref, wu_ref, wd_ref, g_ref, b_ref, o_ref, xb_ref, acc_ref):
    e = pl.program_id(1)

    @pl.when(e == 0)
    def _():
        xb_ref[...] = x_ref[...].astype(BF16)
        acc_ref[...] = jnp.zeros(acc_ref.shape, F32)

    lane128 = lax.broadcasted_iota(jnp.int32, (1, LANES), 1)
    gcol = jnp.sum(jnp.where(lane128 == e, gate_ref[...], 0.0), axis=-1, keepdims=True)
    xb = xb_ref[...]
    hg = jnp.dot(xb, wg_ref[0], preferred_element_type=F32)
    hu = jnp.dot(xb, wu_ref[0], preferred_element_type=F32)
    h = (hg * _sigmoid(hg)) * hu * gcol
    acc_ref[...] += jnp.dot(h.astype(BF16), wd_ref[0], preferred_element_type=F32)

    @pl.when(e == N_EXPERTS - 1)
    def _():
        o_ref[...] = _layernorm(ALPHA * x_ref[...] + acc_ref[...], g_ref[...], b_ref[...])


def _moe(x1, gate, wg, wu, wd, g, b, tm):
    n = x1.shape[0]
    return pl.pallas_call(
        _moe_body,
        grid=(n // tm, N_EXPERTS),
        in_specs=[pl.BlockSpec((tm, D_MODEL), lambda i, e: (i, 0)),
                  pl.BlockSpec((tm, LANES), lambda i, e: (i, 0)),
                  pl.BlockSpec((1, D_MODEL, D_FF), lambda i, e: (e, 0, 0)),
                  pl.BlockSpec((1, D_MODEL, D_FF), lambda i, e: (e, 0, 0)),
                  pl.BlockSpec((1, D_FF, D_MODEL), lambda i, e: (e, 0, 0)),
                  pl.BlockSpec((1, D_MODEL), lambda i, e: (0, 0)),
                  pl.BlockSpec((1, D_MODEL), lambda i, e: (0, 0))],
        out_specs=pl.BlockSpec((tm, D_MODEL), lambda i, e: (i, 0)),
        out_shape=jax.ShapeDtypeStruct((n, D_MODEL), F32),
        scratch_shapes=[pltpu.VMEM((tm, D_MODEL), BF16), pltpu.VMEM((tm, D_MODEL), F32)],
        compiler_params=_cparams("arbitrary", "arbitrary"),
        name="moe_ln",
    )(x1, gate, wg, wu, wd, g, b)


def _overlap_matrix(ncp, nselp):
    cs = np.arange(ncp)[:, None] * CMP_STRIDE
    ss = np.arange(nselp)[None, :] * SEL_BLOCK
    return jnp.asarray(((cs < ss + SEL_BLOCK) & (cs + CMP_LEN > ss)).astype(np.float32))


class _Group:
    def __init__(self, nb, t_real, t_pad, q0, past_len, tm, tq):
        self.nb, self.t_real, self.t_pad, self.q0, self.past_len, self.tm, self.tq = nb, t_real, t_pad, q0, past_len, tm, tq
        self.paged = past_len > 0
        self.lp = past_len if self.paged else t_pad
        self.ltot = self.lp + (t_real if self.paged else 0)
        self.lpad = self.lp + (LANES if self.paged else 0)
        self.qt_n = t_pad // tq
        n_chunks = -(-self.ltot // CMP_STRIDE)
        self.n_cmp = n_chunks - CMP_LEN // CMP_STRIDE + 1
        self.n_sel = -(-self.ltot // SEL_BLOCK)
        self.nselp = -(-self.n_sel // LANES) * LANES
        self.n_keep = min(DSA_TOPK, self.ltot // 4)


def _mixer(gp, x, lw, caches):
    hp = _proj(x.astype(BF16), lw["w_in"], gp.tabs, lw["kinds"], min(2 * gp.tm, gp.nb * gp.t_pad, gp.tabs.shape[1]))
    nb, tq, qt_n, q0 = gp.nb, gp.tq, gp.qt_n, gp.q0
    if gp.paged:
        l, nsa_pool, dsa_pool, kidx_pool, win_src, s_conv, pt = caches
        cmp_lay = _PageLayout("cache", 0, 0, base=0, layer=l, nslot=2)
        slc_lay = _PageLayout("cache", 0, 1, base=0, layer=l, nslot=2)
        dsa_lay = _PageLayout("cache", 0, 0, base=0, layer=l, nslot=2)
        kidx_pool = jnp.transpose(kidx_pool, (0, 1, 3, 2))
        kidx_lay = _PageLayout("cache4t", IDX_DIM, 0, base=0, layer=l, nslot=0)
        tail = hp
        nwb, win_k0, win_layer = WINDOW // PAGE, PAST_LEN - WINDOW, l
        past8 = jnp.concatenate([jnp.zeros((nb, SUBLANES - 2, CONV_DIM), F32), s_conv[l]], axis=1)
    else:
        pt = gp.pt
        nsa_pool = dsa_pool = kidx_pool = hp.reshape(nb * gp.t_pad // PAGE, PAGE, NCOL)
        cmp_lay = _PageLayout("cols", 512, C_CMP // 512, base=0, layer=0, nslot=0)
        slc_lay = _PageLayout("cols", 512, C_SLC // 512, base=0, layer=0, nslot=0)
        dsa_lay = _PageLayout("cols", 512, C_DSA // 512, base=0, layer=0, nslot=0)
        kidx_lay = _PageLayout("cols", LANES, C_MISC // LANES, base=0, layer=0, nslot=0)
        tail = None
        win_src = hp
        nwb, win_k0, win_layer = WINDOW // PAGE + 1, 0, 0
        past8 = jnp.zeros((nb, SUBLANES, CONV_DIM), F32)

    conv_out, cu = _conv(hp, past8, lw["conv_w8"], min(gp.tm, gp.t_pad), gp.t_pad // min(gp.tm, gp.t_pad))

    ab = _cmp1(nsa_pool, cmp_lay, pt, lw["wcat"], lw["pe"])
    cmp = _cmp2(ab, tail, lw["pe"], lw["w1r"], lw["phi_w2"], gp.t_real if gp.paged else 0, C_CMP // 512)
    o_cmp, o_win, kmask = _nsa1(hp, cmp, gp.ov, win_src, tail, nb=nb, tq=tq, qt_n=qt_n, q0=q0, n_cmp=gp.n_cmp,
                                n_sel=gp.n_sel, lpad=gp.lpad, nwb=nwb, win_k0=win_k0, win_layer=win_layer)
    o_sel = _mattn(hp, C_QROT, NSA_HEADS, kmask, nsa_pool, slc_lay, pt, tail, C_SLC // 512,
                   nb=nb, tq=tq, qt_n=qt_n, q0=q0, name="nsa_selected_attn")
    dmask = _dsa1(hp, kidx_pool, kidx_lay, pt, tail, nb=nb, tq=tq, qt_n=qt_n, q0=q0, lpad=gp.lpad,
                  n_keep=gp.n_keep)
    o_dsa = _mattn(hp, C_DQ, DSA_HEADS, dmask, dsa_pool, dsa_lay, pt, tail, C_DSA // 512,
                   nb=nb, tq=tq, qt_n=qt_n, q0=q0, name="dsa_topk_attn")
    z = _mix(hp, o_cmp, o_sel, o_win, o_dsa, conv_out, lw["w_a"], lw["w_b"], lw["w_c"], min(gp.tm, 256))
    return z, hp, cu


def _layer(gp, x, lw, caches):
    z, hp, cu = _mixer(gp, x, lw, caches)
    tm2 = min(gp.tm, 256)
    x1, gate = _outln(z, lw["w_o"], x, lw["ln_mix_g"], lw["ln_mix_b"], lw["wr_t"], lw["rb"], tm2)
    x2 = _moe(x1, gate, lw["w_eg"], lw["w_eu"], lw["w_ed"], lw["ln_ffn_g"], lw["ln_ffn_b"], min(gp.tm, 512))
    return x2, hp, cu


def _layer_weights(l, w_in, nsa_phi_pos, nsa_phi_w1, nsa_phi_w2, conv_w, w_br_a, w_br_b, w_br_c, w_out,
                   ln_mix_g, ln_mix_b, ln_ffn_g, ln_ffn_b, w_router, router_bias, w_e_gate, w_e_up, w_e_down):
    w1r = nsa_phi_w1[l].reshape(2, CMP_LEN, HEAD_DIM, HEAD_DIM)
    wcat = jnp.concatenate([w1r[:, :CMP_STRIDE], w1r[:, CMP_STRIDE:]], axis=-1).astype(BF16)
    return dict(
        w_in=_permute_w_in(w_in, l), kinds=jnp.asarray(_col_kinds()),
        w1r=w1r, wcat=wcat, pe=nsa_phi_pos[l], phi_w2=nsa_phi_w2[l],
        conv_w8=jnp.concatenate([conv_w[l], jnp.zeros((SUBLANES - 3, CONV_DIM), F32)], axis=0),
        w_a=w_br_a[l].astype(BF16), w_b=w_br_b[l].astype(BF16), w_c=w_br_c[l].astype(BF16),
        w_o=w_out[l].astype(BF16),
        ln_mix_g=ln_mix_g[l][None], ln_mix_b=ln_mix_b[l][None],
        ln_ffn_g=ln_ffn_g[l][None], ln_ffn_b=ln_ffn_b[l][None],
        wr_t=w_router.T, rb=jnp.broadcast_to(router_bias[:, None], (N_EXPERTS, LANES)),
        w_eg=w_e_gate[l].astype(BF16), w_eu=w_e_up[l].astype(BF16), w_ed=w_e_down[l].astype(BF16))


def kernel(x_prompt, x_sample, cache_nsa_kv, cache_dsa_kv, cache_dsa_kidx, state_nsa_win, state_conv, page_table,
           w_in, nsa_phi_pos, nsa_phi_w1, nsa_phi_w2, conv_w, w_br_a, w_br_b, w_br_c, w_out, ln_mix_g, ln_mix_b,
           ln_ffn_g, ln_ffn_b, w_router, router_bias, w_e_gate, w_e_up, w_e_down):
    bp, tp, _ = x_prompt.shape
    bs, ts, _ = x_sample.shape
    ts_pad = SUBLANES

    gp_p = _Group(bp, tp, tp, 0, 0, tm=512, tq=128)
    gp_p.tabs = _rope_tables(jnp.arange(tp, dtype=jnp.int32))
    gp_p.pt = jnp.arange(bp * tp // PAGE, dtype=jnp.int32).reshape(bp, tp // PAGE)
    gp_p.ov = _overlap_matrix(LANES, gp_p.nselp)
    gp_s = _Group(bs, ts, ts_pad, PAST_LEN, PAST_LEN, tm=bs * ts_pad, tq=ts_pad)
    pos_s = PAST_LEN + jnp.arange(ts_pad, dtype=jnp.int32)
    gp_s.tabs = jnp.tile(_rope_tables(pos_s), (1, bs, 1))
    gp_s.ov = _overlap_matrix(PAST_LEN // CMP_STRIDE, gp_s.nselp)

    xp = x_prompt.reshape(bp * tp, D_MODEL)
    xs = jnp.concatenate([x_sample, jnp.zeros((bs, ts_pad - ts, D_MODEL), F32)], axis=1).reshape(bs * ts_pad, D_MODEL)

    outs_p = [[] for _ in range(5)]
    outs_s = [[] for _ in range(5)]
    for l in range(DEPTH):
        lw = _layer_weights(l, w_in, nsa_phi_pos, nsa_phi_w1, nsa_phi_w2, conv_w, w_br_a, w_br_b, w_br_c, w_out,
                            ln_mix_g, ln_mix_b, ln_ffn_g, ln_ffn_b, w_router, router_bias, w_e_gate, w_e_up, w_e_down)
        xp, hp_p, cu_p = _layer(gp_p, xp, lw, None)
        xs, hp_s, cu_s = _layer(gp_s, xs, lw, (l, cache_nsa_kv, cache_dsa_kv, cache_dsa_kidx,
                                                 state_nsa_win, state_conv, page_table))
        h3 = hp_p.reshape(bp, tp, NCOL)
        outs_p[0].append(h3[:, :, C_CMP:C_CMP + 1024].reshape(bp, tp, 4, NSA_KV, HEAD_DIM))
        outs_p[1].append(h3[:, :, C_DSA:C_DSA + 512].reshape(bp, tp, 2, DSA_KV, HEAD_DIM))
        outs_p[2].append(h3[:, :, C_MISC:C_MISC + IDX_DIM])
        outs_p[3].append(h3[:, tp - min(WINDOW, tp):, C_WIN:C_WIN + 512].reshape(bp, min(WINDOW, tp), 2, NSA_KV, HEAD_DIM))
        outs_p[4].append(cu_p.reshape(bp, tp, CONV_DIM)[:, tp - 2:])
        s3 = hp_s.reshape(bs, ts_pad, NCOL)[:, :ts]
        outs_s[0].append(s3[:, :, C_CMP:C_CMP + 1024].reshape(bs, ts, 4, NSA_KV, HEAD_DIM))
        outs_s[1].append(s3[:, :, C_DSA:C_DSA + 512].reshape(bs, ts, 2, DSA_KV, HEAD_DIM))
        outs_s[2].append(s3[:, :, C_MISC:C_MISC + IDX_DIM])
        win_new = s3[:, :, C_WIN:C_WIN + 512].reshape(bs, ts, 2, NSA_KV, HEAD_DIM)
        wb = state_nsa_win.shape[2]
        outs_s[3].append(jnp.concatenate([state_nsa_win[l], win_new], axis=1)[:, -wb:])
        ext = jnp.concatenate([state_conv[l], cu_s.reshape(bs, ts_pad, CONV_DIM)[:, :ts]], axis=1)
        outs_s[4].append(ext[:, -2:])
    sp = [jnp.stack(a, axis=0) for a in outs_p]
    ss = [jnp.stack(a, axis=0) for a in outs_s]
    y_p = xp.reshape(bp, tp, D_MODEL)
    y_s = xs.reshape(bs, ts_pad, D_MODEL)[:, :ts]
    return (y_p, y_s, sp[0], ss[0], sp[1], ss[1], sp[2], ss[2], sp[3], ss[3], sp[4], ss[4])
```

```python
import collections
import functools

import numpy as np
import jax
import jax.numpy as jnp
from jax import lax
from jax.experimental import pallas as pl
from jax.experimental.pallas import tpu as pltpu

F32 = jnp.float32
BF16 = jnp.bfloat16
NEG_INF = float("-inf")

D_MODEL = 2048
DEPTH = 2
PAST_LEN = 16384
PAGE = 128
HEAD_DIM = 128
ROPE_THETA = 500000.0
NSA_HEADS = 8
NSA_KV = 2
CMP_LEN = 32
CMP_STRIDE = 16
SEL_BLOCK = 64
SEL_TOP = 16
WINDOW = 512
FORCE_SCORE = 1e4
DSA_HEADS = 4
DSA_KV = 2
IDX_HEADS = 4
IDX_DIM = 64
DSA_TOPK = 256
CONV_DIM = 512
N_EXPERTS = 16
N_GROUPS = 4
EXPERTS_PER_GROUP = 4
D_FF = 512
LN_EPS = 1e-5
ALPHA = (2 * DEPTH) ** 0.25
IN_WIDTHS = (1024, 1536, 24, 512, 512, 256, 64, 4, 1536, 6144)
ATT_SCALE = HEAD_DIM ** -0.5

LANES = 128
SUBLANES = 8
VMEM_LIMIT = 56 * 1024 * 1024

C_QRAW = 0
C_QROT = 1024
C_CMP = 2048
C_SLC = 2560
C_WIN = 3072
C_DQ = 3584
C_DSA = 4096
C_CV = 4608
C_MG = 6144
C_IQ = 12288
C_MISC = 12544
NCOL = 12800
MISC_IW = 64
MISC_NG = 68
PROJ_TN = 1280
PAGES_PER_STEP = 16
KSTEP = PAGES_PER_STEP * PAGE

K_PLAIN, K_ROPE128, K_ROPE64, K_SIGMOID, K_MISC = 0, 1, 2, 3, 4


def _col_kinds():
    kinds = np.zeros(NCOL // LANES, np.int32)

    def mark(c0, n, k):
        kinds[c0 // LANES:(c0 + n) // LANES] = k

    mark(C_QROT, 1024, K_ROPE128)
    mark(C_SLC, 256, K_ROPE128)
    mark(C_WIN, 256, K_ROPE128)
    mark(C_DQ, 512, K_ROPE128)
    mark(C_DSA, 256, K_ROPE128)
    mark(C_IQ, 256, K_ROPE64)
    mark(C_MISC, 128, K_MISC)
    mark(C_MG, 6144, K_SIGMOID)
    return kinds


def _cparams(*sem):
    return pltpu.CompilerParams(dimension_semantics=sem, vmem_limit_bytes=VMEM_LIMIT)


def _sigmoid(x):
    return 1.0 / (1.0 + jnp.exp(-x))


def _layernorm(x, g, b):
    mu = jnp.mean(x, axis=-1, keepdims=True)
    xc = x - mu
    var = jnp.mean(xc * xc, axis=-1, keepdims=True)
    return xc * lax.rsqrt(var + LN_EPS) * g + b


def _dot_nt(a, b, precision=None):
    return lax.dot_general(a, b, (((1,), (1,)), ((), ())), preferred_element_type=F32, precision=precision)


def _softmax_parts(s):
    m = jnp.max(s, axis=-1, keepdims=True)
    m = jnp.where(m == NEG_INF, 0.0, m)
    p = jnp.exp(s - m)
    return p, jnp.maximum(jnp.sum(p, axis=-1, keepdims=True), 1e-30)


def _stack_heads(q_ref, g, r_per_g):
    return jnp.concatenate([q_ref[:, (g * r_per_g + r) * HEAD_DIM:(g * r_per_g + r + 1) * HEAD_DIM]
                            for r in range(r_per_g)], axis=0).astype(BF16)


def _proj_body(kinds_ref, needs_ref, x_ref, wt_ref, tab_ref, o_ref, w_ref):
    nsub = PROJ_TN // LANES
    j = pl.program_id(0)

    @pl.when(pl.program_id(1) == 0)
    def _():
        for s in range(nsub):
            w_ref[:, s * LANES:(s + 1) * LANES] = wt_ref[s * LANES:(s + 1) * LANES, :].astype(F32).T.astype(BF16)

    h = jnp.dot(x_ref[...], w_ref[...], preferred_element_type=F32)
    lane = lax.broadcasted_iota(jnp.int32, (1, LANES), 1)

    def rope(hs, t0, sh):
        return (hs * tab_ref[t0] + pltpu.roll(hs, sh, 1) * tab_ref[t0 + 1]
                + pltpu.roll(hs, LANES - sh, 1) * tab_ref[t0 + 2])

    for s in range(nsub):
        kind = kinds_ref[j * nsub + s]
        hs = h[:, s * LANES:(s + 1) * LANES]
        sl = slice(s * LANES, (s + 1) * LANES)

        @pl.when(kind == K_PLAIN)
        def _():
            o_ref[:, sl] = hs

        @pl.when(kind == K_ROPE128)
        def _():
            o_ref[:, sl] = rope(hs, 0, 16)

        @pl.when(kind == K_ROPE64)
        def _():
            o_ref[:, sl] = rope(hs, 3, 8)

        @pl.when(kind == K_SIGMOID)
        def _():
            o_ref[:, sl] = _sigmoid(hs)

        @pl.when(kind == K_MISC)
        def _():
            r = rope(hs, 3, 8)
            o_ref[:, sl] = jnp.where(lane < MISC_IW, r,
                                     jnp.where(lane < MISC_NG, hs * (IDX_HEADS ** -0.5),
                                               jnp.where(lane < MISC_NG + 24, _sigmoid(hs), hs)))


def _proj(x_bf, wt_bf, tabs, kinds, tm):
    n = x_bf.shape[0]
    n_tab = tabs.shape[1] // tm
    grid = (NCOL // PROJ_TN, n // tm)
    rotary = np.isin(_col_kinds().reshape(-1, PROJ_TN // LANES), (K_ROPE128, K_ROPE64, K_MISC)).any(axis=1)
    needs = jnp.asarray(rotary.astype(np.int32))
    return pl.pallas_call(
        _proj_body,
        grid_spec=pltpu.PrefetchScalarGridSpec(
            num_scalar_prefetch=2, grid=grid,
            in_specs=[pl.BlockSpec((tm, D_MODEL), lambda j, i, k, nd: (i, 0)),
                      pl.BlockSpec((PROJ_TN, D_MODEL), lambda j, i, k, nd: (j, 0)),
                      pl.BlockSpec((6, tm, LANES), lambda j, i, k, nd: (0, (i % n_tab) * nd[j], 0))],
            out_specs=pl.BlockSpec((tm, PROJ_TN), lambda j, i, k, nd: (i, j)),
            scratch_shapes=[pltpu.VMEM((D_MODEL, PROJ_TN), BF16)]),
        out_shape=jax.ShapeDtypeStruct((n, NCOL), F32),
        compiler_params=_cparams("arbitrary", "arbitrary"),
        name="in_proj",
    )(kinds, needs, x_bf, wt_bf, tabs)


def _rope_tables(pos):
    out = []
    lane = jnp.arange(LANES)
    for d in (HEAD_DIM, IDX_DIM):
        rot = d // 4
        half = rot // 2
        inv = ROPE_THETA ** (-jnp.arange(half, dtype=F32) / half)
        ang = pos.astype(F32)[:, None] * inv[None, :]
        cos = jnp.cos(ang)
        sin = jnp.sin(ang)
        li = lane % d
        ci = jnp.take(cos, li % half, axis=1)
        si = jnp.take(sin, li % half, axis=1)
        out.append(jnp.where(li[None] < rot, ci, 1.0))
        out.append(jnp.where((li[None] >= half) & (li[None] < rot), si, 0.0))
        out.append(jnp.where(li[None] < half, -si, 0.0))
    return jnp.stack(out, axis=0)


def _permute_w_in(w_in, l):
    wt = jnp.transpose(w_in, (2, 0, 1))[:, l, :].astype(BF16)
    offs = np.cumsum((0,) + IN_WIDTHS)
    nq, nkv, ng, dq, dkv, iq, ik, iw, cv, mg = [wt[offs[i]:offs[i + 1]] for i in range(10)]
    z = lambda n: jnp.zeros((n, wt.shape[1]), BF16)
    rows = [nq, nq, nkv, dq, dkv, cv, mg, iq, ik, iw, ng, z(LANES - 92), z(NCOL - C_MISC - LANES)]
    return jnp.concatenate(rows, axis=0)


def _conv_body(cv_ref, prev_ref, past_ref, w_ref, y_ref, cu_ref, s_ref, *, tiles_per_seq, tm):
    i = pl.program_id(0)
    b = cv_ref[:, 0:CONV_DIM]
    cu = cv_ref[:, CONV_DIM:2 * CONV_DIM] * cv_ref[:, 2 * CONV_DIM:3 * CONV_DIM]
    first = (i % tiles_per_seq) == 0
    prev = prev_ref[:, CONV_DIM:2 * CONV_DIM] * prev_ref[:, 2 * CONV_DIM:3 * CONV_DIM]
    s_ref[0:SUBLANES, :] = jnp.where(first, past_ref[0], prev)
    s_ref[SUBLANES:SUBLANES + tm, :] = cu
    y = (w_ref[0:1, :] * s_ref[pl.ds(SUBLANES - 2, tm), :] + w_ref[1:2, :] * s_ref[pl.ds(SUBLANES - 1, tm), :]
         + w_ref[2:3, :] * cu)
    y_ref[...] = b * y
    cu_ref[...] = cu


def _conv(hp, past8, conv_w8, tm, tiles_per_seq):
    n = hp.shape[0]
    cvb = C_CV // (3 * CONV_DIM)
    rb = tm // SUBLANES
    return pl.pallas_call(
        functools.partial(_conv_body, tiles_per_seq=tiles_per_seq, tm=tm),
        grid=(n // tm,),
        in_specs=[pl.BlockSpec((tm, 3 * CONV_DIM), lambda i: (i, cvb)),
                  pl.BlockSpec((SUBLANES, 3 * CONV_DIM), lambda i: (jnp.maximum(i * rb - 1, 0), cvb)),
                  pl.BlockSpec((1, SUBLANES, CONV_DIM), lambda i: (i // tiles_per_seq, 0, 0)),
                  pl.BlockSpec((SUBLANES, CONV_DIM), lambda i: (0, 0))],
        out_specs=[pl.BlockSpec((tm, CONV_DIM), lambda i: (i, 0)),
                   pl.BlockSpec((tm, CONV_DIM), lambda i: (i, 0))],
        out_shape=[jax.ShapeDtypeStruct((n, CONV_DIM), F32), jax.ShapeDtypeStruct((n, CONV_DIM), F32)],
        scratch_shapes=[pltpu.VMEM((tm + SUBLANES, CONV_DIM), F32)],
        compiler_params=_cparams("arbitrary"),
        name="short_conv",
    )(hp, hp, past8, conv_w8)


_PageLayout = collections.namedtuple("_PageLayout", "kind width col_block base layer nslot")


def _page_specs(layout, n_lead, kg_n=None):
    def spec(k):
        def imap(*a):
            ids, pt = a[:n_lead], a[-1]
            kg = ids[-1] % kg_n if kg_n else ids[-1]
            page = pt[ids[0], kg * PAGES_PER_STEP + k]
            if layout.kind == "cache":
                return (layout.layer, page, 0, layout.col_block, 0, 0)
            if layout.kind == "cache4t":
                return (layout.layer, page, 0, 0)
            cb = layout.col_block(*ids) if callable(layout.col_block) else layout.col_block
            return (page, 0, cb)
        shape = {"cache": (1, 1, PAGE, layout.nslot, 2, HEAD_DIM), "cache4t": (1, 1, layout.width, PAGE),
                 "cols": (1, PAGE, layout.width)}[layout.kind]
        return pl.BlockSpec(shape, imap)
    return [spec(k) for k in range(PAGES_PER_STEP)]


def _page_tile(layout, ref, j, width=LANES):
    if layout.kind == "cache":
        slot, g = divmod(layout.base + j, 2)
        return ref[0, 0, :, slot, g, :]
    return ref[0, :, j * width:(j + 1) * width]


def _chunk_rows(layout, ref, j, p):
    return ref[0, pl.ds(p, SUBLANES, stride=CMP_STRIDE), :]


def _cmp1_body(pt_ref, *refs, layout, n_inner):
    pages = refs[:PAGES_PER_STEP]
    w_ref, pe_ref = refs[PAGES_PER_STEP:PAGES_PER_STEP + 2]
    o_ref, s_ref = refs[PAGES_PER_STEP + 2:PAGES_PER_STEP + 4]
    if layout.kind == "cache":
        for slot in range(2):
            acc_a = jnp.zeros((2 * PAGE, HEAD_DIM), F32)
            acc_b = jnp.zeros((2 * PAGE, HEAD_DIM), F32)
            for p in range(CMP_STRIDE):
                xp = jnp.concatenate(
                    [pg[0, 0, pl.ds(p, SUBLANES, stride=CMP_STRIDE), slot, :, :].reshape(2 * SUBLANES, HEAD_DIM)
                     for pg in pages], axis=0)
                xa = (xp + pe_ref[slot, p:p + 1, :]).astype(BF16)
                xb = (xp + pe_ref[slot, CMP_STRIDE + p:CMP_STRIDE + p + 1, :]).astype(BF16)
                acc_a = acc_a + jnp.dot(xa, w_ref[slot, p, :, 0:HEAD_DIM], preferred_element_type=F32)
                acc_b = acc_b + jnp.dot(xb, w_ref[slot, p, :, HEAD_DIM:2 * HEAD_DIM], preferred_element_type=F32)
            for half, acc in enumerate((acc_a, acc_b)):
                s_ref[...] = acc
                for g in range(2):
                    c0 = (slot * 2 + g) * 256 + half * HEAD_DIM
                    o_ref[0, :, c0:c0 + HEAD_DIM] = s_ref[pl.ds(g, PAGE, stride=2), :]
        return
    for j in range(n_inner):
        slot = j // 2
        acc_a = jnp.zeros((PAGE, HEAD_DIM), F32)
        acc_b = jnp.zeros((PAGE, HEAD_DIM), F32)
        for p in range(CMP_STRIDE):
            xp = jnp.concatenate([_chunk_rows(layout, pg, j, p) for pg in pages], axis=0)
            xa = (xp + pe_ref[slot, p:p + 1, :]).astype(BF16)
            xb = (xp + pe_ref[slot, CMP_STRIDE + p:CMP_STRIDE + p + 1, :]).astype(BF16)
            acc_a = acc_a + jnp.dot(xa, w_ref[slot, p, :, 0:HEAD_DIM], preferred_element_type=F32)
            acc_b = acc_b + jnp.dot(xb, w_ref[slot, p, :, HEAD_DIM:2 * HEAD_DIM], preferred_element_type=F32)
        o_ref[0, :, j * 256:j * 256 + HEAD_DIM] = acc_a
        o_ref[0, :, j * 256 + HEAD_DIM:(j + 1) * 256] = acc_b


def _cmp1(pool, layout, pt, wcat, pe):
    nb, npg = pt.shape
    kg = npg // PAGES_PER_STEP
    if layout.kind == "cache":
        n_inner, n_sg, wsel, osel, ow = 4, 1, (lambda sg: 0), (lambda sg: 0), 1024
        wblk = 2
    else:
        cb = layout.col_block
        layout = layout._replace(width=HEAD_DIM, col_block=lambda b, sg, k: cb * 4 + sg)
        n_inner, n_sg, wsel, osel, ow = 1, 4, (lambda sg: sg // 2), (lambda sg: sg), 256
        wblk = 1
    return pl.pallas_call(
        functools.partial(_cmp1_body, layout=layout, n_inner=n_inner),
        grid_spec=pltpu.PrefetchScalarGridSpec(
            num_scalar_prefetch=1, grid=(nb, n_sg, kg),
            in_specs=_page_specs(layout, 3)
            + [pl.BlockSpec((wblk, 16, HEAD_DIM, 256), lambda b, sg, k, pt: (wsel(sg), 0, 0, 0)),
               pl.BlockSpec((wblk, CMP_LEN, HEAD_DIM), lambda b, sg, k, pt: (wsel(sg), 0, 0))],
            out_specs=pl.BlockSpec((1, PAGE, ow), lambda b, sg, k, pt: (b, k, osel(sg))),
            scratch_shapes=[pltpu.VMEM((2 * PAGE, HEAD_DIM), F32)]),
        out_shape=jax.ShapeDtypeStruct((nb, npg * SUBLANES, 1024), F32),
        compiler_params=_cparams("arbitrary", "arbitrary", "arbitrary"),
        name="nsa_compress1",
    )(pt, *([pool] * PAGES_PER_STEP), wcat, pe)


def _gelu_tanh(x):
    return 0.5 * x * (1.0 + jnp.tanh(np.sqrt(2.0 / np.pi).astype(np.float32) * (x + 0.044715 * (x * x * x))))


def _cmp2_body(ab_ref, tail_ref, pe_ref, w1_ref, w2_ref, o_ref, s_ref, *, nc, n_tail):
    row8 = lax.broadcasted_iota(jnp.int32, (SUBLANES, 1), 0)
    row16 = lax.broadcasted_iota(jnp.int32, (CMP_STRIDE, 1), 0)
    for sg in range(4):
        slot = sg // 2
        a = ab_ref[0, :, sg * 256:sg * 256 + HEAD_DIM]
        s_ref[0:nc, :] = ab_ref[0, :, sg * 256 + HEAD_DIM:(sg + 1) * 256]
        tb = jnp.zeros((SUBLANES, HEAD_DIM), F32)
        if n_tail:
            x8 = jnp.where(row8 < n_tail, tail_ref[:, sg * HEAD_DIM:(sg + 1) * HEAD_DIM], 0.0)
            x16 = jnp.concatenate([x8, jnp.zeros((CMP_STRIDE - SUBLANES, HEAD_DIM), F32)], axis=0)
            x16 = x16 + pe_ref[slot, CMP_STRIDE:CMP_LEN, :]
            t16 = jnp.zeros((CMP_STRIDE, HEAD_DIM), F32)
            for p in range(CMP_STRIDE):
                xm = jnp.where(row16 == p, x16, 0.0).astype(BF16)
                t16 = t16 + jnp.dot(xm, w1_ref[slot, CMP_STRIDE + p].astype(BF16), preferred_element_type=F32)
            tb = jnp.sum(t16, axis=0, keepdims=True) * jnp.where(row8 == 0, 1.0, 0.0)
        s_ref[nc:nc + SUBLANES, :] = tb
        pre = a + s_ref[pl.ds(1, nc), :]
        o_ref[0, :, sg * HEAD_DIM:(sg + 1) * HEAD_DIM] = jnp.dot(
            _gelu_tanh(pre).astype(BF16), w2_ref[slot].astype(BF16), preferred_element_type=F32)


def _cmp2(ab, tail, pe, w1r, w2, n_tail, tail_col_block):
    nb, nc, _ = ab.shape
    if tail is None:
        tail = jnp.zeros((nb * SUBLANES, 512), F32)
        tail_col_block = 0
    return pl.pallas_call(
        functools.partial(_cmp2_body, nc=nc, n_tail=n_tail),
        grid=(nb,),
        in_specs=[pl.BlockSpec((1, nc, 1024), lambda b: (b, 0, 0)),
                  pl.BlockSpec((SUBLANES, 512), lambda b: (b, tail_col_block)),
                  pl.BlockSpec((2, CMP_LEN, HEAD_DIM), lambda b: (0, 0, 0)),
                  pl.BlockSpec((2, CMP_LEN, HEAD_DIM, HEAD_DIM), lambda b: (0, 0, 0, 0)),
                  pl.BlockSpec((2, HEAD_DIM, HEAD_DIM), lambda b: (0, 0, 0))],
        out_specs=pl.BlockSpec((1, nc, 512), lambda b: (b, 0, 0)),
        out_shape=jax.ShapeDtypeStruct((nb, nc, 512), F32),
        scratch_shapes=[pltpu.VMEM((nc + SUBLANES, HEAD_DIM), F32)],
        compiler_params=_cparams("arbitrary"),
        name="nsa_compress2",
    )(ab, tail, pe, w1r, w2)


def _nsa1_body(*refs, tq, q0, n_cmp, ncp, n_sel, nselp, lpad, nwb, has_tail, win_k0):
    qraw_ref, qrot_ref, cmp_ref, ov_ref = refs[:4]
    wins = refs[4:4 + nwb]
    pos = 4 + nwb
    tail_ref = None
    if has_tail:
        tail_ref = refs[pos]
        pos += 1
    ocmp_ref, owin_ref, kmask_ref = refs[pos:pos + 3]
    qt = pl.program_id(1)
    r_per_g = NSA_HEADS // NSA_KV
    qpos = q0 + qt * tq + lax.broadcasted_iota(jnp.int32, (tq, 1), 0)

    jj = lax.broadcasted_iota(jnp.int32, (1, ncp), 1)
    cmask = (jj * CMP_STRIDE + (CMP_LEN - 1) <= qpos) & (jj < n_cmp)
    blk = lax.broadcasted_iota(jnp.int32, (1, nselp), 1)
    cur = qpos // SEL_BLOCK
    forced = (blk == 0) | (blk == cur) | (blk == cur - 1)
    e_row = lax.broadcasted_iota(jnp.int32, (KSTEP // SEL_BLOCK, KSTEP), 0)
    e_col = lax.broadcasted_iota(jnp.int32, (KSTEP // SEL_BLOCK, KSTEP), 1)
    expand = jnp.where(e_col // SEL_BLOCK == e_row, 1.0, 0.0).astype(BF16)
    lane128 = lax.broadcasted_iota(jnp.int32, (1, LANES), 1)
    for g in range(NSA_KV):
        kc = cmp_ref[0, :, g * HEAD_DIM:(g + 1) * HEAD_DIM].astype(BF16)
        vc = cmp_ref[0, :, (2 + g) * HEAD_DIM:(3 + g) * HEAD_DIM].astype(BF16)
        q = _stack_heads(qraw_ref, g, r_per_g)
        s = _dot_nt(q, kc) * ATT_SCALE + jnp.concatenate([jnp.where(cmask, 0.0, NEG_INF)] * r_per_g, axis=0)
        p, den = _softmax_parts(s)
        p = p / den
        o = jnp.dot(p.astype(BF16), vc, preferred_element_type=F32)
        psum = jnp.zeros((tq, ncp), F32)
        for r in range(r_per_g):
            h = g * r_per_g + r
            psum = psum + p[r * tq:(r + 1) * tq]
            ocmp_ref[:, h * HEAD_DIM:(h + 1) * HEAD_DIM] = o[r * tq:(r + 1) * tq]
        imp = jnp.dot(psum.astype(BF16), ov_ref[...].astype(BF16), preferred_element_type=F32)
        imp = jnp.where(forced, imp + FORCE_SCORE, imp)
        imp = jnp.where((blk <= cur) & (blk < n_sel), imp, NEG_INF)
        rank = jnp.zeros((tq, nselp), F32)
        for i in range(n_sel):
            vi = imp[:, i:i + 1]
            rank = rank + jnp.where(vi > imp, 1.0, jnp.where(vi == imp, jnp.where(blk > i, 1.0, 0.0), 0.0))
        sel = jnp.where((rank < min(SEL_TOP, n_sel)) & (blk < n_sel), 1.0, 0.0).astype(BF16)
        per = KSTEP // SEL_BLOCK
        for c in range(lpad // KSTEP):
            km = jnp.dot(sel[:, c * per:(c + 1) * per], expand, preferred_element_type=F32)
            kmask_ref[0, g, :, c * KSTEP:(c + 1) * KSTEP] = km.astype(BF16)
        if lpad % KSTEP:
            b0 = (lpad // KSTEP) * per
            km = jnp.where(lane128 < SEL_BLOCK, sel[:, b0:b0 + 1].astype(F32), 0.0)
            kmask_ref[0, g, :, (lpad // KSTEP) * KSTEP:lpad] = jnp.broadcast_to(km, (tq, LANES)).astype(BF16)

    nk = nwb * PAGE + (LANES if has_tail else 0)
    kk = lax.broadcasted_iota(jnp.int32, (1, nk), 1)
    if has_tail:
        kpos = jnp.where(kk < nwb * PAGE, win_k0 + kk, q0 + kk - nwb * PAGE)
    else:
        kpos = (qt - (nwb - 1)) * PAGE + kk
    rel = qpos - kpos
    wmask = (rel >= 0) & (rel < WINDOW) & (kpos >= 0)
    for g in range(NSA_KV):
        if has_tail:
            kparts = [w[0, 0, :, 0, g, :] for w in wins]
            vparts = [w[0, 0, :, 1, g, :] for w in wins]
        else:
            kparts = [w[:, g * HEAD_DIM:(g + 1) * HEAD_DIM] for w in wins]
            vparts = [w[:, (2 + g) * HEAD_DIM:(3 + g) * HEAD_DIM] for w in wins]
        if has_tail:
            zpad = jnp.zeros((LANES - SUBLANES, HEAD_DIM), F32)
            kparts += [tail_ref[:, g * HEAD_DIM:(g + 1) * HEAD_DIM], zpad]
            vparts += [tail_ref[:, (2 + g) * HEAD_DIM:(3 + g) * HEAD_DIM], zpad]
        kw = jnp.concatenate(kparts, axis=0).astype(BF16)
        vw = jnp.concatenate(vparts, axis=0).astype(BF16)
        q = _stack_heads(qrot_ref, g, r_per_g)
        s = _dot_nt(q, kw) * ATT_SCALE + jnp.concatenate([jnp.where(wmask, 0.0, NEG_INF)] * r_per_g, axis=0)
        p, den = _softmax_parts(s)
        o = jnp.dot((p / den).astype(BF16), vw, preferred_element_type=F32)
        for r in range(r_per_g):
            h = g * r_per_g + r
            owin_ref[:, h * HEAD_DIM:(h + 1) * HEAD_DIM] = o[r * tq:(r + 1) * tq]


def _nsa1(hp, cmp, ov, win_src, tail_src, *, nb, tq, qt_n, q0, n_cmp, n_sel, lpad, nwb, win_k0, win_layer):
    n = hp.shape[0]
    ncp = cmp.shape[1]
    nselp = ov.shape[1]
    has_tail = tail_src is not None
    rows = lambda b, t: b * qt_n + t
    in_specs = [pl.BlockSpec((tq, 1024), lambda b, t: (rows(b, t), C_QRAW // 1024)),
                pl.BlockSpec((tq, 1024), lambda b, t: (rows(b, t), C_QROT // 1024)),
                pl.BlockSpec((1, ncp, 512), lambda b, t: (b, 0, 0)),
                pl.BlockSpec((ncp, nselp), lambda b, t: (0, 0))]
    if has_tail:
        in_specs += [pl.BlockSpec((1, 1, PAGE, 2, NSA_KV, HEAD_DIM), lambda b, t, k=k: (win_layer, b, k, 0, 0, 0))
                     for k in range(nwb)]
        in_specs += [pl.BlockSpec((SUBLANES, 512), lambda b, t: (b, C_WIN // 512))]
        args = [win_src] * nwb + [tail_src]
    else:
        in_specs += [pl.BlockSpec((PAGE, 512),
                                  lambda b, t, k=k: (b * qt_n + jnp.maximum(t - (nwb - 1) + k, 0), C_WIN // 512))
                     for k in range(nwb)]
        args = [win_src] * nwb
    return pl.pallas_call(
        functools.partial(_nsa1_body, tq=tq, q0=q0, n_cmp=n_cmp, ncp=ncp, n_sel=n_sel, nselp=nselp, lpad=lpad,
                          nwb=nwb, has_tail=has_tail, win_k0=win_k0),
        grid=(nb, qt_n),
        in_specs=in_specs,
        out_specs=[pl.BlockSpec((tq, 1024), lambda b, t: (rows(b, t), 0)),
                   pl.BlockSpec((tq, 1024), lambda b, t: (rows(b, t), 0)),
                   pl.BlockSpec((1, NSA_KV, tq, lpad), lambda b, t: (b, 0, t, 0))],
        out_shape=[jax.ShapeDtypeStruct((n, 1024), F32), jax.ShapeDtypeStruct((n, 1024), F32),
                   jax.ShapeDtypeStruct((nb, NSA_KV, qt_n * tq, lpad), BF16)],
        compiler_params=_cparams("arbitrary", "arbitrary"),
        name="nsa_cmp_select_window",
    )(hp, hp, cmp, ov, *args)


def _mattn_body(pt_ref, *refs, layout, n_g, r_per_g, gm, tq, q0, kg_n, has_tail):
    q_ref, mask_ref = refs[:2]
    pages = refs[2:2 + PAGES_PER_STEP]
    pos = 2 + PAGES_PER_STEP
    if has_tail:
        tail_ref, tmask_ref = refs[pos:pos + 2]
        pos += 2
    o_ref, s_ref, v_ref, x_ref = refs[pos:pos + 4]
    qt = pl.program_id(1)
    step = pl.program_id(2)
    two_pass = kg_n > 1 or has_tail
    kg = step
    qpos = q0 + qt * tq + lax.broadcasted_iota(jnp.int32, (tq, 1), 0)

    def scores(g, kt, kpos, mref):
        picked = jnp.where(mref[0, g if gm > 1 else 0].astype(F32) > 0.5, 0.0, NEG_INF)
        bias = jnp.concatenate([picked + jnp.where(kpos <= qpos, 0.0, NEG_INF)] * r_per_g, axis=0)
        return _dot_nt(_stack_heads(q_ref, g, r_per_g), kt) * ATT_SCALE + bias

    def emit(g, o):
        for r in range(r_per_g):
            h = g * r_per_g + r
            o_ref[:, h * HEAD_DIM:(h + 1) * HEAD_DIM] = o[r * tq:(r + 1) * tq]

    if layout.kind == "cache":
        for slot in range(2):
            for k, pg in enumerate(pages):
                x_ref[slot, k * 2 * PAGE:(k + 1) * 2 * PAGE, :] = pg[0, 0, :, slot, :, :].reshape(2 * PAGE, HEAD_DIM)

    def main_keys(g):
        if layout.kind == "cache":
            return x_ref[0, pl.ds(g, KSTEP, stride=2), :].astype(BF16)
        return jnp.concatenate([_page_tile(layout, pg, g) for pg in pages], axis=0).astype(BF16)

    def main_vals(g):
        if layout.kind == "cache":
            return x_ref[1, pl.ds(g, KSTEP, stride=2), :].astype(BF16)
        return jnp.concatenate([_page_tile(layout, pg, n_g + g) for pg in pages], axis=0).astype(BF16)

    def tail_part(g, off):
        zpad = jnp.zeros((LANES - SUBLANES, HEAD_DIM), F32)
        return jnp.concatenate([tail_ref[:, (off + g) * HEAD_DIM:(off + g + 1) * HEAD_DIM], zpad], axis=0).astype(BF16)

    kpos = kg * KSTEP + lax.broadcasted_iota(jnp.int32, (1, KSTEP), 1)
    tpos = q0 + lax.broadcasted_iota(jnp.int32, (1, LANES), 1)

    if not two_pass:
        for g in range(n_g):
            p, den = _softmax_parts(scores(g, main_keys(g), kpos, mask_ref))
            emit(g, jnp.dot((p / den).astype(BF16), main_vals(g), preferred_element_type=F32))
        return

    for g in range(n_g):
        s_ref[g, kg] = scores(g, main_keys(g), kpos, mask_ref)
        v_ref[g, pl.ds(pl.multiple_of(kg * KSTEP, KSTEP), KSTEP), :] = main_vals(g)

    @pl.when(step == kg_n - 1)
    def _():
        for g in range(n_g):
            chunks = [s_ref[g, c] for c in range(kg_n)]
            vals = [v_ref[g, c * KSTEP:(c + 1) * KSTEP, :] for c in range(kg_n)]
            if has_tail:
                chunks.append(scores(g, tail_part(g, 0), tpos, tmask_ref))
                vals.append(tail_part(g, n_g))
            m = chunks[0].max(axis=-1, keepdims=True)
            for c in chunks[1:]:
                m = jnp.maximum(m, c.max(axis=-1, keepdims=True))
            m = jnp.where(m == NEG_INF, 0.0, m)
            exps = [jnp.exp(c - m) for c in chunks]
            den = exps[0].sum(axis=-1, keepdims=True)
            for e in exps[1:]:
                den = den + e.sum(axis=-1, keepdims=True)
            den = jnp.maximum(den, 1e-30)
            o = jnp.zeros((r_per_g * tq, HEAD_DIM), F32)
            for e, v in zip(exps, vals):
                o = o + jnp.dot((e / den).astype(BF16), v, preferred_element_type=F32)
            emit(g, o)


def _mattn(hp, q_col, n_heads, mask, pool, layout, pt, tail_src, tail_col_block, *, nb, tq, qt_n, q0, name):
    n = hp.shape[0]
    n_g = 2
    r_per_g = n_heads // n_g
    qw = n_heads * HEAD_DIM
    gm = mask.shape[1]
    kg_n = pt.shape[1] // PAGES_PER_STEP
    has_tail = tail_src is not None
    buffered = kg_n > 1 or has_tail
    in_specs = [pl.BlockSpec((tq, qw), lambda b, t, k, pt: (b * qt_n + t, q_col // qw)),
                pl.BlockSpec((1, gm, tq, KSTEP), lambda b, t, k, pt: (b, 0, t, k))]
    in_specs += _page_specs(layout, 3)
    args = [hp, mask] + [pool] * PAGES_PER_STEP
    if has_tail:
        in_specs += [pl.BlockSpec((SUBLANES, 512), lambda b, t, k, pt: (b, tail_col_block)),
                     pl.BlockSpec((1, gm, tq, LANES), lambda b, t, k, pt: (b, 0, t, kg_n * KSTEP // LANES))]
        args += [tail_src, mask]
    return pl.pallas_call(
        functools.partial(_mattn_body, layout=layout, n_g=n_g, r_per_g=r_per_g, gm=gm, tq=tq, q0=q0, kg_n=kg_n, has_tail=has_tail),
        grid_spec=pltpu.PrefetchScalarGridSpec(
            num_scalar_prefetch=1, grid=(nb, qt_n, kg_n),
            in_specs=in_specs,
            out_specs=pl.BlockSpec((tq, qw), lambda b, t, k, pt: (b * qt_n + t, 0)),
            scratch_shapes=[pltpu.VMEM((n_g, kg_n, r_per_g * tq, KSTEP) if buffered else (1, 1, SUBLANES, LANES), F32),
                            pltpu.VMEM((n_g, kg_n * KSTEP, HEAD_DIM) if buffered else (1, 2 * SUBLANES, LANES), BF16),
                            pltpu.VMEM((2, 2 * KSTEP, HEAD_DIM) if layout.kind == "cache" else (1, SUBLANES, LANES),
                                       F32)]),
        out_shape=jax.ShapeDtypeStruct((n, qw), F32),
        compiler_params=_cparams("arbitrary", "arbitrary", "arbitrary"),
        name=name,
    )(pt, *args)


PACKED_ROWS = 2 * SUBLANES


def _topk_mask_packed_t(sc_t, n_keep):
    n_keys, n_q = sc_t.shape
    i16 = jnp.int16
    bits = pltpu.bitcast(sc_t + 0.0, jnp.int32)
    key = jnp.where(bits < 0, bits ^ jnp.int32(0x7FFFFFFF), bits)
    hi = jnp.right_shift(key, 16).astype(i16)
    lo = ((key & jnp.int32(0xFFFF)) - 32768).astype(i16)
    one_b, zero_b = jnp.asarray(1, BF16), jnp.asarray(0, BF16)
    i16_min, i16_max = jnp.asarray(-32768, i16), jnp.asarray(32767, i16)
    assert n_keys % PACKED_ROWS == 0 and n_keys // PACKED_ROWS <= 256

    def colsum(x01):
        parts = [x01[i * PACKED_ROWS:(i + 1) * PACKED_ROWS, :] for i in range(n_keys // PACKED_ROWS)]
        while len(parts) > 1:
            parts = [parts[i] + parts[i + 1] for i in range(0, len(parts), 2)]
        return jnp.sum(parts[0].astype(F32), axis=0, keepdims=True)

    def search(vals, need, nbits, start):
        def body(i, thr):
            cand = thr + jnp.left_shift(jnp.int32(1), nbits - 1 - i)
            cnt = colsum(jnp.where(vals >= cand.astype(i16), one_b, zero_b))
            return jnp.where(cnt >= need, cand, thr)
        return lax.fori_loop(0, nbits, body, jnp.full((1, n_q), start, jnp.int32))

    t_hi = search(hi, float(n_keep), 16, -32768).astype(i16)
    eq = hi == t_hi
    hi_gt = jnp.where(hi > t_hi, one_b, zero_b)
    lo_m = jnp.where(eq, lo, i16_min)
    t_lo = search(lo_m, n_keep - colsum(hi_gt), 16, -32768).astype(i16)
    gt = hi_gt + jnp.where(lo_m > t_lo, one_b, zero_b)
    need = n_keep - colsum(gt)
    idx = lax.broadcasted_iota(jnp.int32, (n_keys, 1), 0).astype(i16)
    tie_idx = jnp.where(eq, jnp.where(lo == t_lo, idx, i16_max), i16_max)
    nbits = int(np.ceil(np.log2(n_keys)))

    def ibody(i, c):
        cand = c + jnp.left_shift(jnp.int32(1), nbits - 1 - i)
        cnt = colsum(jnp.where(tie_idx < cand.astype(i16), one_b, zero_b))
        return jnp.where(cnt < need, cand, c)

    cut = lax.fori_loop(0, nbits, ibody, jnp.zeros((1, n_q), jnp.int32)).astype(i16)
    return jnp.maximum(gt, jnp.where(tie_idx <= cut, one_b, zero_b))


def _dsa1_body(pt_ref, *refs, layout, tq, q0, kg_n, lpad, has_tail, n_keep):
    iq_ref, misc_ref = refs[:2]
    pages = refs[2:2 + PAGES_PER_STEP]
    pos = 2 + PAGES_PER_STEP
    if has_tail:
        tail_ref = refs[pos]
        pos += 1
    mask_ref, sc_ref = refs[pos:pos + 2]
    qt = pl.program_id(1)
    kg = pl.program_id(2)
    qpos = q0 + qt * tq + lax.broadcasted_iota(jnp.int32, (tq, 1), 0)
    lane128 = lax.broadcasted_iota(jnp.int32, (1, LANES), 1)

    if kg_n == 1 and not has_tail and tq == LANES:
        kt = jnp.concatenate([_page_tile(layout, pg, 0, IDX_DIM) for pg in pages], axis=0).astype(BF16)
        misc_t = misc_ref[...].T
        kpos_c = lax.broadcasted_iota(jnp.int32, (KSTEP, 1), 0)
        qpos_r = q0 + qt * tq + lax.broadcasted_iota(jnp.int32, (1, tq), 1)
        acc = jnp.zeros((KSTEP, tq), F32)
        for h in range(IDX_HEADS):
            qi = iq_ref[:, h * IDX_DIM:(h + 1) * IDX_DIM].astype(BF16)
            acc = acc + misc_t[MISC_IW + h:MISC_IW + h + 1, :] * jnp.maximum(_dot_nt(kt, qi), 0.0)
        sel_t = _topk_mask_packed_t(jnp.where(kpos_c <= qpos_r, acc, NEG_INF), n_keep)
        for c in range(KSTEP // LANES):
            blk = sel_t[c * LANES:(c + 1) * LANES, :].astype(F32).T
            mask_ref[0, 0, :, c * LANES:(c + 1) * LANES] = blk.astype(BF16)
        return

    def scores(kt, kpos, keys_on_lanes=False):
        acc = jnp.zeros((tq, kpos.shape[1]), F32)
        for h in range(IDX_HEADS):
            qi = iq_ref[:, h * IDX_DIM:(h + 1) * IDX_DIM].astype(BF16)
            w = jnp.sum(jnp.where(lane128 == MISC_IW + h, misc_ref[...], 0.0), axis=-1, keepdims=True)
            qk = jnp.dot(qi, kt, preferred_element_type=F32) if keys_on_lanes else _dot_nt(qi, kt)
            acc = acc + w * jnp.maximum(qk, 0.0)
        return jnp.where(kpos <= qpos, acc, NEG_INF)

    kpos = kg * KSTEP + lax.broadcasted_iota(jnp.int32, (1, KSTEP), 1)
    if layout.kind == "cache4t":
        kt = jnp.concatenate([pg[0, 0] for pg in pages], axis=1).astype(BF16)
        sc_ref[kg] = scores(kt, kpos, keys_on_lanes=True)
    else:
        kt = jnp.concatenate([_page_tile(layout, pg, 0, IDX_DIM) for pg in pages], axis=0).astype(BF16)
        sc_ref[kg] = scores(kt, kpos)

    @pl.when(kg == kg_n - 1)
    def _():
        nch = sc_ref.shape[0]
        if has_tail:
            zpad = jnp.zeros((KSTEP - SUBLANES, IDX_DIM), F32)
            tk = jnp.concatenate([tail_ref[:, 0:IDX_DIM], zpad], axis=0).astype(BF16)
            tl = lax.broadcasted_iota(jnp.int32, (1, KSTEP), 1)
            sc_ref[kg_n] = jnp.where(tl < LANES, scores(tk, q0 + tl), NEG_INF)
        sc = sc_ref[...] + 0.0
        bits = pltpu.bitcast(sc, jnp.int32)
        key = jnp.where(bits < 0, bits ^ jnp.int32(0x7FFFFFFF), bits)
        int_min = jnp.int32(-2 ** 31)

        def count(pred):
            return jnp.sum(jnp.sum(jnp.where(pred, 1.0, 0.0), axis=-1, keepdims=True), axis=0, keepdims=True)

        def vbit(i, thr):
            cand = thr + jnp.left_shift(jnp.int32(1), 31 - i)
            return jnp.where(count(key >= cand) >= n_keep, cand, thr)

        thr = lax.fori_loop(0, 32, vbit, jnp.full((1, tq, 1), int_min, jnp.int32))
        gt = key > thr
        tie = key == thr
        need = n_keep - count(gt)
        idx = (lax.broadcasted_iota(jnp.int32, (nch, 1, KSTEP), 0) * KSTEP
               + lax.broadcasted_iota(jnp.int32, (nch, 1, KSTEP), 2))
        nbits = int(np.ceil(np.log2(nch * KSTEP)))

        def ibit(i, c):
            cand = c + jnp.left_shift(jnp.int32(1), nbits - 1 - i)
            return jnp.where(count(tie & (idx < cand)) < need, cand, c)

        cut = lax.fori_loop(0, nbits, ibit, jnp.zeros((1, tq, 1), jnp.int32))
        sel = jnp.where(gt | (tie & (idx <= cut)), 1.0, 0.0).astype(BF16)
        for c in range(kg_n):
            mask_ref[0, 0, :, c * KSTEP:(c + 1) * KSTEP] = sel[c]
        if has_tail:
            mask_ref[0, 0, :, kg_n * KSTEP:lpad] = sel[kg_n][:, 0:lpad - kg_n * KSTEP]


def _dsa1(hp, pool, layout, pt, tail_src, *, nb, tq, qt_n, q0, lpad, n_keep):
    kg_n = pt.shape[1] // PAGES_PER_STEP
    has_tail = tail_src is not None
    in_specs = [pl.BlockSpec((tq, 256), lambda b, t, k, pt: (b * qt_n + t, C_IQ // 256)),
                pl.BlockSpec((tq, LANES), lambda b, t, k, pt: (b * qt_n + t, C_MISC // LANES))]
    in_specs += _page_specs(layout, 3)
    args = [hp, hp] + [pool] * PAGES_PER_STEP
    if has_tail:
        in_specs += [pl.BlockSpec((SUBLANES, LANES), lambda b, t, k, pt: (b, C_MISC // LANES))]
        args += [tail_src]
    return pl.pallas_call(
        functools.partial(_dsa1_body, layout=layout, tq=tq, q0=q0, kg_n=kg_n, lpad=lpad, has_tail=has_tail,
                          n_keep=n_keep),
        grid_spec=pltpu.PrefetchScalarGridSpec(
            num_scalar_prefetch=1, grid=(nb, qt_n, kg_n),
            in_specs=in_specs,
            out_specs=pl.BlockSpec((1, 1, tq, lpad), lambda b, t, k, pt: (b, 0, t, 0)),
            scratch_shapes=[pltpu.VMEM((kg_n + (1 if has_tail else 0), tq, KSTEP), F32)]),
        out_shape=jax.ShapeDtypeStruct((nb, 1, qt_n * tq, lpad), BF16),
        compiler_params=_cparams("arbitrary", "arbitrary", "arbitrary"),
        name="dsa_indexer_topk",
    )(pt, *args)


def _mix_body(ocmp_ref, osel_ref, owin_ref, odsa_ref, conv_ref, misc_ref, mg0_ref, mg1_ref, mg2_ref,
              wa_ref, wb_ref, wc_ref, z_ref):
    lane128 = lax.broadcasted_iota(jnp.int32, (1, LANES), 1)
    misc = misc_ref[...]

    def gate(kind, h):
        return jnp.sum(jnp.where(lane128 == MISC_NG + kind * NSA_HEADS + h, misc, 0.0), axis=-1, keepdims=True)

    parts = []
    for h in range(NSA_HEADS):
        sl = slice(h * HEAD_DIM, (h + 1) * HEAD_DIM)
        parts.append((gate(0, h) * ocmp_ref[:, sl] + gate(1, h) * osel_ref[:, sl]
                      + gate(2, h) * owin_ref[:, sl]).astype(BF16))
    o_nsa = jnp.concatenate(parts, axis=1)
    p_a = jnp.dot(o_nsa, wa_ref[...], preferred_element_type=F32)
    p_b = jnp.dot(odsa_ref[...].astype(BF16), wb_ref[...], preferred_element_type=F32)
    p_c = jnp.dot(conv_ref[...].astype(BF16), wc_ref[...], preferred_element_type=F32)
    z_ref[...] = (mg0_ref[...] * p_a + mg1_ref[...] * p_b + mg2_ref[...] * p_c).astype(BF16)


def _mix(hp, o_cmp, o_sel, o_win, o_dsa, conv_out, wa, wb, wc, tm):
    n = hp.shape[0]
    row = lambda w, cb=0: pl.BlockSpec((tm, w), lambda i: (i, cb))
    full = lambda a: pl.BlockSpec(a.shape, lambda i: (0, 0))
    return pl.pallas_call(
        _mix_body,
        grid=(n // tm,),
        in_specs=[row(1024), row(1024), row(1024), row(512), row(512), row(LANES, C_MISC // LANES),
                  row(D_MODEL, C_MG // D_MODEL), row(D_MODEL, C_MG // D_MODEL + 1), row(D_MODEL, C_MG // D_MODEL + 2),
                  full(wa), full(wb), full(wc)],
        out_specs=row(D_MODEL),
        out_shape=jax.ShapeDtypeStruct((n, D_MODEL), BF16),
        compiler_params=_cparams("arbitrary"),
        name="branch_merge",
    )(o_cmp, o_sel, o_win, o_dsa, conv_out, hp, hp, hp, hp, wa, wb, wc)


def _outln_body(z_ref, wo_ref, x_ref, g_ref, b_ref, wr_ref, rb_ref, x1_ref, gate_ref, *, tm):
    y = jnp.dot(z_ref[...], wo_ref[...], preferred_element_type=F32)
    x1 = _layernorm(ALPHA * x_ref[...] + y, g_ref[...], b_ref[...])
    x1_ref[...] = x1
    x_hi = x1.astype(BF16)
    x_lo = (x1 - x_hi.astype(F32)).astype(BF16)
    w_hi = wr_ref[...].astype(BF16)
    w_lo = (wr_ref[...] - w_hi.astype(F32)).astype(BF16)
    aff = _sigmoid(_dot_nt(w_hi, x_hi) + (_dot_nt(w_hi, x_lo) + _dot_nt(w_lo, x_hi)))
    biased = aff + rb_ref[:, 0:1]
    rows = [biased[e:e + 1, :] for e in range(N_EXPERTS)]
    best = None
    g_best = jnp.zeros((1, tm), jnp.int32)
    for g in range(N_GROUPS):
        v = rows[g * EXPERTS_PER_GROUP:(g + 1) * EXPERTS_PER_GROUP]
        score = None
        for a in range(EXPERTS_PER_GROUP):
            for c in range(a + 1, EXPERTS_PER_GROUP):
                pair = v[a] + v[c]
                score = pair if score is None else jnp.maximum(score, pair)
        if best is None:
            best = score
        else:
            better = score > best
            best = jnp.where(better, score, best)
            g_best = jnp.where(better, g, g_best)
    sel_rows = []
    for e in range(N_EXPERTS):
        g = e // EXPERTS_PER_GROUP
        rank = jnp.zeros((1, tm), F32)
        for o in range(g * EXPERTS_PER_GROUP, (g + 1) * EXPERTS_PER_GROUP):
            if o == e:
                continue
            beats = (rows[o] > rows[e]) | ((rows[o] == rows[e]) & (o < e))
            rank = rank + jnp.where(beats, 1.0, 0.0)
        sel_rows.append(jnp.where((g_best == g) & (rank < 2), aff[e:e + 1, :], 0.0))
    tot = sel_rows[0]
    for e in range(1, N_EXPERTS):
        tot = tot + sel_rows[e]
    gate_t = jnp.concatenate(sel_rows + [jnp.zeros((LANES - N_EXPERTS, tm), F32)], axis=0) / tot
    gate_ref[...] = gate_t.T


def _outln(z, wo, x, g, b, wr_t, rb, tm):
    n = z.shape[0]
    row = lambda w: pl.BlockSpec((tm, w), lambda i: (i, 0))
    full = lambda a: pl.BlockSpec(a.shape, lambda i: (0, 0))
    return pl.pallas_call(
        functools.partial(_outln_body, tm=tm),
        grid=(n // tm,),
        in_specs=[row(D_MODEL), full(wo), row(D_MODEL), full(g), full(b), full(wr_t), full(rb)],
        out_specs=[row(D_MODEL), row(LANES)],
        out_shape=[jax.ShapeDtypeStruct((n, D_MODEL), F32), jax.ShapeDtypeStruct((n, LANES), F32)],
        compiler_params=_cparams("arbitrary"),
        name="out_proj_ln_router",
    )(z, wo, x, g, b, wr_t, rb)


def _moe_body(x_ref, gate_ref, wg_ref, wu_ref, wd_ref, g_ref, b_ref, o_ref, xb_ref, acc_ref):
    e = pl.program_id(1)

    @pl.when(e == 0)
    def _():
        xb_ref[...] = x_ref[...].astype(BF16)
        acc_ref[...] = jnp.zeros(acc_ref.shape, F32)

    lane128 = lax.broadcasted_iota(jnp.int32, (1, LANES), 1)
    gcol = jnp.sum(jnp.where(lane128 == e, gate_ref[...], 0.0), axis=-1, keepdims=True)
    xb = xb_ref[...]
    hg = jnp.dot(xb, wg_ref[0], preferred_element_type=F32)
    hu = jnp.dot(xb, wu_ref[0], preferred_element_type=F32)
    h = (hg * _sigmoid(hg)) * hu * gcol
    acc_ref[...] += jnp.dot(h.astype(BF16), wd_ref[0], preferred_element_type=F32)

    @pl.when(e == N_EXPERTS - 1)
    def _():
        o_ref[...] = _layernorm(ALPHA * x_ref[...] + acc_ref[...], g_ref[...], b_ref[...])


def _moe(x1, gate, wg, wu, wd, g, b, tm):
    n = x1.shape[0]
    return pl.pallas_call(
        _moe_body,
        grid=(n // tm, N_EXPERTS),
        in_specs=[pl.BlockSpec((tm, D_MODEL), lambda i, e: (i, 0)),
                  pl.BlockSpec((tm, LANES), lambda i, e: (i, 0)),
                  pl.BlockSpec((1, D_MODEL, D_FF), lambda i, e: (e, 0, 0)),
                  pl.BlockSpec((1, D_MODEL, D_FF), lambda i, e: (e, 0, 0)),
                  pl.BlockSpec((1, D_FF, D_MODEL), lambda i, e: (e, 0, 0)),
                  pl.BlockSpec((1, D_MODEL), lambda i, e: (0, 0)),
                  pl.BlockSpec((1, D_MODEL), lambda i, e: (0, 0))],
        out_specs=pl.BlockSpec((tm, D_MODEL), lambda i, e: (i, 0)),
        out_shape=jax.ShapeDtypeStruct((n, D_MODEL), F32),
        scratch_shapes=[pltpu.VMEM((tm, D_MODEL), BF16), pltpu.VMEM((tm, D_MODEL), F32)],
        compiler_params=_cparams("arbitrary", "arbitrary"),
        name="moe_ln",
    )(x1, gate, wg, wu, wd, g, b)


def _overlap_matrix(ncp, nselp):
    cs = np.arange(ncp)[:, None] * CMP_STRIDE
    ss = np.arange(nselp)[None, :] * SEL_BLOCK
    return jnp.asarray(((cs < ss + SEL_BLOCK) & (cs + CMP_LEN > ss)).astype(np.float32))


class _Group:
    def __init__(self, nb, t_real, t_pad, q0, past_len, tm, tq):
        self.nb, self.t_real, self.t_pad, self.q0, self.past_len, self.tm, self.tq = nb, t_real, t_pad, q0, past_len, tm, tq
        self.paged = past_len > 0
        self.lp = past_len if self.paged else t_pad
        self.ltot = self.lp + (t_real if self.paged else 0)
        self.lpad = self.lp + (LANES if self.paged else 0)
        self.qt_n = t_pad // tq
        n_chunks = -(-self.ltot // CMP_STRIDE)
        self.n_cmp = n_chunks - CMP_LEN // CMP_STRIDE + 1
        self.n_sel = -(-self.ltot // SEL_BLOCK)
        self.nselp = -(-self.n_sel // LANES) * LANES
        self.n_keep = min(DSA_TOPK, self.ltot // 4)


def _mixer(gp, x, lw, caches):
    hp = _proj(x.astype(BF16), lw["w_in"], gp.tabs, lw["kinds"], min(2 * gp.tm, gp.nb * gp.t_pad, gp.tabs.shape[1]))
    nb, tq, qt_n, q0 = gp.nb, gp.tq, gp.qt_n, gp.q0
    if gp.paged:
        l, nsa_pool, dsa_pool, kidx_pool, win_src, s_conv, pt = caches
        cmp_lay = _PageLayout("cache", 0, 0, base=0, layer=l, nslot=2)
        slc_lay = _PageLayout("cache", 0, 1, base=0, layer=l, nslot=2)
        dsa_lay = _PageLayout("cache", 0, 0, base=0, layer=l, nslot=2)
        kidx_pool = jnp.transpose(kidx_pool, (0, 1, 3, 2))
        kidx_lay = _PageLayout("cache4t", IDX_DIM, 0, base=0, layer=l, nslot=0)
        tail = hp
        nwb, win_k0, win_layer = WINDOW // PAGE, PAST_LEN - WINDOW, l
        past8 = jnp.concatenate([jnp.zeros((nb, SUBLANES - 2, CONV_DIM), F32), s_conv[l]], axis=1)
    else:
        pt = gp.pt
        nsa_pool = dsa_pool = kidx_pool = hp.reshape(nb * gp.t_pad // PAGE, PAGE, NCOL)
        cmp_lay = _PageLayout("cols", 512, C_CMP // 512, base=0, layer=0, nslot=0)
        slc_lay = _PageLayout("cols", 512, C_SLC // 512, base=0, layer=0, nslot=0)
        dsa_lay = _PageLayout("cols", 512, C_DSA // 512, base=0, layer=0, nslot=0)
        kidx_lay = _PageLayout("cols", LANES, C_MISC // LANES, base=0, layer=0, nslot=0)
        tail = None
        win_src = hp
        nwb, win_k0, win_layer = WINDOW // PAGE + 1, 0, 0
        past8 = jnp.zeros((nb, SUBLANES, CONV_DIM), F32)

    conv_out, cu = _conv(hp, past8, lw["conv_w8"], min(gp.tm, gp.t_pad), gp.t_pad // min(gp.tm, gp.t_pad))

    ab = _cmp1(nsa_pool, cmp_lay, pt, lw["wcat"], lw["pe"])
    cmp = _cmp2(ab, tail, lw["pe"], lw["w1r"], lw["phi_w2"], gp.t_real if gp.paged else 0, C_CMP // 512)
    o_cmp, o_win, kmask = _nsa1(hp, cmp, gp.ov, win_src, tail, nb=nb, tq=tq, qt_n=qt_n, q0=q0, n_cmp=gp.n_cmp,
                                n_sel=gp.n_sel, lpad=gp.lpad, nwb=nwb, win_k0=win_k0, win_layer=win_layer)
    o_sel = _mattn(hp, C_QROT, NSA_HEADS, kmask, nsa_pool, slc_lay, pt, tail, C_SLC // 512,
                   nb=nb, tq=tq, qt_n=qt_n, q0=q0, name="nsa_selected_attn")
    dmask = _dsa1(hp, kidx_pool, kidx_lay, pt, tail, nb=nb, tq=tq, qt_n=qt_n, q0=q0, lpad=gp.lpad,
                  n_keep=gp.n_keep)
    o_dsa = _mattn(hp, C_DQ, DSA_HEADS, dmask, dsa_pool, dsa_lay, pt, tail, C_DSA // 512,
                   nb=nb, tq=tq, qt_n=qt_n, q0=q0, name="dsa_topk_attn")
    z = _mix(hp, o_cmp, o_sel, o_win, o_dsa, conv_out, lw["w_a"], lw["w_b"], lw["w_c"], min(gp.tm, 256))
    return z, hp, cu


def _layer(gp, x, lw, caches):
    z, hp, cu = _mixer(gp, x, lw, caches)
    tm2 = min(gp.tm, 256)
    x1, gate = _outln(z, lw["w_o"], x, lw["ln_mix_g"], lw["ln_mix_b"], lw["wr_t"], lw["rb"], tm2)
    x2 = _moe(x1, gate, lw["w_eg"], lw["w_eu"], lw["w_ed"], lw["ln_ffn_g"], lw["ln_ffn_b"], min(gp.tm, 512))
    return x2, hp, cu


def _layer_weights(l, w_in, nsa_phi_pos, nsa_phi_w1, nsa_phi_w2, conv_w, w_br_a, w_br_b, w_br_c, w_out,
                   ln_mix_g, ln_mix_b, ln_ffn_g, ln_ffn_b, w_router, router_bias, w_e_gate, w_e_up, w_e_down):
    w1r = nsa_phi_w1[l].reshape(2, CMP_LEN, HEAD_DIM, HEAD_DIM)
    wcat = jnp.concatenate([w1r[:, :CMP_STRIDE], w1r[:, CMP_STRIDE:]], axis=-1).astype(BF16)
    return dict(
        w_in=_permute_w_in(w_in, l), kinds=jnp.asarray(_col_kinds()),
        w1r=w1r, wcat=wcat, pe=nsa_phi_pos[l], phi_w2=nsa_phi_w2[l],
        conv_w8=jnp.concatenate([conv_w[l], jnp.zeros((SUBLANES - 3, CONV_DIM), F32)], axis=0),
        w_a=w_br_a[l].astype(BF16), w_b=w_br_b[l].astype(BF16), w_c=w_br_c[l].astype(BF16),
        w_o=w_out[l].astype(BF16),
        ln_mix_g=ln_mix_g[l][None], ln_mix_b=ln_mix_b[l][None],
        ln_ffn_g=ln_ffn_g[l][None], ln_ffn_b=ln_ffn_b[l][None],
        wr_t=w_router.T, rb=jnp.broadcast_to(router_bias[:, None], (N_EXPERTS, LANES)),
        w_eg=w_e_gate[l].astype(BF16), w_eu=w_e_up[l].astype(BF16), w_ed=w_e_down[l].astype(BF16))


def kernel(x_prompt, x_sample, cache_nsa_kv, cache_dsa_kv, cache_dsa_kidx, state_nsa_win, state_conv, page_table,
           w_in, nsa_phi_pos, nsa_phi_w1, nsa_phi_w2, conv_w, w_br_a, w_br_b, w_br_c, w_out, ln_mix_g, ln_mix_b,
           ln_ffn_g, ln_ffn_b, w_router, router_bias, w_e_gate, w_e_up, w_e_down):
    bp, tp, _ = x_prompt.shape
    bs, ts, _ = x_sample.shape
    ts_pad = SUBLANES

    gp_p = _Group(bp, tp, tp, 0, 0, tm=512, tq=128)
    gp_p.tabs = _rope_tables(jnp.arange(tp, dtype=jnp.int32))
    gp_p.pt = jnp.arange(bp * tp // PAGE, dtype=jnp.int32).reshape(bp, tp // PAGE)
    gp_p.ov = _overlap_matrix(LANES, gp_p.nselp)
    gp_s = _Group(bs, ts, ts_pad, PAST_LEN, PAST_LEN, tm=bs * ts_pad, tq=ts_pad)
    pos_s = PAST_LEN + jnp.arange(ts_pad, dtype=jnp.int32)
    gp_s.tabs = jnp.tile(_rope_tables(pos_s), (1, bs, 1))
    gp_s.ov = _overlap_matrix(PAST_LEN // CMP_STRIDE, gp_s.nselp)

    xp = x_prompt.reshape(bp * tp, D_MODEL)
    xs = jnp.concatenate([x_sample, jnp.zeros((bs, ts_pad - ts, D_MODEL), F32)], axis=1).reshape(bs * ts_pad, D_MODEL)

    outs_p = [[] for _ in range(5)]
    outs_s = [[] for _ in range(5)]
    for l in range(DEPTH):
        lw = _layer_weights(l, w_in, nsa_phi_pos, nsa_phi_w1, nsa_phi_w2, conv_w, w_br_a, w_br_b, w_br_c, w_out,
                            ln_mix_g, ln_mix_b, ln_ffn_g, ln_ffn_b, w_router, router_bias, w_e_gate, w_e_up, w_e_down)
        xp, hp_p, cu_p = _layer(gp_p, xp, lw, None)
        xs, hp_s, cu_s = _layer(gp_s, xs, lw, (l, cache_nsa_kv, cache_dsa_kv, cache_dsa_kidx,
                                                 state_nsa_win, state_conv, page_table))
        h3 = hp_p.reshape(bp, tp, NCOL)
        outs_p[0].append(h3[:, :, C_CMP:C_CMP + 1024].reshape(bp, tp, 4, NSA_KV, HEAD_DIM))
        outs_p[1].append(h3[:, :, C_DSA:C_DSA + 512].reshape(bp, tp, 2, DSA_KV, HEAD_DIM))
        outs_p[2].append(h3[:, :, C_MISC:C_MISC + IDX_DIM])
        outs_p[3].append(h3[:, tp - min(WINDOW, tp):, C_WIN:C_WIN + 512].reshape(bp, min(WINDOW, tp), 2, NSA_KV, HEAD_DIM))
        outs_p[4].append(cu_p.reshape(bp, tp, CONV_DIM)[:, tp - 2:])
        s3 = hp_s.reshape(bs, ts_pad, NCOL)[:, :ts]
        outs_s[0].append(s3[:, :, C_CMP:C_CMP + 1024].reshape(bs, ts, 4, NSA_KV, HEAD_DIM))
        outs_s[1].append(s3[:, :, C_DSA:C_DSA + 512].reshape(bs, ts, 2, DSA_KV, HEAD_DIM))
        outs_s[2].append(s3[:, :, C_MISC:C_MISC + IDX_DIM])
        win_new = s3[:, :, C_WIN:C_WIN + 512].reshape(bs, ts, 2, NSA_KV, HEAD_DIM)
        wb = state_nsa_win.shape[2]
        outs_s[3].append(jnp.concatenate([state_nsa_win[l], win_new], axis=1)[:, -wb:])
        ext = jnp.concatenate([state_conv[l], cu_s.reshape(bs, ts_pad, CONV_DIM)[:, :ts]], axis=1)
        outs_s[4].append(ext[:, -2:])
    sp = [jnp.stack(a, axis=0) for a in outs_p]
    ss = [jnp.stack(a, axis=0) for a in outs_s]
    y_p = xp.reshape(bp, tp, D_MODEL)
    y_s = xs.reshape(bs, ts_pad, D_MODEL)[:, :ts]
    return (y_p, y_s, sp[0], ss[0], sp[1], ss[1], sp[2], ss[2], sp[3], ss[3], sp[4], ss[4])
```

```python
import collections
import functools

import numpy as np
import jax
import jax.numpy as jnp
from jax import lax
from jax.experimental import pallas as pl
from jax.experimental.pallas import tpu as pltpu

F32 = jnp.float32
BF16 = jnp.bfloat16
NEG_INF = float("-inf")

D_MODEL = 2048
DEPTH = 2
PAST_LEN = 16384
PAGE = 128
HEAD_DIM = 128
ROPE_THETA = 500000.0
NSA_HEADS = 8
NSA_KV = 2
CMP_LEN = 32
CMP_STRIDE = 16
SEL_BLOCK = 64
SEL_TOP = 16
WINDOW = 512
FORCE_SCORE = 1e4
DSA_HEADS = 4
DSA_KV = 2
IDX_HEADS = 4
IDX_DIM = 64
DSA_TOPK = 256
CONV_DIM = 512
N_EXPERTS = 16
N_GROUPS = 4
EXPERTS_PER_GROUP = 4
D_FF = 512
LN_EPS = 1e-5
ALPHA = (2 * DEPTH) ** 0.25
IN_WIDTHS = (1024, 1536, 24, 512, 512, 256, 64, 4, 1536, 6144)
ATT_SCALE = HEAD_DIM ** -0.5

LANES = 128
SUBLANES = 8
VMEM_LIMIT = 56 * 1024 * 1024

C_QRAW = 0
C_QROT = 1024
C_CMP = 2048
C_SLC = 2560
C_WIN = 3072
C_DQ = 3584
C_DSA = 4096
C_CV = 4608
C_MG = 6144
C_IQ = 12288
C_MISC = 12544
NCOL = 12800
MISC_IW = 64
MISC_NG = 68
PROJ_TN = 1280
PAGES_PER_STEP = 16
KSTEP = PAGES_PER_STEP * PAGE

K_PLAIN, K_ROPE128, K_ROPE64, K_SIGMOID, K_MISC = 0, 1, 2, 3, 4


def _col_kinds():
    kinds = np.zeros(NCOL // LANES, np.int32)

    def mark(c0, n, k):
        kinds[c0 // LANES:(c0 + n) // LANES] = k

    mark(C_QROT, 1024, K_ROPE128)
    mark(C_SLC, 256, K_ROPE128)
    mark(C_WIN, 256, K_ROPE128)
    mark(C_DQ, 512, K_ROPE128)
    mark(C_DSA, 256, K_ROPE128)
    mark(C_IQ, 256, K_ROPE64)
    mark(C_MISC, 128, K_MISC)
    mark(C_MG, 6144, K_SIGMOID)
    return kinds


def _cparams(*sem):
    return pltpu.CompilerParams(dimension_semantics=sem, vmem_limit_bytes=VMEM_LIMIT)


def _sigmoid(x):
    return 1.0 / (1.0 + jnp.exp(-x))


def _layernorm(x, g, b):
    mu = jnp.mean(x, axis=-1, keepdims=True)
    xc = x - mu
    var = jnp.mean(xc * xc, axis=-1, keepdims=True)
    return xc * lax.rsqrt(var + LN_EPS) * g + b


def _dot_nt(a, b, precision=None):
    return lax.dot_general(a, b, (((1,), (1,)), ((), ())), preferred_element_type=F32, precision=precision)


def _softmax_parts(s):
    m = jnp.max(s, axis=-1, keepdims=True)
    m = jnp.where(m == NEG_INF, 0.0, m)
    p = jnp.exp(s - m)
    return p, jnp.maximum(jnp.sum(p, axis=-1, keepdims=True), 1e-30)


def _stack_heads(q_ref, g, r_per_g):
    return jnp.concatenate([q_ref[:, (g * r_per_g + r) * HEAD_DIM:(g * r_per_g + r + 1) * HEAD_DIM]
                            for r in range(r_per_g)], axis=0).astype(BF16)


def _proj_body(kinds_ref, needs_ref, x_ref, wt_ref, tab_ref, o_ref, w_ref):
    nsub = PROJ_TN // LANES
    j = pl.program_id(0)

    @pl.when(pl.program_id(1) == 0)
    def _():
        for s in range(nsub):
            w_ref[:, s * LANES:(s + 1) * LANES] = wt_ref[s * LANES:(s + 1) * LANES, :].astype(F32).T.astype(BF16)

    h = jnp.dot(x_ref[...], w_ref[...], preferred_element_type=F32)
    lane = lax.broadcasted_iota(jnp.int32, (1, LANES), 1)

    def rope(hs, t0, sh):
        return (hs * tab_ref[t0] + pltpu.roll(hs, sh, 1) * tab_ref[t0 + 1]
                + pltpu.roll(hs, LANES - sh, 1) * tab_ref[t0 + 2])

    for s in range(nsub):
        kind = kinds_ref[j * nsub + s]
        hs = h[:, s * LANES:(s + 1) * LANES]
        sl = slice(s * LANES, (s + 1) * LANES)

        @pl.when(kind == K_PLAIN)
        def _():
            o_ref[:, sl] = hs

        @pl.when(kind == K_ROPE128)
        def _():
            o_ref[:, sl] = rope(hs, 0, 16)

        @pl.when(kind == K_ROPE64)
        def _():
            o_ref[:, sl] = rope(hs, 3, 8)

        @pl.when(kind == K_SIGMOID)
        def _():
            o_ref[:, sl] = _sigmoid(hs)

        @pl.when(kind == K_MISC)
        def _():
            r = rope(hs, 3, 8)
            o_ref[:, sl] = jnp.where(lane < MISC_IW, r,
                                     jnp.where(lane < MISC_NG, hs * (IDX_HEADS ** -0.5),
                                               jnp.where(lane < MISC_NG + 24, _sigmoid(hs), hs)))


def _proj(x_bf, wt_bf, tabs, kinds, tm):
    n = x_bf.shape[0]
    n_tab = tabs.shape[1] // tm
    grid = (NCOL // PROJ_TN, n // tm)
    rotary = np.isin(_col_kinds().reshape(-1, PROJ_TN // LANES), (K_ROPE128, K_ROPE64, K_MISC)).any(axis=1)
    needs = jnp.asarray(rotary.astype(np.int32))
    return pl.pallas_call(
        _proj_body,
        grid_spec=pltpu.PrefetchScalarGridSpec(
            num_scalar_prefetch=2, grid=grid,
            in_specs=[pl.BlockSpec((tm, D_MODEL), lambda j, i, k, nd: (i, 0)),
                      pl.BlockSpec((PROJ_TN, D_MODEL), lambda j, i, k, nd: (j, 0)),
                      pl.BlockSpec((6, tm, LANES), lambda j, i, k, nd: (0, (i % n_tab) * nd[j], 0))],
            out_specs=pl.BlockSpec((tm, PROJ_TN), lambda j, i, k, nd: (i, j)),
            scratch_shapes=[pltpu.VMEM((D_MODEL, PROJ_TN), BF16)]),
        out_shape=jax.ShapeDtypeStruct((n, NCOL), F32),
        compiler_params=_cparams("arbitrary", "arbitrary"),
        name="in_proj",
    )(kinds, needs, x_bf, wt_bf, tabs)


def _rope_tables(pos):
    out = []
    lane = jnp.arange(LANES)
    for d in (HEAD_DIM, IDX_DIM):
        rot = d // 4
        half = rot // 2
        inv = ROPE_THETA ** (-jnp.arange(half, dtype=F32) / half)
        ang = pos.astype(F32)[:, None] * inv[None, :]
        cos = jnp.cos(ang)
        sin = jnp.sin(ang)
        li = lane % d
        ci = jnp.take(cos, li % half, axis=1)
        si = jnp.take(sin, li % half, axis=1)
        out.append(jnp.where(li[None] < rot, ci, 1.0))
        out.append(jnp.where((li[None] >= half) & (li[None] < rot), si, 0.0))
        out.append(jnp.where(li[None] < half, -si, 0.0))
    return jnp.stack(out, axis=0)


def _permute_w_in(w_in, l):
    wt = jnp.transpose(w_in, (2, 0, 1))[:, l, :].astype(BF16)
    offs = np.cumsum((0,) + IN_WIDTHS)
    nq, nkv, ng, dq, dkv, iq, ik, iw, cv, mg = [wt[offs[i]:offs[i + 1]] for i in range(10)]
    z = lambda n: jnp.zeros((n, wt.shape[1]), BF16)
    rows = [nq, nq, nkv, dq, dkv, cv, mg, iq, ik, iw, ng, z(LANES - 92), z(NCOL - C_MISC - LANES)]
    return jnp.concatenate(rows, axis=0)


def _conv_body(cv_ref, prev_ref, past_ref, w_ref, y_ref, cu_ref, s_ref, *, tiles_per_seq, tm):
    i = pl.program_id(0)
    b = cv_ref[:, 0:CONV_DIM]
    cu = cv_ref[:, CONV_DIM:2 * CONV_DIM] * cv_ref[:, 2 * CONV_DIM:3 * CONV_DIM]
    first = (i % tiles_per_seq) == 0
    prev = prev_ref[:, CONV_DIM:2 * CONV_DIM] * prev_ref[:, 2 * CONV_DIM:3 * CONV_DIM]
    s_ref[0:SUBLANES, :] = jnp.where(first, past_ref[0], prev)
    s_ref[SUBLANES:SUBLANES + tm, :] = cu
    y = (w_ref[0:1, :] * s_ref[pl.ds(SUBLANES - 2, tm), :] + w_ref[1:2, :] * s_ref[pl.ds(SUBLANES - 1, tm), :]
         + w_ref[2:3, :] * cu)
    y_ref[...] = b * y
    cu_ref[...] = cu


def _conv(hp, past8, conv_w8, tm, tiles_per_seq):
    n = hp.shape[0]
    cvb = C_CV // (3 * CONV_DIM)
    rb = tm // SUBLANES
    return pl.pallas_call(
        functools.partial(_conv_body, tiles_per_seq=tiles_per_seq, tm=tm),
        grid=(n // tm,),
        in_specs=[pl.BlockSpec((tm, 3 * CONV_DIM), lambda i: (i, cvb)),
                  pl.BlockSpec((SUBLANES, 3 * CONV_DIM), lambda i: (jnp.maximum(i * rb - 1, 0), cvb)),
                  pl.BlockSpec((1, SUBLANES, CONV_DIM), lambda i: (i // tiles_per_seq, 0, 0)),
                  pl.BlockSpec((SUBLANES, CONV_DIM), lambda i: (0, 0))],
        out_specs=[pl.BlockSpec((tm, CONV_DIM), lambda i: (i, 0)),
                   pl.BlockSpec((tm, CONV_DIM), lambda i: (i, 0))],
        out_shape=[jax.ShapeDtypeStruct((n, CONV_DIM), F32), jax.ShapeDtypeStruct((n, CONV_DIM), F32)],
        scratch_shapes=[pltpu.VMEM((tm + SUBLANES, CONV_DIM), F32)],
        compiler_params=_cparams("arbitrary"),
        name="short_conv",
    )(hp, hp, past8, conv_w8)


_PageLayout = collections.namedtuple("_PageLayout", "kind width col_block base layer nslot")


def _page_specs(layout, n_lead, kg_n=None):
    def spec(k):
        def imap(*a):
            ids, pt = a[:n_lead], a[-1]
            kg = ids[-1] % kg_n if kg_n else ids[-1]
            page = pt[ids[0], kg * PAGES_PER_STEP + k]
            if layout.kind == "cache":
                return (layout.layer, page, 0, layout.col_block, 0, 0)
            if layout.kind == "cache4t":
                return (layout.layer, page, 0, 0)
            cb = layout.col_block(*ids) if callable(layout.col_block) else layout.col_block
            return (page, 0, cb)
        shape = {"cache": (1, 1, PAGE, layout.nslot, 2, HEAD_DIM), "cache4t": (1, 1, layout.width, PAGE),
                 "cols": (1, PAGE, layout.width)}[layout.kind]
        return pl.BlockSpec(shape, imap)
    return [spec(k) for k in range(PAGES_PER_STEP)]


def _page_tile(layout, ref, j, width=LANES):
    if layout.kind == "cache":
        slot, g = divmod(layout.base + j, 2)
        return ref[0, 0, :, slot, g, :]
    return ref[0, :, j * width:(j + 1) * width]


def _chunk_rows(layout, ref, j, p):
    return ref[0, pl.ds(p, SUBLANES, stride=CMP_STRIDE), :]


def _cmp1_body(pt_ref, *refs, layout, n_inner):
    pages = refs[:PAGES_PER_STEP]
    w_ref, pe_ref = refs[PAGES_PER_STEP:PAGES_PER_STEP + 2]
    o_ref, s_ref = refs[PAGES_PER_STEP + 2:PAGES_PER_STEP + 4]
    if layout.kind == "cache":
        for slot in range(2):
            acc_a = jnp.zeros((2 * PAGE, HEAD_DIM), F32)
            acc_b = jnp.zeros((2 * PAGE, HEAD_DIM), F32)
            for p in range(CMP_STRIDE):
                xp = jnp.concatenate(
                    [pg[0, 0, pl.ds(p, SUBLANES, stride=CMP_STRIDE), slot, :, :].reshape(2 * SUBLANES, HEAD_DIM)
                     for pg in pages], axis=0)
                xa = (xp + pe_ref[slot, p:p + 1, :]).astype(BF16)
                xb = (xp + pe_ref[slot, CMP_STRIDE + p:CMP_STRIDE + p + 1, :]).astype(BF16)
                acc_a = acc_a + jnp.dot(xa, w_ref[slot, p, :, 0:HEAD_DIM], preferred_element_type=F32)
                acc_b = acc_b + jnp.dot(xb, w_ref[slot, p, :, HEAD_DIM:2 * HEAD_DIM], preferred_element_type=F32)
            for half, acc in enumerate((acc_a, acc_b)):
                s_ref[...] = acc
                for g in range(2):
                    c0 = (slot * 2 + g) * 256 + half * HEAD_DIM
                    o_ref[0, :, c0:c0 + HEAD_DIM] = s_ref[pl.ds(g, PAGE, stride=2), :]
        return
    for j in range(n_inner):
        slot = j // 2
        acc_a = jnp.zeros((PAGE, HEAD_DIM), F32)
        acc_b = jnp.zeros((PAGE, HEAD_DIM), F32)
        for p in range(CMP_STRIDE):
            xp = jnp.concatenate([_chunk_rows(layout, pg, j, p) for pg in pages], axis=0)
            xa = (xp + pe_ref[slot, p:p + 1, :]).astype(BF16)
            xb = (xp + pe_ref[slot, CMP_STRIDE + p:CMP_STRIDE + p + 1, :]).astype(BF16)
            acc_a = acc_a + jnp.dot(xa, w_ref[slot, p, :, 0:HEAD_DIM], preferred_element_type=F32)
            acc_b = acc_b + jnp.dot(xb, w_ref[slot, p, :, HEAD_DIM:2 * HEAD_DIM], preferred_element_type=F32)
        o_ref[0, :, j * 256:j * 256 + HEAD_DIM] = acc_a
        o_ref[0, :, j * 256 + HEAD_DIM:(j + 1) * 256] = acc_b


def _cmp1(pool, layout, pt, wcat, pe):
    nb, npg = pt.shape
    kg = npg // PAGES_PER_STEP
    if layout.kind == "cache":
        n_inner, n_sg, wsel, osel, ow = 4, 1, (lambda sg: 0), (lambda sg: 0), 1024
        wblk = 2
    else:
        cb = layout.col_block
        layout = layout._replace(width=HEAD_DIM, col_block=lambda b, sg, k: cb * 4 + sg)
        n_inner, n_sg, wsel, osel, ow = 1, 4, (lambda sg: sg // 2), (lambda sg: sg), 256
        wblk = 1
    return pl.pallas_call(
        functools.partial(_cmp1_body, layout=layout, n_inner=n_inner),
        grid_spec=pltpu.PrefetchScalarGridSpec(
            num_scalar_prefetch=1, grid=(nb, n_sg, kg),
            in_specs=_page_specs(layout, 3)
            + [pl.BlockSpec((wblk, 16, HEAD_DIM, 256), lambda b, sg, k, pt: (wsel(sg), 0, 0, 0)),
               pl.BlockSpec((wblk, CMP_LEN, HEAD_DIM), lambda b, sg, k, pt: (wsel(sg), 0, 0))],
            out_specs=pl.BlockSpec((1, PAGE, ow), lambda b, sg, k, pt: (b, k, osel(sg))),
            scratch_shapes=[pltpu.VMEM((2 * PAGE, HEAD_DIM), F32)]),
        out_shape=jax.ShapeDtypeStruct((nb, npg * SUBLANES, 1024), F32),
        compiler_params=_cparams("arbitrary", "arbitrary", "arbitrary"),
        name="nsa_compress1",
    )(pt, *([pool] * PAGES_PER_STEP), wcat, pe)


def _gelu_tanh(x):
    return 0.5 * x * (1.0 + jnp.tanh(np.sqrt(2.0 / np.pi).astype(np.float32) * (x + 0.044715 * (x * x * x))))


def _cmp2_body(ab_ref, tail_ref, pe_ref, w1_ref, w2_ref, o_ref, s_ref, *, nc, n_tail):
    row8 = lax.broadcasted_iota(jnp.int32, (SUBLANES, 1), 0)
    row16 = lax.broadcasted_iota(jnp.int32, (CMP_STRIDE, 1), 0)
    for sg in range(4):
        slot = sg // 2
        a = ab_ref[0, :, sg * 256:sg * 256 + HEAD_DIM]
        s_ref[0:nc, :] = ab_ref[0, :, sg * 256 + HEAD_DIM:(sg + 1) * 256]
        tb = jnp.zeros((SUBLANES, HEAD_DIM), F32)
        if n_tail:
            x8 = jnp.where(row8 < n_tail, tail_ref[:, sg * HEAD_DIM:(sg + 1) * HEAD_DIM], 0.0)
            x16 = jnp.concatenate([x8, jnp.zeros((CMP_STRIDE - SUBLANES, HEAD_DIM), F32)], axis=0)
            x16 = x16 + pe_ref[slot, CMP_STRIDE:CMP_LEN, :]
            t16 = jnp.zeros((CMP_STRIDE, HEAD_DIM), F32)
            for p in range(CMP_STRIDE):
                xm = jnp.where(row16 == p, x16, 0.0).astype(BF16)
                t16 = t16 + jnp.dot(xm, w1_ref[slot, CMP_STRIDE + p].astype(BF16), preferred_element_type=F32)
            tb = jnp.sum(t16, axis=0, keepdims=True) * jnp.where(row8 == 0, 1.0, 0.0)
        s_ref[nc:nc + SUBLANES, :] = tb
        pre = a + s_ref[pl.ds(1, nc), :]
        o_ref[0, :, sg * HEAD_DIM:(sg + 1) * HEAD_DIM] = jnp.dot(
            _gelu_tanh(pre).astype(BF16), w2_ref[slot].astype(BF16), preferred_element_type=F32)


def _cmp2(ab, tail, pe, w1r, w2, n_tail, tail_col_block):
    nb, nc, _ = ab.shape
    if tail is None:
        tail = jnp.zeros((nb * SUBLANES, 512), F32)
        tail_col_block = 0
    return pl.pallas_call(
        functools.partial(_cmp2_body, nc=nc, n_tail=n_tail),
        grid=(nb,),
        in_specs=[pl.BlockSpec((1, nc, 1024), lambda b: (b, 0, 0)),
                  pl.BlockSpec((SUBLANES, 512), lambda b: (b, tail_col_block)),
                  pl.BlockSpec((2, CMP_LEN, HEAD_DIM), lambda b: (0, 0, 0)),
                  pl.BlockSpec((2, CMP_LEN, HEAD_DIM, HEAD_DIM), lambda b: (0, 0, 0, 0)),
                  pl.BlockSpec((2, HEAD_DIM, HEAD_DIM), lambda b: (0, 0, 0))],
        out_specs=pl.BlockSpec((1, nc, 512), lambda b: (b, 0, 0)),
        out_shape=jax.ShapeDtypeStruct((nb, nc, 512), F32),
        scratch_shapes=[pltpu.VMEM((nc + SUBLANES, HEAD_DIM), F32)],
        compiler_params=_cparams("arbitrary"),
        name="nsa_compress2",
    )(ab, tail, pe, w1r, w2)


def _nsa1_body(*refs, tq, q0, n_cmp, ncp, n_sel, nselp, lpad, nwb, has_tail, win_k0):
    qraw_ref, qrot_ref, cmp_ref, ov_ref = refs[:4]
    wins = refs[4:4 + nwb]
    pos = 4 + nwb
    tail_ref = None
    if has_tail:
        tail_ref = refs[pos]
        pos += 1
    ocmp_ref, owin_ref, kmask_ref = refs[pos:pos + 3]
    qt = pl.program_id(1)
    r_per_g = NSA_HEADS // NSA_KV
    qpos = q0 + qt * tq + lax.broadcasted_iota(jnp.int32, (tq, 1), 0)

    jj = lax.broadcasted_iota(jnp.int32, (1, ncp), 1)
    cmask = (jj * CMP_STRIDE + (CMP_LEN - 1) <= qpos) & (jj < n_cmp)
    blk = lax.broadcasted_iota(jnp.int32, (1, nselp), 1)
    cur = qpos // SEL_BLOCK
    forced = (blk == 0) | (blk == cur) | (blk == cur - 1)
    e_row = lax.broadcasted_iota(jnp.int32, (KSTEP // SEL_BLOCK, KSTEP), 0)
    e_col = lax.broadcasted_iota(jnp.int32, (KSTEP // SEL_BLOCK, KSTEP), 1)
    expand = jnp.where(e_col // SEL_BLOCK == e_row, 1.0, 0.0).astype(BF16)
    lane128 = lax.broadcasted_iota(jnp.int32, (1, LANES), 1)
    for g in range(NSA_KV):
        kc = cmp_ref[0, :, g * HEAD_DIM:(g + 1) * HEAD_DIM].astype(BF16)
        vc = cmp_ref[0, :, (2 + g) * HEAD_DIM:(3 + g) * HEAD_DIM].astype(BF16)
        q = _stack_heads(qraw_ref, g, r_per_g)
        s = _dot_nt(q, kc) * ATT_SCALE + jnp.concatenate([jnp.where(cmask, 0.0, NEG_INF)] * r_per_g, axis=0)
        p, den = _softmax_parts(s)
        p = p / den
        o = jnp.dot(p.astype(BF16), vc, preferred_element_type=F32)
        psum = jnp.zeros((tq, ncp), F32)
        for r in range(r_per_g):
            h = g * r_per_g + r
            psum = psum + p[r * tq:(r + 1) * tq]
            ocmp_ref[:, h * HEAD_DIM:(h + 1) * HEAD_DIM] = o[r * tq:(r + 1) * tq]
        imp = jnp.dot(psum.astype(BF16), ov_ref[...].astype(BF16), preferred_element_type=F32)
        imp = jnp.where(forced, imp + FORCE_SCORE, imp)
        imp = jnp.where((blk <= cur) & (blk < n_sel), imp, NEG_INF)
        rank = jnp.zeros((tq, nselp), F32)
        for i in range(n_sel):
            vi = imp[:, i:i + 1]
            rank = rank + jnp.where(vi > imp, 1.0, jnp.where(vi == imp, jnp.where(blk > i, 1.0, 0.0), 0.0))
        sel = jnp.where((rank < min(SEL_TOP, n_sel)) & (blk < n_sel), 1.0, 0.0).astype(BF16)
        per = KSTEP // SEL_BLOCK
        for c in range(lpad // KSTEP):
            km = jnp.dot(sel[:, c * per:(c + 1) * per], expand, preferred_element_type=F32)
            kmask_ref[0, g, :, c * KSTEP:(c + 1) * KSTEP] = km.astype(BF16)
        if lpad % KSTEP:
            b0 = (lpad // KSTEP) * per
            km = jnp.where(lane128 < SEL_BLOCK, sel[:, b0:b0 + 1].astype(F32), 0.0)
            kmask_ref[0, g, :, (lpad // KSTEP) * KSTEP:lpad] = jnp.broadcast_to(km, (tq, LANES)).astype(BF16)

    nk = nwb * PAGE + (LANES if has_tail else 0)
    kk = lax.broadcasted_iota(jnp.int32, (1, nk), 1)
    if has_tail:
        kpos = jnp.where(kk < nwb * PAGE, win_k0 + kk, q0 + kk - nwb * PAGE)
    else:
        kpos = (qt - (nwb - 1)) * PAGE + kk
    rel = qpos - kpos
    wmask = (rel >= 0) & (rel < WINDOW) & (kpos >= 0)
    for g in range(NSA_KV):
        if has_tail:
            kparts = [w[0, 0, :, 0, g, :] for w in wins]
            vparts = [w[0, 0, :, 1, g, :] for w in wins]
        else:
            kparts = [w[:, g * HEAD_DIM:(g + 1) * HEAD_DIM] for w in wins]
            vparts = [w[:, (2 + g) * HEAD_DIM:(3 + g) * HEAD_DIM] for w in wins]
        if has_tail:
            zpad = jnp.zeros((LANES - SUBLANES, HEAD_DIM), F32)
            kparts += [tail_ref[:, g * HEAD_DIM:(g + 1) * HEAD_DIM], zpad]
            vparts += [tail_ref[:, (2 + g) * HEAD_DIM:(3 + g) * HEAD_DIM], zpad]
        kw = jnp.concatenate(kparts, axis=0).astype(BF16)
        vw = jnp.concatenate(vparts, axis=0).astype(BF16)
        q = _stack_heads(qrot_ref, g, r_per_g)
        s = _dot_nt(q, kw) * ATT_SCALE + jnp.concatenate([jnp.where(wmask, 0.0, NEG_INF)] * r_per_g, axis=0)
        p, den = _softmax_parts(s)
        o = jnp.dot((p / den).astype(BF16), vw, preferred_element_type=F32)
        for r in range(r_per_g):
            h = g * r_per_g + r
            owin_ref[:, h * HEAD_DIM:(h + 1) * HEAD_DIM] = o[r * tq:(r + 1) * tq]


def _nsa1(hp, cmp, ov, win_src, tail_src, *, nb, tq, qt_n, q0, n_cmp, n_sel, lpad, nwb, win_k0, win_layer):
    n = hp.shape[0]
    ncp = cmp.shape[1]
    nselp = ov.shape[1]
    has_tail = tail_src is not None
    rows = lambda b, t: b * qt_n + t
    in_specs = [pl.BlockSpec((tq, 1024), lambda b, t: (rows(b, t), C_QRAW // 1024)),
                pl.BlockSpec((tq, 1024), lambda b, t: (rows(b, t), C_QROT // 1024)),
                pl.BlockSpec((1, ncp, 512), lambda b, t: (b, 0, 0)),
                pl.BlockSpec((ncp, nselp), lambda b, t: (0, 0))]
    if has_tail:
        in_specs += [pl.BlockSpec((1, 1, PAGE, 2, NSA_KV, HEAD_DIM), lambda b, t, k=k: (win_layer, b, k, 0, 0, 0))
                     for k in range(nwb)]
        in_specs += [pl.BlockSpec((SUBLANES, 512), lambda b, t: (b, C_WIN // 512))]
        args = [win_src] * nwb + [tail_src]
    else:
        in_specs += [pl.BlockSpec((PAGE, 512),
                                  lambda b, t, k=k: (b * qt_n + jnp.maximum(t - (nwb - 1) + k, 0), C_WIN // 512))
                     for k in range(nwb)]
        args = [win_src] * nwb
    return pl.pallas_call(
        functools.partial(_nsa1_body, tq=tq, q0=q0, n_cmp=n_cmp, ncp=ncp, n_sel=n_sel, nselp=nselp, lpad=lpad,
                          nwb=nwb, has_tail=has_tail, win_k0=win_k0),
        grid=(nb, qt_n),
        in_specs=in_specs,
        out_specs=[pl.BlockSpec((tq, 1024), lambda b, t: (rows(b, t), 0)),
                   pl.BlockSpec((tq, 1024), lambda b, t: (rows(b, t), 0)),
                   pl.BlockSpec((1, NSA_KV, tq, lpad), lambda b, t: (b, 0, t, 0))],
        out_shape=[jax.ShapeDtypeStruct((n, 1024), F32), jax.ShapeDtypeStruct((n, 1024), F32),
                   jax.ShapeDtypeStruct((nb, NSA_KV, qt_n * tq, lpad), BF16)],
        compiler_params=_cparams("arbitrary", "arbitrary"),
        name="nsa_cmp_select_window",
    )(hp, hp, cmp, ov, *args)


def _mattn_body(pt_ref, *refs, layout, n_g, r_per_g, gm, tq, q0, kg_n, has_tail):
    q_ref, mask_ref = refs[:2]
    pages = refs[2:2 + PAGES_PER_STEP]
    pos = 2 + PAGES_PER_STEP
    if has_tail:
        tail_ref, tmask_ref = refs[pos:pos + 2]
        pos += 2
    o_ref, s_ref, v_ref, x_ref = refs[pos:pos + 4]
    qt = pl.program_id(1)
    step = pl.program_id(2)
    two_pass = kg_n > 1 or has_tail
    kg = step
    qpos = q0 + qt * tq + lax.broadcasted_iota(jnp.int32, (tq, 1), 0)

    def scores(g, kt, kpos, mref):
        picked = jnp.where(mref[0, g if gm > 1 else 0].astype(F32) > 0.5, 0.0, NEG_INF)
        bias = jnp.concatenate([picked + jnp.where(kpos <= qpos, 0.0, NEG_INF)] * r_per_g, axis=0)
        return _dot_nt(_stack_heads(q_ref, g, r_per_g), kt) * ATT_SCALE + bias

    def emit(g, o):
        for r in range(r_per_g):
            h = g * r_per_g + r
            o_ref[:, h * HEAD_DIM:(h + 1) * HEAD_DIM] = o[r * tq:(r + 1) * tq]

    if layout.kind == "cache":
        for slot in range(2):
            for k, pg in enumerate(pages):
                x_ref[slot, k * 2 * PAGE:(k + 1) * 2 * PAGE, :] = pg[0, 0, :, slot, :, :].reshape(2 * PAGE, HEAD_DIM)

    def main_keys(g):
        if layout.kind == "cache":
            return x_ref[0, pl.ds(g, KSTEP, stride=2), :].astype(BF16)
        return jnp.concatenate([_page_tile(layout, pg, g) for pg in pages], axis=0).astype(BF16)

    def main_vals(g):
        if layout.kind == "cache":
            return x_ref[1, pl.ds(g, KSTEP, stride=2), :].astype(BF16)
        return jnp.concatenate([_page_tile(layout, pg, n_g + g) for pg in pages], axis=0).astype(BF16)

    def tail_part(g, off):
        zpad = jnp.zeros((LANES - SUBLANES, HEAD_DIM), F32)
        return jnp.concatenate([tail_ref[:, (off + g) * HEAD_DIM:(off + g + 1) * HEAD_DIM], zpad], axis=0).astype(BF16)

    kpos = kg * KSTEP + lax.broadcasted_iota(jnp.int32, (1, KSTEP), 1)
    tpos = q0 + lax.broadcasted_iota(jnp.int32, (1, LANES), 1)

    if not two_pass:
        for g in range(n_g):
            p, den = _softmax_parts(scores(g, main_keys(g), kpos, mask_ref))
            emit(g, jnp.dot((p / den).astype(BF16), main_vals(g), preferred_element_type=F32))
        return

    for g in range(n_g):
        s_ref[g, kg] = scores(g, main_keys(g), kpos, mask_ref)
        v_ref[g, pl.ds(pl.multiple_of(kg * KSTEP, KSTEP), KSTEP), :] = main_vals(g)

    @pl.when(step == kg_n - 1)
    def _():
        for g in range(n_g):
            chunks = [s_ref[g, c] for c in range(kg_n)]
            vals = [v_ref[g, c * KSTEP:(c + 1) * KSTEP, :] for c in range(kg_n)]
            if has_tail:
                chunks.append(scores(g, tail_part(g, 0), tpos, tmask_ref))
                vals.append(tail_part(g, n_g))
            m = chunks[0].max(axis=-1, keepdims=True)
            for c in chunks[1:]:
                m = jnp.maximum(m, c.max(axis=-1, keepdims=True))
            m = jnp.where(m == NEG_INF, 0.0, m)
            exps = [jnp.exp(c - m) for c in chunks]
            den = exps[0].sum(axis=-1, keepdims=True)
            for e in exps[1:]:
                den = den + e.sum(axis=-1, keepdims=True)
            den = jnp.maximum(den, 1e-30)
            o = jnp.zeros((r_per_g * tq, HEAD_DIM), F32)
            for e, v in zip(exps, vals):
                o = o + jnp.dot((e / den).astype(BF16), v, preferred_element_type=F32)
            emit(g, o)


def _mattn(hp, q_col, n_heads, mask, pool, layout, pt, tail_src, tail_col_block, *, nb, tq, qt_n, q0, name):
    n = hp.shape[0]
    n_g = 2
    r_per_g = n_heads // n_g
    qw = n_heads * HEAD_DIM
    gm = mask.shape[1]
    kg_n = pt.shape[1] // PAGES_PER_STEP
    has_tail = tail_src is not None
    buffered = kg_n > 1 or has_tail
    in_specs = [pl.BlockSpec((tq, qw), lambda b, t, k, pt: (b * qt_n + t, q_col // qw)),
                pl.BlockSpec((1, gm, tq, KSTEP), lambda b, t, k, pt: (b, 0, t, k))]
    in_specs += _page_specs(layout, 3)
    args = [hp, mask] + [pool] * PAGES_PER_STEP
    if has_tail:
        in_specs += [pl.BlockSpec((SUBLANES, 512), lambda b, t, k, pt: (b, tail_col_block)),
                     pl.BlockSpec((1, gm, tq, LANES), lambda b, t, k, pt: (b, 0, t, kg_n * KSTEP // LANES))]
        args += [tail_src, mask]
    return pl.pallas_call(
        functools.partial(_mattn_body, layout=layout, n_g=n_g, r_per_g=r_per_g, gm=gm, tq=tq, q0=q0, kg_n=kg_n, has_tail=has_tail),
        grid_spec=pltpu.PrefetchScalarGridSpec(
            num_scalar_prefetch=1, grid=(nb, qt_n, kg_n),
            in_specs=in_specs,
            out_specs=pl.BlockSpec((tq, qw), lambda b, t, k, pt: (b * qt_n + t, 0)),
            scratch_shapes=[pltpu.VMEM((n_g, kg_n, r_per_g * tq, KSTEP) if buffered else (1, 1, SUBLANES, LANES), F32),
                            pltpu.VMEM((n_g, kg_n * KSTEP, HEAD_DIM) if buffered else (1, 2 * SUBLANES, LANES), BF16),
                            pltpu.VMEM((2, 2 * KSTEP, HEAD_DIM) if layout.kind == "cache" else (1, SUBLANES, LANES),
                                       F32)]),
        out_shape=jax.ShapeDtypeStruct((n, qw), F32),
        compiler_params=_cparams("arbitrary", "arbitrary", "arbitrary"),
        name=name,
    )(pt, *args)


PACKED_ROWS = 2 * SUBLANES


def _topk_mask_packed_t(sc_t, n_keep):
    n_keys, n_q = sc_t.shape
    i16 = jnp.int16
    bits = pltpu.bitcast(sc_t + 0.0, jnp.int32)
    key = jnp.where(bits < 0, bits ^ jnp.int32(0x7FFFFFFF), bits)
    hi = jnp.right_shift(key, 16).astype(i16)
    lo = ((key & jnp.int32(0xFFFF)) - 32768).astype(i16)
    one_b, zero_b = jnp.asarray(1, BF16), jnp.asarray(0, BF16)
    i16_min, i16_max = jnp.asarray(-32768, i16), jnp.asarray(32767, i16)
    assert n_keys % PACKED_ROWS == 0 and n_keys // PACKED_ROWS <= 256

    def colsum(x01):
        parts = [x01[i * PACKED_ROWS:(i + 1) * PACKED_ROWS, :] for i in range(n_keys // PACKED_ROWS)]
        while len(parts) > 1:
            parts = [parts[i] + parts[i + 1] for i in range(0, len(parts), 2)]
        return jnp.sum(parts[0].astype(F32), axis=0, keepdims=True)

    def search(vals, need, nbits, start):
        def body(i, thr):
            cand = thr + jnp.left_shift(jnp.int32(1), nbits - 1 - i)
            cnt = colsum(jnp.where(vals >= cand.astype(i16), one_b, zero_b))
            return jnp.where(cnt >= need, cand, thr)
        return lax.fori_loop(0, nbits, body, jnp.full((1, n_q), start, jnp.int32))

    t_hi = search(hi, float(n_keep), 16, -32768).astype(i16)
    eq = hi == t_hi
    hi_gt = jnp.where(hi > t_hi, one_b, zero_b)
    lo_m = jnp.where(eq, lo, i16_min)
    t_lo = search(lo_m, n_keep - colsum(hi_gt), 16, -32768).astype(i16)
    gt = hi_gt + jnp.where(lo_m > t_lo, one_b, zero_b)
    need = n_keep - colsum(gt)
    idx = lax.broadcasted_iota(jnp.int32, (n_keys, 1), 0).astype(i16)
    tie_idx = jnp.where(eq, jnp.where(lo == t_lo, idx, i16_max), i16_max)
    nbits = int(np.ceil(np.log2(n_keys)))

    def ibody(i, c):
        cand = c + jnp.left_shift(jnp.int32(1), nbits - 1 - i)
        cnt = colsum(jnp.where(tie_idx < cand.astype(i16), one_b, zero_b))
        return jnp.where(cnt < need, cand, c)

    cut = lax.fori_loop(0, nbits, ibody, jnp.zeros((1, n_q), jnp.int32)).astype(i16)
    return jnp.maximum(gt, jnp.where(tie_idx <= cut, one_b, zero_b))


def _dsa1_body(pt_ref, *refs, layout, tq, q0, kg_n, lpad, has_tail, n_keep):
    iq_ref, misc_ref = refs[:2]
    pages = refs[2:2 + PAGES_PER_STEP]
    pos = 2 + PAGES_PER_STEP
    if has_tail:
        tail_ref = refs[pos]
        pos += 1
    mask_ref, sc_ref = refs[pos:pos + 2]
    qt = pl.program_id(1)
    kg = pl.program_id(2)
    qpos = q0 + qt * tq + lax.broadcasted_iota(jnp.int32, (tq, 1), 0)
    lane128 = lax.broadcasted_iota(jnp.int32, (1, LANES), 1)

    if kg_n == 1 and not has_tail and tq == LANES:
        kt = jnp.concatenate([_page_tile(layout, pg, 0, IDX_DIM) for pg in pages], axis=0).astype(BF16)
        misc_t = misc_ref[...].T
        kpos_c = lax.broadcasted_iota(jnp.int32, (KSTEP, 1), 0)
        qpos_r = q0 + qt * tq + lax.broadcasted_iota(jnp.int32, (1, tq), 1)
        acc = jnp.zeros((KSTEP, tq), F32)
        for h in range(IDX_HEADS):
            qi = iq_ref[:, h * IDX_DIM:(h + 1) * IDX_DIM].astype(BF16)
            acc = acc + misc_t[MISC_IW + h:MISC_IW + h + 1, :] * jnp.maximum(_dot_nt(kt, qi), 0.0)
        sel_t = _topk_mask_packed_t(jnp.where(kpos_c <= qpos_r, acc, NEG_INF), n_keep)
        for c in range(KSTEP // LANES):
            blk = sel_t[c * LANES:(c + 1) * LANES, :].astype(F32).T
            mask_ref[0, 0, :, c * LANES:(c + 1) * LANES] = blk.astype(BF16)
        return

    def scores(kt, kpos, keys_on_lanes=False):
        acc = jnp.zeros((tq, kpos.shape[1]), F32)
        for h in range(IDX_HEADS):
            qi = iq_ref[:, h * IDX_DIM:(h + 1) * IDX_DIM].astype(BF16)
            w = jnp.sum(jnp.where(lane128 == MISC_IW + h, misc_ref[...], 0.0), axis=-1, keepdims=True)
            qk = jnp.dot(qi, kt, preferred_element_type=F32) if keys_on_lanes else _dot_nt(qi, kt)
            acc = acc + w * jnp.maximum(qk, 0.0)
        return jnp.where(kpos <= qpos, acc, NEG_INF)

    kpos = kg * KSTEP + lax.broadcasted_iota(jnp.int32, (1, KSTEP), 1)
    if layout.kind == "cache4t":
        kt = jnp.concatenate([pg[0, 0] for pg in pages], axis=1).astype(BF16)
        sc_ref[kg] = scores(kt, kpos, keys_on_lanes=True)
    else:
        kt = jnp.concatenate([_page_tile(layout, pg, 0, IDX_DIM) for pg in pages], axis=0).astype(BF16)
        sc_ref[kg] = scores(kt, kpos)

    @pl.when(kg == kg_n - 1)
    def _():
        nch = sc_ref.shape[0]
        if has_tail:
            zpad = jnp.zeros((KSTEP - SUBLANES, IDX_DIM), F32)
            tk = jnp.concatenate([tail_ref[:, 0:IDX_DIM], zpad], axis=0).astype(BF16)
            tl = lax.broadcasted_iota(jnp.int32, (1, KSTEP), 1)
            sc_ref[kg_n] = jnp.where(tl < LANES, scores(tk, q0 + tl), NEG_INF)
        sc = sc_ref[...] + 0.0
        bits = pltpu.bitcast(sc, jnp.int32)
        key = jnp.where(bits < 0, bits ^ jnp.int32(0x7FFFFFFF), bits)
        int_min = jnp.int32(-2 ** 31)

        def count(pred):
            return jnp.sum(jnp.sum(jnp.where(pred, 1.0, 0.0), axis=-1, keepdims=True), axis=0, keepdims=True)

        def vbit(i, thr):
            cand = thr + jnp.left_shift(jnp.int32(1), 31 - i)
            return jnp.where(count(key >= cand) >= n_keep, cand, thr)

        thr = lax.fori_loop(0, 32, vbit, jnp.full((1, tq, 1), int_min, jnp.int32))
        gt = key > thr
        tie = key == thr
        need = n_keep - count(gt)
        idx = (lax.broadcasted_iota(jnp.int32, (nch, 1, KSTEP), 0) * KSTEP
               + lax.broadcasted_iota(jnp.int32, (nch, 1, KSTEP), 2))
        nbits = int(np.ceil(np.log2(nch * KSTEP)))

        def ibit(i, c):
            cand = c + jnp.left_shift(jnp.int32(1), nbits - 1 - i)
            return jnp.where(count(tie & (idx < cand)) < need, cand, c)

        cut = lax.fori_loop(0, nbits, ibit, jnp.zeros((1, tq, 1), jnp.int32))
        sel = jnp.where(gt | (tie & (idx <= cut)), 1.0, 0.0).astype(BF16)
        for c in range(kg_n):
            mask_ref[0, 0, :, c * KSTEP:(c + 1) * KSTEP] = sel[c]
        if has_tail:
            mask_ref[0, 0, :, kg_n * KSTEP:lpad] = sel[kg_n][:, 0:lpad - kg_n * KSTEP]


def _dsa1(hp, pool, layout, pt, tail_src, *, nb, tq, qt_n, q0, lpad, n_keep):
    kg_n = pt.shape[1] // PAGES_PER_STEP
    has_tail = tail_src is not None
    in_specs = [pl.BlockSpec((tq, 256), lambda b, t, k, pt: (b * qt_n + t, C_IQ // 256)),
                pl.BlockSpec((tq, LANES), lambda b, t, k, pt: (b * qt_n + t, C_MISC // LANES))]
    in_specs += _page_specs(layout, 3)
    args = [hp, hp] + [pool] * PAGES_PER_STEP
    if has_tail:
        in_specs += [pl.BlockSpec((SUBLANES, LANES), lambda b, t, k, pt: (b, C_MISC // LANES))]
        args += [tail_src]
    return pl.pallas_call(
        functools.partial(_dsa1_body, layout=layout, tq=tq, q0=q0, kg_n=kg_n, lpad=lpad, has_tail=has_tail,
                          n_keep=n_keep),
        grid_spec=pltpu.PrefetchScalarGridSpec(
            num_scalar_prefetch=1, grid=(nb, qt_n, kg_n),
            in_specs=in_specs,
            out_specs=pl.BlockSpec((1, 1, tq, lpad), lambda b, t, k, pt: (b, 0, t, 0)),
            scratch_shapes=[pltpu.VMEM((kg_n + (1 if has_tail else 0), tq, KSTEP), F32)]),
        out_shape=jax.ShapeDtypeStruct((nb, 1, qt_n * tq, lpad), BF16),
        compiler_params=_cparams("arbitrary", "arbitrary", "arbitrary"),
        name="dsa_indexer_topk",
    )(pt, *args)


def _mix_body(ocmp_ref, osel_ref, owin_ref, odsa_ref, conv_ref, misc_ref, mg0_ref, mg1_ref, mg2_ref,
              wa_ref, wb_ref, wc_ref, z_ref):
    lane128 = lax.broadcasted_iota(jnp.int32, (1, LANES), 1)
    misc = misc_ref[...]

    def gate(kind, h):
        return jnp.sum(jnp.where(lane128 == MISC_NG + kind * NSA_HEADS + h, misc, 0.0), axis=-1, keepdims=True)

    parts = []
    for h in range(NSA_HEADS):
        sl = slice(h * HEAD_DIM, (h + 1) * HEAD_DIM)
        parts.append((gate(0, h) * ocmp_ref[:, sl] + gate(1, h) * osel_ref[:, sl]
                      + gate(2, h) * owin_ref[:, sl]).astype(BF16))
    o_nsa = jnp.concatenate(parts, axis=1)
    p_a = jnp.dot(o_nsa, wa_ref[...], preferred_element_type=F32)
    p_b = jnp.dot(odsa_ref[...].astype(BF16), wb_ref[...], preferred_element_type=F32)
    p_c = jnp.dot(conv_ref[...].astype(BF16), wc_ref[...], preferred_element_type=F32)
    z_ref[...] = (mg0_ref[...] * p_a + mg1_ref[...] * p_b + mg2_ref[...] * p_c).astype(BF16)


def _mix(hp, o_cmp, o_sel, o_win, o_dsa, conv_out, wa, wb, wc, tm):
    n = hp.shape[0]
    row = lambda w, cb=0: pl.BlockSpec((tm, w), lambda i: (i, cb))
    full = lambda a: pl.BlockSpec(a.shape, lambda i: (0, 0))
    return pl.pallas_call(
        _mix_body,
        grid=(n // tm,),
        in_specs=[row(1024), row(1024), row(1024), row(512), row(512), row(LANES, C_MISC // LANES),
                  row(D_MODEL, C_MG // D_MODEL), row(D_MODEL, C_MG // D_MODEL + 1), row(D_MODEL, C_MG // D_MODEL + 2),
                  full(wa), full(wb), full(wc)],
        out_specs=row(D_MODEL),
        out_shape=jax.ShapeDtypeStruct((n, D_MODEL), BF16),
        compiler_params=_cparams("arbitrary"),
        name="branch_merge",
    )(o_cmp, o_sel, o_win, o_dsa, conv_out, hp, hp, hp, hp, wa, wb, wc)


def _outln_body(z_ref, wo_ref, x_ref, g_ref, b_ref, wr_ref, rb_ref, x1_ref, gate_ref, *, tm):
    y = jnp.dot(z_ref[...], wo_ref[...], preferred_element_type=F32)
    x1 = _layernorm(ALPHA * x_ref[...] + y, g_ref[...], b_ref[...])
    x1_ref[...] = x1
    x_hi = x1.astype(BF16)
    x_lo = (x1 - x_hi.astype(F32)).astype(BF16)
    w_hi = wr_ref[...].astype(BF16)
    w_lo = (wr_ref[...] - w_hi.astype(F32)).astype(BF16)
    aff = _sigmoid(_dot_nt(w_hi, x_hi) + (_dot_nt(w_hi, x_lo) + _dot_nt(w_lo, x_hi)))
    biased = aff + rb_ref[:, 0:1]
    rows = [biased[e:e + 1, :] for e in range(N_EXPERTS)]
    best = None
    g_best = jnp.zeros((1, tm), jnp.int32)
    for g in range(N_GROUPS):
        v = rows[g * EXPERTS_PER_GROUP:(g + 1) * EXPERTS_PER_GROUP]
        score = None
        for a in range(EXPERTS_PER_GROUP):
            for c in range(a + 1, EXPERTS_PER_GROUP):
                pair = v[a] + v[c]
                score = pair if score is None else jnp.maximum(score, pair)
        if best is None:
            best = score
        else:
            better = score > best
            best = jnp.where(better, score, best)
            g_best = jnp.where(better, g, g_best)
    sel_rows = []
    for e in range(N_EXPERTS):
        g = e // EXPERTS_PER_GROUP
        rank = jnp.zeros((1, tm), F32)
        for o in range(g * EXPERTS_PER_GROUP, (g + 1) * EXPERTS_PER_GROUP):
            if o == e:
                continue
            beats = (rows[o] > rows[e]) | ((rows[o] == rows[e]) & (o < e))
            rank = rank + jnp.where(beats, 1.0, 0.0)
        sel_rows.append(jnp.where((g_best == g) & (rank < 2), aff[e:e + 1, :], 0.0))
    tot = sel_rows[0]
    for e in range(1, N_EXPERTS):
        tot = tot + sel_rows[e]
    gate_t = jnp.concatenate(sel_rows + [jnp.zeros((LANES - N_EXPERTS, tm), F32)], axis=0) / tot
    gate_ref[...] = gate_t.T


def _outln(z, wo, x, g, b, wr_t, rb, tm):
    n = z.shape[0]
    row = lambda w: pl.BlockSpec((tm, w), lambda i: (i, 0))
    full = lambda a: pl.BlockSpec(a.shape, lambda i: (0, 0))
    return pl.pallas_call(
        functools.partial(_outln_body, tm=tm),
        grid=(n // tm,),
        in_specs=[row(D_MODEL), pl.BlockSpec(wo.shape, lambda i: (0, 0), pipeline_mode=pl.Buffered(1)), row(D_MODEL),
                  full(g), full(b), full(wr_t), full(rb)],
        out_specs=[row(D_MODEL), row(LANES)],
        out_shape=[jax.ShapeDtypeStruct((n, D_MODEL), F32), jax.ShapeDtypeStruct((n, LANES), F32)],
        compiler_params=_cparams("arbitrary"),
        name="out_proj_ln_router",
    )(z, wo, x, g, b, wr_t, rb)


def _moe_body(x_ref, gate_ref, wg_ref, wu_ref, wd_ref, g_ref, b_ref, o_ref, xb_ref, acc_ref):
    e = pl.program_id(1)

    @pl.when(e == 0)
    def _():
        xb_ref[...] = x_ref[...].astype(BF16)
        acc_ref[...] = jnp.zeros(acc_ref.shape, F32)

    lane128 = lax.broadcasted_iota(jnp.int32, (1, LANES), 1)
    gcol = jnp.sum(jnp.where(lane128 == e, gate_ref[...], 0.0), axis=-1, keepdims=True)
    xb = xb_ref[...]
    hg = jnp.dot(xb, wg_ref[0], preferred_element_type=F32)
    hu = jnp.dot(xb, wu_ref[0], preferred_element_type=F32)
    h = (hg * _sigmoid(hg)) * hu * gcol
    acc_ref[...] += jnp.dot(h.astype(BF16), wd_ref[0], preferred_element_type=F32)

    @pl.when(e == N_EXPERTS - 1)
    def _():
        o_ref[...] = _layernorm(ALPHA * x_ref[...] + acc_ref[...], g_ref[...], b_ref[...])


def _moe(x1, gate, wg, wu, wd, g, b, tm):
    n = x1.shape[0]
    return pl.pallas_call(
        _moe_body,
        grid=(n // tm, N_EXPERTS),
        in_specs=[pl.BlockSpec((tm, D_MODEL), lambda i, e: (i, 0)),
                  pl.BlockSpec((tm, LANES), lambda i, e: (i, 0)),
                  pl.BlockSpec((1, D_MODEL, D_FF), lambda i, e: (e, 0, 0)),
                  pl.BlockSpec((1, D_MODEL, D_FF), lambda i, e: (e, 0, 0)),
                  pl.BlockSpec((1, D_FF, D_MODEL), lambda i, e: (e, 0, 0)),
                  pl.BlockSpec((1, D_MODEL), lambda i, e: (0, 0)),
                  pl.BlockSpec((1, D_MODEL), lambda i, e: (0, 0))],
        out_specs=pl.BlockSpec((tm, D_MODEL), lambda i, e: (i, 0)),
        out_shape=jax.ShapeDtypeStruct((n, D_MODEL), F32),
        scratch_shapes=[pltpu.VMEM((tm, D_MODEL), BF16), pltpu.VMEM((tm, D_MODEL), F32)],
        compiler_params=_cparams("arbitrary", "arbitrary"),
        name="moe_ln",
    )(x1, gate, wg, wu, wd, g, b)


def _overlap_matrix(ncp, nselp):
    cs = np.arange(ncp)[:, None] * CMP_STRIDE
    ss = np.arange(nselp)[None, :] * SEL_BLOCK
    return jnp.asarray(((cs < ss + SEL_BLOCK) & (cs + CMP_LEN > ss)).astype(np.float32))


class _Group:
    def __init__(self, nb, t_real, t_pad, q0, past_len, tm, tq):
        self.nb, self.t_real, self.t_pad, self.q0, self.past_len, self.tm, self.tq = nb, t_real, t_pad, q0, past_len, tm, tq
        self.paged = past_len > 0
        self.lp = past_len if self.paged else t_pad
        self.ltot = self.lp + (t_real if self.paged else 0)
        self.lpad = self.lp + (LANES if self.paged else 0)
        self.qt_n = t_pad // tq
        n_chunks = -(-self.ltot // CMP_STRIDE)
        self.n_cmp = n_chunks - CMP_LEN // CMP_STRIDE + 1
        self.n_sel = -(-self.ltot // SEL_BLOCK)
        self.nselp = -(-self.n_sel // LANES) * LANES
        self.n_keep = min(DSA_TOPK, self.ltot // 4)


def _mixer(gp, x, lw, caches):
    hp = _proj(x.astype(BF16), lw["w_in"], gp.tabs, lw["kinds"], min(2 * gp.tm, gp.nb * gp.t_pad, gp.tabs.shape[1]))
    nb, tq, qt_n, q0 = gp.nb, gp.tq, gp.qt_n, gp.q0
    if gp.paged:
        l, nsa_pool, dsa_pool, kidx_pool, win_src, s_conv, pt = caches
        cmp_lay = _PageLayout("cache", 0, 0, base=0, layer=l, nslot=2)
        slc_lay = _PageLayout("cache", 0, 1, base=0, layer=l, nslot=2)
        dsa_lay = _PageLayout("cache", 0, 0, base=0, layer=l, nslot=2)
        kidx_pool = jnp.transpose(kidx_pool, (0, 1, 3, 2))
        kidx_lay = _PageLayout("cache4t", IDX_DIM, 0, base=0, layer=l, nslot=0)
        tail = hp
        nwb, win_k0, win_layer = WINDOW // PAGE, PAST_LEN - WINDOW, l
        past8 = jnp.concatenate([jnp.zeros((nb, SUBLANES - 2, CONV_DIM), F32), s_conv[l]], axis=1)
    else:
        pt = gp.pt
        nsa_pool = dsa_pool = kidx_pool = hp.reshape(nb * gp.t_pad // PAGE, PAGE, NCOL)
        cmp_lay = _PageLayout("cols", 512, C_CMP // 512, base=0, layer=0, nslot=0)
        slc_lay = _PageLayout("cols", 512, C_SLC // 512, base=0, layer=0, nslot=0)
        dsa_lay = _PageLayout("cols", 512, C_DSA // 512, base=0, layer=0, nslot=0)
        kidx_lay = _PageLayout("cols", LANES, C_MISC // LANES, base=0, layer=0, nslot=0)
        tail = None
        win_src = hp
        nwb, win_k0, win_layer = WINDOW // PAGE + 1, 0, 0
        past8 = jnp.zeros((nb, SUBLANES, CONV_DIM), F32)

    conv_out, cu = _conv(hp, past8, lw["conv_w8"], min(gp.tm, gp.t_pad), gp.t_pad // min(gp.tm, gp.t_pad))

    ab = _cmp1(nsa_pool, cmp_lay, pt, lw["wcat"], lw["pe"])
    cmp = _cmp2(ab, tail, lw["pe"], lw["w1r"], lw["phi_w2"], gp.t_real if gp.paged else 0, C_CMP // 512)
    o_cmp, o_win, kmask = _nsa1(hp, cmp, gp.ov, win_src, tail, nb=nb, tq=tq, qt_n=qt_n, q0=q0, n_cmp=gp.n_cmp,
                                n_sel=gp.n_sel, lpad=gp.lpad, nwb=nwb, win_k0=win_k0, win_layer=win_layer)
    o_sel = _mattn(hp, C_QROT, NSA_HEADS, kmask, nsa_pool, slc_lay, pt, tail, C_SLC // 512,
                   nb=nb, tq=tq, qt_n=qt_n, q0=q0, name="nsa_selected_attn")
    dmask = _dsa1(hp, kidx_pool, kidx_lay, pt, tail, nb=nb, tq=tq, qt_n=qt_n, q0=q0, lpad=gp.lpad,
                  n_keep=gp.n_keep)
    o_dsa = _mattn(hp, C_DQ, DSA_HEADS, dmask, dsa_pool, dsa_lay, pt, tail, C_DSA // 512,
                   nb=nb, tq=tq, qt_n=qt_n, q0=q0, name="dsa_topk_attn")
    z = _mix(hp, o_cmp, o_sel, o_win, o_dsa, conv_out, lw["w_a"], lw["w_b"], lw["w_c"], min(gp.tm, 256))
    return z, hp, cu


def _layer(gp, x, lw, caches):
    z, hp, cu = _mixer(gp, x, lw, caches)
    tm2 = min(gp.tm, 512)
    x1, gate = _outln(z, lw["w_o"], x, lw["ln_mix_g"], lw["ln_mix_b"], lw["wr_t"], lw["rb"], tm2)
    x2 = _moe(x1, gate, lw["w_eg"], lw["w_eu"], lw["w_ed"], lw["ln_ffn_g"], lw["ln_ffn_b"], min(gp.tm, 512))
    return x2, hp, cu


def _layer_weights(l, w_in, nsa_phi_pos, nsa_phi_w1, nsa_phi_w2, conv_w, w_br_a, w_br_b, w_br_c, w_out,
                   ln_mix_g, ln_mix_b, ln_ffn_g, ln_ffn_b, w_router, router_bias, w_e_gate, w_e_up, w_e_down):
    w1r = nsa_phi_w1[l].reshape(2, CMP_LEN, HEAD_DIM, HEAD_DIM)
    wcat = jnp.concatenate([w1r[:, :CMP_STRIDE], w1r[:, CMP_STRIDE:]], axis=-1).astype(BF16)
    return dict(
        w_in=_permute_w_in(w_in, l), kinds=jnp.asarray(_col_kinds()),
        w1r=w1r, wcat=wcat, pe=nsa_phi_pos[l], phi_w2=nsa_phi_w2[l],
        conv_w8=jnp.concatenate([conv_w[l], jnp.zeros((SUBLANES - 3, CONV_DIM), F32)], axis=0),
        w_a=w_br_a[l].astype(BF16), w_b=w_br_b[l].astype(BF16), w_c=w_br_c[l].astype(BF16),
        w_o=w_out[l].astype(BF16),
        ln_mix_g=ln_mix_g[l][None], ln_mix_b=ln_mix_b[l][None],
        ln_ffn_g=ln_ffn_g[l][None], ln_ffn_b=ln_ffn_b[l][None],
        wr_t=w_router.T, rb=jnp.broadcast_to(router_bias[:, None], (N_EXPERTS, LANES)),
        w_eg=w_e_gate[l].astype(BF16), w_eu=w_e_up[l].astype(BF16), w_ed=w_e_down[l].astype(BF16))


def kernel(x_prompt, x_sample, cache_nsa_kv, cache_dsa_kv, cache_dsa_kidx, state_nsa_win, state_conv, page_table,
           w_in, nsa_phi_pos, nsa_phi_w1, nsa_phi_w2, conv_w, w_br_a, w_br_b, w_br_c, w_out, ln_mix_g, ln_mix_b,
           ln_ffn_g, ln_ffn_b, w_router, router_bias, w_e_gate, w_e_up, w_e_down):
    bp, tp, _ = x_prompt.shape
    bs, ts, _ = x_sample.shape
    ts_pad = SUBLANES

    gp_p = _Group(bp, tp, tp, 0, 0, tm=512, tq=128)
    gp_p.tabs = _rope_tables(jnp.arange(tp, dtype=jnp.int32))
    gp_p.pt = jnp.arange(bp * tp // PAGE, dtype=jnp.int32).reshape(bp, tp // PAGE)
    gp_p.ov = _overlap_matrix(LANES, gp_p.nselp)
    gp_s = _Group(bs, ts, ts_pad, PAST_LEN, PAST_LEN, tm=bs * ts_pad, tq=ts_pad)
    pos_s = PAST_LEN + jnp.arange(ts_pad, dtype=jnp.int32)
    gp_s.tabs = jnp.tile(_rope_tables(pos_s), (1, bs, 1))
    gp_s.ov = _overlap_matrix(PAST_LEN // CMP_STRIDE, gp_s.nselp)

    xp = x_prompt.reshape(bp * tp, D_MODEL)
    xs = jnp.concatenate([x_sample, jnp.zeros((bs, ts_pad - ts, D_MODEL), F32)], axis=1).reshape(bs * ts_pad, D_MODEL)

    outs_p = [[] for _ in range(5)]
    outs_s = [[] for _ in range(5)]
    for l in range(DEPTH):
        lw = _layer_weights(l, w_in, nsa_phi_pos, nsa_phi_w1, nsa_phi_w2, conv_w, w_br_a, w_br_b, w_br_c, w_out,
                            ln_mix_g, ln_mix_b, ln_ffn_g, ln_ffn_b, w_router, router_bias, w_e_gate, w_e_up, w_e_down)
        xp, hp_p, cu_p = _layer(gp_p, xp, lw, None)
        xs, hp_s, cu_s = _layer(gp_s, xs, lw, (l, cache_nsa_kv, cache_dsa_kv, cache_dsa_kidx,
                                                 state_nsa_win, state_conv, page_table))
        h3 = hp_p.reshape(bp, tp, NCOL)
        outs_p[0].append(h3[:, :, C_CMP:C_CMP + 1024].reshape(bp, tp, 4, NSA_KV, HEAD_DIM))
        outs_p[1].append(h3[:, :, C_DSA:C_DSA + 512].reshape(bp, tp, 2, DSA_KV, HEAD_DIM))
        outs_p[2].append(h3[:, :, C_MISC:C_MISC + IDX_DIM])
        outs_p[3].append(h3[:, tp - min(WINDOW, tp):, C_WIN:C_WIN + 512].reshape(bp, min(WINDOW, tp), 2, NSA_KV, HEAD_DIM))
        outs_p[4].append(cu_p.reshape(bp, tp, CONV_DIM)[:, tp - 2:])
        s3 = hp_s.reshape(bs, ts_pad, NCOL)[:, :ts]
        outs_s[0].append(s3[:, :, C_CMP:C_CMP + 1024].reshape(bs, ts, 4, NSA_KV, HEAD_DIM))
        outs_s[1].append(s3[:, :, C_DSA:C_DSA + 512].reshape(bs, ts, 2, DSA_KV, HEAD_DIM))
        outs_s[2].append(s3[:, :, C_MISC:C_MISC + IDX_DIM])
        win_new = s3[:, :, C_WIN:C_WIN + 512].reshape(bs, ts, 2, NSA_KV, HEAD_DIM)
        wb = state_nsa_win.shape[2]
        outs_s[3].append(jnp.concatenate([state_nsa_win[l], win_new], axis=1)[:, -wb:])
        ext = jnp.concatenate([state_conv[l], cu_s.reshape(bs, ts_pad, CONV_DIM)[:, :ts]], axis=1)
        outs_s[4].append(ext[:, -2:])
    sp = [jnp.stack(a, axis=0) for a in outs_p]
    ss = [jnp.stack(a, axis=0) for a in outs_s]
    y_p = xp.reshape(bp, tp, D_MODEL)
    y_s = xs.reshape(bs, ts_pad, D_MODEL)[:, :ts]
    return (y_p, y_s, sp[0], ss[0], sp[1], ss[1], sp[2], ss[2], sp[3], ss[3], sp[4], ss[4])
```
